```python
import math
import jax, jax.numpy as jnp
from jax import lax
import numpy as np

D_MODEL = 1024
BATCH = 8
SEQ = 8192
DEPTH = 2
DEC_BATCH = 4
DEC_SEQ = 4096
PAST_LEN = 128

GRID_W = 64
Q_BLOCK = 128
ROPE_THETA = 10000.0
EPS = 1e-6

D_FF = 2816

N_BRANCH = 3
BRANCH_W = 512

GQA_HEADS = 8
GQA_KV_HEADS = 2
GQA_GROUPS = GQA_HEADS // GQA_KV_HEADS
GQA_HEAD_DIM = 64

HY_WIDTH = 512
HY_ORDER = 64
HY_EMB = 33
HY_BANDS = (HY_EMB - 1) // 2
HY_TARGET = 1e-2
HY_FAST = 0.3
HY_SLOW = 1.5
HY_SHORT = 3

MLA_HEADS = 8
MLA_Q_RANK = 256
MLA_KV_RANK = 128
MLA_NOPE = 64
MLA_ROPE = 32
MLA_V = 64

IN_WIDTHS = (
    GQA_HEADS * GQA_HEAD_DIM,
    GQA_KV_HEADS * GQA_HEAD_DIM,
    GQA_KV_HEADS * GQA_HEAD_DIM,
    3 * HY_WIDTH,
    MLA_Q_RANK,
    MLA_KV_RANK,
    MLA_ROPE,
    N_BRANCH * D_MODEL,
)
IN_WIDTH = sum(IN_WIDTHS)
IN_SPLITS = tuple(int(c) for c in np.cumsum(IN_WIDTHS)[:-1])

kernel_name = "hybrid_gqa_hyena_mla_macaron_encoder"


def rmsnorm(x, g):
    x32 = x.astype(jnp.float32)
    y = x32 * lax.rsqrt(jnp.mean(x32 * x32, axis=-1, keepdims=True) + EPS)
    return (y * g.astype(jnp.float32)).astype(x.dtype)


def swiglu(x, w_gate, w_up, w_down):
    return (jax.nn.silu(x @ w_gate) * (x @ w_up)) @ w_down


def axial_rope(L, d_rot):
    rows = L // GRID_W
    row = jnp.repeat(jnp.arange(rows, dtype=jnp.float32), GRID_W)
    col = jnp.tile(jnp.arange(GRID_W, dtype=jnp.float32), rows)
    n_freq = d_rot // 4
    inv = ROPE_THETA ** (-jnp.arange(n_freq, dtype=jnp.float32) / n_freq)
    ang = jnp.concatenate([row[:, None] * inv, col[:, None] * inv], axis=-1)
    return jnp.cos(ang), jnp.sin(ang)


def apply_rope(x, cos, sin):
    xp = x.reshape(x.shape[:-1] + (-1, 2))
    x0, x1 = xp[..., 0], xp[..., 1]
    c = cos[None, :, None, :].astype(x.dtype)
    s = sin[None, :, None, :].astype(x.dtype)
    return jnp.stack([x0 * c - x1 * s, x0 * s + x1 * c], axis=-1).reshape(x.shape)


def block_attention(q, k, v, scale):
    B, L, Hk, G, dk = q.shape
    dv = v.shape[-1]
    nb = L // Q_BLOCK
    qb = jnp.moveaxis(q.reshape(B, nb, Q_BLOCK, Hk, G, dk), 1, 0)

    def attend(qblk):
        s = jnp.einsum("bqkgd,bskd->bkgqs", qblk, k,
                       preferred_element_type=jnp.float32) * scale
        p = jax.nn.softmax(s, axis=-1).astype(v.dtype)
        return jnp.einsum("bkgqs,bskd->bqkgd", p, v)

    o = lax.map(attend, qb)
    return jnp.moveaxis(o, 0, 1).reshape(B, L, Hk, G, dv)


def gqa_branch(q, k, v, g_q, g_k, cos, sin):
    B, L, _ = q.shape
    q = apply_rope(rmsnorm(q.reshape(B, L, GQA_HEADS, GQA_HEAD_DIM), g_q), cos, sin)
    k = apply_rope(rmsnorm(k.reshape(B, L, GQA_KV_HEADS, GQA_HEAD_DIM), g_k), cos, sin)
    v = v.reshape(B, L, GQA_KV_HEADS, GQA_HEAD_DIM)
    q = q.reshape(B, L, GQA_KV_HEADS, GQA_GROUPS, GQA_HEAD_DIM)
    o = block_attention(q, k, v, GQA_HEAD_DIM ** -0.5)
    return o.reshape(B, L, GQA_HEADS * GQA_HEAD_DIM)


def hyena_positions(L):
    t = jnp.linspace(0.0, 1.0, L, dtype=jnp.float32)[:, None]
    w = 2.0 * math.pi * jnp.arange(L, dtype=jnp.float32)[:, None] / L
    f = jnp.linspace(1e-4, HY_BANDS - 1, HY_BANDS, dtype=jnp.float32)[None, :]
    z = jnp.concatenate([t, jnp.cos(f * w), -jnp.sin(f * w)], axis=-1)
    max_decay = math.log(HY_TARGET) / HY_FAST
    min_decay = math.log(HY_TARGET) / HY_SLOW
    deltas = jnp.abs(jnp.linspace(min_decay, max_decay, HY_WIDTH, dtype=jnp.float32))
    window = jnp.exp(-t * deltas[None, :])
    return z, window


def hyena_filter(z, window, w1, b1, w2, b2, w3, freq):
    f32 = jnp.float32
    fr = freq.astype(f32)
    h = jnp.sin(fr * (z @ w1.astype(f32) + b1.astype(f32)))
    h = jnp.sin(fr * (h @ w2.astype(f32) + b2.astype(f32)))
    h = (h @ w3.astype(f32)).reshape(-1, 2, HY_WIDTH) * window[:, None, :]
    fwd, bwd = h[:, 0], h[:, 1]
    return jnp.concatenate([fwd, jnp.zeros((1, HY_WIDTH), f32), bwd[:0:-1]], axis=0)


def hyena_branch(hy, z_pos, window, w_short, b_short, w1, b1, w2, b2, w3, freq, bias):
    B, L, _ = hy.shape
    hp = jnp.pad(hy, ((0, 0), (1, 1), (0, 0)))
    hy = hp[:, :-2] * w_short[0] + hp[:, 1:-1] * w_short[1] + hp[:, 2:] * w_short[2] + b_short
    x0, x1, v = jnp.split(hy, 3, axis=-1)
    kc = hyena_filter(z_pos, window, w1, b1, w2, b2, w3, freq)
    s = (x1 * v).astype(jnp.float32)
    spec = jnp.fft.rfft(s, n=2 * L, axis=1) * jnp.fft.rfft(kc, axis=0)[None]
    y = jnp.fft.irfft(spec, n=2 * L, axis=1)[:, :L] + s * bias.astype(jnp.float32)
    return (x0.astype(jnp.float32) * y).astype(hy.dtype)


def mla_branch(cq, ckv, kr, g_q, w_uq, g_kv, w_ukv, cos, sin):
    B, L, _ = cq.shape
    qh = (rmsnorm(cq, g_q) @ w_uq).reshape(B, L, MLA_HEADS, MLA_NOPE + MLA_ROPE)
    q_nope, q_rope = qh[..., :MLA_NOPE], qh[..., MLA_NOPE:]
    q_rope = apply_rope(q_rope, cos, sin)
    kvh = (rmsnorm(ckv, g_kv) @ w_ukv).reshape(B, L, MLA_HEADS, MLA_NOPE + MLA_V)
    k_nope, vh = kvh[..., :MLA_NOPE], kvh[..., MLA_NOPE:]
    k_rope = apply_rope(kr.reshape(B, L, 1, MLA_ROPE), cos, sin)
    qf = jnp.concatenate([q_nope, q_rope], axis=-1)[:, :, :, None, :]
    kf = jnp.concatenate([k_nope, jnp.broadcast_to(k_rope, (B, L, MLA_HEADS, MLA_ROPE))], axis=-1)
    o = block_attention(qf, kf, vh, (MLA_NOPE + MLA_ROPE) ** -0.5)
    return o.reshape(B, L, MLA_HEADS * MLA_V)


def encoder(x, params):
    (g_ffn1, w_ffn1_gate, w_ffn1_up, w_ffn1_down, g_mix, w_in, g_qnorm, g_knorm,
     w_hy_short, b_hy_short, w_hy_f1, b_hy_f1, w_hy_f2, b_hy_f2, w_hy_f3, hy_sin_freq, hy_bias,
     g_mla_q, w_mla_uq, g_mla_kv, w_mla_ukv, w_branch, w_out,
     g_ffn2, w_ffn2_gate, w_ffn2_up, w_ffn2_down, g_final) = params
    B, L, _ = x.shape
    cos_a, sin_a = axial_rope(L, GQA_HEAD_DIM)
    cos_m, sin_m = axial_rope(L, MLA_ROPE)
    z_pos, window = hyena_positions(L)
    for l in range(DEPTH):
        x = x + 0.5 * swiglu(rmsnorm(x, g_ffn1[l]), w_ffn1_gate[l], w_ffn1_up[l], w_ffn1_down[l])
        u = rmsnorm(x, g_mix[l])
        zc = u @ w_in[l]
        q, k, v, hy, cq, ckv, kr, gl = jnp.split(zc, IN_SPLITS, axis=-1)
        y_a = gqa_branch(q, k, v, g_qnorm[l], g_knorm[l], cos_a, sin_a)
        y_b = hyena_branch(hy, z_pos, window, w_hy_short[l], b_hy_short[l], w_hy_f1[l], b_hy_f1[l],
                           w_hy_f2[l], b_hy_f2[l], w_hy_f3[l], hy_sin_freq[l], hy_bias[l])
        y_c = mla_branch(cq, ckv, kr, g_mla_q[l], w_mla_uq[l], g_mla_kv[l], w_mla_ukv[l], cos_m, sin_m)
        gates = jax.nn.sigmoid(gl.reshape(B, L, N_BRANCH, D_MODEL))
        merged = (gates[:, :, 0] * (y_a @ w_branch[l, 0])
                  + gates[:, :, 1] * (y_b @ w_branch[l, 1])
                  + gates[:, :, 2] * (y_c @ w_branch[l, 2]))
        x = x + merged @ w_out[l]
        x = x + 0.5 * swiglu(rmsnorm(x, g_ffn2[l]), w_ffn2_gate[l], w_ffn2_up[l], w_ffn2_down[l])
    return rmsnorm(x, g_final)


def setup_inputs(seed: int = 0) -> dict:
    key = jax.random.key(seed)
    ks = jax.random.split(key, 40)
    cnt = [0]

    def nxt():
        cnt[0] += 1
        return ks[cnt[0] - 1]

    def nrm(shape, scale):
        return scale * jax.random.normal(nxt(), shape, jnp.float32)

    def gain(shape):
        return 1.0 + 0.01 * jax.random.normal(nxt(), shape, jnp.float32)

    D = D_MODEL
    return {
        "x_prompt": nrm((BATCH, SEQ, D), 1.0),
        "x_sample": nrm((DEC_BATCH, DEC_SEQ, D), 1.0),
        "g_ffn1": gain((DEPTH, D)),
        "w_ffn1_gate": nrm((DEPTH, D, D_FF), D ** -0.5),
        "w_ffn1_up": nrm((DEPTH, D, D_FF), D ** -0.5),
        "w_ffn1_down": nrm((DEPTH, D_FF, D), D_FF ** -0.5),
        "g_mix": gain((DEPTH, D)),
        "w_in": nrm((DEPTH, D, IN_WIDTH), D ** -0.5),
        "g_qnorm": gain((DEPTH, GQA_HEAD_DIM)),
        "g_knorm": gain((DEPTH, GQA_HEAD_DIM)),
        "w_hy_short": nrm((DEPTH, HY_SHORT, 3 * HY_WIDTH), HY_SHORT ** -0.5),
        "b_hy_short": nrm((DEPTH, 3 * HY_WIDTH), 0.02),
        "w_hy_f1": nrm((DEPTH, HY_EMB, HY_ORDER), HY_EMB ** -0.5),
        "b_hy_f1": nrm((DEPTH, HY_ORDER), 0.02),
        "w_hy_f2": nrm((DEPTH, HY_ORDER, HY_ORDER), HY_ORDER ** -0.5),
        "b_hy_f2": nrm((DEPTH, HY_ORDER), 0.02),
        "w_hy_f3": nrm((DEPTH, HY_ORDER, 2 * HY_WIDTH), 0.02),
        "hy_sin_freq": gain((DEPTH, HY_ORDER)),
        "hy_bias": nrm((DEPTH, HY_WIDTH), 0.5),
        "g_mla_q": gain((DEPTH, MLA_Q_RANK)),
        "w_mla_uq": nrm((DEPTH, MLA_Q_RANK, MLA_HEADS * (MLA_NOPE + MLA_ROPE)), MLA_Q_RANK ** -0.5),
        "g_mla_kv": gain((DEPTH, MLA_KV_RANK)),
        "w_mla_ukv": nrm((DEPTH, MLA_KV_RANK, MLA_HEADS * (MLA_NOPE + MLA_V)), MLA_KV_RANK ** -0.5),
        "w_branch": nrm((DEPTH, N_BRANCH, BRANCH_W, D), BRANCH_W ** -0.5),
        "w_out": nrm((DEPTH, D, D), D ** -0.5),
        "g_ffn2": gain((DEPTH, D)),
        "w_ffn2_gate": nrm((DEPTH, D, D_FF), D ** -0.5),
        "w_ffn2_up": nrm((DEPTH, D, D_FF), D ** -0.5),
        "w_ffn2_down": nrm((DEPTH, D_FF, D), D_FF ** -0.5),
        "g_final": gain((D,)),
    }


def reference(x_prompt, x_sample, g_ffn1, w_ffn1_gate, w_ffn1_up, w_ffn1_down, g_mix, w_in,
              g_qnorm, g_knorm, w_hy_short, b_hy_short, w_hy_f1, b_hy_f1, w_hy_f2, b_hy_f2,
              w_hy_f3, hy_sin_freq, hy_bias, g_mla_q, w_mla_uq, g_mla_kv, w_mla_ukv,
              w_branch, w_out, g_ffn2, w_ffn2_gate, w_ffn2_up, w_ffn2_down, g_final):
    params = (g_ffn1, w_ffn1_gate, w_ffn1_up, w_ffn1_down, g_mix, w_in, g_qnorm, g_knorm,
              w_hy_short, b_hy_short, w_hy_f1, b_hy_f1, w_hy_f2, b_hy_f2, w_hy_f3, hy_sin_freq,
              hy_bias, g_mla_q, w_mla_uq, g_mla_kv, w_mla_ukv, w_branch, w_out,
              g_ffn2, w_ffn2_gate, w_ffn2_up, w_ffn2_down, g_final)
    y_prompt = encoder(x_prompt, params)
    y_sample = encoder(x_sample, params)
    return (y_prompt, y_sample)
```

```python
import functools
import math

import numpy as np
import jax
import jax.numpy as jnp
from jax import lax
from jax.experimental import pallas as pl
from jax.experimental.pallas import tpu as pltpu

F32 = jnp.float32
BF16 = jnp.bfloat16

D_MODEL = 1024
DEPTH = 2
GRID_W = 64
ROPE_THETA = 10000.0
EPS = 1e-6
D_FF = 2816
N_BRANCH = 3
BRANCH_W = 512
GQA_HEADS = 8
GQA_KV_HEADS = 2
GQA_HEAD_DIM = 64
HY_WIDTH = 512
HY_ORDER = 64
HY_EMB = 33
HY_BANDS = (HY_EMB - 1) // 2
HY_TARGET = 1e-2
HY_FAST = 0.3
HY_SLOW = 1.5
MLA_HEADS = 8
MLA_Q_RANK = 256
MLA_KV_RANK = 128
MLA_NOPE = 64
MLA_ROPE = 32
MLA_V = 64
IN_WIDTHS = (512, 128, 128, 3 * HY_WIDTH, MLA_Q_RANK, MLA_KV_RANK, MLA_ROPE, N_BRANCH * D_MODEL)
IN_OFFS = tuple(int(c) for c in np.cumsum((0,) + IN_WIDTHS))
IN_WIDTH = IN_OFFS[-1]

LANES = 128
V7X_VMEM_BYTES = 64 * 1024 * 1024

TOK_TILE = 512
ATT_TQ = 256
ATT_TK = TOK_TILE
FF_CHUNKS = ((0, 1024), (1024, 2048), (2048, D_FF))
DFT_N2 = 128
HY_COLS = 8192
ONES_ROWS = 16
NEG_BIG = -1e30


def _cparams(sem, vmem_mb):
    return pltpu.CompilerParams(dimension_semantics=sem, vmem_limit_bytes=vmem_mb * 1024 * 1024)


def _const_spec(shape):
    nd = len(shape)
    return pl.BlockSpec(shape, lambda *_: (0,) * nd, pipeline_mode=pl.Buffered(1))


def _rms(x, g):
    return x * lax.rsqrt(jnp.mean(x * x, axis=-1, keepdims=True) + EPS) * g


def _dot(a, b):
    return jnp.dot(a, b, preferred_element_type=F32)


def _dot_nt(a, b):
    return lax.dot_general(a, b, (((1,), (1,)), ((), ())), preferred_element_type=F32)


def _dot_tn(a, b):
    return lax.dot_general(a, b, (((0,), (0,)), ((), ())), preferred_element_type=F32)


def _split(a):
    hi = a.astype(BF16)
    return hi, (a - hi.astype(F32)).astype(BF16)


def _dot3(a, b):
    ah, al = _split(a)
    bh, bl = _split(b)
    return _dot(ah, bh) + (_dot(ah, bl) + _dot(al, bh))


def _ffn_kernel(x_ref, g_ref, wg_ref, wu_ref, wd_ref, gf_ref, o_ref, *, final):
    x = x_ref[...]
    xb = _rms(x, g_ref[...]).astype(BF16)
    acc = jnp.zeros_like(x)
    for c0, c1 in FF_CHUNKS:
        gate = _dot(xb, wg_ref[:, c0:c1])
        up = _dot(xb, wu_ref[:, c0:c1])
        h = (gate * jax.nn.sigmoid(gate) * up).astype(BF16)
        acc = acc + _dot(h, wd_ref[c0:c1, :])
    y = x + 0.5 * acc
    if final:
        y = _rms(y, gf_ref[...])
    o_ref[...] = y


def _ffn(x2d, g, wg, wu, wd, g_final, final):
    t = x2d.shape[0]
    row = pl.BlockSpec((TOK_TILE, D_MODEL), lambda i: (i, 0))
    return pl.pallas_call(
        functools.partial(_ffn_kernel, final=final),
        out_shape=jax.ShapeDtypeStruct((t, D_MODEL), F32),
        grid=(t // TOK_TILE,),
        in_specs=[row, _const_spec((1, D_MODEL)), _const_spec((D_MODEL, D_FF)),
                  _const_spec((D_MODEL, D_FF)), _const_spec((D_FF, D_MODEL)), _const_spec((1, D_MODEL))],
        out_specs=row,
        compiler_params=_cparams(("parallel",), 48),
        name="ffn",
    )(x2d, g, wg, wu, wd, g_final)


PA_Q, PA_QS, PA_K, PA_KS, PA_V, PA_CQ, PA_CKV, PA_KR, PA_KRS, PA_END = (
    0, 1024, 2048, 2176, 2304, 2432, 2688, 2816, 2944, 3072)
GQA_KC = 128
MLA_KC = 256


def _proj_a_kernel(x_ref, g_ref, w_ref, cosa_ref, sina_ref, cosm_ref, sinm_ref,
                   gq_ref, gqs_ref, gk_ref, gks_ref, gcq_ref, gckv_ref,
                   wuq_ref, wuqs_ref, wukv_ref,
                   qg_ref, kg_ref, vgt_ref, qm_ref, km_ref, vmt_ref):
    ub = _rms(x_ref[0], g_ref[...]).astype(BF16)
    cosa, sina = cosa_ref[...], sina_ref[...]
    cosm, sinm = cosm_ref[...], sinm_ref[...]

    zq = _dot(ub, w_ref[:, PA_Q:PA_QS])
    zqs = _dot(ub, w_ref[:, PA_QS:PA_K])
    tq_c = gq_ref[...] * cosa
    tq_s = gqs_ref[...] * sina
    scale_a = GQA_HEAD_DIM ** -0.5
    for h in range(GQA_HEADS):
        a = zq[:, h * LANES:(h + 1) * LANES]
        a_sw = zqs[:, h * LANES:(h + 1) * LANES]
        r = lax.rsqrt(jnp.sum(a * a, axis=-1, keepdims=True) * (1.0 / GQA_HEAD_DIM) + EPS)
        qg_ref[0, :, h * LANES:(h + 1) * LANES] = ((a * tq_c + a_sw * tq_s) * (r * scale_a)).astype(BF16)

    zk = _dot(ub, w_ref[:, PA_K:PA_KS])
    zks = _dot(ub, w_ref[:, PA_KS:PA_V])
    first = lax.broadcasted_iota(jnp.int32, zk.shape, 1) < GQA_HEAD_DIM
    sq = zk * zk
    r0 = lax.rsqrt(jnp.sum(jnp.where(first, sq, 0.0), axis=-1, keepdims=True) * (1.0 / GQA_HEAD_DIM) + EPS)
    r1 = lax.rsqrt(jnp.sum(jnp.where(first, 0.0, sq), axis=-1, keepdims=True) * (1.0 / GQA_HEAD_DIM) + EPS)
    kr = (zk * (gk_ref[...] * cosa) + zks * (gks_ref[...] * sina)) * jnp.where(first, r0, r1)
    kg_ref[0] = kr.astype(BF16)

    vgt_ref[0, 0] = _dot(ub, w_ref[:, PA_V:PA_CQ]).T.astype(BF16)

    scale_m = (MLA_NOPE + MLA_ROPE) ** -0.5
    cqn = _rms(_dot(ub, w_ref[:, PA_CQ:PA_CKV]), gcq_ref[...]).astype(BF16)
    zq2 = _dot(cqn, wuq_ref[...])
    zq2s = _dot(cqn, wuqs_ref[...])
    for h in range(MLA_HEADS):
        b0 = h * MLA_KC
        qm_ref[0, :, b0:b0 + LANES] = (zq2[:, b0:b0 + LANES] * scale_m).astype(BF16)
        rope = zq2[:, b0 + LANES:b0 + 2 * LANES] * cosm + zq2s[:, h * LANES:(h + 1) * LANES] * sinm
        qm_ref[0, :, b0 + LANES:b0 + 2 * LANES] = (rope * scale_m).astype(BF16)

    ckvn = _rms(_dot(ub, w_ref[:, PA_CKV:PA_KR]), gckv_ref[...]).astype(BF16)
    zkv = _dot(ckvn, wukv_ref[...])
    krope = (_dot(ub, w_ref[:, PA_KR:PA_KRS]) * cosm + _dot(ub, w_ref[:, PA_KRS:PA_END]) * sinm).astype(BF16)
    for p in range(MLA_HEADS // 2):
        b0 = p * MLA_KC
        km_ref[0, :, b0:b0 + LANES] = zkv[:, p * LANES:(p + 1) * LANES].astype(BF16)
        km_ref[0, :, b0 + LANES:b0 + 2 * LANES] = krope
    vmt_ref[0, 0] = zkv[:, MLA_HEADS * MLA_NOPE:].T.astype(BF16)


def _proj_a(x, g_mix, w_a, tabs, gq, gqs, gk, gks, gcq, gckv, wuq, wuqs, wukv):
    b, l, _ = x.shape
    tm = TOK_TILE
    nt = l // tm
    cosa, sina, cosm, sinm = tabs
    tab = pl.BlockSpec((tm, LANES), lambda bi, i: (i, 0))

    def tok(w):
        return pl.BlockSpec((1, tm, w), lambda bi, i: (bi, i, 0))

    def tr(rows):
        return pl.BlockSpec((1, 1, rows, tm), lambda bi, i: (bi, i, 0, 0))

    out_shape = (
        jax.ShapeDtypeStruct((b, l, GQA_HEADS * GQA_KC), BF16),
        jax.ShapeDtypeStruct((b, l, GQA_KC), BF16),
        jax.ShapeDtypeStruct((b, nt, GQA_KV_HEADS * GQA_HEAD_DIM, tm), BF16),
        jax.ShapeDtypeStruct((b, l, MLA_HEADS * MLA_KC), BF16),
        jax.ShapeDtypeStruct((b, l, (MLA_HEADS // 2) * MLA_KC), BF16),
        jax.ShapeDtypeStruct((b, nt, MLA_HEADS * MLA_V, tm), BF16),
    )
    return pl.pallas_call(
        _proj_a_kernel,
        out_shape=out_shape,
        grid=(b, nt),
        in_specs=[tok(D_MODEL), _const_spec((1, D_MODEL)), _const_spec(w_a.shape), tab, tab, tab, tab,
                  _const_spec((1, LANES)), _const_spec((1, LANES)), _const_spec((1, LANES)),
                  _const_spec((1, LANES)), _const_spec((1, MLA_Q_RANK)), _const_spec((1, MLA_KV_RANK)),
                  _const_spec(wuq.shape), _const_spec(wuqs.shape), _const_spec(wukv.shape)],
        out_specs=(tok(GQA_HEADS * GQA_KC), tok(GQA_KC), tr(GQA_KV_HEADS * GQA_HEAD_DIM),
                   tok(MLA_HEADS * MLA_KC), tok((MLA_HEADS // 2) * MLA_KC), tr(MLA_HEADS * MLA_V)),
        compiler_params=_cparams(("parallel", "parallel"), 48),
        name="proj_a",
    )(x, g_mix, w_a, cosa, sina, cosm, sinm, gq, gqs, gk, gks, gcq, gckv, wuq, wuqs, wukv)


def _proj_b_kernel(x_ref, g_ref, w_ref, hy_ref, gt_ref):
    ub = _rms(x_ref[...], g_ref[...]).astype(BF16)
    hy_ref[...] = _dot(ub, w_ref[:, :3 * HY_WIDTH]).astype(BF16)
    gt_ref[...] = jax.nn.sigmoid(_dot(ub, w_ref[:, 3 * HY_WIDTH:])).astype(BF16)


def _proj_b(x2d, g_mix, w_b):
    t = x2d.shape[0]
    tm = TOK_TILE

    def row(w):
        return pl.BlockSpec((tm, w), lambda i: (i, 0))

    return pl.pallas_call(
        _proj_b_kernel,
        out_shape=(jax.ShapeDtypeStruct((t, 3 * HY_WIDTH), BF16),
                   jax.ShapeDtypeStruct((t, N_BRANCH * D_MODEL), BF16)),
        grid=(t // tm,),
        in_specs=[row(D_MODEL), _const_spec((1, D_MODEL)), _const_spec(w_b.shape)],
        out_specs=(row(3 * HY_WIDTH), row(N_BRANCH * D_MODEL)),
        compiler_params=_cparams(("parallel",), 48),
        name="proj_b",
    )(x2d, g_mix, w_b)


def _attn_kernel(q_ref, k_ref, vt_ref, o_ref, *, heads, kc, v_stride, nk):
    tq = q_ref.shape[1]
    tk = vt_ref.shape[3]
    ones = jnp.ones((ONES_ROWS, tk), BF16)
    dv = MLA_V
    for h in range(heads):
        qh = q_ref[0, :, h * kc:(h + 1) * kc]

        def body(j, carry, qh=qh, h=h):
            m, acc = carry
            kblk = k_ref[0, pl.ds(pl.multiple_of(j * tk, tk), tk), :]
            st = _dot_nt(kblk, qh)
            m_new = jnp.maximum(m, jnp.max(st, axis=0, keepdims=True))
            p = jnp.exp(st - m_new).astype(BF16)
            alpha = jnp.exp(m - m_new)
            vext = jnp.concatenate([vt_ref[0, j, h * v_stride:h * v_stride + dv, :], ones], axis=0)
            return m_new, alpha * acc + _dot(vext, p)

        m0 = jnp.full((1, tq), NEG_BIG, F32)
        acc0 = jnp.zeros((dv + ONES_ROWS, tq), F32)
        _, acc = lax.fori_loop(0, nk, body, (m0, acc0))
        o_ref[0, h * dv:(h + 1) * dv, :] = (acc[:dv] / acc[dv:dv + 1]).astype(BF16)


def _attention(q, k, vt, *, groups, heads, kc, shared_kv):
    b, l, _ = q.shape
    nk, tk = vt.shape[1], vt.shape[3]
    tq = ATT_TQ
    dv = MLA_V
    v_rows = dv if shared_kv else heads * dv
    k_spec = pl.BlockSpec((1, l, kc), (lambda bi, g, i: (bi, 0, 0)) if shared_kv else (lambda bi, g, i: (bi, 0, g)))
    return pl.pallas_call(
        functools.partial(_attn_kernel, heads=heads, kc=kc, v_stride=0 if shared_kv else dv, nk=nk),
        out_shape=jax.ShapeDtypeStruct((b, groups * heads * dv, l), BF16),
        grid=(b, groups, l // tq),
        in_specs=[pl.BlockSpec((1, tq, heads * kc), lambda bi, g, i: (bi, i, g)),
                  k_spec,
                  pl.BlockSpec((1, nk, v_rows, tk), lambda bi, g, i: (bi, 0, g, 0))],
        out_specs=pl.BlockSpec((1, heads * dv, tq), lambda bi, g, i: (bi, g, i)),
        compiler_params=_cparams(("parallel", "parallel", "parallel"), 40),
        name="attn_gqa" if shared_kv else "attn_mla",
    )(q, k, vt)


def _hy_pre_kernel(x0_ref, x1_ref, v_ref, w_ref, b_ref, s_ref, x0o_ref):
    rows = x0_ref.shape[1]
    t = lax.broadcasted_iota(jnp.int32, (rows, LANES), 0)
    not_first = t > 0
    not_last = t < rows - 1

    def conv(ref, j):
        a = ref[0].astype(F32)
        prev = jnp.where(not_first, pltpu.roll(a, 1, 0), 0.0)
        nxt = jnp.where(not_last, pltpu.roll(a, rows - 1, 0), 0.0)
        return prev * w_ref[0, j] + a * w_ref[1, j] + nxt * w_ref[2, j] + b_ref[j]

    x0o_ref[0] = conv(x0_ref, 0).astype(BF16)
    s_ref[0] = (conv(x1_ref, 1) * conv(v_ref, 2)).astype(BF16)


def _hy_pre(hy, w_short, b_short):
    b, l, _ = hy.shape
    nc = HY_WIDTH // LANES
    w4 = w_short.reshape(3, 3, nc, 1, LANES)
    b4 = b_short.reshape(3, nc, 1, LANES)

    def seg(j):
        return pl.BlockSpec((1, l, LANES), lambda bi, c, j=j: (bi, 0, j * nc + c))

    out = pl.BlockSpec((1, l, LANES), lambda bi, c: (bi, 0, c))
    return pl.pallas_call(
        _hy_pre_kernel,
        out_shape=(jax.ShapeDtypeStruct((b, l, HY_WIDTH), BF16), jax.ShapeDtypeStruct((b, l, HY_WIDTH), BF16)),
        grid=(b, nc),
        in_specs=[seg(0), seg(1), seg(2),
                  pl.BlockSpec((3, 3, None, 1, LANES), lambda bi, c: (0, 0, c, 0, 0)),
                  pl.BlockSpec((3, None, 1, LANES), lambda bi, c: (0, c, 0, 0))],
        out_specs=(out, out),
        compiler_params=_cparams(("parallel", "parallel"), 48),
        name="hy_pre",
    )(hy, hy, hy, w4, b4)


def _left_mm_kernel(f_ref, x_ref, o_ref, *, hi):
    if hi:
        o_ref[0] = _dot3(f_ref[...], x_ref[0]).astype(o_ref.dtype)
    else:
        o_ref[0] = _dot(f_ref[...], x_ref[0]).astype(o_ref.dtype)


def _left_mm(f, x, out_dtype, hi=False):
    b, k, n = x.shape
    m = f.shape[0]
    cb = min(HY_COLS, n)
    return pl.pallas_call(
        functools.partial(_left_mm_kernel, hi=hi),
        out_shape=jax.ShapeDtypeStruct((b, m, n), out_dtype),
        grid=(b, n // cb),
        in_specs=[_const_spec((m, k)), pl.BlockSpec((1, k, cb), lambda bi, c: (bi, 0, c))],
        out_specs=pl.BlockSpec((1, m, cb), lambda bi, c: (bi, 0, c)),
        compiler_params=_cparams(("parallel", "parallel"), 48),
        name="hy_outer_dft",
    )(f, x)


def _hy_mid_kernel(f_ref, kf_ref, g_ref, a_ref, o_ref):
    n2 = DFT_N2
    kr, ki = kf_ref[0, :n2], kf_ref[0, n2:]
    for bi in range(a_ref.shape[0]):
        x = _dot(f_ref[0], a_ref[bi, 0])
        xr, xi = x[:n2], x[n2:]
        y = jnp.concatenate([xr * kr - xi * ki, xr * ki + xi * kr], axis=0).astype(BF16)
        o_ref[bi, 0] = _dot(g_ref[0], y).astype(BF16)


def _hy_mid(a4, f2, kf, g2):
    b, n1, r, c = a4.shape
    mat = pl.BlockSpec((1, r, r), lambda i: (i, 0, 0))
    dat = pl.BlockSpec((b, 1, r, c), lambda i: (0, i, 0, 0))
    return pl.pallas_call(
        _hy_mid_kernel,
        out_shape=jax.ShapeDtypeStruct(a4.shape, BF16),
        grid=(n1,),
        in_specs=[mat, pl.BlockSpec((1, r, c), lambda i: (i, 0, 0)), mat, dat],
        out_specs=dat,
        compiler_params=_cparams(("parallel",), 48),
        name="hy_mid",
    )(f2, kf, g2, a4)


def _hy_post_kernel(g_ref, bi_ref, s_ref, x0_ref, bias_ref, o_ref):
    y = _dot(g_ref[...], bi_ref[0]) + s_ref[0].astype(F32) * bias_ref[...]
    o_ref[0] = (x0_ref[0].astype(F32) * y).astype(BF16)


def _hy_post(g1, bi2, s2, x02, bias_t):
    b, k, n = bi2.shape
    m = g1.shape[0]
    cb = HY_COLS
    dat = pl.BlockSpec((1, m, cb), lambda bi, c: (bi, 0, c))
    return pl.pallas_call(
        _hy_post_kernel,
        out_shape=jax.ShapeDtypeStruct((b, m, n), BF16),
        grid=(b, n // cb),
        in_specs=[_const_spec((m, k)), pl.BlockSpec((1, k, cb), lambda bi, c: (bi, 0, c)), dat, dat,
                  _const_spec((1, cb))],
        out_specs=dat,
        compiler_params=_cparams(("parallel", "parallel"), 48),
        name="hy_post",
    )(g1, bi2, s2, x02, bias_t)


def _hy_filter_kernel(z_ref, win_ref, w1_ref, b1_ref, w2_ref, b2_ref, w3_ref, fr_ref, o_ref, *, half_tiles):
    fr = fr_ref[...]
    h = jnp.sin(fr * (_dot3(z_ref[...], w1_ref[...]) + b1_ref[...]))
    h = jnp.sin(fr * (_dot3(h, w2_ref[...]) + b2_ref[...]))
    h3 = _dot3(h, w3_ref[...])
    backward = pl.program_id(0) >= half_tiles
    o_ref[...] = jnp.where(backward, h3[:, HY_WIDTH:], h3[:, :HY_WIDTH]) * win_ref[...]


def _hy_filter(z2, win2, w1p, b1, w2, b2, w3, fr):
    n = z2.shape[0]
    tl = 512
    return pl.pallas_call(
        functools.partial(_hy_filter_kernel, half_tiles=n // (2 * tl)),
        out_shape=jax.ShapeDtypeStruct((n, HY_WIDTH), F32),
        grid=(n // tl,),
        in_specs=[pl.BlockSpec((tl, LANES), lambda i: (i, 0)), pl.BlockSpec((tl, HY_WIDTH), lambda i: (i, 0)),
                  _const_spec(w1p.shape), _const_spec(b1.shape), _const_spec(w2.shape), _const_spec(b2.shape),
                  _const_spec(w3.shape), _const_spec(fr.shape)],
        out_specs=pl.BlockSpec((tl, HY_WIDTH), lambda i: (i, 0)),
        compiler_params=_cparams(("parallel",), 32),
        name="hy_filter",
    )(z2, win2, w1p, b1, w2, b2, w3, fr)


def _hy_spec_kernel(f_ref, a_ref, o_ref):
    o_ref[0] = _dot3(f_ref[0], a_ref[0])


def _hy_spec(f2, a3):
    n1, r, c = a3.shape
    return pl.pallas_call(
        _hy_spec_kernel,
        out_shape=jax.ShapeDtypeStruct(a3.shape, F32),
        grid=(n1,),
        in_specs=[pl.BlockSpec((1, r, r), lambda i: (i, 0, 0)), pl.BlockSpec((1, r, c), lambda i: (i, 0, 0))],
        out_specs=pl.BlockSpec((1, r, c), lambda i: (i, 0, 0)),
        compiler_params=_cparams(("parallel",), 32),
        name="hy_filter_spectrum",
    )(f2, a3)


def _merge_kernel(x_ref, yat_ref, yb_ref, yct_ref, gt_ref, wb_ref, wo_ref, o_ref):
    d = D_MODEL
    ba = _dot_tn(yat_ref[0], wb_ref[0])
    bb = _dot(yb_ref[0], wb_ref[1])
    bc = _dot_tn(yct_ref[0], wb_ref[2])
    merged = (gt_ref[0, :, 0:d].astype(F32) * ba + gt_ref[0, :, d:2 * d].astype(F32) * bb
              + gt_ref[0, :, 2 * d:3 * d].astype(F32) * bc)
    o_ref[0] = x_ref[0] + _dot(merged.astype(BF16), wo_ref[...])


def _merge(x, yat, yb, yct, gates, wb, wo):
    b, l, _ = x.shape
    tm = TOK_TILE

    def tok(w):
        return pl.BlockSpec((1, tm, w), lambda bi, i: (bi, i, 0))

    tr = pl.BlockSpec((1, BRANCH_W, tm), lambda bi, i: (bi, 0, i))
    return pl.pallas_call(
        _merge_kernel,
        out_shape=jax.ShapeDtypeStruct(x.shape, F32),
        grid=(b, l // tm),
        in_specs=[tok(D_MODEL), tr, tok(BRANCH_W), tr, tok(N_BRANCH * D_MODEL),
                  _const_spec(wb.shape), _const_spec(wo.shape)],
        out_specs=tok(D_MODEL),
        compiler_params=_cparams(("parallel", "parallel"), 48),
        name="merge",
    )(x, yat, yb, yct, gates, wb, wo)


def _rope_tables(l):
    rows = l // GRID_W
    row = jnp.repeat(jnp.arange(rows, dtype=F32), GRID_W)
    col = jnp.tile(jnp.arange(GRID_W, dtype=F32), rows)

    def tab(d_rot):
        n_freq = d_rot // 4
        inv = ROPE_THETA ** (-jnp.arange(n_freq, dtype=F32) / n_freq)
        ang = jnp.concatenate([row[:, None] * inv, col[:, None] * inv], axis=-1)
        c = jnp.repeat(jnp.cos(ang), 2, axis=-1)
        s = jnp.repeat(jnp.sin(ang), 2, axis=-1) * jnp.tile(jnp.array([-1.0, 1.0], F32), d_rot // 2)
        return c, s

    ca, sa = tab(GQA_HEAD_DIM)
    cm, sm = tab(MLA_ROPE)
    pad = ((0, 0), (0, LANES - MLA_ROPE))
    return (jnp.tile(ca, (1, 2)), jnp.tile(sa, (1, 2)), jnp.pad(cm, pad), jnp.pad(sm, pad))


def _hy_positions(l):
    t = jnp.linspace(0.0, 1.0, l, dtype=F32)[:, None]
    w = 2.0 * math.pi * jnp.arange(l, dtype=F32)[:, None] / l
    f = jnp.linspace(1e-4, HY_BANDS - 1, HY_BANDS, dtype=F32)[None, :]
    z = jnp.concatenate([t, jnp.cos(f * w), -jnp.sin(f * w)], axis=-1)
    max_decay = math.log(HY_TARGET) / HY_FAST
    min_decay = math.log(HY_TARGET) / HY_SLOW
    deltas = jnp.abs(jnp.linspace(min_decay, max_decay, HY_WIDTH, dtype=F32))
    window = jnp.exp(-t * deltas[None, :])
    zu = jnp.roll(z[::-1], 1, axis=0)
    wu = jnp.roll(window[::-1], 1, axis=0).at[0].set(0.0)
    z2 = jnp.pad(jnp.concatenate([z, zu], axis=0), ((0, 0), (0, LANES - HY_EMB)))
    return z2, jnp.concatenate([window, wu], axis=0)


def _dft_tables(l):
    n = 2 * l
    n2 = DFT_N2
    n1 = n // n2
    two_pi = 2.0 * math.pi
    i1 = jnp.arange(n1, dtype=jnp.int32)
    ang1 = ((i1[:, None] * i1[None, :]) % n1).astype(F32) * (two_pi / n1)
    c1, s1 = jnp.cos(ang1), jnp.sin(ang1)
    f1_full = jnp.stack([c1, -s1], axis=1).reshape(2 * n1, n1)
    g1 = jnp.stack([c1[:n1 // 2], -s1[:n1 // 2]], axis=2).reshape(n1 // 2, 2 * n1) * (1.0 / n)
    i2 = jnp.arange(n2, dtype=jnp.int32)
    kk = i1[:, None, None] + n1 * i2[None, :, None]
    ang2 = ((kk * i2[None, None, :]) % n).astype(F32) * (two_pi / n)
    c2, s2 = jnp.cos(ang2), jnp.sin(ang2)
    f2 = jnp.concatenate([jnp.concatenate([c2, s2], axis=2), jnp.concatenate([-s2, c2], axis=2)], axis=1)
    c2t, s2t = jnp.swapaxes(c2, 1, 2), jnp.swapaxes(s2, 1, 2)
    g2 = jnp.concatenate([jnp.concatenate([c2t, -s2t], axis=2), jnp.concatenate([s2t, c2t], axis=2)], axis=1)
    return f1_full, g1, f2, g2


def _swap_pairs(n):
    return np.arange(n) ^ 1


def _proj_a_columns():
    zero = IN_WIDTH
    cols = np.full((PA_END,), zero, np.int64)
    q0, k0, v0, _, cq0, ckv0, kr0, _ = IN_OFFS[:8]
    for h in range(GQA_HEADS):
        dst = h * LANES + (h // (GQA_HEADS // GQA_KV_HEADS)) * GQA_HEAD_DIM
        src = q0 + h * GQA_HEAD_DIM + np.arange(GQA_HEAD_DIM)
        cols[PA_Q + dst:PA_Q + dst + GQA_HEAD_DIM] = src
        cols[PA_QS + dst:PA_QS + dst + GQA_HEAD_DIM] = q0 + h * GQA_HEAD_DIM + _swap_pairs(GQA_HEAD_DIM)
    cols[PA_K:PA_K + 128] = k0 + np.arange(128)
    cols[PA_KS:PA_KS + 128] = k0 + _swap_pairs(128)
    cols[PA_V:PA_V + 128] = v0 + np.arange(128)
    cols[PA_CQ:PA_CQ + MLA_Q_RANK] = cq0 + np.arange(MLA_Q_RANK)
    cols[PA_CKV:PA_CKV + MLA_KV_RANK] = ckv0 + np.arange(MLA_KV_RANK)
    cols[PA_KR:PA_KR + MLA_ROPE] = kr0 + np.arange(MLA_ROPE)
    cols[PA_KRS:PA_KRS + MLA_ROPE] = kr0 + _swap_pairs(MLA_ROPE)
    return cols


def _mla_q_columns():
    hd = MLA_NOPE + MLA_ROPE
    zero = MLA_HEADS * hd
    main = np.full((MLA_HEADS * MLA_KC,), zero, np.int64)
    swap = np.full((MLA_HEADS * LANES,), zero, np.int64)
    for h in range(MLA_HEADS):
        b0 = h * MLA_KC + (h % 2) * MLA_NOPE
        main[b0:b0 + MLA_NOPE] = h * hd + np.arange(MLA_NOPE)
        r0 = h * MLA_KC + LANES
        main[r0:r0 + MLA_ROPE] = h * hd + MLA_NOPE + np.arange(MLA_ROPE)
        swap[h * LANES:h * LANES + MLA_ROPE] = h * hd + MLA_NOPE + _swap_pairs(MLA_ROPE)
    return main, swap


def _mla_kv_columns():
    hd = MLA_NOPE + MLA_V
    knope = np.concatenate([h * hd + np.arange(MLA_NOPE) for h in range(MLA_HEADS)])
    val = np.concatenate([h * hd + MLA_NOPE + np.arange(MLA_V) for h in range(MLA_HEADS)])
    return np.concatenate([knope, val])


def _take_cols(w, cols):
    wz = jnp.concatenate([w, jnp.zeros((w.shape[0], 1), w.dtype)], axis=1)
    return jnp.take(wz, jnp.asarray(cols, jnp.int32), axis=1).astype(BF16)


def _tile2(g):
    return jnp.tile(g, 2)[None, :]


def _encoder(x, lw, g_final):
    b, l, d = x.shape
    n2 = DFT_N2
    n1 = 2 * l // n2
    c = HY_WIDTH
    tabs = _rope_tables(l)
    z2, win2 = _hy_positions(l)
    f1_full, g1, f2, g2 = _dft_tables(l)
    f1_b = f1_full[:, :n1 // 2].astype(BF16)
    g1_b = g1.astype(BF16)
    f2_b = f2.astype(BF16)
    g2_b = g2.astype(BF16)
    for li, w in enumerate(lw):
        x2 = _ffn(x.reshape(b * l, d), w["g_ffn1"], w["wg1"], w["wu1"], w["wd1"], g_final, False)
        x = x2.reshape(b, l, d)
        qg, kg, vgt, qm, km, vmt = _proj_a(x, w["g_mix"], w["w_a"], tabs, w["gq"], w["gqs"], w["gk"], w["gks"],
                                           w["gcq"], w["gckv"], w["wuq"], w["wuqs"], w["wukv"])
        hy, gates = _proj_b(x2, w["g_mix"], w["w_b"])
        yat = _attention(qg, kg, vgt, groups=GQA_KV_HEADS, heads=GQA_HEADS // GQA_KV_HEADS, kc=GQA_KC,
                         shared_kv=True)
        yct = _attention(qm, km, vmt, groups=MLA_HEADS // 2, heads=2, kc=MLA_KC, shared_kv=False)
        kc_time = _hy_filter(z2, win2, w["w1p"], w["b1"], w["w2"], w["b2"], w["w3"], w["fr"])
        ka = _left_mm(f1_full, kc_time.reshape(1, n1, n2 * c), F32, hi=True)
        kf = _hy_spec(f2, ka.reshape(n1, 2 * n2, c))
        s, x0 = _hy_pre(hy.reshape(b, l, 3 * c), w["w_short"], w["b_short"])
        a = _left_mm(f1_b, s.reshape(b, n1 // 2, n2 * c), BF16)
        bi = _hy_mid(a.reshape(b, n1, 2 * n2, c), f2_b, kf, g2_b)
        yb = _hy_post(g1_b, bi.reshape(b, 2 * n1, n2 * c), s.reshape(b, n1 // 2, n2 * c),
                      x0.reshape(b, n1 // 2, n2 * c), w["bias_t"])
        x = _merge(x, yat, yb.reshape(b, l, c), yct, gates.reshape(b, l, N_BRANCH * d), w["wb"], w["wo"])
        x2 = _ffn(x.reshape(b * l, d), w["g_ffn2"], w["wg2"], w["wu2"], w["wd2"], g_final, li == len(lw) - 1)
        x = x2.reshape(b, l, d)
    return x


def kernel(x_prompt, x_sample, g_ffn1, w_ffn1_gate, w_ffn1_up, w_ffn1_down, g_mix, w_in, g_qnorm, g_knorm,
           w_hy_short, b_hy_short, w_hy_f1, b_hy_f1, w_hy_f2, b_hy_f2, w_hy_f3, hy_sin_freq, hy_bias,
           g_mla_q, w_mla_uq, g_mla_kv, w_mla_ukv, w_branch, w_out, g_ffn2, w_ffn2_gate, w_ffn2_up,
           w_ffn2_down, g_final):
    cols_a = _proj_a_columns()
    uq_main, uq_swap = _mla_q_columns()
    ukv_cols = _mla_kv_columns()
    sw64 = _swap_pairs(GQA_HEAD_DIM)
    hy0 = IN_OFFS[3]
    gt0 = IN_OFFS[7]
    lw = []
    for l in range(DEPTH):
        lw.append(dict(
            g_ffn1=g_ffn1[l][None], wg1=w_ffn1_gate[l].astype(BF16), wu1=w_ffn1_up[l].astype(BF16),
            wd1=w_ffn1_down[l].astype(BF16),
            g_ffn2=g_ffn2[l][None], wg2=w_ffn2_gate[l].astype(BF16), wu2=w_ffn2_up[l].astype(BF16),
            wd2=w_ffn2_down[l].astype(BF16),
            g_mix=g_mix[l][None],
            w_a=_take_cols(w_in[l], cols_a),
            w_b=jnp.concatenate([w_in[l][:, hy0:hy0 + 3 * HY_WIDTH], w_in[l][:, gt0:]], axis=1).astype(BF16),
            gq=_tile2(g_qnorm[l]), gqs=_tile2(g_qnorm[l][sw64]),
            gk=_tile2(g_knorm[l]), gks=_tile2(g_knorm[l][sw64]),
            gcq=g_mla_q[l][None], gckv=g_mla_kv[l][None],
            wuq=_take_cols(w_mla_uq[l], uq_main), wuqs=_take_cols(w_mla_uq[l], uq_swap),
            wukv=_take_cols(w_mla_ukv[l], ukv_cols),
            w_short=w_hy_short[l], b_short=b_hy_short[l],
            w1p=jnp.pad(w_hy_f1[l], ((0, LANES - HY_EMB), (0, 0))), b1=b_hy_f1[l][None],
            w2=w_hy_f2[l], b2=b_hy_f2[l][None], w3=w_hy_f3[l], fr=hy_sin_freq[l][None],
            bias_t=jnp.tile(hy_bias[l], HY_COLS // HY_WIDTH)[None],
            wb=w_branch[l].astype(BF16), wo=w_out[l].astype(BF16),
        ))
    gf = g_final[None]
    return _encoder(x_prompt, lw, gf), _encoder(x_sample, lw, gf)
```

```python
import functools
import math

import numpy as np
import jax
import jax.numpy as jnp
from jax import lax
from jax.experimental import pallas as pl
from jax.experimental.pallas import tpu as pltpu

F32 = jnp.float32
BF16 = jnp.bfloat16

D_MODEL = 1024
DEPTH = 2
GRID_W = 64
ROPE_THETA = 10000.0
EPS = 1e-6
D_FF = 2816
N_BRANCH = 3
BRANCH_W = 512
GQA_HEADS = 8
GQA_KV_HEADS = 2
GQA_HEAD_DIM = 64
HY_WIDTH = 512
HY_ORDER = 64
HY_EMB = 33
HY_BANDS = (HY_EMB - 1) // 2
HY_TARGET = 1e-2
HY_FAST = 0.3
HY_SLOW = 1.5
MLA_HEADS = 8
MLA_Q_RANK = 256
MLA_KV_RANK = 128
MLA_NOPE = 64
MLA_ROPE = 32
MLA_V = 64
IN_WIDTHS = (512, 128, 128, 3 * HY_WIDTH, MLA_Q_RANK, MLA_KV_RANK, MLA_ROPE, N_BRANCH * D_MODEL)
IN_OFFS = tuple(int(c) for c in np.cumsum((0,) + IN_WIDTHS))
IN_WIDTH = IN_OFFS[-1]

LANES = 128
V7X_VMEM_BYTES = 64 * 1024 * 1024

TOK_TILE = 512
ATT_TQ = 256
ATT_TK = TOK_TILE
FF_CHUNKS = ((0, 1024), (1024, 2048), (2048, D_FF))
DFT_N2 = 128
HY_COLS = 8192
ONES_ROWS = 16
NEG_BIG = -1e30
LOG2E = math.log2(math.e)


def _cparams(sem, vmem_mb):
    return pltpu.CompilerParams(dimension_semantics=sem, vmem_limit_bytes=vmem_mb * 1024 * 1024)


def _const_spec(shape):
    nd = len(shape)
    return pl.BlockSpec(shape, lambda *_: (0,) * nd, pipeline_mode=pl.Buffered(1))


def _rms(x, g):
    return x * lax.rsqrt(jnp.mean(x * x, axis=-1, keepdims=True) + EPS) * g


def _dot(a, b):
    return jnp.dot(a, b, preferred_element_type=F32)


def _dot_nt(a, b):
    return lax.dot_general(a, b, (((1,), (1,)), ((), ())), preferred_element_type=F32)


def _dot_tn(a, b):
    return lax.dot_general(a, b, (((0,), (0,)), ((), ())), preferred_element_type=F32)


def _split(a):
    hi = a.astype(BF16)
    return hi, (a - hi.astype(F32)).astype(BF16)


def _dot3(a, b):
    ah, al = _split(a)
    bh, bl = _split(b)
    return _dot(ah, bh) + (_dot(ah, bl) + _dot(al, bh))


def _ffn_kernel(x_ref, g_ref, wg_ref, wu_ref, wd_ref, gf_ref, o_ref, *, final):
    x = x_ref[...]
    xb = _rms(x, g_ref[...]).astype(BF16)
    acc = jnp.zeros_like(x)
    for c0, c1 in FF_CHUNKS:
        gate = _dot(xb, wg_ref[:, c0:c1])
        up = _dot(xb, wu_ref[:, c0:c1])
        h = (gate * jax.nn.sigmoid(gate) * up).astype(BF16)
        acc = acc + _dot(h, wd_ref[c0:c1, :])
    y = x + 0.5 * acc
    if final:
        y = _rms(y, gf_ref[...])
    o_ref[...] = y


def _ffn(x2d, g, wg, wu, wd, g_final, final):
    t = x2d.shape[0]
    row = pl.BlockSpec((TOK_TILE, D_MODEL), lambda i: (i, 0))
    return pl.pallas_call(
        functools.partial(_ffn_kernel, final=final),
        out_shape=jax.ShapeDtypeStruct((t, D_MODEL), F32),
        grid=(t // TOK_TILE,),
        in_specs=[row, _const_spec((1, D_MODEL)), _const_spec((D_MODEL, D_FF)),
                  _const_spec((D_MODEL, D_FF)), _const_spec((D_FF, D_MODEL)), _const_spec((1, D_MODEL))],
        out_specs=row,
        compiler_params=_cparams(("parallel",), 48),
        name="ffn",
    )(x2d, g, wg, wu, wd, g_final)


PA_Q, PA_QS, PA_K, PA_KS, PA_V, PA_CQ, PA_CKV, PA_KR, PA_KRS, PA_END = (
    0, 1024, 2048, 2176, 2304, 2432, 2688, 2816, 2944, 3072)
GQA_KC = 128
MLA_KC = 256


def _proj_a_kernel(x_ref, g_ref, w_ref, cosa_ref, sina_ref, cosm_ref, sinm_ref,
                   gq_ref, gqs_ref, gk_ref, gks_ref, gcq_ref, gckv_ref,
                   wuq_ref, wuqs_ref, wukv_ref,
                   qg_ref, kg_ref, vgt_ref, qm_ref, km_ref, vmt_ref):
    ub = _rms(x_ref[0], g_ref[...]).astype(BF16)
    cosa, sina = cosa_ref[...], sina_ref[...]
    cosm, sinm = cosm_ref[...], sinm_ref[...]

    zq = _dot(ub, w_ref[:, PA_Q:PA_QS])
    zqs = _dot(ub, w_ref[:, PA_QS:PA_K])
    tq_c = gq_ref[...] * cosa
    tq_s = gqs_ref[...] * sina
    scale_a = GQA_HEAD_DIM ** -0.5 * LOG2E
    for h in range(GQA_HEADS):
        a = zq[:, h * LANES:(h + 1) * LANES]
        a_sw = zqs[:, h * LANES:(h + 1) * LANES]
        r = lax.rsqrt(jnp.sum(a * a, axis=-1, keepdims=True) * (1.0 / GQA_HEAD_DIM) + EPS)
        qg_ref[0, :, h * LANES:(h + 1) * LANES] = ((a * tq_c + a_sw * tq_s) * (r * scale_a)).astype(BF16)

    zk = _dot(ub, w_ref[:, PA_K:PA_KS])
    zks = _dot(ub, w_ref[:, PA_KS:PA_V])
    first = lax.broadcasted_iota(jnp.int32, zk.shape, 1) < GQA_HEAD_DIM
    sq = zk * zk
    r0 = lax.rsqrt(jnp.sum(jnp.where(first, sq, 0.0), axis=-1, keepdims=True) * (1.0 / GQA_HEAD_DIM) + EPS)
    r1 = lax.rsqrt(jnp.sum(jnp.where(first, 0.0, sq), axis=-1, keepdims=True) * (1.0 / GQA_HEAD_DIM) + EPS)
    kr = (zk * (gk_ref[...] * cosa) + zks * (gks_ref[...] * sina)) * jnp.where(first, r0, r1)
    kg_ref[0] = kr.astype(BF16)

    vgt_ref[0, 0] = _dot(ub, w_ref[:, PA_V:PA_CQ]).T.astype(BF16)

    scale_m = (MLA_NOPE + MLA_ROPE) ** -0.5 * LOG2E
    cqn = _rms(_dot(ub, w_ref[:, PA_CQ:PA_CKV]), gcq_ref[...]).astype(BF16)
    zq2 = _dot(cqn, wuq_ref[...])
    zq2s = _dot(cqn, wuqs_ref[...])
    for h in range(MLA_HEADS):
        b0 = h * MLA_KC
        qm_ref[0, :, b0:b0 + LANES] = (zq2[:, b0:b0 + LANES] * scale_m).astype(BF16)
        rope = zq2[:, b0 + LANES:b0 + 2 * LANES] * cosm + zq2s[:, h * LANES:(h + 1) * LANES] * sinm
        qm_ref[0, :, b0 + LANES:b0 + 2 * LANES] = (rope * scale_m).astype(BF16)

    ckvn = _rms(_dot(ub, w_ref[:, PA_CKV:PA_KR]), gckv_ref[...]).astype(BF16)
    zkv = _dot(ckvn, wukv_ref[...])
    krope = (_dot(ub, w_ref[:, PA_KR:PA_KRS]) * cosm + _dot(ub, w_ref[:, PA_KRS:PA_END]) * sinm).astype(BF16)
    for p in range(MLA_HEADS // 2):
        b0 = p * MLA_KC
        km_ref[0, :, b0:b0 + LANES] = zkv[:, p * LANES:(p + 1) * LANES].astype(BF16)
        km_ref[0, :, b0 + LANES:b0 + 2 * LANES] = krope
    vmt_ref[0, 0] = zkv[:, MLA_HEADS * MLA_NOPE:].T.astype(BF16)


def _proj_a(x, g_mix, w_a, tabs, gq, gqs, gk, gks, gcq, gckv, wuq, wuqs, wukv):
    b, l, _ = x.shape
    tm = TOK_TILE
    nt = l // tm
    cosa, sina, cosm, sinm = tabs
    tab = pl.BlockSpec((tm, LANES), lambda bi, i: (i, 0))

    def tok(w):
        return pl.BlockSpec((1, tm, w), lambda bi, i: (bi, i, 0))

    def tr(rows):
        return pl.BlockSpec((1, 1, rows, tm), lambda bi, i: (bi, i, 0, 0))

    out_shape = (
        jax.ShapeDtypeStruct((b, l, GQA_HEADS * GQA_KC), BF16),
        jax.ShapeDtypeStruct((b, l, GQA_KC), BF16),
        jax.ShapeDtypeStruct((b, nt, GQA_KV_HEADS * GQA_HEAD_DIM, tm), BF16),
        jax.ShapeDtypeStruct((b, l, MLA_HEADS * MLA_KC), BF16),
        jax.ShapeDtypeStruct((b, l, (MLA_HEADS // 2) * MLA_KC), BF16),
        jax.ShapeDtypeStruct((b, nt, MLA_HEADS * MLA_V, tm), BF16),
    )
    return pl.pallas_call(
        _proj_a_kernel,
        out_shape=out_shape,
        grid=(b, nt),
        in_specs=[tok(D_MODEL), _const_spec((1, D_MODEL)), _const_spec(w_a.shape), tab, tab, tab, tab,
                  _const_spec((1, LANES)), _const_spec((1, LANES)), _const_spec((1, LANES)),
                  _const_spec((1, LANES)), _const_spec((1, MLA_Q_RANK)), _const_spec((1, MLA_KV_RANK)),
                  _const_spec(wuq.shape), _const_spec(wuqs.shape), _const_spec(wukv.shape)],
        out_specs=(tok(GQA_HEADS * GQA_KC), tok(GQA_KC), tr(GQA_KV_HEADS * GQA_HEAD_DIM),
                   tok(MLA_HEADS * MLA_KC), tok((MLA_HEADS // 2) * MLA_KC), tr(MLA_HEADS * MLA_V)),
        compiler_params=_cparams(("parallel", "parallel"), 48),
        name="proj_a",
    )(x, g_mix, w_a, cosa, sina, cosm, sinm, gq, gqs, gk, gks, gcq, gckv, wuq, wuqs, wukv)


def _proj_b_kernel(x_ref, g_ref, w_ref, hy_ref, gt_ref):
    ub = _rms(x_ref[...], g_ref[...]).astype(BF16)
    hy_ref[...] = _dot(ub, w_ref[:, :3 * HY_WIDTH]).astype(BF16)
    gt_ref[...] = jax.nn.sigmoid(_dot(ub, w_ref[:, 3 * HY_WIDTH:])).astype(BF16)


def _proj_b(x2d, g_mix, w_b):
    t = x2d.shape[0]
    tm = TOK_TILE

    def row(w):
        return pl.BlockSpec((tm, w), lambda i: (i, 0))

    return pl.pallas_call(
        _proj_b_kernel,
        out_shape=(jax.ShapeDtypeStruct((t, 3 * HY_WIDTH), BF16),
                   jax.ShapeDtypeStruct((t, N_BRANCH * D_MODEL), BF16)),
        grid=(t // tm,),
        in_specs=[row(D_MODEL), _const_spec((1, D_MODEL)), _const_spec(w_b.shape)],
        out_specs=(row(3 * HY_WIDTH), row(N_BRANCH * D_MODEL)),
        compiler_params=_cparams(("parallel",), 48),
        name="proj_b",
    )(x2d, g_mix, w_b)


def _attn_kernel(q_ref, k_ref, vt_ref, o_ref, m_scr, acc_scr, s0_scr, s1_scr, t0_scr, t1_scr,
                 *, heads, kc, heads_per_kblock, v_stride, nk):
    tk = vt_ref.shape[3]
    ones = jnp.ones((ONES_ROWS, tk), BF16)
    dv = MLA_V
    m_scr[...] = jnp.full(m_scr.shape, NEG_BIG, F32)
    acc_scr[...] = jnp.zeros(acc_scr.shape, F32)

    def scores(j, h, s_scr, t_scr):
        kb = h // heads_per_kblock
        rows = pl.ds(pl.multiple_of(j * tk, tk), tk)
        st = _dot_nt(k_ref[0, rows, kb * kc:(kb + 1) * kc], q_ref[0, :, h * kc:(h + 1) * kc])
        s_scr[h] = st
        t_scr[h] = jnp.max(st, axis=0, keepdims=True)

    def consume(j, h, s_scr, t_scr):
        m = m_scr[h]
        m_new = jnp.maximum(m, t_scr[h])
        p = jnp.exp2(s_scr[h] - m_new).astype(BF16)
        alpha = jnp.exp2(m - m_new)
        vext = jnp.concatenate([vt_ref[0, j, h * v_stride:h * v_stride + dv, :], ones], axis=0)
        acc_scr[h] = alpha * acc_scr[h] + _dot(vext, p)
        m_scr[h] = m_new

    def step(j, cur, nxt):
        for h in range(heads):
            scores(j + 1, h, *nxt)
            consume(j, h, *cur)

    buf0, buf1 = (s0_scr, t0_scr), (s1_scr, t1_scr)
    for h in range(heads):
        scores(0, h, *buf0)

    def body(jj, carry):
        step(2 * jj, buf0, buf1)
        step(2 * jj + 1, buf1, buf0)
        return carry

    lax.fori_loop(0, nk // 2 - 1, body, 0)
    step(nk - 2, buf0, buf1)
    for h in range(heads):
        consume(nk - 1, h, *buf1)
    for h in range(heads):
        acc = acc_scr[h]
        o_ref[0, h * dv:(h + 1) * dv, :] = (acc[:dv] / acc[dv:dv + 1]).astype(BF16)


def _attention(q, k, vt, *, heads, kc, heads_per_kblock, shared_v):
    b, l, qw = q.shape
    groups = qw // (heads * kc)
    nk, tk = vt.shape[1], vt.shape[3]
    tq = ATT_TQ
    dv = MLA_V
    v_rows = dv if shared_v else heads * dv
    kw = (heads // heads_per_kblock) * kc
    k_groups = k.shape[2] // kw
    return pl.pallas_call(
        functools.partial(_attn_kernel, heads=heads, kc=kc, heads_per_kblock=heads_per_kblock,
                          v_stride=0 if shared_v else dv, nk=nk),
        out_shape=jax.ShapeDtypeStruct((b, groups * heads * dv, l), BF16),
        grid=(b, groups, l // tq),
        in_specs=[pl.BlockSpec((1, tq, heads * kc), lambda bi, g, i: (bi, i, g)),
                  pl.BlockSpec((1, l, kw), lambda bi, g, i: (bi, 0, g % k_groups)),
                  pl.BlockSpec((1, nk, v_rows, tk), lambda bi, g, i: (bi, 0, g, 0))],
        out_specs=pl.BlockSpec((1, heads * dv, tq), lambda bi, g, i: (bi, g, i)),
        scratch_shapes=[pltpu.VMEM((heads, 1, tq), F32), pltpu.VMEM((heads, dv + ONES_ROWS, tq), F32),
                        pltpu.VMEM((heads, tk, tq), F32), pltpu.VMEM((heads, tk, tq), F32),
                        pltpu.VMEM((heads, 1, tq), F32), pltpu.VMEM((heads, 1, tq), F32)],
        compiler_params=_cparams(("parallel", "parallel", "parallel"), 48),
        name="attn_gqa" if shared_v else "attn_mla",
    )(q, k, vt)


def _hy_pre_kernel(x0_ref, x1_ref, v_ref, w_ref, b_ref, s_ref, x0o_ref):
    rows = x0_ref.shape[1]
    t = lax.broadcasted_iota(jnp.int32, (rows, LANES), 0)
    not_first = t > 0
    not_last = t < rows - 1

    def conv(ref, j):
        a = ref[0].astype(F32)
        prev = jnp.where(not_first, pltpu.roll(a, 1, 0), 0.0)
        nxt = jnp.where(not_last, pltpu.roll(a, rows - 1, 0), 0.0)
        return prev * w_ref[0, j] + a * w_ref[1, j] + nxt * w_ref[2, j] + b_ref[j]

    x0o_ref[0] = conv(x0_ref, 0).astype(BF16)
    s_ref[0] = (conv(x1_ref, 1) * conv(v_ref, 2)).astype(BF16)


def _hy_pre(hy, w_short, b_short):
    b, l, _ = hy.shape
    nc = HY_WIDTH // LANES
    w4 = w_short.reshape(3, 3, nc, 1, LANES)
    b4 = b_short.reshape(3, nc, 1, LANES)

    def seg(j):
        return pl.BlockSpec((1, l, LANES), lambda bi, c, j=j: (bi, 0, j * nc + c))

    out = pl.BlockSpec((1, l, LANES), lambda bi, c: (bi, 0, c))
    return pl.pallas_call(
        _hy_pre_kernel,
        out_shape=(jax.ShapeDtypeStruct((b, l, HY_WIDTH), BF16), jax.ShapeDtypeStruct((b, l, HY_WIDTH), BF16)),
        grid=(b, nc),
        in_specs=[seg(0), seg(1), seg(2),
                  pl.BlockSpec((3, 3, None, 1, LANES), lambda bi, c: (0, 0, c, 0, 0)),
                  pl.BlockSpec((3, None, 1, LANES), lambda bi, c: (0, c, 0, 0))],
        out_specs=(out, out),
        compiler_params=_cparams(("parallel", "parallel"), 48),
        name="hy_pre",
    )(hy, hy, hy, w4, b4)


def _left_mm_kernel(f_ref, x_ref, o_ref, *, hi):
    if hi:
        o_ref[0] = _dot3(f_ref[...], x_ref[0]).astype(o_ref.dtype)
    else:
        o_ref[0] = _dot(f_ref[...], x_ref[0]).astype(o_ref.dtype)


def _left_mm(f, x, out_dtype, hi=False):
    b, k, n = x.shape
    m = f.shape[0]
    cb = min(HY_COLS, n)
    return pl.pallas_call(
        functools.partial(_left_mm_kernel, hi=hi),
        out_shape=jax.ShapeDtypeStruct((b, m, n), out_dtype),
        grid=(b, n // cb),
        in_specs=[_const_spec((m, k)), pl.BlockSpec((1, k, cb), lambda bi, c: (bi, 0, c))],
        out_specs=pl.BlockSpec((1, m, cb), lambda bi, c: (bi, 0, c)),
        compiler_params=_cparams(("parallel", "parallel"), 48),
        name="hy_outer_dft",
    )(f, x)


def _hy_mid_kernel(f_ref, kf_ref, g_ref, a_ref, o_ref):
    n2 = DFT_N2
    kr, ki = kf_ref[0, :n2], kf_ref[0, n2:]
    for bi in range(a_ref.shape[0]):
        x = _dot(f_ref[0], a_ref[bi, 0])
        xr, xi = x[:n2], x[n2:]
        y = jnp.concatenate([xr * kr - xi * ki, xr * ki + xi * kr], axis=0).astype(BF16)
        o_ref[bi, 0] = _dot(g_ref[0], y).astype(BF16)


def _hy_mid(a4, f2, kf, g2):
    b, n1, r, c = a4.shape
    mat = pl.BlockSpec((1, r, r), lambda i: (i, 0, 0))
    dat = pl.BlockSpec((b, 1, r, c), lambda i: (0, i, 0, 0))
    return pl.pallas_call(
        _hy_mid_kernel,
        out_shape=jax.ShapeDtypeStruct(a4.shape, BF16),
        grid=(n1,),
        in_specs=[mat, pl.BlockSpec((1, r, c), lambda i: (i, 0, 0)), mat, dat],
        out_specs=dat,
        compiler_params=_cparams(("parallel",), 48),
        name="hy_mid",
    )(f2, kf, g2, a4)


def _hy_post_kernel(g_ref, bi_ref, s_ref, x0_ref, bias_ref, o_ref):
    y = _dot(g_ref[...], bi_ref[0]) + s_ref[0].astype(F32) * bias_ref[...]
    o_ref[0] = (x0_ref[0].astype(F32) * y).astype(BF16)


def _hy_post(g1, bi2, s2, x02, bias_t):
    b, k, n = bi2.shape
    m = g1.shape[0]
    cb = HY_COLS
    dat = pl.BlockSpec((1, m, cb), lambda bi, c: (bi, 0, c))
    return pl.pallas_call(
        _hy_post_kernel,
        out_shape=jax.ShapeDtypeStruct((b, m, n), BF16),
        grid=(b, n // cb),
        in_specs=[_const_spec((m, k)), pl.BlockSpec((1, k, cb), lambda bi, c: (bi, 0, c)), dat, dat,
                  _const_spec((1, cb))],
        out_specs=dat,
        compiler_params=_cparams(("parallel", "parallel"), 48),
        name="hy_post",
    )(g1, bi2, s2, x02, bias_t)


def _hy_filter_kernel(z_ref, win_ref, w1_ref, b1_ref, w2_ref, b2_ref, w3_ref, fr_ref, o_ref, *, half_tiles):
    fr = fr_ref[...]
    h = jnp.sin(fr * (_dot3(z_ref[...], w1_ref[...]) + b1_ref[...]))
    h = jnp.sin(fr * (_dot3(h, w2_ref[...]) + b2_ref[...]))
    h3 = _dot3(h, w3_ref[...])
    backward = pl.program_id(0) >= half_tiles
    o_ref[...] = jnp.where(backward, h3[:, HY_WIDTH:], h3[:, :HY_WIDTH]) * win_ref[...]


def _hy_filter(z2, win2, w1p, b1, w2, b2, w3, fr):
    n = z2.shape[0]
    tl = 512
    return pl.pallas_call(
        functools.partial(_hy_filter_kernel, half_tiles=n // (2 * tl)),
        out_shape=jax.ShapeDtypeStruct((n, HY_WIDTH), F32),
        grid=(n // tl,),
        in_specs=[pl.BlockSpec((tl, LANES), lambda i: (i, 0)), pl.BlockSpec((tl, HY_WIDTH), lambda i: (i, 0)),
                  _const_spec(w1p.shape), _const_spec(b1.shape), _const_spec(w2.shape), _const_spec(b2.shape),
                  _const_spec(w3.shape), _const_spec(fr.shape)],
        out_specs=pl.BlockSpec((tl, HY_WIDTH), lambda i: (i, 0)),
        compiler_params=_cparams(("parallel",), 32),
        name="hy_filter",
    )(z2, win2, w1p, b1, w2, b2, w3, fr)


def _hy_spec_kernel(f_ref, a_ref, o_ref):
    o_ref[0] = _dot3(f_ref[0], a_ref[0])


def _hy_spec(f2, a3):
    n1, r, c = a3.shape
    return pl.pallas_call(
        _hy_spec_kernel,
        out_shape=jax.ShapeDtypeStruct(a3.shape, F32),
        grid=(n1,),
        in_specs=[pl.BlockSpec((1, r, r), lambda i: (i, 0, 0)), pl.BlockSpec((1, r, c), lambda i: (i, 0, 0))],
        out_specs=pl.BlockSpec((1, r, c), lambda i: (i, 0, 0)),
        compiler_params=_cparams(("parallel",), 32),
        name="hy_filter_spectrum",
    )(f2, a3)


def _merge_kernel(x_ref, yat_ref, yb_ref, yct_ref, gt_ref, wb_ref, wo_ref, o_ref):
    d = D_MODEL
    ba = _dot_tn(yat_ref[0], wb_ref[0])
    bb = _dot(yb_ref[0], wb_ref[1])
    bc = _dot_tn(yct_ref[0], wb_ref[2])
    merged = (gt_ref[0, :, 0:d].astype(F32) * ba + gt_ref[0, :, d:2 * d].astype(F32) * bb
              + gt_ref[0, :, 2 * d:3 * d].astype(F32) * bc)
    o_ref[0] = x_ref[0] + _dot(merged.astype(BF16), wo_ref[...])


def _merge(x, yat, yb, yct, gates, wb, wo):
    b, l, _ = x.shape
    tm = TOK_TILE

    def tok(w):
        return pl.BlockSpec((1, tm, w), lambda bi, i: (bi, i, 0))

    tr = pl.BlockSpec((1, BRANCH_W, tm), lambda bi, i: (bi, 0, i))
    return pl.pallas_call(
        _merge_kernel,
        out_shape=jax.ShapeDtypeStruct(x.shape, F32),
        grid=(b, l // tm),
        in_specs=[tok(D_MODEL), tr, tok(BRANCH_W), tr, tok(N_BRANCH * D_MODEL),
                  _const_spec(wb.shape), _const_spec(wo.shape)],
        out_specs=tok(D_MODEL),
        compiler_params=_cparams(("parallel", "parallel"), 48),
        name="merge",
    )(x, yat, yb, yct, gates, wb, wo)


def _rope_tables(l):
    rows = l // GRID_W
    row = jnp.repeat(jnp.arange(rows, dtype=F32), GRID_W)
    col = jnp.tile(jnp.arange(GRID_W, dtype=F32), rows)

    def tab(d_rot):
        n_freq = d_rot // 4
        inv = ROPE_THETA ** (-jnp.arange(n_freq, dtype=F32) / n_freq)
        ang = jnp.concatenate([row[:, None] * inv, col[:, None] * inv], axis=-1)
        c = jnp.repeat(jnp.cos(ang), 2, axis=-1)
        s = jnp.repeat(jnp.sin(ang), 2, axis=-1) * jnp.tile(jnp.array([-1.0, 1.0], F32), d_rot // 2)
        return c, s

    ca, sa = tab(GQA_HEAD_DIM)
    cm, sm = tab(MLA_ROPE)
    pad = ((0, 0), (0, LANES - MLA_ROPE))
    return (jnp.tile(ca, (1, 2)), jnp.tile(sa, (1, 2)), jnp.pad(cm, pad), jnp.pad(sm, pad))


def _hy_positions(l):
    t = jnp.linspace(0.0, 1.0, l, dtype=F32)[:, None]
    w = 2.0 * math.pi * jnp.arange(l, dtype=F32)[:, None] / l
    f = jnp.linspace(1e-4, HY_BANDS - 1, HY_BANDS, dtype=F32)[None, :]
    z = jnp.concatenate([t, jnp.cos(f * w), -jnp.sin(f * w)], axis=-1)
    max_decay = math.log(HY_TARGET) / HY_FAST
    min_decay = math.log(HY_TARGET) / HY_SLOW
    deltas = jnp.abs(jnp.linspace(min_decay, max_decay, HY_WIDTH, dtype=F32))
    window = jnp.exp(-t * deltas[None, :])
    zu = jnp.roll(z[::-1], 1, axis=0)
    wu = jnp.roll(window[::-1], 1, axis=0).at[0].set(0.0)
    z2 = jnp.pad(jnp.concatenate([z, zu], axis=0), ((0, 0), (0, LANES - HY_EMB)))
    return z2, jnp.concatenate([window, wu], axis=0)


def _dft_tables(l):
    n = 2 * l
    n2 = DFT_N2
    n1 = n // n2
    two_pi = 2.0 * math.pi
    i1 = jnp.arange(n1, dtype=jnp.int32)
    ang1 = ((i1[:, None] * i1[None, :]) % n1).astype(F32) * (two_pi / n1)
    c1, s1 = jnp.cos(ang1), jnp.sin(ang1)
    f1_full = jnp.stack([c1, -s1], axis=1).reshape(2 * n1, n1)
    g1 = jnp.stack([c1[:n1 // 2], -s1[:n1 // 2]], axis=2).reshape(n1 // 2, 2 * n1) * (1.0 / n)
    i2 = jnp.arange(n2, dtype=jnp.int32)
    kk = i1[:, None, None] + n1 * i2[None, :, None]
    ang2 = ((kk * i2[None, None, :]) % n).astype(F32) * (two_pi / n)
    c2, s2 = jnp.cos(ang2), jnp.sin(ang2)
    f2 = jnp.concatenate([jnp.concatenate([c2, s2], axis=2), jnp.concatenate([-s2, c2], axis=2)], axis=1)
    c2t, s2t = jnp.swapaxes(c2, 1, 2), jnp.swapaxes(s2, 1, 2)
    g2 = jnp.concatenate([jnp.concatenate([c2t, -s2t], axis=2), jnp.concatenate([s2t, c2t], axis=2)], axis=1)
    return f1_full, g1, f2, g2


def _swap_pairs(n):
    return np.arange(n) ^ 1


def _proj_a_columns():
    zero = IN_WIDTH
    cols = np.full((PA_END,), zero, np.int64)
    q0, k0, v0, _, cq0, ckv0, kr0, _ = IN_OFFS[:8]
    for h in range(GQA_HEADS):
        dst = h * LANES + (h // (GQA_HEADS // GQA_KV_HEADS)) * GQA_HEAD_DIM
        src = q0 + h * GQA_HEAD_DIM + np.arange(GQA_HEAD_DIM)
        cols[PA_Q + dst:PA_Q + dst + GQA_HEAD_DIM] = src
        cols[PA_QS + dst:PA_QS + dst + GQA_HEAD_DIM] = q0 + h * GQA_HEAD_DIM + _swap_pairs(GQA_HEAD_DIM)
    cols[PA_K:PA_K + 128] = k0 + np.arange(128)
    cols[PA_KS:PA_KS + 128] = k0 + _swap_pairs(128)
    cols[PA_V:PA_V + 128] = v0 + np.arange(128)
    cols[PA_CQ:PA_CQ + MLA_Q_RANK] = cq0 + np.arange(MLA_Q_RANK)
    cols[PA_CKV:PA_CKV + MLA_KV_RANK] = ckv0 + np.arange(MLA_KV_RANK)
    cols[PA_KR:PA_KR + MLA_ROPE] = kr0 + np.arange(MLA_ROPE)
    cols[PA_KRS:PA_KRS + MLA_ROPE] = kr0 + _swap_pairs(MLA_ROPE)
    return cols


def _mla_q_columns():
    hd = MLA_NOPE + MLA_ROPE
    zero = MLA_HEADS * hd
    main = np.full((MLA_HEADS * MLA_KC,), zero, np.int64)
    swap = np.full((MLA_HEADS * LANES,), zero, np.int64)
    for h in range(MLA_HEADS):
        b0 = h * MLA_KC + (h % 2) * MLA_NOPE
        main[b0:b0 + MLA_NOPE] = h * hd + np.arange(MLA_NOPE)
        r0 = h * MLA_KC + LANES
        main[r0:r0 + MLA_ROPE] = h * hd + MLA_NOPE + np.arange(MLA_ROPE)
        swap[h * LANES:h * LANES + MLA_ROPE] = h * hd + MLA_NOPE + _swap_pairs(MLA_ROPE)
    return main, swap


def _mla_kv_columns():
    hd = MLA_NOPE + MLA_V
    knope = np.concatenate([h * hd + np.arange(MLA_NOPE) for h in range(MLA_HEADS)])
    val = np.concatenate([h * hd + MLA_NOPE + np.arange(MLA_V) for h in range(MLA_HEADS)])
    return np.concatenate([knope, val])


def _take_cols(w, cols):
    wz = jnp.concatenate([w, jnp.zeros((w.shape[0], 1), w.dtype)], axis=1)
    return jnp.take(wz, jnp.asarray(cols, jnp.int32), axis=1).astype(BF16)


def _tile2(g):
    return jnp.tile(g, 2)[None, :]


def _encoder(x, lw, g_final):
    b, l, d = x.shape
    n2 = DFT_N2
    n1 = 2 * l // n2
    c = HY_WIDTH
    tabs = _rope_tables(l)
    z2, win2 = _hy_positions(l)
    f1_full, g1, f2, g2 = _dft_tables(l)
    f1_b = f1_full[:, :n1 // 2].astype(BF16)
    g1_b = g1.astype(BF16)
    f2_b = f2.astype(BF16)
    g2_b = g2.astype(BF16)
    for li, w in enumerate(lw):
        x2 = _ffn(x.reshape(b * l, d), w["g_ffn1"], w["wg1"], w["wu1"], w["wd1"], g_final, False)
        x = x2.reshape(b, l, d)
        qg, kg, vgt, qm, km, vmt = _proj_a(x, w["g_mix"], w["w_a"], tabs, w["gq"], w["gqs"], w["gk"], w["gks"],
                                           w["gcq"], w["gckv"], w["wuq"], w["wuqs"], w["wukv"])
        hy, gates = _proj_b(x2, w["g_mix"], w["w_b"])
        yat = _attention(qg, kg, vgt, heads=GQA_HEADS // GQA_KV_HEADS, kc=GQA_KC,
                         heads_per_kblock=GQA_HEADS // GQA_KV_HEADS, shared_v=True)
        yct = _attention(qm, km, vmt, heads=4, kc=MLA_KC, heads_per_kblock=2, shared_v=False)
        kc_time = _hy_filter(z2, win2, w["w1p"], w["b1"], w["w2"], w["b2"], w["w3"], w["fr"])
        ka = _left_mm(f1_full, kc_time.reshape(1, n1, n2 * c), F32, hi=True)
        kf = _hy_spec(f2, ka.reshape(n1, 2 * n2, c))
        s, x0 = _hy_pre(hy.reshape(b, l, 3 * c), w["w_short"], w["b_short"])
        a = _left_mm(f1_b, s.reshape(b, n1 // 2, n2 * c), BF16)
        bi = _hy_mid(a.reshape(b, n1, 2 * n2, c), f2_b, kf, g2_b)
        yb = _hy_post(g1_b, bi.reshape(b, 2 * n1, n2 * c), s.reshape(b, n1 // 2, n2 * c),
                      x0.reshape(b, n1 // 2, n2 * c), w["bias_t"])
        x = _merge(x, yat, yb.reshape(b, l, c), yct, gates.reshape(b, l, N_BRANCH * d), w["wb"], w["wo"])
        x2 = _ffn(x.reshape(b * l, d), w["g_ffn2"], w["wg2"], w["wu2"], w["wd2"], g_final, li == len(lw) - 1)
        x = x2.reshape(b, l, d)
    return x


def kernel(x_prompt, x_sample, g_ffn1, w_ffn1_gate, w_ffn1_up, w_ffn1_down, g_mix, w_in, g_qnorm, g_knorm,
           w_hy_short, b_hy_short, w_hy_f1, b_hy_f1, w_hy_f2, b_hy_f2, w_hy_f3, hy_sin_freq, hy_bias,
           g_mla_q, w_mla_uq, g_mla_kv, w_mla_ukv, w_branch, w_out, g_ffn2, w_ffn2_gate, w_ffn2_up,
           w_ffn2_down, g_final):
    cols_a = _proj_a_columns()
    uq_main, uq_swap = _mla_q_columns()
    ukv_cols = _mla_kv_columns()
    sw64 = _swap_pairs(GQA_HEAD_DIM)
    hy0 = IN_OFFS[3]
    gt0 = IN_OFFS[7]
    lw = []
    for l in range(DEPTH):
        lw.append(dict(
            g_ffn1=g_ffn1[l][None], wg1=w_ffn1_gate[l].astype(BF16), wu1=w_ffn1_up[l].astype(BF16),
            wd1=w_ffn1_down[l].astype(BF16),
            g_ffn2=g_ffn2[l][None], wg2=w_ffn2_gate[l].astype(BF16), wu2=w_ffn2_up[l].astype(BF16),
            wd2=w_ffn2_down[l].astype(BF16),
            g_mix=g_mix[l][None],
            w_a=_take_cols(w_in[l], cols_a),
            w_b=jnp.concatenate([w_in[l][:, hy0:hy0 + 3 * HY_WIDTH], w_in[l][:, gt0:]], axis=1).astype(BF16),
            gq=_tile2(g_qnorm[l]), gqs=_tile2(g_qnorm[l][sw64]),
            gk=_tile2(g_knorm[l]), gks=_tile2(g_knorm[l][sw64]),
            gcq=g_mla_q[l][None], gckv=g_mla_kv[l][None],
            wuq=_take_cols(w_mla_uq[l], uq_main), wuqs=_take_cols(w_mla_uq[l], uq_swap),
            wukv=_take_cols(w_mla_ukv[l], ukv_cols),
            w_short=w_hy_short[l], b_short=b_hy_short[l],
            w1p=jnp.pad(w_hy_f1[l], ((0, LANES - HY_EMB), (0, 0))), b1=b_hy_f1[l][None],
            w2=w_hy_f2[l], b2=b_hy_f2[l][None], w3=w_hy_f3[l], fr=hy_sin_freq[l][None],
            bias_t=jnp.tile(hy_bias[l], HY_COLS // HY_WIDTH)[None],
            wb=w_branch[l].astype(BF16), wo=w_out[l].astype(BF16),
        ))
    gf = g_final[None]
    return _encoder(x_prompt, lw, gf), _encoder(x_sample, lw, gf)
```

```python
import functools
import math

import numpy as np
import jax
import jax.numpy as jnp
from jax import lax
from jax.experimental import pallas as pl
from jax.experimental.pallas import tpu as pltpu

F32 = jnp.float32
BF16 = jnp.bfloat16

D_MODEL = 1024
DEPTH = 2
GRID_W = 64
ROPE_THETA = 10000.0
EPS = 1e-6
D_FF = 2816
N_BRANCH = 3
BRANCH_W = 512
GQA_HEADS = 8
GQA_KV_HEADS = 2
GQA_HEAD_DIM = 64
HY_WIDTH = 512
HY_ORDER = 64
HY_EMB = 33
HY_BANDS = (HY_EMB - 1) // 2
HY_TARGET = 1e-2
HY_FAST = 0.3
HY_SLOW = 1.5
MLA_HEADS = 8
MLA_Q_RANK = 256
MLA_KV_RANK = 128
MLA_NOPE = 64
MLA_ROPE = 32
MLA_V = 64
IN_WIDTHS = (512, 128, 128, 3 * HY_WIDTH, MLA_Q_RANK, MLA_KV_RANK, MLA_ROPE, N_BRANCH * D_MODEL)
IN_OFFS = tuple(int(c) for c in np.cumsum((0,) + IN_WIDTHS))
IN_WIDTH = IN_OFFS[-1]

LANES = 128
V7X_VMEM_BYTES = 64 * 1024 * 1024

TOK_TILE = 512
ATT_TQ = 256
ATT_TQB = 512
ATT_HEADS = 4
ATT_UNROLL = 4
ATT_TK = TOK_TILE
FF_CHUNKS = ((0, 1024), (1024, 2048), (2048, D_FF))
DFT_N2 = 128
HY_COLS = 8192
ONES_ROWS = 16
NEG_BIG = -1e30
LOG2E = math.log2(math.e)


def _cparams(sem, vmem_mb):
    return pltpu.CompilerParams(dimension_semantics=sem, vmem_limit_bytes=vmem_mb * 1024 * 1024)


def _const_spec(shape):
    nd = len(shape)
    return pl.BlockSpec(shape, lambda *_: (0,) * nd, pipeline_mode=pl.Buffered(1))


def _rms(x, g):
    return x * lax.rsqrt(jnp.mean(x * x, axis=-1, keepdims=True) + EPS) * g


def _dot(a, b):
    return jnp.dot(a, b, preferred_element_type=F32)


def _dot_nt(a, b):
    return lax.dot_general(a, b, (((1,), (1,)), ((), ())), preferred_element_type=F32)


def _dot_tn(a, b):
    return lax.dot_general(a, b, (((0,), (0,)), ((), ())), preferred_element_type=F32)


def _split(a):
    hi = a.astype(BF16)
    return hi, (a - hi.astype(F32)).astype(BF16)


def _dot3(a, b):
    ah, al = _split(a)
    bh, bl = _split(b)
    return _dot(ah, bh) + (_dot(ah, bl) + _dot(al, bh))


def _ffn_kernel(x_ref, g_ref, wg_ref, wu_ref, wd_ref, gf_ref, o_ref, *, final):
    x = x_ref[...]
    xb = _rms(x, g_ref[...]).astype(BF16)
    acc = jnp.zeros_like(x)
    for c0, c1 in FF_CHUNKS:
        gate = _dot(xb, wg_ref[:, c0:c1])
        up = _dot(xb, wu_ref[:, c0:c1])
        h = (gate * jax.nn.sigmoid(gate) * up).astype(BF16)
        acc = acc + _dot(h, wd_ref[c0:c1, :])
    y = x + 0.5 * acc
    if final:
        y = _rms(y, gf_ref[...])
    o_ref[...] = y


def _ffn(x2d, g, wg, wu, wd, g_final, final):
    t = x2d.shape[0]
    row = pl.BlockSpec((TOK_TILE, D_MODEL), lambda i: (i, 0))
    return pl.pallas_call(
        functools.partial(_ffn_kernel, final=final),
        out_shape=jax.ShapeDtypeStruct((t, D_MODEL), F32),
        grid=(t // TOK_TILE,),
        in_specs=[row, _const_spec((1, D_MODEL)), _const_spec((D_MODEL, D_FF)),
                  _const_spec((D_MODEL, D_FF)), _const_spec((D_FF, D_MODEL)), _const_spec((1, D_MODEL))],
        out_specs=row,
        compiler_params=_cparams(("parallel",), 48),
        name="ffn",
    )(x2d, g, wg, wu, wd, g_final)


PA_Q, PA_QS, PA_K, PA_KS, PA_V, PA_CQ, PA_CKV, PA_KR, PA_KRS, PA_END = (
    0, 1024, 2048, 2176, 2304, 2432, 2688, 2816, 2944, 3072)
GQA_KC = 128
MLA_KC = 256


def _proj_a_kernel(x_ref, g_ref, w_ref, cosa_ref, sina_ref, cosm_ref, sinm_ref,
                   gq_ref, gqs_ref, gk_ref, gks_ref, gcq_ref, gckv_ref,
                   wuq_ref, wuqs_ref, wukv_ref,
                   qg_ref, kg_ref, vgt_ref, qm_ref, km_ref, vmt_ref):
    ub = _rms(x_ref[0], g_ref[...]).astype(BF16)
    cosa, sina = cosa_ref[...], sina_ref[...]
    cosm, sinm = cosm_ref[...], sinm_ref[...]

    zq = _dot(ub, w_ref[:, PA_Q:PA_QS])
    zqs = _dot(ub, w_ref[:, PA_QS:PA_K])
    tq_c = gq_ref[...] * cosa
    tq_s = gqs_ref[...] * sina
    scale_a = GQA_HEAD_DIM ** -0.5 * LOG2E
    for h in range(GQA_HEADS):
        a = zq[:, h * LANES:(h + 1) * LANES]
        a_sw = zqs[:, h * LANES:(h + 1) * LANES]
        r = lax.rsqrt(jnp.sum(a * a, axis=-1, keepdims=True) * (1.0 / GQA_HEAD_DIM) + EPS)
        qg_ref[0, :, h * LANES:(h + 1) * LANES] = ((a * tq_c + a_sw * tq_s) * (r * scale_a)).astype(BF16)

    zk = _dot(ub, w_ref[:, PA_K:PA_KS])
    zks = _dot(ub, w_ref[:, PA_KS:PA_V])
    first = lax.broadcasted_iota(jnp.int32, zk.shape, 1) < GQA_HEAD_DIM
    sq = zk * zk
    r0 = lax.rsqrt(jnp.sum(jnp.where(first, sq, 0.0), axis=-1, keepdims=True) * (1.0 / GQA_HEAD_DIM) + EPS)
    r1 = lax.rsqrt(jnp.sum(jnp.where(first, 0.0, sq), axis=-1, keepdims=True) * (1.0 / GQA_HEAD_DIM) + EPS)
    kr = (zk * (gk_ref[...] * cosa) + zks * (gks_ref[...] * sina)) * jnp.where(first, r0, r1)
    kg_ref[0] = kr.astype(BF16)

    vgt_ref[0, 0] = _dot(ub, w_ref[:, PA_V:PA_CQ]).T.astype(BF16)

    scale_m = (MLA_NOPE + MLA_ROPE) ** -0.5 * LOG2E
    cqn = _rms(_dot(ub, w_ref[:, PA_CQ:PA_CKV]), gcq_ref[...]).astype(BF16)
    zq2 = _dot(cqn, wuq_ref[...])
    zq2s = _dot(cqn, wuqs_ref[...])
    for h in range(MLA_HEADS):
        b0 = h * MLA_KC
        qm_ref[0, :, b0:b0 + LANES] = (zq2[:, b0:b0 + LANES] * scale_m).astype(BF16)
        rope = zq2[:, b0 + LANES:b0 + 2 * LANES] * cosm + zq2s[:, h * LANES:(h + 1) * LANES] * sinm
        qm_ref[0, :, b0 + LANES:b0 + 2 * LANES] = (rope * scale_m).astype(BF16)

    ckvn = _rms(_dot(ub, w_ref[:, PA_CKV:PA_KR]), gckv_ref[...]).astype(BF16)
    zkv = _dot(ckvn, wukv_ref[...])
    krope = (_dot(ub, w_ref[:, PA_KR:PA_KRS]) * cosm + _dot(ub, w_ref[:, PA_KRS:PA_END]) * sinm).astype(BF16)
    for p in range(MLA_HEADS // 2):
        b0 = p * MLA_KC
        km_ref[0, :, b0:b0 + LANES] = zkv[:, p * LANES:(p + 1) * LANES].astype(BF16)
        km_ref[0, :, b0 + LANES:b0 + 2 * LANES] = krope
    vmt_ref[0, 0] = zkv[:, MLA_HEADS * MLA_NOPE:].T.astype(BF16)


def _proj_a(x, g_mix, w_a, tabs, gq, gqs, gk, gks, gcq, gckv, wuq, wuqs, wukv):
    b, l, _ = x.shape
    tm = TOK_TILE
    nt = l // tm
    cosa, sina, cosm, sinm = tabs
    tab = pl.BlockSpec((tm, LANES), lambda bi, i: (i, 0))

    def tok(w):
        return pl.BlockSpec((1, tm, w), lambda bi, i: (bi, i, 0))

    def tr(rows):
        return pl.BlockSpec((1, 1, rows, tm), lambda bi, i: (bi, i, 0, 0))

    out_shape = (
        jax.ShapeDtypeStruct((b, l, GQA_HEADS * GQA_KC), BF16),
        jax.ShapeDtypeStruct((b, l, GQA_KC), BF16),
        jax.ShapeDtypeStruct((b, nt, GQA_KV_HEADS * GQA_HEAD_DIM, tm), BF16),
        jax.ShapeDtypeStruct((b, l, MLA_HEADS * MLA_KC), BF16),
        jax.ShapeDtypeStruct((b, l, (MLA_HEADS // 2) * MLA_KC), BF16),
        jax.ShapeDtypeStruct((b, nt, MLA_HEADS * MLA_V, tm), BF16),
    )
    return pl.pallas_call(
        _proj_a_kernel,
        out_shape=out_shape,
        grid=(b, nt),
        in_specs=[tok(D_MODEL), _const_spec((1, D_MODEL)), _const_spec(w_a.shape), tab, tab, tab, tab,
                  _const_spec((1, LANES)), _const_spec((1, LANES)), _const_spec((1, LANES)),
                  _const_spec((1, LANES)), _const_spec((1, MLA_Q_RANK)), _const_spec((1, MLA_KV_RANK)),
                  _const_spec(wuq.shape), _const_spec(wuqs.shape), _const_spec(wukv.shape)],
        out_specs=(tok(GQA_HEADS * GQA_KC), tok(GQA_KC), tr(GQA_KV_HEADS * GQA_HEAD_DIM),
                   tok(MLA_HEADS * MLA_KC), tok((MLA_HEADS // 2) * MLA_KC), tr(MLA_HEADS * MLA_V)),
        compiler_params=_cparams(("parallel", "parallel"), 48),
        name="proj_a",
    )(x, g_mix, w_a, cosa, sina, cosm, sinm, gq, gqs, gk, gks, gcq, gckv, wuq, wuqs, wukv)


def _proj_b_kernel(x_ref, g_ref, w_ref, hy_ref, gt_ref):
    ub = _rms(x_ref[...], g_ref[...]).astype(BF16)
    hy_ref[...] = _dot(ub, w_ref[:, :3 * HY_WIDTH]).astype(BF16)
    gt_ref[...] = jax.nn.sigmoid(_dot(ub, w_ref[:, 3 * HY_WIDTH:])).astype(BF16)


def _proj_b(x2d, g_mix, w_b):
    t = x2d.shape[0]
    tm = TOK_TILE

    def row(w):
        return pl.BlockSpec((tm, w), lambda i: (i, 0))

    return pl.pallas_call(
        _proj_b_kernel,
        out_shape=(jax.ShapeDtypeStruct((t, 3 * HY_WIDTH), BF16),
                   jax.ShapeDtypeStruct((t, N_BRANCH * D_MODEL), BF16)),
        grid=(t // tm,),
        in_specs=[row(D_MODEL), _const_spec((1, D_MODEL)), _const_spec(w_b.shape)],
        out_specs=(row(3 * HY_WIDTH), row(N_BRANCH * D_MODEL)),
        compiler_params=_cparams(("parallel",), 48),
        name="proj_b",
    )(x2d, g_mix, w_b)


def _attn_kernel(q_ref, k_ref, vt_ref, o_ref, m_scr, acc_scr, s0_scr, s1_scr, t0_scr, t1_scr,
                 *, heads, groups, kc, key_of_head, value_of_head, nk):
    tk = vt_ref.shape[3]
    tq = ATT_TQ
    ones = jnp.ones((ONES_ROWS, tk), BF16)
    dv = MLA_V
    bufs = ((s0_scr, t0_scr), (s1_scr, t1_scr))
    streams = [(sub, g) for sub in range(q_ref.shape[1] // tq) for g in range(groups)]

    def scores(stream, j, hh, buf):
        sub, g = stream
        h = g * heads + hh
        kb = key_of_head[h]
        s_scr, t_scr = buf
        rows = j * tk if isinstance(j, int) else pl.multiple_of(j * tk, tk)
        st = _dot_nt(k_ref[0, pl.ds(rows, tk), kb * kc:(kb + 1) * kc],
                     q_ref[0, sub * tq:(sub + 1) * tq, h * kc:(h + 1) * kc])
        s_scr[hh] = st
        t_scr[hh] = jnp.max(st, axis=0, keepdims=True)

    def consume(si, stream, j, hh, buf):
        sub, g = stream
        vb = value_of_head[g * heads + hh]
        s_scr, t_scr = buf
        m = m_scr[si % 2, hh]
        m_new = jnp.maximum(m, t_scr[hh])
        p = jnp.exp2(s_scr[hh] - m_new).astype(BF16)
        alpha = jnp.exp2(m - m_new)
        vext = jnp.concatenate([vt_ref[0, j, vb * dv:(vb + 1) * dv, :], ones], axis=0)
        acc_scr[si % 2, hh] = alpha * acc_scr[si % 2, hh] + _dot(vext, p)
        m_scr[si % 2, hh] = m_new

    m_scr[...] = jnp.full(m_scr.shape, NEG_BIG, F32)
    acc_scr[...] = jnp.zeros(acc_scr.shape, F32)
    for hh in range(heads):
        scores(streams[0], 0, hh, bufs[0])
    for si, stream in enumerate(streams):
        def step(j, cur, nxt, si=si, stream=stream):
            scores(stream, j + 1, 0, nxt)
            for hh in range(heads):
                if hh + 1 < heads:
                    scores(stream, j + 1, hh + 1, nxt)
                consume(si, stream, j, hh, cur)

        def body(jj, carry, step=step):
            for u in range(ATT_UNROLL):
                step(ATT_UNROLL * jj + u, bufs[u % 2], bufs[1 - u % 2])
            return carry

        lax.fori_loop(0, nk // ATT_UNROLL - 1, body, 0)
        for u in range(ATT_UNROLL - 1):
            step(nk - ATT_UNROLL + u, bufs[u % 2], bufs[1 - u % 2])
        for hh in range(heads):
            if si + 1 < len(streams):
                scores(streams[si + 1], 0, hh, bufs[0])
            consume(si, stream, nk - 1, hh, bufs[1])
        sub, g = stream
        for hh in range(heads):
            acc = acc_scr[si % 2, hh]
            h = g * heads + hh
            o_ref[0, h * dv:(h + 1) * dv, sub * tq:(sub + 1) * tq] = (acc[:dv] / acc[dv:dv + 1]).astype(BF16)
            if si + 2 < len(streams):
                m_scr[si % 2, hh] = jnp.full((1, tq), NEG_BIG, F32)
                acc_scr[si % 2, hh] = jnp.zeros((dv + ONES_ROWS, tq), F32)


def _attention(q, k, vt, *, heads, kc, key_of_head, value_of_head, name):
    b, l, qw = q.shape
    n_heads = qw // kc
    nk, tk = vt.shape[1], vt.shape[3]
    tqb = ATT_TQB
    tq = ATT_TQ
    dv = MLA_V
    return pl.pallas_call(
        functools.partial(_attn_kernel, heads=heads, groups=n_heads // heads, kc=kc, key_of_head=key_of_head,
                          value_of_head=value_of_head, nk=nk),
        out_shape=jax.ShapeDtypeStruct((b, n_heads * dv, l), BF16),
        grid=(b, l // tqb),
        in_specs=[pl.BlockSpec((1, tqb, qw), lambda bi, i: (bi, i, 0)),
                  pl.BlockSpec((1, l, k.shape[2]), lambda bi, i: (bi, 0, 0), pipeline_mode=pl.Buffered(1)),
                  pl.BlockSpec((1, nk, vt.shape[2], tk), lambda bi, i: (bi, 0, 0, 0),
                               pipeline_mode=pl.Buffered(1))],
        out_specs=pl.BlockSpec((1, n_heads * dv, tqb), lambda bi, i: (bi, 0, i)),
        scratch_shapes=[pltpu.VMEM((2, heads, 1, tq), F32), pltpu.VMEM((2, heads, dv + ONES_ROWS, tq), F32),
                        pltpu.VMEM((heads, tk, tq), F32), pltpu.VMEM((heads, tk, tq), F32),
                        pltpu.VMEM((heads, 1, tq), F32), pltpu.VMEM((heads, 1, tq), F32)],
        compiler_params=_cparams(("parallel", "parallel"), 48),
        name=name,
    )(q, k, vt)


def _hy_pre_kernel(x0_ref, x1_ref, v_ref, w_ref, b_ref, s_ref, x0o_ref):
    rows = x0_ref.shape[1]
    t = lax.broadcasted_iota(jnp.int32, (rows, LANES), 0)
    not_first = t > 0
    not_last = t < rows - 1

    def conv(ref, j):
        a = ref[0].astype(F32)
        prev = jnp.where(not_first, pltpu.roll(a, 1, 0), 0.0)
        nxt = jnp.where(not_last, pltpu.roll(a, rows - 1, 0), 0.0)
        return prev * w_ref[0, j] + a * w_ref[1, j] + nxt * w_ref[2, j] + b_ref[j]

    x0o_ref[0] = conv(x0_ref, 0).astype(BF16)
    s_ref[0] = (conv(x1_ref, 1) * conv(v_ref, 2)).astype(BF16)


def _hy_pre(hy, w_short, b_short):
    b, l, _ = hy.shape
    nc = HY_WIDTH // LANES
    w4 = w_short.reshape(3, 3, nc, 1, LANES)
    b4 = b_short.reshape(3, nc, 1, LANES)

    def seg(j):
        return pl.BlockSpec((1, l, LANES), lambda bi, c, j=j: (bi, 0, j * nc + c))

    out = pl.BlockSpec((1, l, LANES), lambda bi, c: (bi, 0, c))
    return pl.pallas_call(
        _hy_pre_kernel,
        out_shape=(jax.ShapeDtypeStruct((b, l, HY_WIDTH), BF16), jax.ShapeDtypeStruct((b, l, HY_WIDTH), BF16)),
        grid=(b, nc),
        in_specs=[seg(0), seg(1), seg(2),
                  pl.BlockSpec((3, 3, None, 1, LANES), lambda bi, c: (0, 0, c, 0, 0)),
                  pl.BlockSpec((3, None, 1, LANES), lambda bi, c: (0, c, 0, 0))],
        out_specs=(out, out),
        compiler_params=_cparams(("parallel", "parallel"), 48),
        name="hy_pre",
    )(hy, hy, hy, w4, b4)


def _left_mm_kernel(f_ref, x_ref, o_ref, *, hi):
    if hi:
        o_ref[0] = _dot3(f_ref[...], x_ref[0]).astype(o_ref.dtype)
    else:
        o_ref[0] = _dot(f_ref[...], x_ref[0]).astype(o_ref.dtype)


def _left_mm(f, x, out_dtype, hi=False):
    b, k, n = x.shape
    m = f.shape[0]
    cb = min(HY_COLS, n)
    return pl.pallas_call(
        functools.partial(_left_mm_kernel, hi=hi),
        out_shape=jax.ShapeDtypeStruct((b, m, n), out_dtype),
        grid=(b, n // cb),
        in_specs=[_const_spec((m, k)), pl.BlockSpec((1, k, cb), lambda bi, c: (bi, 0, c))],
        out_specs=pl.BlockSpec((1, m, cb), lambda bi, c: (bi, 0, c)),
        compiler_params=_cparams(("parallel", "parallel"), 48),
        name="hy_outer_dft",
    )(f, x)


def _hy_mid_kernel(f_ref, kf_ref, g_ref, a_ref, o_ref):
    n2 = DFT_N2
    kr, ki = kf_ref[0, :n2], kf_ref[0, n2:]
    for bi in range(a_ref.shape[0]):
        x = _dot(f_ref[0], a_ref[bi, 0])
        xr, xi = x[:n2], x[n2:]
        y = jnp.concatenate([xr * kr - xi * ki, xr * ki + xi * kr], axis=0).astype(BF16)
        o_ref[bi, 0] = _dot(g_ref[0], y).astype(BF16)


def _hy_mid(a4, f2, kf, g2):
    b, n1, r, c = a4.shape
    mat = pl.BlockSpec((1, r, r), lambda i: (i, 0, 0))
    dat = pl.BlockSpec((b, 1, r, c), lambda i: (0, i, 0, 0))
    return pl.pallas_call(
        _hy_mid_kernel,
        out_shape=jax.ShapeDtypeStruct(a4.shape, BF16),
        grid=(n1,),
        in_specs=[mat, pl.BlockSpec((1, r, c), lambda i: (i, 0, 0)), mat, dat],
        out_specs=dat,
        compiler_params=_cparams(("parallel",), 48),
        name="hy_mid",
    )(f2, kf, g2, a4)


def _hy_post_kernel(g_ref, bi_ref, s_ref, x0_ref, bias_ref, o_ref):
    y = _dot(g_ref[...], bi_ref[0]) + s_ref[0].astype(F32) * bias_ref[...]
    o_ref[0] = (x0_ref[0].astype(F32) * y).astype(BF16)


def _hy_post(g1, bi2, s2, x02, bias_t):
    b, k, n = bi2.shape
    m = g1.shape[0]
    cb = HY_COLS
    dat = pl.BlockSpec((1, m, cb), lambda bi, c: (bi, 0, c))
    return pl.pallas_call(
        _hy_post_kernel,
        out_shape=jax.ShapeDtypeStruct((b, m, n), BF16),
        grid=(b, n // cb),
        in_specs=[_const_spec((m, k)), pl.BlockSpec((1, k, cb), lambda bi, c: (bi, 0, c)), dat, dat,
                  _const_spec((1, cb))],
        out_specs=dat,
        compiler_params=_cparams(("parallel", "parallel"), 48),
        name="hy_post",
    )(g1, bi2, s2, x02, bias_t)


def _hy_filter_kernel(z_ref, win_ref, w1_ref, b1_ref, w2_ref, b2_ref, w3_ref, fr_ref, o_ref, *, half_tiles):
    fr = fr_ref[...]
    h = jnp.sin(fr * (_dot3(z_ref[...], w1_ref[...]) + b1_ref[...]))
    h = jnp.sin(fr * (_dot3(h, w2_ref[...]) + b2_ref[...]))
    h3 = _dot3(h, w3_ref[...])
    backward = pl.program_id(0) >= half_tiles
    o_ref[...] = jnp.where(backward, h3[:, HY_WIDTH:], h3[:, :HY_WIDTH]) * win_ref[...]


def _hy_filter(z2, win2, w1p, b1, w2, b2, w3, fr):
    n = z2.shape[0]
    tl = 512
    return pl.pallas_call(
        functools.partial(_hy_filter_kernel, half_tiles=n // (2 * tl)),
        out_shape=jax.ShapeDtypeStruct((n, HY_WIDTH), F32),
        grid=(n // tl,),
        in_specs=[pl.BlockSpec((tl, LANES), lambda i: (i, 0)), pl.BlockSpec((tl, HY_WIDTH), lambda i: (i, 0)),
                  _const_spec(w1p.shape), _const_spec(b1.shape), _const_spec(w2.shape), _const_spec(b2.shape),
                  _const_spec(w3.shape), _const_spec(fr.shape)],
        out_specs=pl.BlockSpec((tl, HY_WIDTH), lambda i: (i, 0)),
        compiler_params=_cparams(("parallel",), 32),
        name="hy_filter",
    )(z2, win2, w1p, b1, w2, b2, w3, fr)


def _hy_spec_kernel(f_ref, a_ref, o_ref):
    o_ref[0] = _dot3(f_ref[0], a_ref[0])


def _hy_spec(f2, a3):
    n1, r, c = a3.shape
    return pl.pallas_call(
        _hy_spec_kernel,
        out_shape=jax.ShapeDtypeStruct(a3.shape, F32),
        grid=(n1,),
        in_specs=[pl.BlockSpec((1, r, r), lambda i: (i, 0, 0)), pl.BlockSpec((1, r, c), lambda i: (i, 0, 0))],
        out_specs=pl.BlockSpec((1, r, c), lambda i: (i, 0, 0)),
        compiler_params=_cparams(("parallel",), 32),
        name="hy_filter_spectrum",
    )(f2, a3)


def _merge_kernel(x_ref, yat_ref, yb_ref, yct_ref, gt_ref, wb_ref, wo_ref, o_ref):
    d = D_MODEL
    ba = _dot_tn(yat_ref[0], wb_ref[0])
    bb = _dot(yb_ref[0], wb_ref[1])
    bc = _dot_tn(yct_ref[0], wb_ref[2])
    merged = (gt_ref[0, :, 0:d].astype(F32) * ba + gt_ref[0, :, d:2 * d].astype(F32) * bb
              + gt_ref[0, :, 2 * d:3 * d].astype(F32) * bc)
    o_ref[0] = x_ref[0] + _dot(merged.astype(BF16), wo_ref[...])


def _merge(x, yat, yb, yct, gates, wb, wo):
    b, l, _ = x.shape
    tm = TOK_TILE

    def tok(w):
        return pl.BlockSpec((1, tm, w), lambda bi, i: (bi, i, 0))

    tr = pl.BlockSpec((1, BRANCH_W, tm), lambda bi, i: (bi, 0, i))
    return pl.pallas_call(
        _merge_kernel,
        out_shape=jax.ShapeDtypeStruct(x.shape, F32),
        grid=(b, l // tm),
        in_specs=[tok(D_MODEL), tr, tok(BRANCH_W), tr, tok(N_BRANCH * D_MODEL),
                  _const_spec(wb.shape), _const_spec(wo.shape)],
        out_specs=tok(D_MODEL),
        compiler_params=_cparams(("parallel", "parallel"), 48),
        name="merge",
    )(x, yat, yb, yct, gates, wb, wo)


def _rope_tables(l):
    rows = l // GRID_W
    row = jnp.repeat(jnp.arange(rows, dtype=F32), GRID_W)
    col = jnp.tile(jnp.arange(GRID_W, dtype=F32), rows)

    def tab(d_rot):
        n_freq = d_rot // 4
        inv = ROPE_THETA ** (-jnp.arange(n_freq, dtype=F32) / n_freq)
        ang = jnp.concatenate([row[:, None] * inv, col[:, None] * inv], axis=-1)
        c = jnp.repeat(jnp.cos(ang), 2, axis=-1)
        s = jnp.repeat(jnp.sin(ang), 2, axis=-1) * jnp.tile(jnp.array([-1.0, 1.0], F32), d_rot // 2)
        return c, s

    ca, sa = tab(GQA_HEAD_DIM)
    cm, sm = tab(MLA_ROPE)
    pad = ((0, 0), (0, LANES - MLA_ROPE))
    return (jnp.tile(ca, (1, 2)), jnp.tile(sa, (1, 2)), jnp.pad(cm, pad), jnp.pad(sm, pad))


def _hy_positions(l):
    t = jnp.linspace(0.0, 1.0, l, dtype=F32)[:, None]
    w = 2.0 * math.pi * jnp.arange(l, dtype=F32)[:, None] / l
    f = jnp.linspace(1e-4, HY_BANDS - 1, HY_BANDS, dtype=F32)[None, :]
    z = jnp.concatenate([t, jnp.cos(f * w), -jnp.sin(f * w)], axis=-1)
    max_decay = math.log(HY_TARGET) / HY_FAST
    min_decay = math.log(HY_TARGET) / HY_SLOW
    deltas = jnp.abs(jnp.linspace(min_decay, max_decay, HY_WIDTH, dtype=F32))
    window = jnp.exp(-t * deltas[None, :])
    zu = jnp.roll(z[::-1], 1, axis=0)
    wu = jnp.roll(window[::-1], 1, axis=0).at[0].set(0.0)
    z2 = jnp.pad(jnp.concatenate([z, zu], axis=0), ((0, 0), (0, LANES - HY_EMB)))
    return z2, jnp.concatenate([window, wu], axis=0)


def _dft_tables(l):
    n = 2 * l
    n2 = DFT_N2
    n1 = n // n2
    n1c = n1 // 2 + 1
    two_pi = 2.0 * math.pi
    i1 = jnp.arange(n1, dtype=jnp.int32)
    ang1 = ((i1[:n1c, None] * i1[None, :]) % n1).astype(F32) * (two_pi / n1)
    c1, s1 = jnp.cos(ang1), jnp.sin(ang1)
    f1_full = jnp.stack([c1, -s1], axis=1).reshape(2 * n1c, n1)
    fold = jnp.where((i1[:n1c] == 0) | (i1[:n1c] == n1 // 2), 1.0, 2.0)[:, None] * (1.0 / n)
    g1 = jnp.stack([(c1 * fold)[:, :n1 // 2].T, (-s1 * fold)[:, :n1 // 2].T], axis=2).reshape(n1 // 2, 2 * n1c)
    i2 = jnp.arange(n2, dtype=jnp.int32)
    kk = i1[:n1c, None, None] + n1 * i2[None, :, None]
    ang2 = ((kk * i2[None, None, :]) % n).astype(F32) * (two_pi / n)
    c2, s2 = jnp.cos(ang2), jnp.sin(ang2)
    f2 = jnp.concatenate([jnp.concatenate([c2, s2], axis=2), jnp.concatenate([-s2, c2], axis=2)], axis=1)
    c2t, s2t = jnp.swapaxes(c2, 1, 2), jnp.swapaxes(s2, 1, 2)
    g2 = jnp.concatenate([jnp.concatenate([c2t, -s2t], axis=2), jnp.concatenate([s2t, c2t], axis=2)], axis=1)
    return f1_full, g1, f2, g2


def _swap_pairs(n):
    return np.arange(n) ^ 1


def _proj_a_columns():
    zero = IN_WIDTH
    cols = np.full((PA_END,), zero, np.int64)
    q0, k0, v0, _, cq0, ckv0, kr0, _ = IN_OFFS[:8]
    for h in range(GQA_HEADS):
        dst = h * LANES + (h // (GQA_HEADS // GQA_KV_HEADS)) * GQA_HEAD_DIM
        src = q0 + h * GQA_HEAD_DIM + np.arange(GQA_HEAD_DIM)
        cols[PA_Q + dst:PA_Q + dst + GQA_HEAD_DIM] = src
        cols[PA_QS + dst:PA_QS + dst + GQA_HEAD_DIM] = q0 + h * GQA_HEAD_DIM + _swap_pairs(GQA_HEAD_DIM)
    cols[PA_K:PA_K + 128] = k0 + np.arange(128)
    cols[PA_KS:PA_KS + 128] = k0 + _swap_pairs(128)
    cols[PA_V:PA_V + 128] = v0 + np.arange(128)
    cols[PA_CQ:PA_CQ + MLA_Q_RANK] = cq0 + np.arange(MLA_Q_RANK)
    cols[PA_CKV:PA_CKV + MLA_KV_RANK] = ckv0 + np.arange(MLA_KV_RANK)
    cols[PA_KR:PA_KR + MLA_ROPE] = kr0 + np.arange(MLA_ROPE)
    cols[PA_KRS:PA_KRS + MLA_ROPE] = kr0 + _swap_pairs(MLA_ROPE)
    return cols


def _mla_q_columns():
    hd = MLA_NOPE + MLA_ROPE
    zero = MLA_HEADS * hd
    main = np.full((MLA_HEADS * MLA_KC,), zero, np.int64)
    swap = np.full((MLA_HEADS * LANES,), zero, np.int64)
    for h in range(MLA_HEADS):
        b0 = h * MLA_KC + (h % 2) * MLA_NOPE
        main[b0:b0 + MLA_NOPE] = h * hd + np.arange(MLA_NOPE)
        r0 = h * MLA_KC + LANES
        main[r0:r0 + MLA_ROPE] = h * hd + MLA_NOPE + np.arange(MLA_ROPE)
        swap[h * LANES:h * LANES + MLA_ROPE] = h * hd + MLA_NOPE + _swap_pairs(MLA_ROPE)
    return main, swap


def _mla_kv_columns():
    hd = MLA_NOPE + MLA_V
    knope = np.concatenate([h * hd + np.arange(MLA_NOPE) for h in range(MLA_HEADS)])
    val = np.concatenate([h * hd + MLA_NOPE + np.arange(MLA_V) for h in range(MLA_HEADS)])
    return np.concatenate([knope, val])


def _take_cols(w, cols):
    wz = jnp.concatenate([w, jnp.zeros((w.shape[0], 1), w.dtype)], axis=1)
    return jnp.take(wz, jnp.asarray(cols, jnp.int32), axis=1).astype(BF16)


def _tile2(g):
    return jnp.tile(g, 2)[None, :]


def _encoder(x, lw, g_final):
    b, l, d = x.shape
    n2 = DFT_N2
    n1 = 2 * l // n2
    c = HY_WIDTH
    tabs = _rope_tables(l)
    z2, win2 = _hy_positions(l)
    f1_full, g1, f2, g2 = _dft_tables(l)
    f1_b = f1_full[:, :n1 // 2].astype(BF16)
    g1_b = g1.astype(BF16)
    f2_b = f2.astype(BF16)
    g2_b = g2.astype(BF16)
    for li, w in enumerate(lw):
        x2 = _ffn(x.reshape(b * l, d), w["g_ffn1"], w["wg1"], w["wu1"], w["wd1"], g_final, False)
        x = x2.reshape(b, l, d)
        qg, kg, vgt, qm, km, vmt = _proj_a(x, w["g_mix"], w["w_a"], tabs, w["gq"], w["gqs"], w["gk"], w["gks"],
                                           w["gcq"], w["gckv"], w["wuq"], w["wuqs"], w["wukv"])
        hy, gates = _proj_b(x2, w["g_mix"], w["w_b"])
        per_kv = GQA_HEADS // GQA_KV_HEADS
        yat = _attention(qg, kg, vgt, heads=ATT_HEADS, kc=GQA_KC, key_of_head=(0,) * GQA_HEADS,
                         value_of_head=tuple(h // per_kv for h in range(GQA_HEADS)), name="attn_gqa")
        yct = _attention(qm, km, vmt, heads=ATT_HEADS, kc=MLA_KC,
                         key_of_head=tuple(h // 2 for h in range(MLA_HEADS)),
                         value_of_head=tuple(range(MLA_HEADS)), name="attn_mla")
        kc_time = _hy_filter(z2, win2, w["w1p"], w["b1"], w["w2"], w["b2"], w["w3"], w["fr"])
        n1c = n1 // 2 + 1
        ka = _left_mm(f1_full, kc_time.reshape(1, n1, n2 * c), F32, hi=True)
        kf = _hy_spec(f2, ka.reshape(n1c, 2 * n2, c))
        s, x0 = _hy_pre(hy.reshape(b, l, 3 * c), w["w_short"], w["b_short"])
        a = _left_mm(f1_b, s.reshape(b, n1 // 2, n2 * c), BF16)
        bi = _hy_mid(a.reshape(b, n1c, 2 * n2, c), f2_b, kf, g2_b)
        yb = _hy_post(g1_b, bi.reshape(b, 2 * n1c, n2 * c), s.reshape(b, n1 // 2, n2 * c),
                      x0.reshape(b, n1 // 2, n2 * c), w["bias_t"])
        x = _merge(x, yat, yb.reshape(b, l, c), yct, gates.reshape(b, l, N_BRANCH * d), w["wb"], w["wo"])
        x2 = _ffn(x.reshape(b * l, d), w["g_ffn2"], w["wg2"], w["wu2"], w["wd2"], g_final, li == len(lw) - 1)
        x = x2.reshape(b, l, d)
    return x


def kernel(x_prompt, x_sample, g_ffn1, w_ffn1_gate, w_ffn1_up, w_ffn1_down, g_mix, w_in, g_qnorm, g_knorm,
           w_hy_short, b_hy_short, w_hy_f1, b_hy_f1, w_hy_f2, b_hy_f2, w_hy_f3, hy_sin_freq, hy_bias,
           g_mla_q, w_mla_uq, g_mla_kv, w_mla_ukv, w_branch, w_out, g_ffn2, w_ffn2_gate, w_ffn2_up,
           w_ffn2_down, g_final):
    cols_a = _proj_a_columns()
    uq_main, uq_swap = _mla_q_columns()
    ukv_cols = _mla_kv_columns()
    sw64 = _swap_pairs(GQA_HEAD_DIM)
    hy0 = IN_OFFS[3]
    gt0 = IN_OFFS[7]
    lw = []
    for l in range(DEPTH):
        lw.append(dict(
            g_ffn1=g_ffn1[l][None], wg1=w_ffn1_gate[l].astype(BF16), wu1=w_ffn1_up[l].astype(BF16),
            wd1=w_ffn1_down[l].astype(BF16),
            g_ffn2=g_ffn2[l][None], wg2=w_ffn2_gate[l].astype(BF16), wu2=w_ffn2_up[l].astype(BF16),
            wd2=w_ffn2_down[l].astype(BF16),
            g_mix=g_mix[l][None],
            w_a=_take_cols(w_in[l], cols_a),
            w_b=jnp.concatenate([w_in[l][:, hy0:hy0 + 3 * HY_WIDTH], w_in[l][:, gt0:]], axis=1).astype(BF16),
            gq=_tile2(g_qnorm[l]), gqs=_tile2(g_qnorm[l][sw64]),
            gk=_tile2(g_knorm[l]), gks=_tile2(g_knorm[l][sw64]),
            gcq=g_mla_q[l][None], gckv=g_mla_kv[l][None],
            wuq=_take_cols(w_mla_uq[l], uq_main), wuqs=_take_cols(w_mla_uq[l], uq_swap),
            wukv=_take_cols(w_mla_ukv[l], ukv_cols),
            w_short=w_hy_short[l], b_short=b_hy_short[l],
            w1p=jnp.pad(w_hy_f1[l], ((0, LANES - HY_EMB), (0, 0))), b1=b_hy_f1[l][None],
            w2=w_hy_f2[l], b2=b_hy_f2[l][None], w3=w_hy_f3[l], fr=hy_sin_freq[l][None],
            bias_t=jnp.tile(hy_bias[l], HY_COLS // HY_WIDTH)[None],
            wb=w_branch[l].astype(BF16), wo=w_out[l].astype(BF16),
        ))
    gf = g_final[None]
    return _encoder(x_prompt, lw, gf), _encoder(x_sample, lw, gf)
```

```python
import functools
import math

import numpy as np
import jax
import jax.numpy as jnp
from jax import lax
from jax.experimental import pallas as pl
from jax.experimental.pallas import tpu as pltpu

F32 = jnp.float32
BF16 = jnp.bfloat16

D_MODEL = 1024
DEPTH = 2
GRID_W = 64
ROPE_THETA = 10000.0
EPS = 1e-6
D_FF = 2816
N_BRANCH = 3
BRANCH_W = 512
GQA_HEADS = 8
GQA_KV_HEADS = 2
GQA_HEAD_DIM = 64
HY_WIDTH = 512
HY_ORDER = 64
HY_EMB = 33
HY_BANDS = (HY_EMB - 1) // 2
HY_TARGET = 1e-2
HY_FAST = 0.3
HY_SLOW = 1.5
MLA_HEADS = 8
MLA_Q_RANK = 256
MLA_KV_RANK = 128
MLA_NOPE = 64
MLA_ROPE = 32
MLA_V = 64
IN_WIDTHS = (512, 128, 128, 3 * HY_WIDTH, MLA_Q_RANK, MLA_KV_RANK, MLA_ROPE, N_BRANCH * D_MODEL)
IN_OFFS = tuple(int(c) for c in np.cumsum((0,) + IN_WIDTHS))
IN_WIDTH = IN_OFFS[-1]

LANES = 128
V7X_VMEM_BYTES = 64 * 1024 * 1024

TOK_TILE = 512
ATT_TQ = 256
ATT_TQB = 512
ATT_HEADS = 4
ATT_UNROLL = 4
ATT_TK = TOK_TILE
FF_CHUNKS = ((0, 1024), (1024, 2048), (2048, D_FF))
DFT_N2 = 128
HY_COLS = 8192
ONES_ROWS = 16
NEG_BIG = -1e30
LOG2E = math.log2(math.e)


def _cparams(sem, vmem_mb):
    return pltpu.CompilerParams(dimension_semantics=sem, vmem_limit_bytes=vmem_mb * 1024 * 1024)


def _const_spec(shape):
    nd = len(shape)
    return pl.BlockSpec(shape, lambda *_: (0,) * nd, pipeline_mode=pl.Buffered(1))


def _rms(x, g):
    return x * lax.rsqrt(jnp.mean(x * x, axis=-1, keepdims=True) + EPS) * g


def _dot(a, b):
    return jnp.dot(a, b, preferred_element_type=F32)


def _dot_nt(a, b):
    return lax.dot_general(a, b, (((1,), (1,)), ((), ())), preferred_element_type=F32)


def _dot_tn(a, b):
    return lax.dot_general(a, b, (((0,), (0,)), ((), ())), preferred_element_type=F32)


def _split(a):
    hi = a.astype(BF16)
    return hi, (a - hi.astype(F32)).astype(BF16)


def _dot3(a, b):
    ah, al = _split(a)
    bh, bl = _split(b)
    return _dot(ah, bh) + (_dot(ah, bl) + _dot(al, bh))


def _ffn_kernel(x_ref, g_ref, wg_ref, wu_ref, wd_ref, gf_ref, o_ref, *, final):
    x = x_ref[...]
    xb = _rms(x, g_ref[...]).astype(BF16)
    acc = jnp.zeros_like(x)
    for c0, c1 in FF_CHUNKS:
        gate = _dot(xb, wg_ref[:, c0:c1])
        up = _dot(xb, wu_ref[:, c0:c1])
        h = (gate * jax.nn.sigmoid(gate) * up).astype(BF16)
        acc = acc + _dot(h, wd_ref[c0:c1, :])
    y = x + 0.5 * acc
    if final:
        y = _rms(y, gf_ref[...])
    o_ref[...] = y


def _ffn(x2d, g, wg, wu, wd, g_final, final):
    t = x2d.shape[0]
    row = pl.BlockSpec((TOK_TILE, D_MODEL), lambda i: (i, 0))
    return pl.pallas_call(
        functools.partial(_ffn_kernel, final=final),
        out_shape=jax.ShapeDtypeStruct((t, D_MODEL), F32),
        grid=(t // TOK_TILE,),
        in_specs=[row, _const_spec((1, D_MODEL)), _const_spec((D_MODEL, D_FF)),
                  _const_spec((D_MODEL, D_FF)), _const_spec((D_FF, D_MODEL)), _const_spec((1, D_MODEL))],
        out_specs=row,
        compiler_params=_cparams(("parallel",), 48),
        name="ffn",
    )(x2d, g, wg, wu, wd, g_final)


PA_Q, PA_QS, PA_K, PA_KS, PA_V, PA_CQ, PA_CKV, PA_KR, PA_KRS, PA_END = (
    0, 1024, 2048, 2176, 2304, 2432, 2688, 2816, 2944, 3072)
GQA_KC = 128
MLA_KC = 256


def _proj_a_kernel(x_ref, g_ref, w_ref, cosa_ref, sina_ref, cosm_ref, sinm_ref,
                   gq_ref, gqs_ref, gk_ref, gks_ref, gcq_ref, gckv_ref,
                   wuq_ref, wuqs_ref, wukv_ref,
                   qg_ref, kg_ref, vgt_ref, qm_ref, km_ref, vmt_ref):
    ub = _rms(x_ref[0], g_ref[...]).astype(BF16)
    cosa, sina = cosa_ref[...], sina_ref[...]
    cosm, sinm = cosm_ref[...], sinm_ref[...]

    zq = _dot(ub, w_ref[:, PA_Q:PA_QS])
    zqs = _dot(ub, w_ref[:, PA_QS:PA_K])
    tq_c = gq_ref[...] * cosa
    tq_s = gqs_ref[...] * sina
    scale_a = GQA_HEAD_DIM ** -0.5 * LOG2E
    for h in range(GQA_HEADS):
        a = zq[:, h * LANES:(h + 1) * LANES]
        a_sw = zqs[:, h * LANES:(h + 1) * LANES]
        r = lax.rsqrt(jnp.sum(a * a, axis=-1, keepdims=True) * (1.0 / GQA_HEAD_DIM) + EPS)
        qg_ref[0, :, h * LANES:(h + 1) * LANES] = ((a * tq_c + a_sw * tq_s) * (r * scale_a)).astype(BF16)

    zk = _dot(ub, w_ref[:, PA_K:PA_KS])
    zks = _dot(ub, w_ref[:, PA_KS:PA_V])
    first = lax.broadcasted_iota(jnp.int32, zk.shape, 1) < GQA_HEAD_DIM
    sq = zk * zk
    r0 = lax.rsqrt(jnp.sum(jnp.where(first, sq, 0.0), axis=-1, keepdims=True) * (1.0 / GQA_HEAD_DIM) + EPS)
    r1 = lax.rsqrt(jnp.sum(jnp.where(first, 0.0, sq), axis=-1, keepdims=True) * (1.0 / GQA_HEAD_DIM) + EPS)
    kr = (zk * (gk_ref[...] * cosa) + zks * (gks_ref[...] * sina)) * jnp.where(first, r0, r1)
    kg_ref[0] = kr.astype(BF16)

    vgt_ref[0, 0] = _dot(ub, w_ref[:, PA_V:PA_CQ]).T.astype(BF16)

    scale_m = (MLA_NOPE + MLA_ROPE) ** -0.5 * LOG2E
    cqn = _rms(_dot(ub, w_ref[:, PA_CQ:PA_CKV]), gcq_ref[...]).astype(BF16)
    zq2 = _dot(cqn, wuq_ref[...])
    zq2s = _dot(cqn, wuqs_ref[...])
    for h in range(MLA_HEADS):
        b0 = h * MLA_KC
        qm_ref[0, :, b0:b0 + LANES] = (zq2[:, b0:b0 + LANES] * scale_m).astype(BF16)
        rope = zq2[:, b0 + LANES:b0 + 2 * LANES] * cosm + zq2s[:, h * LANES:(h + 1) * LANES] * sinm
        qm_ref[0, :, b0 + LANES:b0 + 2 * LANES] = (rope * scale_m).astype(BF16)

    ckvn = _rms(_dot(ub, w_ref[:, PA_CKV:PA_KR]), gckv_ref[...]).astype(BF16)
    zkv = _dot(ckvn, wukv_ref[...])
    krope = (_dot(ub, w_ref[:, PA_KR:PA_KRS]) * cosm + _dot(ub, w_ref[:, PA_KRS:PA_END]) * sinm).astype(BF16)
    for p in range(MLA_HEADS // 2):
        b0 = p * MLA_KC
        km_ref[0, :, b0:b0 + LANES] = zkv[:, p * LANES:(p + 1) * LANES].astype(BF16)
        km_ref[0, :, b0 + LANES:b0 + 2 * LANES] = krope
    vmt_ref[0, 0] = zkv[:, MLA_HEADS * MLA_NOPE:].T.astype(BF16)


def _proj_a(x, g_mix, w_a, tabs, gq, gqs, gk, gks, gcq, gckv, wuq, wuqs, wukv):
    b, l, _ = x.shape
    tm = TOK_TILE
    nt = l // tm
    cosa, sina, cosm, sinm = tabs
    tab = pl.BlockSpec((tm, LANES), lambda bi, i: (i, 0))

    def tok(w):
        return pl.BlockSpec((1, tm, w), lambda bi, i: (bi, i, 0))

    def tr(rows):
        return pl.BlockSpec((1, 1, rows, tm), lambda bi, i: (bi, i, 0, 0))

    out_shape = (
        jax.ShapeDtypeStruct((b, l, GQA_HEADS * GQA_KC), BF16),
        jax.ShapeDtypeStruct((b, l, GQA_KC), BF16),
        jax.ShapeDtypeStruct((b, nt, GQA_KV_HEADS * GQA_HEAD_DIM, tm), BF16),
        jax.ShapeDtypeStruct((b, l, MLA_HEADS * MLA_KC), BF16),
        jax.ShapeDtypeStruct((b, l, (MLA_HEADS // 2) * MLA_KC), BF16),
        jax.ShapeDtypeStruct((b, nt, MLA_HEADS * MLA_V, tm), BF16),
    )
    return pl.pallas_call(
        _proj_a_kernel,
        out_shape=out_shape,
        grid=(b, nt),
        in_specs=[tok(D_MODEL), _const_spec((1, D_MODEL)), _const_spec(w_a.shape), tab, tab, tab, tab,
                  _const_spec((1, LANES)), _const_spec((1, LANES)), _const_spec((1, LANES)),
                  _const_spec((1, LANES)), _const_spec((1, MLA_Q_RANK)), _const_spec((1, MLA_KV_RANK)),
                  _const_spec(wuq.shape), _const_spec(wuqs.shape), _const_spec(wukv.shape)],
        out_specs=(tok(GQA_HEADS * GQA_KC), tok(GQA_KC), tr(GQA_KV_HEADS * GQA_HEAD_DIM),
                   tok(MLA_HEADS * MLA_KC), tok((MLA_HEADS // 2) * MLA_KC), tr(MLA_HEADS * MLA_V)),
        compiler_params=_cparams(("parallel", "parallel"), 48),
        name="proj_a",
    )(x, g_mix, w_a, cosa, sina, cosm, sinm, gq, gqs, gk, gks, gcq, gckv, wuq, wuqs, wukv)


def _proj_b_kernel(x_ref, g_ref, w_ref, hy_ref, gt_ref):
    ub = _rms(x_ref[...], g_ref[...]).astype(BF16)
    hy_ref[...] = _dot(ub, w_ref[:, :3 * HY_WIDTH]).astype(BF16)
    gt_ref[...] = jax.nn.sigmoid(_dot(ub, w_ref[:, 3 * HY_WIDTH:])).astype(BF16)


def _proj_b(x2d, g_mix, w_b):
    t = x2d.shape[0]
    tm = TOK_TILE

    def row(w):
        return pl.BlockSpec((tm, w), lambda i: (i, 0))

    return pl.pallas_call(
        _proj_b_kernel,
        out_shape=(jax.ShapeDtypeStruct((t, 3 * HY_WIDTH), BF16),
                   jax.ShapeDtypeStruct((t, N_BRANCH * D_MODEL), BF16)),
        grid=(t // tm,),
        in_specs=[row(D_MODEL), _const_spec((1, D_MODEL)), _const_spec(w_b.shape)],
        out_specs=(row(3 * HY_WIDTH), row(N_BRANCH * D_MODEL)),
        compiler_params=_cparams(("parallel",), 48),
        name="proj_b",
    )(x2d, g_mix, w_b)


def _attn_kernel(q_ref, k_ref, vt_ref, o_ref, m_scr, acc_scr, s0_scr, s1_scr, t0_scr, t1_scr,
                 *, heads, groups, kc, key_of_head, value_of_head, nk):
    tk = vt_ref.shape[3]
    tq = ATT_TQ
    ones = jnp.ones((ONES_ROWS, tk), BF16)
    dv = MLA_V
    bufs = ((s0_scr, t0_scr), (s1_scr, t1_scr))
    streams = [(sub, g) for sub in range(q_ref.shape[1] // tq) for g in range(groups)]

    def scores(stream, j, hh, buf):
        sub, g = stream
        h = g * heads + hh
        kb = key_of_head[h]
        s_scr, t_scr = buf
        rows = j * tk if isinstance(j, int) else pl.multiple_of(j * tk, tk)
        st = _dot_nt(k_ref[0, pl.ds(rows, tk), kb * kc:(kb + 1) * kc],
                     q_ref[0, sub * tq:(sub + 1) * tq, h * kc:(h + 1) * kc])
        s_scr[hh] = st
        t_scr[hh] = jnp.max(st, axis=0, keepdims=True)

    def consume(si, stream, j, hh, buf):
        sub, g = stream
        vb = value_of_head[g * heads + hh]
        s_scr, t_scr = buf
        m = m_scr[si % 2, hh]
        m_new = jnp.maximum(m, t_scr[hh])
        p = jnp.exp2(s_scr[hh] - m_new).astype(BF16)
        alpha = jnp.exp2(m - m_new)
        vext = jnp.concatenate([vt_ref[0, j, vb * dv:(vb + 1) * dv, :], ones], axis=0)
        acc_scr[si % 2, hh] = alpha * acc_scr[si % 2, hh] + _dot(vext, p)
        m_scr[si % 2, hh] = m_new

    m_scr[...] = jnp.full(m_scr.shape, NEG_BIG, F32)
    acc_scr[...] = jnp.zeros(acc_scr.shape, F32)
    for hh in range(heads):
        scores(streams[0], 0, hh, bufs[0])
    for si, stream in enumerate(streams):
        def step(j, cur, nxt, si=si, stream=stream):
            scores(stream, j + 1, 0, nxt)
            for hh in range(heads):
                if hh + 1 < heads:
                    scores(stream, j + 1, hh + 1, nxt)
                consume(si, stream, j, hh, cur)

        def body(jj, carry, step=step):
            for u in range(ATT_UNROLL):
                step(ATT_UNROLL * jj + u, bufs[u % 2], bufs[1 - u % 2])
            return carry

        lax.fori_loop(0, nk // ATT_UNROLL - 1, body, 0)
        for u in range(ATT_UNROLL - 1):
            step(nk - ATT_UNROLL + u, bufs[u % 2], bufs[1 - u % 2])
        for hh in range(heads):
            if si + 1 < len(streams):
                scores(streams[si + 1], 0, hh, bufs[0])
            consume(si, stream, nk - 1, hh, bufs[1])
        sub, g = stream
        for hh in range(heads):
            acc = acc_scr[si % 2, hh]
            h = g * heads + hh
            o_ref[0, h * dv:(h + 1) * dv, sub * tq:(sub + 1) * tq] = (acc[:dv] / acc[dv:dv + 1]).astype(BF16)
            if si + 2 < len(streams):
                m_scr[si % 2, hh] = jnp.full((1, tq), NEG_BIG, F32)
                acc_scr[si % 2, hh] = jnp.zeros((dv + ONES_ROWS, tq), F32)


def _attention(q, k, vt, *, heads, kc, key_of_head, value_of_head, name):
    b, l, qw = q.shape
    n_heads = qw // kc
    nk, tk = vt.shape[1], vt.shape[3]
    tqb = ATT_TQB
    tq = ATT_TQ
    dv = MLA_V
    return pl.pallas_call(
        functools.partial(_attn_kernel, heads=heads, groups=n_heads // heads, kc=kc, key_of_head=key_of_head,
                          value_of_head=value_of_head, nk=nk),
        out_shape=jax.ShapeDtypeStruct((b, n_heads * dv, l), BF16),
        grid=(b, l // tqb),
        in_specs=[pl.BlockSpec((1, tqb, qw), lambda bi, i: (bi, i, 0)),
                  pl.BlockSpec((1, l, k.shape[2]), lambda bi, i: (bi, 0, 0), pipeline_mode=pl.Buffered(1)),
                  pl.BlockSpec((1, nk, vt.shape[2], tk), lambda bi, i: (bi, 0, 0, 0),
                               pipeline_mode=pl.Buffered(1))],
        out_specs=pl.BlockSpec((1, n_heads * dv, tqb), lambda bi, i: (bi, 0, i)),
        scratch_shapes=[pltpu.VMEM((2, heads, 1, tq), F32), pltpu.VMEM((2, heads, dv + ONES_ROWS, tq), F32),
                        pltpu.VMEM((heads, tk, tq), F32), pltpu.VMEM((heads, tk, tq), F32),
                        pltpu.VMEM((heads, 1, tq), F32), pltpu.VMEM((heads, 1, tq), F32)],
        compiler_params=_cparams(("parallel", "parallel"), 48),
        name=name,
    )(q, k, vt)


def _hy_pre_kernel(x0_ref, x1_ref, v_ref, w_ref, b_ref, s_ref, x0o_ref):
    rows = x0_ref.shape[1]
    t = lax.broadcasted_iota(jnp.int32, (rows, LANES), 0)
    not_first = t > 0
    not_last = t < rows - 1

    def conv(ref, j):
        a = ref[0].astype(F32)
        prev = jnp.where(not_first, pltpu.roll(a, 1, 0), 0.0)
        nxt = jnp.where(not_last, pltpu.roll(a, rows - 1, 0), 0.0)
        return prev * w_ref[0, j] + a * w_ref[1, j] + nxt * w_ref[2, j] + b_ref[j]

    x0o_ref[0] = conv(x0_ref, 0).astype(BF16)
    s_ref[0] = (conv(x1_ref, 1) * conv(v_ref, 2)).astype(BF16)


def _hy_pre(hy, w_short, b_short):
    b, l, _ = hy.shape
    nc = HY_WIDTH // LANES
    w4 = w_short.reshape(3, 3, nc, 1, LANES)
    b4 = b_short.reshape(3, nc, 1, LANES)

    def seg(j):
        return pl.BlockSpec((1, l, LANES), lambda bi, c, j=j: (bi, 0, j * nc + c))

    out = pl.BlockSpec((1, l, LANES), lambda bi, c: (bi, 0, c))
    return pl.pallas_call(
        _hy_pre_kernel,
        out_shape=(jax.ShapeDtypeStruct((b, l, HY_WIDTH), BF16), jax.ShapeDtypeStruct((b, l, HY_WIDTH), BF16)),
        grid=(b, nc),
        in_specs=[seg(0), seg(1), seg(2),
                  pl.BlockSpec((3, 3, None, 1, LANES), lambda bi, c: (0, 0, c, 0, 0)),
                  pl.BlockSpec((3, None, 1, LANES), lambda bi, c: (0, c, 0, 0))],
        out_specs=(out, out),
        compiler_params=_cparams(("parallel", "parallel"), 48),
        name="hy_pre",
    )(hy, hy, hy, w4, b4)


def _left_mm_kernel(f_ref, x_ref, o_ref, *, hi):
    if hi:
        o_ref[0] = _dot3(f_ref[...], x_ref[0]).astype(o_ref.dtype)
    else:
        o_ref[0] = _dot(f_ref[...], x_ref[0]).astype(o_ref.dtype)


def _left_mm(f, x, out_dtype, hi=False):
    b, k, n = x.shape
    m = f.shape[0]
    cb = min(HY_COLS, n)
    return pl.pallas_call(
        functools.partial(_left_mm_kernel, hi=hi),
        out_shape=jax.ShapeDtypeStruct((b, m, n), out_dtype),
        grid=(b, n // cb),
        in_specs=[_const_spec((m, k)), pl.BlockSpec((1, k, cb), lambda bi, c: (bi, 0, c))],
        out_specs=pl.BlockSpec((1, m, cb), lambda bi, c: (bi, 0, c)),
        compiler_params=_cparams(("parallel", "parallel"), 48),
        name="hy_outer_dft",
    )(f, x)


def _hy_mid_kernel(f_ref, kf_ref, g_ref, a_ref, o_ref):
    n2 = DFT_N2
    kr, ki = kf_ref[0, :n2], kf_ref[0, n2:]
    for bi in range(a_ref.shape[0]):
        x = _dot(f_ref[0], a_ref[bi, 0])
        xr, xi = x[:n2], x[n2:]
        y = jnp.concatenate([xr * kr - xi * ki, xr * ki + xi * kr], axis=0).astype(BF16)
        o_ref[bi, 0] = _dot(g_ref[0], y).astype(BF16)


def _hy_mid(a4, f2, kf, g2):
    b, n1, r, c = a4.shape
    mat = pl.BlockSpec((1, r, r), lambda i: (i, 0, 0))
    dat = pl.BlockSpec((b, 1, r, c), lambda i: (0, i, 0, 0))
    return pl.pallas_call(
        _hy_mid_kernel,
        out_shape=jax.ShapeDtypeStruct(a4.shape, BF16),
        grid=(n1,),
        in_specs=[mat, pl.BlockSpec((1, r, c), lambda i: (i, 0, 0)), mat, dat],
        out_specs=dat,
        compiler_params=_cparams(("parallel",), 48),
        name="hy_mid",
    )(f2, kf, g2, a4)


def _hy_post_kernel(g_ref, bi_ref, s_ref, x0_ref, bias_ref, o_ref):
    y = _dot(g_ref[...], bi_ref[0]) + s_ref[0].astype(F32) * bias_ref[...]
    o_ref[0] = (x0_ref[0].astype(F32) * y).astype(BF16)


def _hy_post(g1, bi2, s2, x02, bias_t):
    b, k, n = bi2.shape
    m = g1.shape[0]
    cb = HY_COLS
    dat = pl.BlockSpec((1, m, cb), lambda bi, c: (bi, 0, c))
    return pl.pallas_call(
        _hy_post_kernel,
        out_shape=jax.ShapeDtypeStruct((b, m, n), BF16),
        grid=(b, n // cb),
        in_specs=[_const_spec((m, k)), pl.BlockSpec((1, k, cb), lambda bi, c: (bi, 0, c)), dat, dat,
                  _const_spec((1, cb))],
        out_specs=dat,
        compiler_params=_cparams(("parallel", "parallel"), 48),
        name="hy_post",
    )(g1, bi2, s2, x02, bias_t)


def _hy_filter_kernel(z_ref, win_ref, w1_ref, b1_ref, w2_ref, b2_ref, w3_ref, fr_ref, o_ref, *, half_tiles):
    fr = fr_ref[...]
    h = jnp.sin(fr * (_dot3(z_ref[...], w1_ref[...]) + b1_ref[...]))
    h = jnp.sin(fr * (_dot3(h, w2_ref[...]) + b2_ref[...]))
    h3 = _dot3(h, w3_ref[...])
    backward = pl.program_id(0) >= half_tiles
    o_ref[...] = jnp.where(backward, h3[:, HY_WIDTH:], h3[:, :HY_WIDTH]) * win_ref[...]


def _hy_filter(z2, win2, w1p, b1, w2, b2, w3, fr):
    n = z2.shape[0]
    tl = 512
    return pl.pallas_call(
        functools.partial(_hy_filter_kernel, half_tiles=n // (2 * tl)),
        out_shape=jax.ShapeDtypeStruct((n, HY_WIDTH), F32),
        grid=(n // tl,),
        in_specs=[pl.BlockSpec((tl, LANES), lambda i: (i, 0)), pl.BlockSpec((tl, HY_WIDTH), lambda i: (i, 0)),
                  _const_spec(w1p.shape), _const_spec(b1.shape), _const_spec(w2.shape), _const_spec(b2.shape),
                  _const_spec(w3.shape), _const_spec(fr.shape)],
        out_specs=pl.BlockSpec((tl, HY_WIDTH), lambda i: (i, 0)),
        compiler_params=_cparams(("parallel",), 32),
        name="hy_filter",
    )(z2, win2, w1p, b1, w2, b2, w3, fr)


def _hy_spec_kernel(f_ref, a_ref, o_ref):
    o_ref[0] = _dot3(f_ref[0], a_ref[0])


def _hy_spec(f2, a3):
    n1, r, c = a3.shape
    return pl.pallas_call(
        _hy_spec_kernel,
        out_shape=jax.ShapeDtypeStruct(a3.shape, F32),
        grid=(n1,),
        in_specs=[pl.BlockSpec((1, r, r), lambda i: (i, 0, 0)), pl.BlockSpec((1, r, c), lambda i: (i, 0, 0))],
        out_specs=pl.BlockSpec((1, r, c), lambda i: (i, 0, 0)),
        compiler_params=_cparams(("parallel",), 32),
        name="hy_filter_spectrum",
    )(f2, a3)


def _merge_kernel(x_ref, yat_ref, yb_ref, yct_ref, gt_ref, wb_ref, wo_ref, o_ref):
    d = D_MODEL
    ba = _dot_tn(yat_ref[0], wb_ref[0])
    bb = _dot(yb_ref[0], wb_ref[1])
    bc = _dot_tn(yct_ref[0], wb_ref[2])
    merged = (gt_ref[0, :, 0:d].astype(F32) * ba + gt_ref[0, :, d:2 * d].astype(F32) * bb
              + gt_ref[0, :, 2 * d:3 * d].astype(F32) * bc)
    o_ref[0] = x_ref[0] + _dot(merged.astype(BF16), wo_ref[...])


def _merge(x, yat, yb, yct, gates, wb, wo):
    b, l, _ = x.shape
    tm = TOK_TILE

    def tok(w):
        return pl.BlockSpec((1, tm, w), lambda bi, i: (bi, i, 0))

    tr = pl.BlockSpec((1, BRANCH_W, tm), lambda bi, i: (bi, 0, i))
    return pl.pallas_call(
        _merge_kernel,
        out_shape=jax.ShapeDtypeStruct(x.shape, F32),
        grid=(b, l // tm),
        in_specs=[tok(D_MODEL), tr, tok(BRANCH_W), tr, tok(N_BRANCH * D_MODEL),
                  _const_spec(wb.shape), _const_spec(wo.shape)],
        out_specs=tok(D_MODEL),
        compiler_params=_cparams(("parallel", "parallel"), 48),
        name="merge",
    )(x, yat, yb, yct, gates, wb, wo)


def _rope_tables(l):
    rows = l // GRID_W
    row = jnp.repeat(jnp.arange(rows, dtype=F32), GRID_W)
    col = jnp.tile(jnp.arange(GRID_W, dtype=F32), rows)

    def tab(d_rot):
        n_freq = d_rot // 4
        inv = ROPE_THETA ** (-jnp.arange(n_freq, dtype=F32) / n_freq)
        ang = jnp.concatenate([row[:, None] * inv, col[:, None] * inv], axis=-1)
        c = jnp.repeat(jnp.cos(ang), 2, axis=-1)
        s = jnp.repeat(jnp.sin(ang), 2, axis=-1) * jnp.tile(jnp.array([-1.0, 1.0], F32), d_rot // 2)
        return c, s

    ca, sa = tab(GQA_HEAD_DIM)
    cm, sm = tab(MLA_ROPE)
    pad = ((0, 0), (0, LANES - MLA_ROPE))
    return (jnp.tile(ca, (1, 2)), jnp.tile(sa, (1, 2)), jnp.pad(cm, pad), jnp.pad(sm, pad))


def _hy_positions(l):
    t = jnp.linspace(0.0, 1.0, l, dtype=F32)[:, None]
    w = 2.0 * math.pi * jnp.arange(l, dtype=F32)[:, None] / l
    f = jnp.linspace(1e-4, HY_BANDS - 1, HY_BANDS, dtype=F32)[None, :]
    z = jnp.concatenate([t, jnp.cos(f * w), -jnp.sin(f * w)], axis=-1)
    max_decay = math.log(HY_TARGET) / HY_FAST
    min_decay = math.log(HY_TARGET) / HY_SLOW
    deltas = jnp.abs(jnp.linspace(min_decay, max_decay, HY_WIDTH, dtype=F32))
    window = jnp.exp(-t * deltas[None, :])
    zu = jnp.roll(z[::-1], 1, axis=0)
    wu = jnp.roll(window[::-1], 1, axis=0).at[0].set(0.0)
    z2 = jnp.pad(jnp.concatenate([z, zu], axis=0), ((0, 0), (0, LANES - HY_EMB)))
    return z2, jnp.concatenate([window, wu], axis=0)


def _hy_k1_count(l):
    n1 = 2 * l // DFT_N2
    return -(-(n1 // 2 + 1) // 8) * 8


def _dft_tables(l):
    n = 2 * l
    n2 = DFT_N2
    n1 = n // n2
    n1c = _hy_k1_count(l)
    two_pi = 2.0 * math.pi
    i1 = jnp.arange(n1, dtype=jnp.int32)
    ang1 = ((i1[:n1c, None] * i1[None, :]) % n1).astype(F32) * (two_pi / n1)
    live = (i1[:n1c] <= n1 // 2)[:, None]
    c1, s1 = jnp.where(live, jnp.cos(ang1), 0.0), jnp.where(live, jnp.sin(ang1), 0.0)
    f1_full = jnp.stack([c1, -s1], axis=1).reshape(2 * n1c, n1)
    fold = jnp.where((i1[:n1c] == 0) | (i1[:n1c] == n1 // 2), 1.0, 2.0)[:, None] * (1.0 / n)
    g1 = jnp.stack([(c1 * fold)[:, :n1 // 2].T, (-s1 * fold)[:, :n1 // 2].T], axis=2).reshape(n1 // 2, 2 * n1c)
    i2 = jnp.arange(n2, dtype=jnp.int32)
    kk = i1[:n1c, None, None] + n1 * i2[None, :, None]
    ang2 = ((kk * i2[None, None, :]) % n).astype(F32) * (two_pi / n)
    c2, s2 = jnp.cos(ang2), jnp.sin(ang2)
    f2 = jnp.concatenate([jnp.concatenate([c2, s2], axis=2), jnp.concatenate([-s2, c2], axis=2)], axis=1)
    c2t, s2t = jnp.swapaxes(c2, 1, 2), jnp.swapaxes(s2, 1, 2)
    g2 = jnp.concatenate([jnp.concatenate([c2t, -s2t], axis=2), jnp.concatenate([s2t, c2t], axis=2)], axis=1)
    return f1_full, g1, f2, g2


def _swap_pairs(n):
    return np.arange(n) ^ 1


def _proj_a_columns():
    zero = IN_WIDTH
    cols = np.full((PA_END,), zero, np.int64)
    q0, k0, v0, _, cq0, ckv0, kr0, _ = IN_OFFS[:8]
    for h in range(GQA_HEADS):
        dst = h * LANES + (h // (GQA_HEADS // GQA_KV_HEADS)) * GQA_HEAD_DIM
        src = q0 + h * GQA_HEAD_DIM + np.arange(GQA_HEAD_DIM)
        cols[PA_Q + dst:PA_Q + dst + GQA_HEAD_DIM] = src
        cols[PA_QS + dst:PA_QS + dst + GQA_HEAD_DIM] = q0 + h * GQA_HEAD_DIM + _swap_pairs(GQA_HEAD_DIM)
    cols[PA_K:PA_K + 128] = k0 + np.arange(128)
    cols[PA_KS:PA_KS + 128] = k0 + _swap_pairs(128)
    cols[PA_V:PA_V + 128] = v0 + np.arange(128)
    cols[PA_CQ:PA_CQ + MLA_Q_RANK] = cq0 + np.arange(MLA_Q_RANK)
    cols[PA_CKV:PA_CKV + MLA_KV_RANK] = ckv0 + np.arange(MLA_KV_RANK)
    cols[PA_KR:PA_KR + MLA_ROPE] = kr0 + np.arange(MLA_ROPE)
    cols[PA_KRS:PA_KRS + MLA_ROPE] = kr0 + _swap_pairs(MLA_ROPE)
    return cols


def _mla_q_columns():
    hd = MLA_NOPE + MLA_ROPE
    zero = MLA_HEADS * hd
    main = np.full((MLA_HEADS * MLA_KC,), zero, np.int64)
    swap = np.full((MLA_HEADS * LANES,), zero, np.int64)
    for h in range(MLA_HEADS):
        b0 = h * MLA_KC + (h % 2) * MLA_NOPE
        main[b0:b0 + MLA_NOPE] = h * hd + np.arange(MLA_NOPE)
        r0 = h * MLA_KC + LANES
        main[r0:r0 + MLA_ROPE] = h * hd + MLA_NOPE + np.arange(MLA_ROPE)
        swap[h * LANES:h * LANES + MLA_ROPE] = h * hd + MLA_NOPE + _swap_pairs(MLA_ROPE)
    return main, swap


def _mla_kv_columns():
    hd = MLA_NOPE + MLA_V
    knope = np.concatenate([h * hd + np.arange(MLA_NOPE) for h in range(MLA_HEADS)])
    val = np.concatenate([h * hd + MLA_NOPE + np.arange(MLA_V) for h in range(MLA_HEADS)])
    return np.concatenate([knope, val])


def _take_cols(w, cols):
    wz = jnp.concatenate([w, jnp.zeros((w.shape[0], 1), w.dtype)], axis=1)
    return jnp.take(wz, jnp.asarray(cols, jnp.int32), axis=1).astype(BF16)


def _tile2(g):
    return jnp.tile(g, 2)[None, :]


def _encoder(x, lw, g_final):
    b, l, d = x.shape
    n2 = DFT_N2
    n1 = 2 * l // n2
    c = HY_WIDTH
    tabs = _rope_tables(l)
    z2, win2 = _hy_positions(l)
    f1_full, g1, f2, g2 = _dft_tables(l)
    f1_b = f1_full[:, :n1 // 2].astype(BF16)
    g1_b = g1.astype(BF16)
    f2_b = f2.astype(BF16)
    g2_b = g2.astype(BF16)
    for li, w in enumerate(lw):
        x2 = _ffn(x.reshape(b * l, d), w["g_ffn1"], w["wg1"], w["wu1"], w["wd1"], g_final, False)
        x = x2.reshape(b, l, d)
        qg, kg, vgt, qm, km, vmt = _proj_a(x, w["g_mix"], w["w_a"], tabs, w["gq"], w["gqs"], w["gk"], w["gks"],
                                           w["gcq"], w["gckv"], w["wuq"], w["wuqs"], w["wukv"])
        hy, gates = _proj_b(x2, w["g_mix"], w["w_b"])
        per_kv = GQA_HEADS // GQA_KV_HEADS
        yat = _attention(qg, kg, vgt, heads=ATT_HEADS, kc=GQA_KC, key_of_head=(0,) * GQA_HEADS,
                         value_of_head=tuple(h // per_kv for h in range(GQA_HEADS)), name="attn_gqa")
        yct = _attention(qm, km, vmt, heads=ATT_HEADS, kc=MLA_KC,
                         key_of_head=tuple(h // 2 for h in range(MLA_HEADS)),
                         value_of_head=tuple(range(MLA_HEADS)), name="attn_mla")
        kc_time = _hy_filter(z2, win2, w["w1p"], w["b1"], w["w2"], w["b2"], w["w3"], w["fr"])
        n1c = _hy_k1_count(l)
        ka =_left_mm(f1_full, kc_time.reshape(1, n1, n2 * c), F32, hi=True)
        kf = _hy_spec(f2, ka.reshape(n1c, 2 * n2, c))
        s, x0 = _hy_pre(hy.reshape(b, l, 3 * c), w["w_short"], w["b_short"])
        a = _left_mm(f1_b, s.reshape(b, n1 // 2, n2 * c), BF16)
        bi = _hy_mid(a.reshape(b, n1c, 2 * n2, c), f2_b, kf, g2_b)
        yb = _hy_post(g1_b, bi.reshape(b, 2 * n1c, n2 * c), s.reshape(b, n1 // 2, n2 * c),
                      x0.reshape(b, n1 // 2, n2 * c), w["bias_t"])
        x = _merge(x, yat, yb.reshape(b, l, c), yct, gates.reshape(b, l, N_BRANCH * d), w["wb"], w["wo"])
        x2 = _ffn(x.reshape(b * l, d), w["g_ffn2"], w["wg2"], w["wu2"], w["wd2"], g_final, li == len(lw) - 1)
        x = x2.reshape(b, l, d)
    return x


def kernel(x_prompt, x_sample, g_ffn1, w_ffn1_gate, w_ffn1_up, w_ffn1_down, g_mix, w_in, g_qnorm, g_knorm,
           w_hy_short, b_hy_short, w_hy_f1, b_hy_f1, w_hy_f2, b_hy_f2, w_hy_f3, hy_sin_freq, hy_bias,
           g_mla_q, w_mla_uq, g_mla_kv, w_mla_ukv, w_branch, w_out, g_ffn2, w_ffn2_gate, w_ffn2_up,
           w_ffn2_down, g_final):
    cols_a = _proj_a_columns()
    uq_main, uq_swap = _mla_q_columns()
    ukv_cols = _mla_kv_columns()
    sw64 = _swap_pairs(GQA_HEAD_DIM)
    hy0 = IN_OFFS[3]
    gt0 = IN_OFFS[7]
    lw = []
    for l in range(DEPTH):
        lw.append(dict(
            g_ffn1=g_ffn1[l][None], wg1=w_ffn1_gate[l].astype(BF16), wu1=w_ffn1_up[l].astype(BF16),
            wd1=w_ffn1_down[l].astype(BF16),
            g_ffn2=g_ffn2[l][None], wg2=w_ffn2_gate[l].astype(BF16), wu2=w_ffn2_up[l].astype(BF16),
            wd2=w_ffn2_down[l].astype(BF16),
            g_mix=g_mix[l][None],
            w_a=_take_cols(w_in[l], cols_a),
            w_b=jnp.concatenate([w_in[l][:, hy0:hy0 + 3 * HY_WIDTH], w_in[l][:, gt0:]], axis=1).astype(BF16),
            gq=_tile2(g_qnorm[l]), gqs=_tile2(g_qnorm[l][sw64]),
            gk=_tile2(g_knorm[l]), gks=_tile2(g_knorm[l][sw64]),
            gcq=g_mla_q[l][None], gckv=g_mla_kv[l][None],
            wuq=_take_cols(w_mla_uq[l], uq_main), wuqs=_take_cols(w_mla_uq[l], uq_swap),
            wukv=_take_cols(w_mla_ukv[l], ukv_cols),
            w_short=w_hy_short[l], b_short=b_hy_short[l],
            w1p=jnp.pad(w_hy_f1[l], ((0, LANES - HY_EMB), (0, 0))), b1=b_hy_f1[l][None],
            w2=w_hy_f2[l], b2=b_hy_f2[l][None], w3=w_hy_f3[l], fr=hy_sin_freq[l][None],
            bias_t=jnp.tile(hy_bias[l], HY_COLS // HY_WIDTH)[None],
            wb=w_branch[l].astype(BF16), wo=w_out[l].astype(BF16),
        ))
    gf = g_final[None]
    return _encoder(x_prompt, lw, gf), _encoder(x_sample, lw, gf)
```

```python
import functools
import math

import numpy as np
import jax
import jax.numpy as jnp
from jax import lax
from jax.experimental import pallas as pl
from jax.experimental.pallas import tpu as pltpu

F32 = jnp.float32
BF16 = jnp.bfloat16

D_MODEL = 1024
DEPTH = 2
GRID_W = 64
ROPE_THETA = 10000.0
EPS = 1e-6
D_FF = 2816
N_BRANCH = 3
BRANCH_W = 512
GQA_HEADS = 8
GQA_KV_HEADS = 2
GQA_HEAD_DIM = 64
HY_WIDTH = 512
HY_ORDER = 64
HY_EMB = 33
HY_BANDS = (HY_EMB - 1) // 2
HY_TARGET = 1e-2
HY_FAST = 0.3
HY_SLOW = 1.5
MLA_HEADS = 8
MLA_Q_RANK = 256
MLA_KV_RANK = 128
MLA_NOPE = 64
MLA_ROPE = 32
MLA_V = 64
IN_WIDTHS = (512, 128, 128, 3 * HY_WIDTH, MLA_Q_RANK, MLA_KV_RANK, MLA_ROPE, N_BRANCH * D_MODEL)
IN_OFFS = tuple(int(c) for c in np.cumsum((0,) + IN_WIDTHS))
IN_WIDTH = IN_OFFS[-1]

LANES = 128
V7X_VMEM_BYTES = 64 * 1024 * 1024

TOK_TILE = 512
ATT_TQ = 256
ATT_TQB = 512
ATT_HEADS = 4
ATT_UNROLL = 4
ATT_SAFE_SCORE = 64.0
ATT_TK = TOK_TILE
FF_CHUNKS = ((0, 1024), (1024, 2048), (2048, D_FF))
DFT_N2 = 128
HY_COLS = 8192
ONES_ROWS = 16
NEG_BIG = -1e30
LOG2E = math.log2(math.e)


def _cparams(sem, vmem_mb):
    return pltpu.CompilerParams(dimension_semantics=sem, vmem_limit_bytes=vmem_mb * 1024 * 1024)


def _const_spec(shape):
    nd = len(shape)
    return pl.BlockSpec(shape, lambda *_: (0,) * nd, pipeline_mode=pl.Buffered(1))


def _rms(x, g):
    return x * lax.rsqrt(jnp.mean(x * x, axis=-1, keepdims=True) + EPS) * g


def _dot(a, b):
    return jnp.dot(a, b, preferred_element_type=F32)


def _dot_nt(a, b):
    return lax.dot_general(a, b, (((1,), (1,)), ((), ())), preferred_element_type=F32)


def _dot_tn(a, b):
    return lax.dot_general(a, b, (((0,), (0,)), ((), ())), preferred_element_type=F32)


def _split(a):
    hi = a.astype(BF16)
    return hi, (a - hi.astype(F32)).astype(BF16)


def _dot3(a, b):
    ah, al = _split(a)
    bh, bl = _split(b)
    return _dot(ah, bh) + (_dot(ah, bl) + _dot(al, bh))


def _ffn_kernel(x_ref, g_ref, wg_ref, wu_ref, wd_ref, gf_ref, o_ref, *, final):
    x = x_ref[...]
    xb = _rms(x, g_ref[...]).astype(BF16)
    acc = jnp.zeros_like(x)
    for c0, c1 in FF_CHUNKS:
        gate = _dot(xb, wg_ref[:, c0:c1])
        up = _dot(xb, wu_ref[:, c0:c1])
        h = (gate * jax.nn.sigmoid(gate) * up).astype(BF16)
        acc = acc + _dot(h, wd_ref[c0:c1, :])
    y = x + 0.5 * acc
    if final:
        y = _rms(y, gf_ref[...])
    o_ref[...] = y


def _ffn(x2d, g, wg, wu, wd, g_final, final):
    t = x2d.shape[0]
    row = pl.BlockSpec((TOK_TILE, D_MODEL), lambda i: (i, 0))
    return pl.pallas_call(
        functools.partial(_ffn_kernel, final=final),
        out_shape=jax.ShapeDtypeStruct((t, D_MODEL), F32),
        grid=(t // TOK_TILE,),
        in_specs=[row, _const_spec((1, D_MODEL)), _const_spec((D_MODEL, D_FF)),
                  _const_spec((D_MODEL, D_FF)), _const_spec((D_FF, D_MODEL)), _const_spec((1, D_MODEL))],
        out_specs=row,
        compiler_params=_cparams(("parallel",), 48),
        name="ffn",
    )(x2d, g, wg, wu, wd, g_final)


PA_Q, PA_QS, PA_K, PA_KS, PA_V, PA_CQ, PA_CKV, PA_KR, PA_KRS, PA_END = (
    0, 1024, 2048, 2176, 2304, 2432, 2688, 2816, 2944, 3072)
GQA_KC = 128
MLA_KC = 256


def _sq_norm(a_bf16):
    a = a_bf16.astype(F32)
    return jnp.sum(a * a, axis=-1, keepdims=True)


def _proj_a_kernel(x_ref, g_ref, w_ref, cosa_ref, sina_ref, cosm_ref, sinm_ref,
                   gq_ref, gqs_ref, gk_ref, gks_ref, gcq_ref, gckv_ref,
                   wuq_ref, wuqs_ref, wukv_ref,
                   qg_ref, kg_ref, vgt_ref, qm_ref, km_ref, vmt_ref, bnd_ref):
    ub = _rms(x_ref[0], g_ref[...]).astype(BF16)
    cosa, sina = cosa_ref[...], sina_ref[...]
    cosm, sinm = cosm_ref[...], sinm_ref[...]

    zq = _dot(ub, w_ref[:, PA_Q:PA_QS])
    zqs = _dot(ub, w_ref[:, PA_QS:PA_K])
    tq_c = gq_ref[...] * cosa
    tq_s = gqs_ref[...] * sina
    scale_a = GQA_HEAD_DIM ** -0.5 * LOG2E
    qa_n2 = None
    for h in range(GQA_HEADS):
        a = zq[:, h * LANES:(h + 1) * LANES]
        a_sw = zqs[:, h * LANES:(h + 1) * LANES]
        r = lax.rsqrt(jnp.sum(a * a, axis=-1, keepdims=True) * (1.0 / GQA_HEAD_DIM) + EPS)
        qb = ((a * tq_c + a_sw * tq_s) * (r * scale_a)).astype(BF16)
        qg_ref[0, :, h * LANES:(h + 1) * LANES] = qb
        qa_n2 = _sq_norm(qb) if qa_n2 is None else jnp.maximum(qa_n2, _sq_norm(qb))

    zk = _dot(ub, w_ref[:, PA_K:PA_KS])
    zks = _dot(ub, w_ref[:, PA_KS:PA_V])
    first = lax.broadcasted_iota(jnp.int32, zk.shape, 1) < GQA_HEAD_DIM
    sq = zk * zk
    r0 = lax.rsqrt(jnp.sum(jnp.where(first, sq, 0.0), axis=-1, keepdims=True) * (1.0 / GQA_HEAD_DIM) + EPS)
    r1 = lax.rsqrt(jnp.sum(jnp.where(first, 0.0, sq), axis=-1, keepdims=True) * (1.0 / GQA_HEAD_DIM) + EPS)
    kb = ((zk * (gk_ref[...] * cosa) + zks * (gks_ref[...] * sina)) * jnp.where(first, r0, r1)).astype(BF16)
    kg_ref[0] = kb
    kb32 = kb.astype(F32)
    kb_sq = kb32 * kb32
    ka_n2 = jnp.maximum(jnp.sum(jnp.where(first, kb_sq, 0.0), axis=-1, keepdims=True),
                        jnp.sum(jnp.where(first, 0.0, kb_sq), axis=-1, keepdims=True))

    vgt_ref[0, 0] = _dot(ub, w_ref[:, PA_V:PA_CQ]).T.astype(BF16)

    scale_m = (MLA_NOPE + MLA_ROPE) ** -0.5 * LOG2E
    cqn = _rms(_dot(ub, w_ref[:, PA_CQ:PA_CKV]), gcq_ref[...]).astype(BF16)
    zq2 = _dot(cqn, wuq_ref[...])
    zq2s = _dot(cqn, wuqs_ref[...])
    qm_n2 = None
    for h in range(MLA_HEADS):
        b0 = h * MLA_KC
        nope = (zq2[:, b0:b0 + LANES] * scale_m).astype(BF16)
        rope = zq2[:, b0 + LANES:b0 + 2 * LANES] * cosm + zq2s[:, h * LANES:(h + 1) * LANES] * sinm
        rope = (rope * scale_m).astype(BF16)
        qm_ref[0, :, b0:b0 + LANES] = nope
        qm_ref[0, :, b0 + LANES:b0 + 2 * LANES] = rope
        n2 = _sq_norm(nope) + _sq_norm(rope)
        qm_n2 = n2 if qm_n2 is None else jnp.maximum(qm_n2, n2)

    ckvn = _rms(_dot(ub, w_ref[:, PA_CKV:PA_KR]), gckv_ref[...]).astype(BF16)
    zkv = _dot(ckvn, wukv_ref[...])
    krope = (_dot(ub, w_ref[:, PA_KR:PA_KRS]) * cosm + _dot(ub, w_ref[:, PA_KRS:PA_END]) * sinm).astype(BF16)
    km_n2 = None
    for p in range(MLA_HEADS // 2):
        b0 = p * MLA_KC
        nope = zkv[:, p * LANES:(p + 1) * LANES].astype(BF16)
        km_ref[0, :, b0:b0 + LANES] = nope
        km_ref[0, :, b0 + LANES:b0 + 2 * LANES] = krope
        km_n2 = _sq_norm(nope) if km_n2 is None else jnp.maximum(km_n2, _sq_norm(nope))
    km_n2 = km_n2 + _sq_norm(krope)
    vmt_ref[0, 0] = zkv[:, MLA_HEADS * MLA_NOPE:].T.astype(BF16)

    row = lax.broadcasted_iota(jnp.int32, (8, LANES), 0)
    tops = [jnp.max(v, axis=0, keepdims=True) for v in (qa_n2, ka_n2, qm_n2, km_n2)]
    bnd_ref[0, 0] = jnp.where(row == 0, tops[0], jnp.where(row == 1, tops[1], jnp.where(row == 2, tops[2], tops[3])))


def _proj_a(x, g_mix, w_a, tabs, gq, gqs, gk, gks, gcq, gckv, wuq, wuqs, wukv):
    b, l, _ = x.shape
    tm = TOK_TILE
    nt = l // tm
    cosa, sina, cosm, sinm = tabs
    tab = pl.BlockSpec((tm, LANES), lambda bi, i: (i, 0))

    def tok(w):
        return pl.BlockSpec((1, tm, w), lambda bi, i: (bi, i, 0))

    def tr(rows):
        return pl.BlockSpec((1, 1, rows, tm), lambda bi, i: (bi, i, 0, 0))

    out_shape = (
        jax.ShapeDtypeStruct((b, l, GQA_HEADS * GQA_KC), BF16),
        jax.ShapeDtypeStruct((b, l, GQA_KC), BF16),
        jax.ShapeDtypeStruct((b, nt, GQA_KV_HEADS * GQA_HEAD_DIM, tm), BF16),
        jax.ShapeDtypeStruct((b, l, MLA_HEADS * MLA_KC), BF16),
        jax.ShapeDtypeStruct((b, l, (MLA_HEADS // 2) * MLA_KC), BF16),
        jax.ShapeDtypeStruct((b, nt, MLA_HEADS * MLA_V, tm), BF16),
        jax.ShapeDtypeStruct((b, nt, 8, LANES), F32),
    )
    return pl.pallas_call(
        _proj_a_kernel,
        out_shape=out_shape,
        grid=(b, nt),
        in_specs=[tok(D_MODEL), _const_spec((1, D_MODEL)), _const_spec(w_a.shape), tab, tab, tab, tab,
                  _const_spec((1, LANES)), _const_spec((1, LANES)), _const_spec((1, LANES)),
                  _const_spec((1, LANES)), _const_spec((1, MLA_Q_RANK)), _const_spec((1, MLA_KV_RANK)),
                  _const_spec(wuq.shape), _const_spec(wuqs.shape), _const_spec(wukv.shape)],
        out_specs=(tok(GQA_HEADS * GQA_KC), tok(GQA_KC), tr(GQA_KV_HEADS * GQA_HEAD_DIM),
                   tok(MLA_HEADS * MLA_KC), tok((MLA_HEADS // 2) * MLA_KC), tr(MLA_HEADS * MLA_V),
                   pl.BlockSpec((1, 1, 8, LANES), lambda bi, i: (bi, i, 0, 0))),
        compiler_params=_cparams(("parallel", "parallel"), 48),
        name="proj_a",
    )(x, g_mix, w_a, cosa, sina, cosm, sinm, gq, gqs, gk, gks, gcq, gckv, wuq, wuqs, wukv)


def _proj_b_kernel(x_ref, g_ref, w_ref, hy_ref, gt_ref):
    ub = _rms(x_ref[...], g_ref[...]).astype(BF16)
    hy_ref[...] = _dot(ub, w_ref[:, :3 * HY_WIDTH]).astype(BF16)
    gt_ref[...] = jax.nn.sigmoid(_dot(ub, w_ref[:, 3 * HY_WIDTH:])).astype(BF16)


def _proj_b(x2d, g_mix, w_b):
    t = x2d.shape[0]
    tm = TOK_TILE

    def row(w):
        return pl.BlockSpec((tm, w), lambda i: (i, 0))

    return pl.pallas_call(
        _proj_b_kernel,
        out_shape=(jax.ShapeDtypeStruct((t, 3 * HY_WIDTH), BF16),
                   jax.ShapeDtypeStruct((t, N_BRANCH * D_MODEL), BF16)),
        grid=(t // tm,),
        in_specs=[row(D_MODEL), _const_spec((1, D_MODEL)), _const_spec(w_b.shape)],
        out_specs=(row(3 * HY_WIDTH), row(N_BRANCH * D_MODEL)),
        compiler_params=_cparams(("parallel",), 48),
        name="proj_b",
    )(x2d, g_mix, w_b)


def _attn_kernel(safe_ref, q_ref, k_ref, vt_ref, o_ref, acc_scr, p0_scr, p1_scr,
                 *, heads, groups, kc, key_of_head, value_of_head, nk):
    tk = vt_ref.shape[3]
    tq = ATT_TQ
    ones = jnp.ones((ONES_ROWS, tk), BF16)
    dv = MLA_V
    n_sub = q_ref.shape[1] // tq
    safe = safe_ref[pl.program_id(0), pl.program_id(1)] != 0

    def key_tile(j, h):
        kb = key_of_head[h]
        rows = j * tk if isinstance(j, int) else pl.multiple_of(j * tk, tk)
        return k_ref[0, pl.ds(rows, tk), kb * kc:(kb + 1) * kc]

    def value_tile(j, h):
        vb = value_of_head[h]
        return jnp.concatenate([vt_ref[0, j, vb * dv:(vb + 1) * dv, :], ones], axis=0)

    def write_out(acc, h, sub):
        o_ref[0, h * dv:(h + 1) * dv, sub * tq:(sub + 1) * tq] = (acc[:dv] / acc[dv:dv + 1]).astype(BF16)

    @pl.when(safe)
    def _():
        bufs = (p0_scr, p1_scr)
        streams = [(sub, g) for sub in range(n_sub) for g in range(groups)]

        def probs(stream, j, hh, buf):
            sub, g = stream
            h = g * heads + hh
            st = _dot_nt(key_tile(j, h), q_ref[0, sub * tq:(sub + 1) * tq, h * kc:(h + 1) * kc])
            buf[hh] = jnp.exp2(st).astype(BF16)

        def accumulate(si, stream, j, hh, buf):
            acc_scr[si % 2, hh] += _dot(value_tile(j, stream[1] * heads + hh), buf[hh])

        acc_scr[...] = jnp.zeros(acc_scr.shape, F32)
        for hh in range(heads):
            probs(streams[0], 0, hh, bufs[0])
        for si, stream in enumerate(streams):
            def step(j, cur, nxt, si=si, stream=stream):
                for hh in range(heads):
                    probs(stream, j + 1, hh, nxt)
                    accumulate(si, stream, j, hh, cur)

            def body(jj, carry, step=step):
                for u in range(ATT_UNROLL):
                    step(ATT_UNROLL * jj + u, bufs[u % 2], bufs[1 - u % 2])
                return carry

            lax.fori_loop(0, nk // ATT_UNROLL - 1, body, 0)
            for u in range(ATT_UNROLL - 1):
                step(nk - ATT_UNROLL + u, bufs[u % 2], bufs[1 - u % 2])
            for hh in range(heads):
                if si + 1 < len(streams):
                    probs(streams[si + 1], 0, hh, bufs[0])
                accumulate(si, stream, nk - 1, hh, bufs[1])
            sub, g = stream
            for hh in range(heads):
                write_out(acc_scr[si % 2, hh], g * heads + hh, sub)
                if si + 2 < len(streams):
                    acc_scr[si % 2, hh] = jnp.zeros((dv + ONES_ROWS, tq), F32)

    @pl.when(jnp.logical_not(safe))
    def _():
        for h in range(groups * heads):
            for sub in range(n_sub):
                qh = q_ref[0, sub * tq:(sub + 1) * tq, h * kc:(h + 1) * kc]

                def body(j, carry, qh=qh, h=h):
                    m, acc = carry
                    st = _dot_nt(key_tile(j, h), qh)
                    m_new = jnp.maximum(m, jnp.max(st, axis=0, keepdims=True))
                    p = jnp.exp2(st - m_new).astype(BF16)
                    return m_new, jnp.exp2(m - m_new) * acc + _dot(value_tile(j, h), p)

                init = (jnp.full((1, tq), NEG_BIG, F32), jnp.zeros((dv + ONES_ROWS, tq), F32))
                _, acc = lax.fori_loop(0, nk, body, init)
                write_out(acc, h, sub)


def _attention(safe, q, k, vt, *, heads, kc, key_of_head, value_of_head, name):
    b, l, qw = q.shape
    n_heads = qw // kc
    nk, tk = vt.shape[1], vt.shape[3]
    tqb = ATT_TQB
    tq = ATT_TQ
    dv = MLA_V
    grid_spec = pltpu.PrefetchScalarGridSpec(
        num_scalar_prefetch=1,
        grid=(b, l // tqb),
        in_specs=[pl.BlockSpec((1, tqb, qw), lambda bi, i, s: (bi, i, 0)),
                  pl.BlockSpec((1, l, k.shape[2]), lambda bi, i, s: (bi, 0, 0), pipeline_mode=pl.Buffered(1)),
                  pl.BlockSpec((1, nk, vt.shape[2], tk), lambda bi, i, s: (bi, 0, 0, 0),
                               pipeline_mode=pl.Buffered(1))],
        out_specs=pl.BlockSpec((1, n_heads * dv, tqb), lambda bi, i, s: (bi, 0, i)),
        scratch_shapes=[pltpu.VMEM((2, heads, dv + ONES_ROWS, tq), F32),
                        pltpu.VMEM((heads, tk, tq), BF16), pltpu.VMEM((heads, tk, tq), BF16)],
    )
    return pl.pallas_call(
        functools.partial(_attn_kernel, heads=heads, groups=n_heads // heads, kc=kc, key_of_head=key_of_head,
                          value_of_head=value_of_head, nk=nk),
        out_shape=jax.ShapeDtypeStruct((b, n_heads * dv, l), BF16),
        grid_spec=grid_spec,
        compiler_params=_cparams(("parallel", "parallel"), 48),
        name=name,
    )(safe, q, k, vt)


def _hy_pre_kernel(x0_ref, x1_ref, v_ref, w_ref, b_ref, s_ref, x0o_ref):
    rows = x0_ref.shape[1]
    t = lax.broadcasted_iota(jnp.int32, (rows, LANES), 0)
    not_first = t > 0
    not_last = t < rows - 1

    def conv(ref, j):
        a = ref[0].astype(F32)
        prev = jnp.where(not_first, pltpu.roll(a, 1, 0), 0.0)
        nxt = jnp.where(not_last, pltpu.roll(a, rows - 1, 0), 0.0)
        return prev * w_ref[0, j] + a * w_ref[1, j] + nxt * w_ref[2, j] + b_ref[j]

    x0o_ref[0] = conv(x0_ref, 0).astype(BF16)
    s_ref[0] = (conv(x1_ref, 1) * conv(v_ref, 2)).astype(BF16)


def _hy_pre(hy, w_short, b_short):
    b, l, _ = hy.shape
    nc = HY_WIDTH // LANES
    w4 = w_short.reshape(3, 3, nc, 1, LANES)
    b4 = b_short.reshape(3, nc, 1, LANES)

    def seg(j):
        return pl.BlockSpec((1, l, LANES), lambda bi, c, j=j: (bi, 0, j * nc + c))

    out = pl.BlockSpec((1, l, LANES), lambda bi, c: (bi, 0, c))
    return pl.pallas_call(
        _hy_pre_kernel,
        out_shape=(jax.ShapeDtypeStruct((b, l, HY_WIDTH), BF16), jax.ShapeDtypeStruct((b, l, HY_WIDTH), BF16)),
        grid=(b, nc),
        in_specs=[seg(0), seg(1), seg(2),
                  pl.BlockSpec((3, 3, None, 1, LANES), lambda bi, c: (0, 0, c, 0, 0)),
                  pl.BlockSpec((3, None, 1, LANES), lambda bi, c: (0, c, 0, 0))],
        out_specs=(out, out),
        compiler_params=_cparams(("parallel", "parallel"), 48),
        name="hy_pre",
    )(hy, hy, hy, w4, b4)


def _left_mm_kernel(f_ref, x_ref, o_ref, *, hi):
    if hi:
        o_ref[0] = _dot3(f_ref[...], x_ref[0]).astype(o_ref.dtype)
    else:
        o_ref[0] = _dot(f_ref[...], x_ref[0]).astype(o_ref.dtype)


def _left_mm(f, x, out_dtype, hi=False):
    b, k, n = x.shape
    m = f.shape[0]
    cb = min(HY_COLS, n)
    return pl.pallas_call(
        functools.partial(_left_mm_kernel, hi=hi),
        out_shape=jax.ShapeDtypeStruct((b, m, n), out_dtype),
        grid=(b, n // cb),
        in_specs=[_const_spec((m, k)), pl.BlockSpec((1, k, cb), lambda bi, c: (bi, 0, c))],
        out_specs=pl.BlockSpec((1, m, cb), lambda bi, c: (bi, 0, c)),
        compiler_params=_cparams(("parallel", "parallel"), 48),
        name="hy_outer_dft",
    )(f, x)


def _hy_mid_kernel(f_ref, kf_ref, g_ref, a_ref, o_ref):
    n2 = DFT_N2
    kr, ki = kf_ref[0, :n2], kf_ref[0, n2:]
    for bi in range(a_ref.shape[0]):
        x = _dot(f_ref[0], a_ref[bi, 0])
        xr, xi = x[:n2], x[n2:]
        y = jnp.concatenate([xr * kr - xi * ki, xr * ki + xi * kr], axis=0).astype(BF16)
        o_ref[bi, 0] = _dot(g_ref[0], y).astype(BF16)


def _hy_mid(a4, f2, kf, g2):
    b, n1, r, c = a4.shape
    mat = pl.BlockSpec((1, r, r), lambda i: (i, 0, 0))
    dat = pl.BlockSpec((b, 1, r, c), lambda i: (0, i, 0, 0))
    return pl.pallas_call(
        _hy_mid_kernel,
        out_shape=jax.ShapeDtypeStruct(a4.shape, BF16),
        grid=(n1,),
        in_specs=[mat, pl.BlockSpec((1, r, c), lambda i: (i, 0, 0)), mat, dat],
        out_specs=dat,
        compiler_params=_cparams(("parallel",), 48),
        name="hy_mid",
    )(f2, kf, g2, a4)


def _hy_post_kernel(g_ref, bi_ref, s_ref, x0_ref, bias_ref, o_ref):
    y = _dot(g_ref[...], bi_ref[0]) + s_ref[0].astype(F32) * bias_ref[...]
    o_ref[0] = (x0_ref[0].astype(F32) * y).astype(BF16)


def _hy_post(g1, bi2, s2, x02, bias_t):
    b, k, n = bi2.shape
    m = g1.shape[0]
    cb = HY_COLS
    dat = pl.BlockSpec((1, m, cb), lambda bi, c: (bi, 0, c))
    return pl.pallas_call(
        _hy_post_kernel,
        out_shape=jax.ShapeDtypeStruct((b, m, n), BF16),
        grid=(b, n // cb),
        in_specs=[_const_spec((m, k)), pl.BlockSpec((1, k, cb), lambda bi, c: (bi, 0, c)), dat, dat,
                  _const_spec((1, cb))],
        out_specs=dat,
        compiler_params=_cparams(("parallel", "parallel"), 48),
        name="hy_post",
    )(g1, bi2, s2, x02, bias_t)


def _hy_filter_kernel(z_ref, win_ref, w1_ref, b1_ref, w2_ref, b2_ref, w3_ref, fr_ref, o_ref, *, half_tiles):
    fr = fr_ref[...]
    h = jnp.sin(fr * (_dot3(z_ref[...], w1_ref[...]) + b1_ref[...]))
    h = jnp.sin(fr * (_dot3(h, w2_ref[...]) + b2_ref[...]))
    h3 = _dot3(h, w3_ref[...])
    backward = pl.program_id(0) >= half_tiles
    o_ref[...] = jnp.where(backward, h3[:, HY_WIDTH:], h3[:, :HY_WIDTH]) * win_ref[...]


def _hy_filter(z2, win2, w1p, b1, w2, b2, w3, fr):
    n = z2.shape[0]
    tl = 512
    return pl.pallas_call(
        functools.partial(_hy_filter_kernel, half_tiles=n // (2 * tl)),
        out_shape=jax.ShapeDtypeStruct((n, HY_WIDTH), F32),
        grid=(n // tl,),
        in_specs=[pl.BlockSpec((tl, LANES), lambda i: (i, 0)), pl.BlockSpec((tl, HY_WIDTH), lambda i: (i, 0)),
                  _const_spec(w1p.shape), _const_spec(b1.shape), _const_spec(w2.shape), _const_spec(b2.shape),
                  _const_spec(w3.shape), _const_spec(fr.shape)],
        out_specs=pl.BlockSpec((tl, HY_WIDTH), lambda i: (i, 0)),
        compiler_params=_cparams(("parallel",), 32),
        name="hy_filter",
    )(z2, win2, w1p, b1, w2, b2, w3, fr)


def _hy_spec_kernel(f_ref, a_ref, o_ref):
    o_ref[0] = _dot3(f_ref[0], a_ref[0])


def _hy_spec(f2, a3):
    n1, r, c = a3.shape
    return pl.pallas_call(
        _hy_spec_kernel,
        out_shape=jax.ShapeDtypeStruct(a3.shape, F32),
        grid=(n1,),
        in_specs=[pl.BlockSpec((1, r, r), lambda i: (i, 0, 0)), pl.BlockSpec((1, r, c), lambda i: (i, 0, 0))],
        out_specs=pl.BlockSpec((1, r, c), lambda i: (i, 0, 0)),
        compiler_params=_cparams(("parallel",), 32),
        name="hy_filter_spectrum",
    )(f2, a3)


def _merge_kernel(x_ref, yat_ref, yb_ref, yct_ref, gt_ref, wb_ref, wo_ref, o_ref):
    d = D_MODEL
    ba = _dot_tn(yat_ref[0], wb_ref[0])
    bb = _dot(yb_ref[0], wb_ref[1])
    bc = _dot_tn(yct_ref[0], wb_ref[2])
    merged = (gt_ref[0, :, 0:d].astype(F32) * ba + gt_ref[0, :, d:2 * d].astype(F32) * bb
              + gt_ref[0, :, 2 * d:3 * d].astype(F32) * bc)
    o_ref[0] = x_ref[0] + _dot(merged.astype(BF16), wo_ref[...])


def _merge(x, yat, yb, yct, gates, wb, wo):
    b, l, _ = x.shape
    tm = TOK_TILE

    def tok(w):
        return pl.BlockSpec((1, tm, w), lambda bi, i: (bi, i, 0))

    tr = pl.BlockSpec((1, BRANCH_W, tm), lambda bi, i: (bi, 0, i))
    return pl.pallas_call(
        _merge_kernel,
        out_shape=jax.ShapeDtypeStruct(x.shape, F32),
        grid=(b, l // tm),
        in_specs=[tok(D_MODEL), tr, tok(BRANCH_W), tr, tok(N_BRANCH * D_MODEL),
                  _const_spec(wb.shape), _const_spec(wo.shape)],
        out_specs=tok(D_MODEL),
        compiler_params=_cparams(("parallel", "parallel"), 48),
        name="merge",
    )(x, yat, yb, yct, gates, wb, wo)


def _rope_tables(l):
    rows = l // GRID_W
    row = jnp.repeat(jnp.arange(rows, dtype=F32), GRID_W)
    col = jnp.tile(jnp.arange(GRID_W, dtype=F32), rows)

    def tab(d_rot):
        n_freq = d_rot // 4
        inv = ROPE_THETA ** (-jnp.arange(n_freq, dtype=F32) / n_freq)
        ang = jnp.concatenate([row[:, None] * inv, col[:, None] * inv], axis=-1)
        c = jnp.repeat(jnp.cos(ang), 2, axis=-1)
        s = jnp.repeat(jnp.sin(ang), 2, axis=-1) * jnp.tile(jnp.array([-1.0, 1.0], F32), d_rot // 2)
        return c, s

    ca, sa = tab(GQA_HEAD_DIM)
    cm, sm = tab(MLA_ROPE)
    pad = ((0, 0), (0, LANES - MLA_ROPE))
    return (jnp.tile(ca, (1, 2)), jnp.tile(sa, (1, 2)), jnp.pad(cm, pad), jnp.pad(sm, pad))


def _hy_positions(l):
    t = jnp.linspace(0.0, 1.0, l, dtype=F32)[:, None]
    w = 2.0 * math.pi * jnp.arange(l, dtype=F32)[:, None] / l
    f = jnp.linspace(1e-4, HY_BANDS - 1, HY_BANDS, dtype=F32)[None, :]
    z = jnp.concatenate([t, jnp.cos(f * w), -jnp.sin(f * w)], axis=-1)
    max_decay = math.log(HY_TARGET) / HY_FAST
    min_decay = math.log(HY_TARGET) / HY_SLOW
    deltas = jnp.abs(jnp.linspace(min_decay, max_decay, HY_WIDTH, dtype=F32))
    window = jnp.exp(-t * deltas[None, :])
    zu = jnp.roll(z[::-1], 1, axis=0)
    wu = jnp.roll(window[::-1], 1, axis=0).at[0].set(0.0)
    z2 = jnp.pad(jnp.concatenate([z, zu], axis=0), ((0, 0), (0, LANES - HY_EMB)))
    return z2, jnp.concatenate([window, wu], axis=0)


def _hy_k1_count(l):
    n1 = 2 * l // DFT_N2
    return -(-(n1 // 2 + 1) // 8) * 8


def _dft_tables(l):
    n = 2 * l
    n2 = DFT_N2
    n1 = n // n2
    n1c = _hy_k1_count(l)
    two_pi = 2.0 * math.pi
    i1 = jnp.arange(n1, dtype=jnp.int32)
    ang1 = ((i1[:n1c, None] * i1[None, :]) % n1).astype(F32) * (two_pi / n1)
    live = (i1[:n1c] <= n1 // 2)[:, None]
    c1, s1 = jnp.where(live, jnp.cos(ang1), 0.0), jnp.where(live, jnp.sin(ang1), 0.0)
    f1_full = jnp.stack([c1, -s1], axis=1).reshape(2 * n1c, n1)
    fold = jnp.where((i1[:n1c] == 0) | (i1[:n1c] == n1 // 2), 1.0, 2.0)[:, None] * (1.0 / n)
    g1 = jnp.stack([(c1 * fold)[:, :n1 // 2].T, (-s1 * fold)[:, :n1 // 2].T], axis=2).reshape(n1 // 2, 2 * n1c)
    i2 = jnp.arange(n2, dtype=jnp.int32)
    kk = i1[:n1c, None, None] + n1 * i2[None, :, None]
    ang2 = ((kk * i2[None, None, :]) % n).astype(F32) * (two_pi / n)
    c2, s2 = jnp.cos(ang2), jnp.sin(ang2)
    f2 = jnp.concatenate([jnp.concatenate([c2, s2], axis=2), jnp.concatenate([-s2, c2], axis=2)], axis=1)
    c2t, s2t = jnp.swapaxes(c2, 1, 2), jnp.swapaxes(s2, 1, 2)
    g2 = jnp.concatenate([jnp.concatenate([c2t, -s2t], axis=2), jnp.concatenate([s2t, c2t], axis=2)], axis=1)
    return f1_full, g1, f2, g2


def _swap_pairs(n):
    return np.arange(n) ^ 1


def _proj_a_columns():
    zero = IN_WIDTH
    cols = np.full((PA_END,), zero, np.int64)
    q0, k0, v0, _, cq0, ckv0, kr0, _ = IN_OFFS[:8]
    for h in range(GQA_HEADS):
        dst = h * LANES + (h // (GQA_HEADS // GQA_KV_HEADS)) * GQA_HEAD_DIM
        src = q0 + h * GQA_HEAD_DIM + np.arange(GQA_HEAD_DIM)
        cols[PA_Q + dst:PA_Q + dst + GQA_HEAD_DIM] = src
        cols[PA_QS + dst:PA_QS + dst + GQA_HEAD_DIM] = q0 + h * GQA_HEAD_DIM + _swap_pairs(GQA_HEAD_DIM)
    cols[PA_K:PA_K + 128] = k0 + np.arange(128)
    cols[PA_KS:PA_KS + 128] = k0 + _swap_pairs(128)
    cols[PA_V:PA_V + 128] = v0 + np.arange(128)
    cols[PA_CQ:PA_CQ + MLA_Q_RANK] = cq0 + np.arange(MLA_Q_RANK)
    cols[PA_CKV:PA_CKV + MLA_KV_RANK] = ckv0 + np.arange(MLA_KV_RANK)
    cols[PA_KR:PA_KR + MLA_ROPE] = kr0 + np.arange(MLA_ROPE)
    cols[PA_KRS:PA_KRS + MLA_ROPE] = kr0 + _swap_pairs(MLA_ROPE)
    return cols


def _mla_q_columns():
    hd = MLA_NOPE + MLA_ROPE
    zero = MLA_HEADS * hd
    main = np.full((MLA_HEADS * MLA_KC,), zero, np.int64)
    swap = np.full((MLA_HEADS * LANES,), zero, np.int64)
    for h in range(MLA_HEADS):
        b0 = h * MLA_KC + (h % 2) * MLA_NOPE
        main[b0:b0 + MLA_NOPE] = h * hd + np.arange(MLA_NOPE)
        r0 = h * MLA_KC + LANES
        main[r0:r0 + MLA_ROPE] = h * hd + MLA_NOPE + np.arange(MLA_ROPE)
        swap[h * LANES:h * LANES + MLA_ROPE] = h * hd + MLA_NOPE + _swap_pairs(MLA_ROPE)
    return main, swap


def _mla_kv_columns():
    hd = MLA_NOPE + MLA_V
    knope = np.concatenate([h * hd + np.arange(MLA_NOPE) for h in range(MLA_HEADS)])
    val = np.concatenate([h * hd + MLA_NOPE + np.arange(MLA_V) for h in range(MLA_HEADS)])
    return np.concatenate([knope, val])


def _take_cols(w, cols):
    wz = jnp.concatenate([w, jnp.zeros((w.shape[0], 1), w.dtype)], axis=1)
    return jnp.take(wz, jnp.asarray(cols, jnp.int32), axis=1).astype(BF16)


def _tile2(g):
    return jnp.tile(g, 2)[None, :]


def _encoder(x, lw, g_final):
    b, l, d = x.shape
    n2 = DFT_N2
    n1 = 2 * l // n2
    c = HY_WIDTH
    tabs = _rope_tables(l)
    z2, win2 = _hy_positions(l)
    f1_full, g1, f2, g2 = _dft_tables(l)
    f1_b = f1_full[:, :n1 // 2].astype(BF16)
    g1_b = g1.astype(BF16)
    f2_b = f2.astype(BF16)
    g2_b = g2.astype(BF16)
    for li, w in enumerate(lw):
        x2 = _ffn(x.reshape(b * l, d), w["g_ffn1"], w["wg1"], w["wu1"], w["wd1"], g_final, False)
        x = x2.reshape(b, l, d)
        qg, kg, vgt, qm, km, vmt, bnd = _proj_a(x, w["g_mix"], w["w_a"], tabs, w["gq"], w["gqs"], w["gk"],
                                                w["gks"], w["gcq"], w["gckv"], w["wuq"], w["wuqs"], w["wukv"])
        hy, gates = _proj_b(x2, w["g_mix"], w["w_b"])
        top = bnd[:, :, :4, 0]
        limit = ATT_SAFE_SCORE ** 2
        safe_a = (top[:, :, 0] * jnp.max(top[:, :, 1], axis=1, keepdims=True) <= limit).astype(jnp.int32)
        safe_m = (top[:, :, 2] * jnp.max(top[:, :, 3], axis=1, keepdims=True) <= limit).astype(jnp.int32)
        per_kv = GQA_HEADS // GQA_KV_HEADS
        yat = _attention(safe_a, qg, kg, vgt, heads=ATT_HEADS, kc=GQA_KC, key_of_head=(0,) * GQA_HEADS,
                         value_of_head=tuple(h // per_kv for h in range(GQA_HEADS)), name="attn_gqa")
        yct = _attention(safe_m, qm, km, vmt, heads=ATT_HEADS, kc=MLA_KC,
                         key_of_head=tuple(h // 2 for h in range(MLA_HEADS)),
                         value_of_head=tuple(range(MLA_HEADS)), name="attn_mla")
        kc_time = _hy_filter(z2, win2, w["w1p"], w["b1"], w["w2"], w["b2"], w["w3"], w["fr"])
        n1c = _hy_k1_count(l)
        ka =_left_mm(f1_full, kc_time.reshape(1, n1, n2 * c), F32, hi=True)
        kf = _hy_spec(f2, ka.reshape(n1c, 2 * n2, c))
        s, x0 = _hy_pre(hy.reshape(b, l, 3 * c), w["w_short"], w["b_short"])
        a = _left_mm(f1_b, s.reshape(b, n1 // 2, n2 * c), BF16)
        bi = _hy_mid(a.reshape(b, n1c, 2 * n2, c), f2_b, kf, g2_b)
        yb = _hy_post(g1_b, bi.reshape(b, 2 * n1c, n2 * c), s.reshape(b, n1 // 2, n2 * c),
                      x0.reshape(b, n1 // 2, n2 * c), w["bias_t"])
        x = _merge(x, yat, yb.reshape(b, l, c), yct, gates.reshape(b, l, N_BRANCH * d), w["wb"], w["wo"])
        x2 = _ffn(x.reshape(b * l, d), w["g_ffn2"], w["wg2"], w["wu2"], w["wd2"], g_final, li == len(lw) - 1)
        x = x2.reshape(b, l, d)
    return x


def kernel(x_prompt, x_sample, g_ffn1, w_ffn1_gate, w_ffn1_up, w_ffn1_down, g_mix, w_in, g_qnorm, g_knorm,
           w_hy_short, b_hy_short, w_hy_f1, b_hy_f1, w_hy_f2, b_hy_f2, w_hy_f3, hy_sin_freq, hy_bias,
           g_mla_q, w_mla_uq, g_mla_kv, w_mla_ukv, w_branch, w_out, g_ffn2, w_ffn2_gate, w_ffn2_up,
           w_ffn2_down, g_final):
    cols_a = _proj_a_columns()
    uq_main, uq_swap = _mla_q_columns()
    ukv_cols = _mla_kv_columns()
    sw64 = _swap_pairs(GQA_HEAD_DIM)
    hy0 = IN_OFFS[3]
    gt0 = IN_OFFS[7]
    lw = []
    for l in range(DEPTH):
        lw.append(dict(
            g_ffn1=g_ffn1[l][None], wg1=w_ffn1_gate[l].astype(BF16), wu1=w_ffn1_up[l].astype(BF16),
            wd1=w_ffn1_down[l].astype(BF16),
            g_ffn2=g_ffn2[l][None], wg2=w_ffn2_gate[l].astype(BF16), wu2=w_ffn2_up[l].astype(BF16),
            wd2=w_ffn2_down[l].astype(BF16),
            g_mix=g_mix[l][None],
            w_a=_take_cols(w_in[l], cols_a),
            w_b=jnp.concatenate([w_in[l][:, hy0:hy0 + 3 * HY_WIDTH], w_in[l][:, gt0:]], axis=1).astype(BF16),
            gq=_tile2(g_qnorm[l]), gqs=_tile2(g_qnorm[l][sw64]),
            gk=_tile2(g_knorm[l]), gks=_tile2(g_knorm[l][sw64]),
            gcq=g_mla_q[l][None], gckv=g_mla_kv[l][None],
            wuq=_take_cols(w_mla_uq[l], uq_main), wuqs=_take_cols(w_mla_uq[l], uq_swap),
            wukv=_take_cols(w_mla_ukv[l], ukv_cols),
            w_short=w_hy_short[l], b_short=b_hy_short[l],
            w1p=jnp.pad(w_hy_f1[l], ((0, LANES - HY_EMB), (0, 0))), b1=b_hy_f1[l][None],
            w2=w_hy_f2[l], b2=b_hy_f2[l][None], w3=w_hy_f3[l], fr=hy_sin_freq[l][None],
            bias_t=jnp.tile(hy_bias[l], HY_COLS // HY_WIDTH)[None],
            wb=w_branch[l].astype(BF16), wo=w_out[l].astype(BF16),
        ))
    gf = g_final[None]
    return _encoder(x_prompt, lw, gf), _encoder(x_sample, lw, gf)
```

```python
import functools
import math

import numpy as np
import jax
import jax.numpy as jnp
from jax import lax
from jax.experimental import pallas as pl
from jax.experimental.pallas import tpu as pltpu

F32 = jnp.float32
BF16 = jnp.bfloat16

D_MODEL = 1024
DEPTH = 2
GRID_W = 64
ROPE_THETA = 10000.0
EPS = 1e-6
D_FF = 2816
N_BRANCH = 3
BRANCH_W = 512
GQA_HEADS = 8
GQA_KV_HEADS = 2
GQA_HEAD_DIM = 64
HY_WIDTH = 512
HY_ORDER = 64
HY_EMB = 33
HY_BANDS = (HY_EMB - 1) // 2
HY_TARGET = 1e-2
HY_FAST = 0.3
HY_SLOW = 1.5
MLA_HEADS = 8
MLA_Q_RANK = 256
MLA_KV_RANK = 128
MLA_NOPE = 64
MLA_ROPE = 32
MLA_V = 64
IN_WIDTHS = (512, 128, 128, 3 * HY_WIDTH, MLA_Q_RANK, MLA_KV_RANK, MLA_ROPE, N_BRANCH * D_MODEL)
IN_OFFS = tuple(int(c) for c in np.cumsum((0,) + IN_WIDTHS))
IN_WIDTH = IN_OFFS[-1]

LANES = 128
V7X_VMEM_BYTES = 64 * 1024 * 1024

TOK_TILE = 512
ATT_TQ = 256
ATT_TQB = 512
ATT_HEADS = 4
ATT_UNROLL = 4
ATT_SAFE_SCORE = 64.0
ATT_TK = 2 * TOK_TILE
FF_CHUNKS = ((0, 1024), (1024, 2048), (2048, D_FF))
DFT_N2 = 128
HY_COLS = 8192
ONES_ROWS = 16
NEG_BIG = -1e30
LOG2E = math.log2(math.e)


def _cparams(sem, vmem_mb):
    return pltpu.CompilerParams(dimension_semantics=sem, vmem_limit_bytes=vmem_mb * 1024 * 1024)


def _const_spec(shape):
    nd = len(shape)
    return pl.BlockSpec(shape, lambda *_: (0,) * nd, pipeline_mode=pl.Buffered(1))


def _rms(x, g):
    return x * lax.rsqrt(jnp.mean(x * x, axis=-1, keepdims=True) + EPS) * g


def _dot(a, b):
    return jnp.dot(a, b, preferred_element_type=F32)


def _dot_nt(a, b):
    return lax.dot_general(a, b, (((1,), (1,)), ((), ())), preferred_element_type=F32)


def _dot_tn(a, b):
    return lax.dot_general(a, b, (((0,), (0,)), ((), ())), preferred_element_type=F32)


def _split(a):
    hi = a.astype(BF16)
    return hi, (a - hi.astype(F32)).astype(BF16)


def _dot3(a, b):
    ah, al = _split(a)
    bh, bl = _split(b)
    return _dot(ah, bh) + (_dot(ah, bl) + _dot(al, bh))


def _ffn_kernel(x_ref, g_ref, wg_ref, wu_ref, wd_ref, gf_ref, o_ref, *, final):
    x = x_ref[...]
    xb = _rms(x, g_ref[...]).astype(BF16)
    acc = jnp.zeros_like(x)
    for c0, c1 in FF_CHUNKS:
        gate = _dot(xb, wg_ref[:, c0:c1])
        up = _dot(xb, wu_ref[:, c0:c1])
        h = (gate * jax.nn.sigmoid(gate) * up).astype(BF16)
        acc = acc + _dot(h, wd_ref[c0:c1, :])
    y = x + 0.5 * acc
    if final:
        y = _rms(y, gf_ref[...])
    o_ref[...] = y


def _ffn(x2d, g, wg, wu, wd, g_final, final):
    t = x2d.shape[0]
    row = pl.BlockSpec((TOK_TILE, D_MODEL), lambda i: (i, 0))
    return pl.pallas_call(
        functools.partial(_ffn_kernel, final=final),
        out_shape=jax.ShapeDtypeStruct((t, D_MODEL), F32),
        grid=(t // TOK_TILE,),
        in_specs=[row, _const_spec((1, D_MODEL)), _const_spec((D_MODEL, D_FF)),
                  _const_spec((D_MODEL, D_FF)), _const_spec((D_FF, D_MODEL)), _const_spec((1, D_MODEL))],
        out_specs=row,
        compiler_params=_cparams(("parallel",), 48),
        name="ffn",
    )(x2d, g, wg, wu, wd, g_final)


PA_Q, PA_QS, PA_K, PA_KS, PA_V, PA_CQ, PA_CKV, PA_KR, PA_KRS, PA_END = (
    0, 1024, 2048, 2176, 2304, 2432, 2688, 2816, 2944, 3072)
GQA_KC = 128
MLA_KC = 256


def _sq_norm(a_bf16):
    a = a_bf16.astype(F32)
    return jnp.sum(a * a, axis=-1, keepdims=True)


def _proj_a_kernel(x_ref, g_ref, w_ref, cosa_ref, sina_ref, cosm_ref, sinm_ref,
                   gq_ref, gqs_ref, gk_ref, gks_ref, gcq_ref, gckv_ref,
                   wuq_ref, wuqs_ref, wukv_ref,
                   qg_ref, kg_ref, vgt_ref, qm_ref, km_ref, vmt_ref, bnd_ref):
    ub = _rms(x_ref[0], g_ref[...]).astype(BF16)
    cosa, sina = cosa_ref[...], sina_ref[...]
    cosm, sinm = cosm_ref[...], sinm_ref[...]

    zq = _dot(ub, w_ref[:, PA_Q:PA_QS])
    zqs = _dot(ub, w_ref[:, PA_QS:PA_K])
    tq_c = gq_ref[...] * cosa
    tq_s = gqs_ref[...] * sina
    scale_a = GQA_HEAD_DIM ** -0.5 * LOG2E
    qa_n2 = None
    for h in range(GQA_HEADS):
        a = zq[:, h * LANES:(h + 1) * LANES]
        a_sw = zqs[:, h * LANES:(h + 1) * LANES]
        r = lax.rsqrt(jnp.sum(a * a, axis=-1, keepdims=True) * (1.0 / GQA_HEAD_DIM) + EPS)
        qb = ((a * tq_c + a_sw * tq_s) * (r * scale_a)).astype(BF16)
        qg_ref[0, :, h * LANES:(h + 1) * LANES] = qb
        qa_n2 = _sq_norm(qb) if qa_n2 is None else jnp.maximum(qa_n2, _sq_norm(qb))

    zk = _dot(ub, w_ref[:, PA_K:PA_KS])
    zks = _dot(ub, w_ref[:, PA_KS:PA_V])
    first = lax.broadcasted_iota(jnp.int32, zk.shape, 1) < GQA_HEAD_DIM
    sq = zk * zk
    r0 = lax.rsqrt(jnp.sum(jnp.where(first, sq, 0.0), axis=-1, keepdims=True) * (1.0 / GQA_HEAD_DIM) + EPS)
    r1 = lax.rsqrt(jnp.sum(jnp.where(first, 0.0, sq), axis=-1, keepdims=True) * (1.0 / GQA_HEAD_DIM) + EPS)
    kb = ((zk * (gk_ref[...] * cosa) + zks * (gks_ref[...] * sina)) * jnp.where(first, r0, r1)).astype(BF16)
    kg_ref[0] = kb
    kb32 = kb.astype(F32)
    kb_sq = kb32 * kb32
    ka_n2 = jnp.maximum(jnp.sum(jnp.where(first, kb_sq, 0.0), axis=-1, keepdims=True),
                        jnp.sum(jnp.where(first, 0.0, kb_sq), axis=-1, keepdims=True))

    vgt_ref[0, 0] = _dot(ub, w_ref[:, PA_V:PA_CQ]).T.astype(BF16)

    scale_m = (MLA_NOPE + MLA_ROPE) ** -0.5 * LOG2E
    cqn = _rms(_dot(ub, w_ref[:, PA_CQ:PA_CKV]), gcq_ref[...]).astype(BF16)
    zq2 = _dot(cqn, wuq_ref[...])
    zq2s = _dot(cqn, wuqs_ref[...])
    qm_n2 = None
    for h in range(MLA_HEADS):
        b0 = h * MLA_KC
        nope = (zq2[:, b0:b0 + LANES] * scale_m).astype(BF16)
        rope = zq2[:, b0 + LANES:b0 + 2 * LANES] * cosm + zq2s[:, h * LANES:(h + 1) * LANES] * sinm
        rope = (rope * scale_m).astype(BF16)
        qm_ref[0, :, b0:b0 + LANES] = nope
        qm_ref[0, :, b0 + LANES:b0 + 2 * LANES] = rope
        n2 = _sq_norm(nope) + _sq_norm(rope)
        qm_n2 = n2 if qm_n2 is None else jnp.maximum(qm_n2, n2)

    ckvn = _rms(_dot(ub, w_ref[:, PA_CKV:PA_KR]), gckv_ref[...]).astype(BF16)
    zkv = _dot(ckvn, wukv_ref[...])
    krope = (_dot(ub, w_ref[:, PA_KR:PA_KRS]) * cosm + _dot(ub, w_ref[:, PA_KRS:PA_END]) * sinm).astype(BF16)
    km_n2 = None
    for p in range(MLA_HEADS // 2):
        b0 = p * MLA_KC
        nope = zkv[:, p * LANES:(p + 1) * LANES].astype(BF16)
        km_ref[0, :, b0:b0 + LANES] = nope
        km_ref[0, :, b0 + LANES:b0 + 2 * LANES] = krope
        km_n2 = _sq_norm(nope) if km_n2 is None else jnp.maximum(km_n2, _sq_norm(nope))
    km_n2 = km_n2 + _sq_norm(krope)
    vmt_ref[0, 0] = zkv[:, MLA_HEADS * MLA_NOPE:].T.astype(BF16)

    row = lax.broadcasted_iota(jnp.int32, (8, LANES), 0)
    tops = [jnp.max(v, axis=0, keepdims=True) for v in (qa_n2, ka_n2, qm_n2, km_n2)]
    bnd_ref[0, 0] = jnp.where(row == 0, tops[0], jnp.where(row == 1, tops[1], jnp.where(row == 2, tops[2], tops[3])))


def _proj_a(x, g_mix, w_a, tabs, gq, gqs, gk, gks, gcq, gckv, wuq, wuqs, wukv):
    b, l, _ = x.shape
    tm = TOK_TILE
    nt = l // tm
    cosa, sina, cosm, sinm = tabs
    tab = pl.BlockSpec((tm, LANES), lambda bi, i: (i, 0))

    def tok(w):
        return pl.BlockSpec((1, tm, w), lambda bi, i: (bi, i, 0))

    def tr(rows):
        return pl.BlockSpec((1, 1, rows, tm), lambda bi, i: (bi, i, 0, 0))

    out_shape = (
        jax.ShapeDtypeStruct((b, l, GQA_HEADS * GQA_KC), BF16),
        jax.ShapeDtypeStruct((b, l, GQA_KC), BF16),
        jax.ShapeDtypeStruct((b, nt, GQA_KV_HEADS * GQA_HEAD_DIM, tm), BF16),
        jax.ShapeDtypeStruct((b, l, MLA_HEADS * MLA_KC), BF16),
        jax.ShapeDtypeStruct((b, l, (MLA_HEADS // 2) * MLA_KC), BF16),
        jax.ShapeDtypeStruct((b, nt, MLA_HEADS * MLA_V, tm), BF16),
        jax.ShapeDtypeStruct((b, nt, 8, LANES), F32),
    )
    return pl.pallas_call(
        _proj_a_kernel,
        out_shape=out_shape,
        grid=(b, nt),
        in_specs=[tok(D_MODEL), _const_spec((1, D_MODEL)), _const_spec(w_a.shape), tab, tab, tab, tab,
                  _const_spec((1, LANES)), _const_spec((1, LANES)), _const_spec((1, LANES)),
                  _const_spec((1, LANES)), _const_spec((1, MLA_Q_RANK)), _const_spec((1, MLA_KV_RANK)),
                  _const_spec(wuq.shape), _const_spec(wuqs.shape), _const_spec(wukv.shape)],
        out_specs=(tok(GQA_HEADS * GQA_KC), tok(GQA_KC), tr(GQA_KV_HEADS * GQA_HEAD_DIM),
                   tok(MLA_HEADS * MLA_KC), tok((MLA_HEADS // 2) * MLA_KC), tr(MLA_HEADS * MLA_V),
                   pl.BlockSpec((1, 1, 8, LANES), lambda bi, i: (bi, i, 0, 0))),
        compiler_params=_cparams(("parallel", "parallel"), 48),
        name="proj_a",
    )(x, g_mix, w_a, cosa, sina, cosm, sinm, gq, gqs, gk, gks, gcq, gckv, wuq, wuqs, wukv)


def _proj_b_kernel(x_ref, g_ref, w_ref, hy_ref, gt_ref):
    ub = _rms(x_ref[...], g_ref[...]).astype(BF16)
    hy_ref[...] = _dot(ub, w_ref[:, :3 * HY_WIDTH]).astype(BF16)
    gt_ref[...] = jax.nn.sigmoid(_dot(ub, w_ref[:, 3 * HY_WIDTH:])).astype(BF16)


def _proj_b(x2d, g_mix, w_b):
    t = x2d.shape[0]
    tm = TOK_TILE

    def row(w):
        return pl.BlockSpec((tm, w), lambda i: (i, 0))

    return pl.pallas_call(
        _proj_b_kernel,
        out_shape=(jax.ShapeDtypeStruct((t, 3 * HY_WIDTH), BF16),
                   jax.ShapeDtypeStruct((t, N_BRANCH * D_MODEL), BF16)),
        grid=(t // tm,),
        in_specs=[row(D_MODEL), _const_spec((1, D_MODEL)), _const_spec(w_b.shape)],
        out_specs=(row(3 * HY_WIDTH), row(N_BRANCH * D_MODEL)),
        compiler_params=_cparams(("parallel",), 48),
        name="proj_b",
    )(x2d, g_mix, w_b)


def _attn_kernel(safe_ref, q_ref, k_ref, vt_ref, o_ref, acc_scr, l_scr, p0_scr, p1_scr,
                 *, heads, groups, kc, key_of_head, value_of_head, nk, nk_tile):
    tv = vt_ref.shape[3]
    tk = nk_tile
    tq = ATT_TQ
    ones = jnp.ones((ONES_ROWS, tk), BF16)
    dv = MLA_V
    n_sub = q_ref.shape[1] // tq
    safe = safe_ref[pl.program_id(0), pl.program_id(1)] != 0

    def key_tile(j, h):
        kb = key_of_head[h]
        rows = j * tk if isinstance(j, int) else pl.multiple_of(j * tk, tk)
        return k_ref[0, pl.ds(rows, tk), kb * kc:(kb + 1) * kc]

    def values(j, h):
        vb = value_of_head[h]
        per = tk // tv
        return jnp.concatenate([vt_ref[0, j * per + u, vb * dv:(vb + 1) * dv, :] for u in range(per)], axis=1)

    def value_tile(j, h):
        return jnp.concatenate([values(j, h), ones], axis=0)

    def write_out(acc, l, h, sub):
        o_ref[0, h * dv:(h + 1) * dv, sub * tq:(sub + 1) * tq] = (acc / l).astype(BF16)

    @pl.when(safe)
    def _():
        bufs = (p0_scr, p1_scr)
        streams = [(sub, g) for sub in range(n_sub) for g in range(groups)]

        def probs(si, stream, j, hh, buf):
            sub, g = stream
            h = g * heads + hh
            st = _dot_nt(key_tile(j, h), q_ref[0, sub * tq:(sub + 1) * tq, h * kc:(h + 1) * kc])
            e = jnp.exp2(st)
            buf[hh] = e.astype(BF16)
            l_scr[si % 2, hh] += jnp.sum(e, axis=0, keepdims=True)

        def accumulate(si, stream, j, hh, buf):
            acc_scr[si % 2, hh] += _dot(values(j, stream[1] * heads + hh), buf[hh])

        acc_scr[...] = jnp.zeros(acc_scr.shape, F32)
        l_scr[...] = jnp.zeros(l_scr.shape, F32)
        for hh in range(heads):
            probs(0, streams[0], 0, hh, bufs[0])
        for si, stream in enumerate(streams):
            def step(j, cur, nxt, si=si, stream=stream):
                for hh in range(heads):
                    probs(si, stream, j + 1, hh, nxt)
                    accumulate(si, stream, j, hh, cur)

            unroll = min(ATT_UNROLL, nk)

            def body(jj, carry, step=step):
                for u in range(unroll):
                    step(unroll * jj + u, bufs[u % 2], bufs[1 - u % 2])
                return carry

            lax.fori_loop(0, nk // unroll - 1, body, 0)
            for u in range(unroll - 1):
                step(nk - unroll + u, bufs[u % 2], bufs[1 - u % 2])
            for hh in range(heads):
                if si + 1 < len(streams):
                    probs(si + 1, streams[si + 1], 0, hh, bufs[0])
                accumulate(si, stream, nk - 1, hh, bufs[1])
            sub, g = stream
            for hh in range(heads):
                write_out(acc_scr[si % 2, hh], l_scr[si % 2, hh], g * heads + hh, sub)
                if si + 2 < len(streams):
                    acc_scr[si % 2, hh] = jnp.zeros((dv, tq), F32)
                    l_scr[si % 2, hh] = jnp.zeros((1, tq), F32)

    @pl.when(jnp.logical_not(safe))
    def _():
        for h in range(groups * heads):
            for sub in range(n_sub):
                qh = q_ref[0, sub * tq:(sub + 1) * tq, h * kc:(h + 1) * kc]

                def body(j, carry, qh=qh, h=h):
                    m, acc = carry
                    st = _dot_nt(key_tile(j, h), qh)
                    m_new = jnp.maximum(m, jnp.max(st, axis=0, keepdims=True))
                    p = jnp.exp2(st - m_new).astype(BF16)
                    return m_new, jnp.exp2(m - m_new) * acc + _dot(value_tile(j, h), p)

                init = (jnp.full((1, tq), NEG_BIG, F32), jnp.zeros((dv + ONES_ROWS, tq), F32))
                _, acc = lax.fori_loop(0, nk, body, init)
                write_out(acc[:dv], acc[dv:dv + 1], h, sub)


def _attention(safe, q, k, vt, *, heads, kc, key_of_head, value_of_head, name):
    b, l, qw = q.shape
    n_heads = qw // kc
    nv, tv = vt.shape[1], vt.shape[3]
    tk = ATT_TK
    nk = l // tk
    tqb = ATT_TQB
    tq = ATT_TQ
    dv = MLA_V
    grid_spec = pltpu.PrefetchScalarGridSpec(
        num_scalar_prefetch=1,
        grid=(b, l // tqb),
        in_specs=[pl.BlockSpec((1, tqb, qw), lambda bi, i, s: (bi, i, 0)),
                  pl.BlockSpec((1, l, k.shape[2]), lambda bi, i, s: (bi, 0, 0), pipeline_mode=pl.Buffered(1)),
                  pl.BlockSpec((1, nv, vt.shape[2], tv), lambda bi, i, s: (bi, 0, 0, 0),
                               pipeline_mode=pl.Buffered(1))],
        out_specs=pl.BlockSpec((1, n_heads * dv, tqb), lambda bi, i, s: (bi, 0, i)),
        scratch_shapes=[pltpu.VMEM((2, heads, dv, tq), F32), pltpu.VMEM((2, heads, 1, tq), F32),
                        pltpu.VMEM((heads, tk, tq), BF16), pltpu.VMEM((heads, tk, tq), BF16)],
    )
    return pl.pallas_call(
        functools.partial(_attn_kernel, heads=heads, groups=n_heads // heads, kc=kc, key_of_head=key_of_head,
                          value_of_head=value_of_head, nk=nk, nk_tile=tk),
        out_shape=jax.ShapeDtypeStruct((b, n_heads * dv, l), BF16),
        grid_spec=grid_spec,
        compiler_params=_cparams(("parallel", "parallel"), 48),
        name=name,
    )(safe, q, k, vt)


def _hy_pre_kernel(x0_ref, x1_ref, v_ref, w_ref, b_ref, s_ref, x0o_ref):
    rows = x0_ref.shape[1]
    t = lax.broadcasted_iota(jnp.int32, (rows, LANES), 0)
    not_first = t > 0
    not_last = t < rows - 1

    def conv(ref, j):
        a = ref[0].astype(F32)
        prev = jnp.where(not_first, pltpu.roll(a, 1, 0), 0.0)
        nxt = jnp.where(not_last, pltpu.roll(a, rows - 1, 0), 0.0)
        return prev * w_ref[0, j] + a * w_ref[1, j] + nxt * w_ref[2, j] + b_ref[j]

    x0o_ref[0] = conv(x0_ref, 0).astype(BF16)
    s_ref[0] = (conv(x1_ref, 1) * conv(v_ref, 2)).astype(BF16)


def _hy_pre(hy, w_short, b_short):
    b, l, _ = hy.shape
    nc = HY_WIDTH // LANES
    w4 = w_short.reshape(3, 3, nc, 1, LANES)
    b4 = b_short.reshape(3, nc, 1, LANES)

    def seg(j):
        return pl.BlockSpec((1, l, LANES), lambda bi, c, j=j: (bi, 0, j * nc + c))

    out = pl.BlockSpec((1, l, LANES), lambda bi, c: (bi, 0, c))
    return pl.pallas_call(
        _hy_pre_kernel,
        out_shape=(jax.ShapeDtypeStruct((b, l, HY_WIDTH), BF16), jax.ShapeDtypeStruct((b, l, HY_WIDTH), BF16)),
        grid=(b, nc),
        in_specs=[seg(0), seg(1), seg(2),
                  pl.BlockSpec((3, 3, None, 1, LANES), lambda bi, c: (0, 0, c, 0, 0)),
                  pl.BlockSpec((3, None, 1, LANES), lambda bi, c: (0, c, 0, 0))],
        out_specs=(out, out),
        compiler_params=_cparams(("parallel", "parallel"), 48),
        name="hy_pre",
    )(hy, hy, hy, w4, b4)


def _left_mm_kernel(f_ref, x_ref, o_ref, *, hi):
    if hi:
        o_ref[0] = _dot3(f_ref[...], x_ref[0]).astype(o_ref.dtype)
    else:
        o_ref[0] = _dot(f_ref[...], x_ref[0]).astype(o_ref.dtype)


def _left_mm(f, x, out_dtype, hi=False):
    b, k, n = x.shape
    m = f.shape[0]
    cb = min(HY_COLS, n)
    return pl.pallas_call(
        functools.partial(_left_mm_kernel, hi=hi),
        out_shape=jax.ShapeDtypeStruct((b, m, n), out_dtype),
        grid=(b, n // cb),
        in_specs=[_const_spec((m, k)), pl.BlockSpec((1, k, cb), lambda bi, c: (bi, 0, c))],
        out_specs=pl.BlockSpec((1, m, cb), lambda bi, c: (bi, 0, c)),
        compiler_params=_cparams(("parallel", "parallel"), 48),
        name="hy_outer_dft",
    )(f, x)


def _hy_mid_kernel(f_ref, kf_ref, g_ref, a_ref, o_ref):
    n2 = DFT_N2
    kr, ki = kf_ref[0, :n2], kf_ref[0, n2:]
    for bi in range(a_ref.shape[0]):
        x = _dot(f_ref[0], a_ref[bi, 0])
        xr, xi = x[:n2], x[n2:]
        y = jnp.concatenate([xr * kr - xi * ki, xr * ki + xi * kr], axis=0).astype(BF16)
        o_ref[bi, 0] = _dot(g_ref[0], y).astype(BF16)


def _hy_mid(a4, f2, kf, g2):
    b, n1, r, c = a4.shape
    mat = pl.BlockSpec((1, r, r), lambda i: (i, 0, 0))
    dat = pl.BlockSpec((b, 1, r, c), lambda i: (0, i, 0, 0))
    return pl.pallas_call(
        _hy_mid_kernel,
        out_shape=jax.ShapeDtypeStruct(a4.shape, BF16),
        grid=(n1,),
        in_specs=[mat, pl.BlockSpec((1, r, c), lambda i: (i, 0, 0)), mat, dat],
        out_specs=dat,
        compiler_params=_cparams(("parallel",), 48),
        name="hy_mid",
    )(f2, kf, g2, a4)


def _hy_post_kernel(g_ref, bi_ref, s_ref, x0_ref, bias_ref, o_ref):
    y = _dot(g_ref[...], bi_ref[0]) + s_ref[0].astype(F32) * bias_ref[...]
    o_ref[0] = (x0_ref[0].astype(F32) * y).astype(BF16)


def _hy_post(g1, bi2, s2, x02, bias_t):
    b, k, n = bi2.shape
    m = g1.shape[0]
    cb = HY_COLS
    dat = pl.BlockSpec((1, m, cb), lambda bi, c: (bi, 0, c))
    return pl.pallas_call(
        _hy_post_kernel,
        out_shape=jax.ShapeDtypeStruct((b, m, n), BF16),
        grid=(b, n // cb),
        in_specs=[_const_spec((m, k)), pl.BlockSpec((1, k, cb), lambda bi, c: (bi, 0, c)), dat, dat,
                  _const_spec((1, cb))],
        out_specs=dat,
        compiler_params=_cparams(("parallel", "parallel"), 48),
        name="hy_post",
    )(g1, bi2, s2, x02, bias_t)


def _hy_filter_kernel(z_ref, win_ref, w1_ref, b1_ref, w2_ref, b2_ref, w3_ref, fr_ref, o_ref, *, half_tiles):
    fr = fr_ref[...]
    h = jnp.sin(fr * (_dot3(z_ref[...], w1_ref[...]) + b1_ref[...]))
    h = jnp.sin(fr * (_dot3(h, w2_ref[...]) + b2_ref[...]))
    h3 = _dot3(h, w3_ref[...])
    backward = pl.program_id(0) >= half_tiles
    o_ref[...] = jnp.where(backward, h3[:, HY_WIDTH:], h3[:, :HY_WIDTH]) * win_ref[...]


def _hy_filter(z2, win2, w1p, b1, w2, b2, w3, fr):
    n = z2.shape[0]
    tl = 512
    return pl.pallas_call(
        functools.partial(_hy_filter_kernel, half_tiles=n // (2 * tl)),
        out_shape=jax.ShapeDtypeStruct((n, HY_WIDTH), F32),
        grid=(n // tl,),
        in_specs=[pl.BlockSpec((tl, LANES), lambda i: (i, 0)), pl.BlockSpec((tl, HY_WIDTH), lambda i: (i, 0)),
                  _const_spec(w1p.shape), _const_spec(b1.shape), _const_spec(w2.shape), _const_spec(b2.shape),
                  _const_spec(w3.shape), _const_spec(fr.shape)],
        out_specs=pl.BlockSpec((tl, HY_WIDTH), lambda i: (i, 0)),
        compiler_params=_cparams(("parallel",), 32),
        name="hy_filter",
    )(z2, win2, w1p, b1, w2, b2, w3, fr)


def _hy_spec_kernel(f_ref, a_ref, o_ref):
    o_ref[0] = _dot3(f_ref[0], a_ref[0])


def _hy_spec(f2, a3):
    n1, r, c = a3.shape
    return pl.pallas_call(
        _hy_spec_kernel,
        out_shape=jax.ShapeDtypeStruct(a3.shape, F32),
        grid=(n1,),
        in_specs=[pl.BlockSpec((1, r, r), lambda i: (i, 0, 0)), pl.BlockSpec((1, r, c), lambda i: (i, 0, 0))],
        out_specs=pl.BlockSpec((1, r, c), lambda i: (i, 0, 0)),
        compiler_params=_cparams(("parallel",), 32),
        name="hy_filter_spectrum",
    )(f2, a3)


def _merge_kernel(x_ref, yat_ref, yb_ref, yct_ref, gt_ref, wb_ref, wo_ref, o_ref):
    d = D_MODEL
    ba = _dot_tn(yat_ref[0], wb_ref[0])
    bb = _dot(yb_ref[0], wb_ref[1])
    bc = _dot_tn(yct_ref[0], wb_ref[2])
    merged = (gt_ref[0, :, 0:d].astype(F32) * ba + gt_ref[0, :, d:2 * d].astype(F32) * bb
              + gt_ref[0, :, 2 * d:3 * d].astype(F32) * bc)
    o_ref[0] = x_ref[0] + _dot(merged.astype(BF16), wo_ref[...])


def _merge(x, yat, yb, yct, gates, wb, wo):
    b, l, _ = x.shape
    tm = TOK_TILE

    def tok(w):
        return pl.BlockSpec((1, tm, w), lambda bi, i: (bi, i, 0))

    tr = pl.BlockSpec((1, BRANCH_W, tm), lambda bi, i: (bi, 0, i))
    return pl.pallas_call(
        _merge_kernel,
        out_shape=jax.ShapeDtypeStruct(x.shape, F32),
        grid=(b, l // tm),
        in_specs=[tok(D_MODEL), tr, tok(BRANCH_W), tr, tok(N_BRANCH * D_MODEL),
                  _const_spec(wb.shape), _const_spec(wo.shape)],
        out_specs=tok(D_MODEL),
        compiler_params=_cparams(("parallel", "parallel"), 48),
        name="merge",
    )(x, yat, yb, yct, gates, wb, wo)


def _rope_tables(l):
    rows = l // GRID_W
    row = jnp.repeat(jnp.arange(rows, dtype=F32), GRID_W)
    col = jnp.tile(jnp.arange(GRID_W, dtype=F32), rows)

    def tab(d_rot):
        n_freq = d_rot // 4
        inv = ROPE_THETA ** (-jnp.arange(n_freq, dtype=F32) / n_freq)
        ang = jnp.concatenate([row[:, None] * inv, col[:, None] * inv], axis=-1)
        c = jnp.repeat(jnp.cos(ang), 2, axis=-1)
        s = jnp.repeat(jnp.sin(ang), 2, axis=-1) * jnp.tile(jnp.array([-1.0, 1.0], F32), d_rot // 2)
        return c, s

    ca, sa = tab(GQA_HEAD_DIM)
    cm, sm = tab(MLA_ROPE)
    pad = ((0, 0), (0, LANES - MLA_ROPE))
    return (jnp.tile(ca, (1, 2)), jnp.tile(sa, (1, 2)), jnp.pad(cm, pad), jnp.pad(sm, pad))


def _hy_positions(l):
    r = jnp.arange(l, dtype=jnp.int32)
    pos = jnp.concatenate([r, (l - r) % l]).astype(F32)[:, None]
    live = jnp.concatenate([jnp.ones((l,), F32), (r > 0).astype(F32)])[:, None]
    t = pos * (1.0 / (l - 1))
    w = 2.0 * math.pi * pos / l
    f = jnp.linspace(1e-4, HY_BANDS - 1, HY_BANDS, dtype=F32)[None, :]
    z2 = jnp.concatenate([t, jnp.cos(f * w), -jnp.sin(f * w), jnp.zeros((2 * l, LANES - HY_EMB), F32)], axis=-1)
    max_decay = math.log(HY_TARGET) / HY_FAST
    min_decay = math.log(HY_TARGET) / HY_SLOW
    deltas = jnp.abs(jnp.linspace(min_decay, max_decay, HY_WIDTH, dtype=F32))
    return z2, jnp.exp(-t * deltas[None, :]) * live


def _hy_k1_count(l):
    n1 = 2 * l // DFT_N2
    return -(-(n1 // 2 + 1) // 8) * 8


def _dft_tables(l):
    n = 2 * l
    n2 = DFT_N2
    n1 = n // n2
    n1c = _hy_k1_count(l)
    two_pi = 2.0 * math.pi
    i1 = jnp.arange(n1, dtype=jnp.int32)
    ang1 = ((i1[:n1c, None] * i1[None, :]) % n1).astype(F32) * (two_pi / n1)
    live = (i1[:n1c] <= n1 // 2)[:, None]
    c1, s1 = jnp.where(live, jnp.cos(ang1), 0.0), jnp.where(live, jnp.sin(ang1), 0.0)
    f1_full = jnp.stack([c1, -s1], axis=1).reshape(2 * n1c, n1)
    fold = jnp.where((i1[:n1c] == 0) | (i1[:n1c] == n1 // 2), 1.0, 2.0)[:, None] * (1.0 / n)
    g1 = jnp.stack([(c1 * fold)[:, :n1 // 2].T, (-s1 * fold)[:, :n1 // 2].T], axis=2).reshape(n1 // 2, 2 * n1c)
    i2 = jnp.arange(n2, dtype=jnp.int32)
    kk = i1[:n1c, None, None] + n1 * i2[None, :, None]
    ang2 = ((kk * i2[None, None, :]) % n).astype(F32) * (two_pi / n)
    c2, s2 = jnp.cos(ang2), jnp.sin(ang2)
    f2 = jnp.concatenate([jnp.concatenate([c2, s2], axis=2), jnp.concatenate([-s2, c2], axis=2)], axis=1)
    c2t, s2t = jnp.swapaxes(c2, 1, 2), jnp.swapaxes(s2, 1, 2)
    g2 = jnp.concatenate([jnp.concatenate([c2t, -s2t], axis=2), jnp.concatenate([s2t, c2t], axis=2)], axis=1)
    return f1_full, g1, f2, g2


def _swap_pairs(n):
    return np.arange(n) ^ 1


def _proj_a_columns():
    zero = IN_WIDTH
    cols = np.full((PA_END,), zero, np.int64)
    q0, k0, v0, _, cq0, ckv0, kr0, _ = IN_OFFS[:8]
    for h in range(GQA_HEADS):
        dst = h * LANES + (h // (GQA_HEADS // GQA_KV_HEADS)) * GQA_HEAD_DIM
        src = q0 + h * GQA_HEAD_DIM + np.arange(GQA_HEAD_DIM)
        cols[PA_Q + dst:PA_Q + dst + GQA_HEAD_DIM] = src
        cols[PA_QS + dst:PA_QS + dst + GQA_HEAD_DIM] = q0 + h * GQA_HEAD_DIM + _swap_pairs(GQA_HEAD_DIM)
    cols[PA_K:PA_K + 128] = k0 + np.arange(128)
    cols[PA_KS:PA_KS + 128] = k0 + _swap_pairs(128)
    cols[PA_V:PA_V + 128] = v0 + np.arange(128)
    cols[PA_CQ:PA_CQ + MLA_Q_RANK] = cq0 + np.arange(MLA_Q_RANK)
    cols[PA_CKV:PA_CKV + MLA_KV_RANK] = ckv0 + np.arange(MLA_KV_RANK)
    cols[PA_KR:PA_KR + MLA_ROPE] = kr0 + np.arange(MLA_ROPE)
    cols[PA_KRS:PA_KRS + MLA_ROPE] = kr0 + _swap_pairs(MLA_ROPE)
    return cols


def _mla_q_columns():
    hd = MLA_NOPE + MLA_ROPE
    zero = MLA_HEADS * hd
    main = np.full((MLA_HEADS * MLA_KC,), zero, np.int64)
    swap = np.full((MLA_HEADS * LANES,), zero, np.int64)
    for h in range(MLA_HEADS):
        b0 = h * MLA_KC + (h % 2) * MLA_NOPE
        main[b0:b0 + MLA_NOPE] = h * hd + np.arange(MLA_NOPE)
        r0 = h * MLA_KC + LANES
        main[r0:r0 + MLA_ROPE] = h * hd + MLA_NOPE + np.arange(MLA_ROPE)
        swap[h * LANES:h * LANES + MLA_ROPE] = h * hd + MLA_NOPE + _swap_pairs(MLA_ROPE)
    return main, swap


def _mla_kv_columns():
    hd = MLA_NOPE + MLA_V
    knope = np.concatenate([h * hd + np.arange(MLA_NOPE) for h in range(MLA_HEADS)])
    val = np.concatenate([h * hd + MLA_NOPE + np.arange(MLA_V) for h in range(MLA_HEADS)])
    return np.concatenate([knope, val])


def _take_cols(w, cols):
    wz = jnp.concatenate([w, jnp.zeros((w.shape[0], 1), w.dtype)], axis=1)
    return jnp.take(wz, jnp.asarray(cols, jnp.int32), axis=1).astype(BF16)


def _tile2(g):
    return jnp.tile(g, 2)[None, :]


def _encoder(x, lw, g_final):
    b, l, d = x.shape
    n2 = DFT_N2
    n1 = 2 * l // n2
    c = HY_WIDTH
    tabs = _rope_tables(l)
    z2, win2 = _hy_positions(l)
    f1_full, g1, f2, g2 = _dft_tables(l)
    f1_b = f1_full[:, :n1 // 2].astype(BF16)
    g1_b = g1.astype(BF16)
    f2_b = f2.astype(BF16)
    g2_b = g2.astype(BF16)
    for li, w in enumerate(lw):
        x2 = _ffn(x.reshape(b * l, d), w["g_ffn1"], w["wg1"], w["wu1"], w["wd1"], g_final, False)
        x = x2.reshape(b, l, d)
        qg, kg, vgt, qm, km, vmt, bnd = _proj_a(x, w["g_mix"], w["w_a"], tabs, w["gq"], w["gqs"], w["gk"],
                                                w["gks"], w["gcq"], w["gckv"], w["wuq"], w["wuqs"], w["wukv"])
        hy, gates = _proj_b(x2, w["g_mix"], w["w_b"])
        top = bnd[:, :, :4, 0]
        limit = ATT_SAFE_SCORE ** 2
        safe_a = (top[:, :, 0] * jnp.max(top[:, :, 1], axis=1, keepdims=True) <= limit).astype(jnp.int32)
        safe_m = (top[:, :, 2] * jnp.max(top[:, :, 3], axis=1, keepdims=True) <= limit).astype(jnp.int32)
        per_kv = GQA_HEADS // GQA_KV_HEADS
        yat = _attention(safe_a, qg, kg, vgt, heads=ATT_HEADS, kc=GQA_KC, key_of_head=(0,) * GQA_HEADS,
                         value_of_head=tuple(h // per_kv for h in range(GQA_HEADS)), name="attn_gqa")
        yct = _attention(safe_m, qm, km, vmt, heads=ATT_HEADS, kc=MLA_KC,
                         key_of_head=tuple(h // 2 for h in range(MLA_HEADS)),
                         value_of_head=tuple(range(MLA_HEADS)), name="attn_mla")
        kc_time = _hy_filter(z2, win2, w["w1p"], w["b1"], w["w2"], w["b2"], w["w3"], w["fr"])
        n1c = _hy_k1_count(l)
        ka =_left_mm(f1_full, kc_time.reshape(1, n1, n2 * c), F32, hi=True)
        kf = _hy_spec(f2, ka.reshape(n1c, 2 * n2, c))
        s, x0 = _hy_pre(hy.reshape(b, l, 3 * c), w["w_short"], w["b_short"])
        a = _left_mm(f1_b, s.reshape(b, n1 // 2, n2 * c), BF16)
        bi = _hy_mid(a.reshape(b, n1c, 2 * n2, c), f2_b, kf, g2_b)
        yb = _hy_post(g1_b, bi.reshape(b, 2 * n1c, n2 * c), s.reshape(b, n1 // 2, n2 * c),
                      x0.reshape(b, n1 // 2, n2 * c), w["bias_t"])
        x = _merge(x, yat, yb.reshape(b, l, c), yct, gates.reshape(b, l, N_BRANCH * d), w["wb"], w["wo"])
        x2 = _ffn(x.reshape(b * l, d), w["g_ffn2"], w["wg2"], w["wu2"], w["wd2"], g_final, li == len(lw) - 1)
        x = x2.reshape(b, l, d)
    return x


def kernel(x_prompt, x_sample, g_ffn1, w_ffn1_gate, w_ffn1_up, w_ffn1_down, g_mix, w_in, g_qnorm, g_knorm,
           w_hy_short, b_hy_short, w_hy_f1, b_hy_f1, w_hy_f2, b_hy_f2, w_hy_f3, hy_sin_freq, hy_bias,
           g_mla_q, w_mla_uq, g_mla_kv, w_mla_ukv, w_branch, w_out, g_ffn2, w_ffn2_gate, w_ffn2_up,
           w_ffn2_down, g_final):
    cols_a = _proj_a_columns()
    uq_main, uq_swap = _mla_q_columns()
    ukv_cols = _mla_kv_columns()
    sw64 = _swap_pairs(GQA_HEAD_DIM)
    hy0 = IN_OFFS[3]
    gt0 = IN_OFFS[7]
    lw = []
    for l in range(DEPTH):
        lw.append(dict(
            g_ffn1=g_ffn1[l][None], wg1=w_ffn1_gate[l].astype(BF16), wu1=w_ffn1_up[l].astype(BF16),
            wd1=w_ffn1_down[l].astype(BF16),
            g_ffn2=g_ffn2[l][None], wg2=w_ffn2_gate[l].astype(BF16), wu2=w_ffn2_up[l].astype(BF16),
            wd2=w_ffn2_down[l].astype(BF16),
            g_mix=g_mix[l][None],
            w_a=_take_cols(w_in[l], cols_a),
            w_b=jnp.concatenate([w_in[l][:, hy0:hy0 + 3 * HY_WIDTH], w_in[l][:, gt0:]], axis=1).astype(BF16),
            gq=_tile2(g_qnorm[l]), gqs=_tile2(g_qnorm[l][sw64]),
            gk=_tile2(g_knorm[l]), gks=_tile2(g_knorm[l][sw64]),
            gcq=g_mla_q[l][None], gckv=g_mla_kv[l][None],
            wuq=_take_cols(w_mla_uq[l], uq_main), wuqs=_take_cols(w_mla_uq[l], uq_swap),
            wukv=_take_cols(w_mla_ukv[l], ukv_cols),
            w_short=w_hy_short[l], b_short=b_hy_short[l],
            w1p=jnp.pad(w_hy_f1[l], ((0, LANES - HY_EMB), (0, 0))), b1=b_hy_f1[l][None],
            w2=w_hy_f2[l], b2=b_hy_f2[l][None], w3=w_hy_f3[l], fr=hy_sin_freq[l][None],
            bias_t=jnp.tile(hy_bias[l], HY_COLS // HY_WIDTH)[None],
            wb=w_branch[l].astype(BF16), wo=w_out[l].astype(BF16),
        ))
    gf = g_final[None]
    return _encoder(x_prompt, lw, gf), _encoder(x_sample, lw, gf)
```

```python
import functools
import math

import numpy as np
import jax
import jax.numpy as jnp
from jax import lax
from jax.experimental import pallas as pl
from jax.experimental.pallas import tpu as pltpu

F32 = jnp.float32
BF16 = jnp.bfloat16

D_MODEL = 1024
DEPTH = 2
GRID_W = 64
ROPE_THETA = 10000.0
EPS = 1e-6
D_FF = 2816
N_BRANCH = 3
BRANCH_W = 512
GQA_HEADS = 8
GQA_KV_HEADS = 2
GQA_HEAD_DIM = 64
HY_WIDTH = 512
HY_ORDER = 64
HY_EMB = 33
HY_BANDS = (HY_EMB - 1) // 2
HY_TARGET = 1e-2
HY_FAST = 0.3
HY_SLOW = 1.5
MLA_HEADS = 8
MLA_Q_RANK = 256
MLA_KV_RANK = 128
MLA_NOPE = 64
MLA_ROPE = 32
MLA_V = 64
IN_WIDTHS = (512, 128, 128, 3 * HY_WIDTH, MLA_Q_RANK, MLA_KV_RANK, MLA_ROPE, N_BRANCH * D_MODEL)
IN_OFFS = tuple(int(c) for c in np.cumsum((0,) + IN_WIDTHS))
IN_WIDTH = IN_OFFS[-1]

LANES = 128
V7X_VMEM_BYTES = 64 * 1024 * 1024

TOK_TILE = 512
ATT_TQ = 256
ATT_TQB = 512
ATT_HEADS = 4
ATT_UNROLL = 4
ATT_SAFE_SCORE = 64.0
ATT_TK_GQA = 2 * TOK_TILE
ATT_TK_MLA = TOK_TILE
FF_CHUNKS = ((0, 1024), (1024, 2048), (2048, D_FF))
DFT_N2 = 128
HY_COLS = 8192
ONES_ROWS = 16
NEG_BIG = -1e30
LOG2E = math.log2(math.e)


def _cparams(sem, vmem_mb):
    return pltpu.CompilerParams(dimension_semantics=sem, vmem_limit_bytes=vmem_mb * 1024 * 1024)


def _const_spec(shape):
    nd = len(shape)
    return pl.BlockSpec(shape, lambda *_: (0,) * nd, pipeline_mode=pl.Buffered(1))


def _rms(x, g):
    return x * lax.rsqrt(jnp.mean(x * x, axis=-1, keepdims=True) + EPS) * g


def _dot(a, b):
    return jnp.dot(a, b, preferred_element_type=F32)


def _dot_nt(a, b):
    return lax.dot_general(a, b, (((1,), (1,)), ((), ())), preferred_element_type=F32)


def _dot_tn(a, b):
    return lax.dot_general(a, b, (((0,), (0,)), ((), ())), preferred_element_type=F32)


def _split(a):
    hi = a.astype(BF16)
    return hi, (a - hi.astype(F32)).astype(BF16)


def _dot3(a, b):
    ah, al = _split(a)
    bh, bl = _split(b)
    return _dot(ah, bh) + (_dot(ah, bl) + _dot(al, bh))


def _ffn_kernel(x_ref, g_ref, wg_ref, wu_ref, wd_ref, gf_ref, o_ref, *, final):
    x = x_ref[...]
    xb = _rms(x, g_ref[...]).astype(BF16)
    acc = jnp.zeros_like(x)
    for c0, c1 in FF_CHUNKS:
        gate = _dot(xb, wg_ref[:, c0:c1])
        up = _dot(xb, wu_ref[:, c0:c1])
        h = (gate * jax.nn.sigmoid(gate) * up).astype(BF16)
        acc = acc + _dot(h, wd_ref[c0:c1, :])
    y = x + 0.5 * acc
    if final:
        y = _rms(y, gf_ref[...])
    o_ref[...] = y


def _ffn(x2d, g, wg, wu, wd, g_final, final):
    t = x2d.shape[0]
    row = pl.BlockSpec((TOK_TILE, D_MODEL), lambda i: (i, 0))
    return pl.pallas_call(
        functools.partial(_ffn_kernel, final=final),
        out_shape=jax.ShapeDtypeStruct((t, D_MODEL), F32),
        grid=(t // TOK_TILE,),
        in_specs=[row, _const_spec((1, D_MODEL)), _const_spec((D_MODEL, D_FF)),
                  _const_spec((D_MODEL, D_FF)), _const_spec((D_FF, D_MODEL)), _const_spec((1, D_MODEL))],
        out_specs=row,
        compiler_params=_cparams(("parallel",), 48),
        name="ffn",
    )(x2d, g, wg, wu, wd, g_final)


PA_Q, PA_QS, PA_K, PA_KS, PA_V, PA_CQ, PA_CKV, PA_KR, PA_KRS, PA_END = (
    0, 1024, 2048, 2176, 2304, 2432, 2688, 2816, 2944, 3072)
GQA_KC = 128
MLA_KC = 256


def _sq_norm(a_bf16):
    a = a_bf16.astype(F32)
    return jnp.sum(a * a, axis=-1, keepdims=True)


def _proj_a_kernel(x_ref, g_ref, w_ref, cosa_ref, sina_ref, cosm_ref, sinm_ref,
                   gq_ref, gqs_ref, gk_ref, gks_ref, gcq_ref, gckv_ref,
                   wuq_ref, wuqs_ref, wukv_ref,
                   qg_ref, kg_ref, vgt_ref, qm_ref, km_ref, vmt_ref, bnd_ref):
    ub = _rms(x_ref[0], g_ref[...]).astype(BF16)
    cosa, sina = cosa_ref[...], sina_ref[...]
    cosm, sinm = cosm_ref[...], sinm_ref[...]

    zq = _dot(ub, w_ref[:, PA_Q:PA_QS])
    zqs = _dot(ub, w_ref[:, PA_QS:PA_K])
    tq_c = gq_ref[...] * cosa
    tq_s = gqs_ref[...] * sina
    scale_a = GQA_HEAD_DIM ** -0.5 * LOG2E
    qa_n2 = None
    for h in range(GQA_HEADS):
        a = zq[:, h * LANES:(h + 1) * LANES]
        a_sw = zqs[:, h * LANES:(h + 1) * LANES]
        r = lax.rsqrt(jnp.sum(a * a, axis=-1, keepdims=True) * (1.0 / GQA_HEAD_DIM) + EPS)
        qb = ((a * tq_c + a_sw * tq_s) * (r * scale_a)).astype(BF16)
        qg_ref[0, :, h * LANES:(h + 1) * LANES] = qb
        qa_n2 = _sq_norm(qb) if qa_n2 is None else jnp.maximum(qa_n2, _sq_norm(qb))

    zk = _dot(ub, w_ref[:, PA_K:PA_KS])
    zks = _dot(ub, w_ref[:, PA_KS:PA_V])
    first = lax.broadcasted_iota(jnp.int32, zk.shape, 1) < GQA_HEAD_DIM
    sq = zk * zk
    r0 = lax.rsqrt(jnp.sum(jnp.where(first, sq, 0.0), axis=-1, keepdims=True) * (1.0 / GQA_HEAD_DIM) + EPS)
    r1 = lax.rsqrt(jnp.sum(jnp.where(first, 0.0, sq), axis=-1, keepdims=True) * (1.0 / GQA_HEAD_DIM) + EPS)
    kb = ((zk * (gk_ref[...] * cosa) + zks * (gks_ref[...] * sina)) * jnp.where(first, r0, r1)).astype(BF16)
    kg_ref[0] = kb
    kb32 = kb.astype(F32)
    kb_sq = kb32 * kb32
    ka_n2 = jnp.maximum(jnp.sum(jnp.where(first, kb_sq, 0.0), axis=-1, keepdims=True),
                        jnp.sum(jnp.where(first, 0.0, kb_sq), axis=-1, keepdims=True))

    vgt_ref[0, 0] = _dot(ub, w_ref[:, PA_V:PA_CQ]).T.astype(BF16)

    scale_m = (MLA_NOPE + MLA_ROPE) ** -0.5 * LOG2E
    cqn = _rms(_dot(ub, w_ref[:, PA_CQ:PA_CKV]), gcq_ref[...]).astype(BF16)
    zq2 = _dot(cqn, wuq_ref[...])
    zq2s = _dot(cqn, wuqs_ref[...])
    qm_n2 = None
    for h in range(MLA_HEADS):
        b0 = h * MLA_KC
        nope = (zq2[:, b0:b0 + LANES] * scale_m).astype(BF16)
        rope = zq2[:, b0 + LANES:b0 + 2 * LANES] * cosm + zq2s[:, h * LANES:(h + 1) * LANES] * sinm
        rope = (rope * scale_m).astype(BF16)
        qm_ref[0, :, b0:b0 + LANES] = nope
        qm_ref[0, :, b0 + LANES:b0 + 2 * LANES] = rope
        n2 = _sq_norm(nope) + _sq_norm(rope)
        qm_n2 = n2 if qm_n2 is None else jnp.maximum(qm_n2, n2)

    ckvn = _rms(_dot(ub, w_ref[:, PA_CKV:PA_KR]), gckv_ref[...]).astype(BF16)
    zkv = _dot(ckvn, wukv_ref[...])
    krope = (_dot(ub, w_ref[:, PA_KR:PA_KRS]) * cosm + _dot(ub, w_ref[:, PA_KRS:PA_END]) * sinm).astype(BF16)
    km_n2 = None
    for p in range(MLA_HEADS // 2):
        b0 = p * MLA_KC
        nope = zkv[:, p * LANES:(p + 1) * LANES].astype(BF16)
        km_ref[0, :, b0:b0 + LANES] = nope
        km_ref[0, :, b0 + LANES:b0 + 2 * LANES] = krope
        km_n2 = _sq_norm(nope) if km_n2 is None else jnp.maximum(km_n2, _sq_norm(nope))
    km_n2 = km_n2 + _sq_norm(krope)
    vmt_ref[0, 0] = zkv[:, MLA_HEADS * MLA_NOPE:].T.astype(BF16)

    row = lax.broadcasted_iota(jnp.int32, (8, LANES), 0)
    tops = [jnp.max(v, axis=0, keepdims=True) for v in (qa_n2, ka_n2, qm_n2, km_n2)]
    bnd_ref[0, 0] = jnp.where(row == 0, tops[0], jnp.where(row == 1, tops[1], jnp.where(row == 2, tops[2], tops[3])))


def _proj_a(x, g_mix, w_a, tabs, gq, gqs, gk, gks, gcq, gckv, wuq, wuqs, wukv):
    b, l, _ = x.shape
    tm = TOK_TILE
    nt = l // tm
    cosa, sina, cosm, sinm = tabs
    tab = pl.BlockSpec((tm, LANES), lambda bi, i: (i, 0))

    def tok(w):
        return pl.BlockSpec((1, tm, w), lambda bi, i: (bi, i, 0))

    def tr(rows):
        return pl.BlockSpec((1, 1, rows, tm), lambda bi, i: (bi, i, 0, 0))

    out_shape = (
        jax.ShapeDtypeStruct((b, l, GQA_HEADS * GQA_KC), BF16),
        jax.ShapeDtypeStruct((b, l, GQA_KC), BF16),
        jax.ShapeDtypeStruct((b, nt, GQA_KV_HEADS * GQA_HEAD_DIM, tm), BF16),
        jax.ShapeDtypeStruct((b, l, MLA_HEADS * MLA_KC), BF16),
        jax.ShapeDtypeStruct((b, l, (MLA_HEADS // 2) * MLA_KC), BF16),
        jax.ShapeDtypeStruct((b, nt, MLA_HEADS * MLA_V, tm), BF16),
        jax.ShapeDtypeStruct((b, nt, 8, LANES), F32),
    )
    return pl.pallas_call(
        _proj_a_kernel,
        out_shape=out_shape,
        grid=(b, nt),
        in_specs=[tok(D_MODEL), _const_spec((1, D_MODEL)), _const_spec(w_a.shape), tab, tab, tab, tab,
                  _const_spec((1, LANES)), _const_spec((1, LANES)), _const_spec((1, LANES)),
                  _const_spec((1, LANES)), _const_spec((1, MLA_Q_RANK)), _const_spec((1, MLA_KV_RANK)),
                  _const_spec(wuq.shape), _const_spec(wuqs.shape), _const_spec(wukv.shape)],
        out_specs=(tok(GQA_HEADS * GQA_KC), tok(GQA_KC), tr(GQA_KV_HEADS * GQA_HEAD_DIM),
                   tok(MLA_HEADS * MLA_KC), tok((MLA_HEADS // 2) * MLA_KC), tr(MLA_HEADS * MLA_V),
                   pl.BlockSpec((1, 1, 8, LANES), lambda bi, i: (bi, i, 0, 0))),
        compiler_params=_cparams(("parallel", "parallel"), 48),
        name="proj_a",
    )(x, g_mix, w_a, cosa, sina, cosm, sinm, gq, gqs, gk, gks, gcq, gckv, wuq, wuqs, wukv)


def _proj_b_kernel(x_ref, g_ref, w_ref, hy_ref, gt_ref):
    ub = _rms(x_ref[...], g_ref[...]).astype(BF16)
    hy_ref[...] = _dot(ub, w_ref[:, :3 * HY_WIDTH]).astype(BF16)
    gt_ref[...] = jax.nn.sigmoid(_dot(ub, w_ref[:, 3 * HY_WIDTH:])).astype(BF16)


def _proj_b(x2d, g_mix, w_b):
    t = x2d.shape[0]
    tm = TOK_TILE

    def row(w):
        return pl.BlockSpec((tm, w), lambda i: (i, 0))

    return pl.pallas_call(
        _proj_b_kernel,
        out_shape=(jax.ShapeDtypeStruct((t, 3 * HY_WIDTH), BF16),
                   jax.ShapeDtypeStruct((t, N_BRANCH * D_MODEL), BF16)),
        grid=(t // tm,),
        in_specs=[row(D_MODEL), _const_spec((1, D_MODEL)), _const_spec(w_b.shape)],
        out_specs=(row(3 * HY_WIDTH), row(N_BRANCH * D_MODEL)),
        compiler_params=_cparams(("parallel",), 48),
        name="proj_b",
    )(x2d, g_mix, w_b)


def _attn_kernel(safe_ref, q_ref, k_ref, vt_ref, o_ref, acc_scr, l_scr, p0_scr, p1_scr,
                 *, heads, groups, kc, key_of_head, value_of_head, nk, nk_tile):
    tv = vt_ref.shape[3]
    tk = nk_tile
    tq = ATT_TQ
    ones = jnp.ones((ONES_ROWS, tk), BF16)
    dv = MLA_V
    n_sub = q_ref.shape[1] // tq
    safe = safe_ref[pl.program_id(0), pl.program_id(1)] != 0

    def key_tile(j, h):
        kb = key_of_head[h]
        rows = j * tk if isinstance(j, int) else pl.multiple_of(j * tk, tk)
        return k_ref[0, pl.ds(rows, tk), kb * kc:(kb + 1) * kc]

    def values(j, h):
        vb = value_of_head[h]
        per = tk // tv
        return jnp.concatenate([vt_ref[0, j * per + u, vb * dv:(vb + 1) * dv, :] for u in range(per)], axis=1)

    def value_tile(j, h):
        return jnp.concatenate([values(j, h), ones], axis=0)

    def write_out(acc, l, h, sub):
        o_ref[0, h * dv:(h + 1) * dv, sub * tq:(sub + 1) * tq] = (acc / l).astype(BF16)

    @pl.when(safe)
    def _():
        bufs = (p0_scr, p1_scr)
        streams = [(sub, g) for sub in range(n_sub) for g in range(groups)]

        def probs(si, stream, j, hh, buf):
            sub, g = stream
            h = g * heads + hh
            st = _dot_nt(key_tile(j, h), q_ref[0, sub * tq:(sub + 1) * tq, h * kc:(h + 1) * kc])
            e = jnp.exp2(st)
            buf[hh] = e.astype(BF16)
            l_scr[si % 2, hh] += jnp.sum(e, axis=0, keepdims=True)

        def accumulate(si, stream, j, hh, buf):
            acc_scr[si % 2, hh] += _dot(values(j, stream[1] * heads + hh), buf[hh])

        acc_scr[...] = jnp.zeros(acc_scr.shape, F32)
        l_scr[...] = jnp.zeros(l_scr.shape, F32)
        for hh in range(heads):
            probs(0, streams[0], 0, hh, bufs[0])
        for si, stream in enumerate(streams):
            def step(j, cur, nxt, si=si, stream=stream):
                for hh in range(heads):
                    probs(si, stream, j + 1, hh, nxt)
                    accumulate(si, stream, j, hh, cur)

            unroll = min(ATT_UNROLL, nk)

            def body(jj, carry, step=step):
                for u in range(unroll):
                    step(unroll * jj + u, bufs[u % 2], bufs[1 - u % 2])
                return carry

            lax.fori_loop(0, nk // unroll - 1, body, 0)
            for u in range(unroll - 1):
                step(nk - unroll + u, bufs[u % 2], bufs[1 - u % 2])
            for hh in range(heads):
                if si + 1 < len(streams):
                    probs(si + 1, streams[si + 1], 0, hh, bufs[0])
                accumulate(si, stream, nk - 1, hh, bufs[1])
            sub, g = stream
            for hh in range(heads):
                write_out(acc_scr[si % 2, hh], l_scr[si % 2, hh], g * heads + hh, sub)
                if si + 2 < len(streams):
                    acc_scr[si % 2, hh] = jnp.zeros((dv, tq), F32)
                    l_scr[si % 2, hh] = jnp.zeros((1, tq), F32)

    @pl.when(jnp.logical_not(safe))
    def _():
        for h in range(groups * heads):
            for sub in range(n_sub):
                qh = q_ref[0, sub * tq:(sub + 1) * tq, h * kc:(h + 1) * kc]

                def body(j, carry, qh=qh, h=h):
                    m, acc = carry
                    st = _dot_nt(key_tile(j, h), qh)
                    m_new = jnp.maximum(m, jnp.max(st, axis=0, keepdims=True))
                    p = jnp.exp2(st - m_new).astype(BF16)
                    return m_new, jnp.exp2(m - m_new) * acc + _dot(value_tile(j, h), p)

                init = (jnp.full((1, tq), NEG_BIG, F32), jnp.zeros((dv + ONES_ROWS, tq), F32))
                _, acc = lax.fori_loop(0, nk, body, init)
                write_out(acc[:dv], acc[dv:dv + 1], h, sub)


def _attention(safe, q, k, vt, *, heads, kc, tk, key_of_head, value_of_head, name):
    b, l, qw = q.shape
    n_heads = qw // kc
    nv, tv = vt.shape[1], vt.shape[3]
    nk = l // tk
    tqb = ATT_TQB
    tq = ATT_TQ
    dv = MLA_V
    grid_spec = pltpu.PrefetchScalarGridSpec(
        num_scalar_prefetch=1,
        grid=(b, l // tqb),
        in_specs=[pl.BlockSpec((1, tqb, qw), lambda bi, i, s: (bi, i, 0)),
                  pl.BlockSpec((1, l, k.shape[2]), lambda bi, i, s: (bi, 0, 0), pipeline_mode=pl.Buffered(1)),
                  pl.BlockSpec((1, nv, vt.shape[2], tv), lambda bi, i, s: (bi, 0, 0, 0),
                               pipeline_mode=pl.Buffered(1))],
        out_specs=pl.BlockSpec((1, n_heads * dv, tqb), lambda bi, i, s: (bi, 0, i)),
        scratch_shapes=[pltpu.VMEM((2, heads, dv, tq), F32), pltpu.VMEM((2, heads, 1, tq), F32),
                        pltpu.VMEM((heads, tk, tq), BF16), pltpu.VMEM((heads, tk, tq), BF16)],
    )
    return pl.pallas_call(
        functools.partial(_attn_kernel, heads=heads, groups=n_heads // heads, kc=kc, key_of_head=key_of_head,
                          value_of_head=value_of_head, nk=nk, nk_tile=tk),
        out_shape=jax.ShapeDtypeStruct((b, n_heads * dv, l), BF16),
        grid_spec=grid_spec,
        compiler_params=_cparams(("parallel", "parallel"), 48),
        name=name,
    )(safe, q, k, vt)


def _hy_pre_kernel(x0_ref, x1_ref, v_ref, w_ref, b_ref, s_ref, x0o_ref):
    rows = x0_ref.shape[1]
    t = lax.broadcasted_iota(jnp.int32, (rows, LANES), 0)
    not_first = t > 0
    not_last = t < rows - 1

    def conv(ref, j):
        a = ref[0].astype(F32)
        prev = jnp.where(not_first, pltpu.roll(a, 1, 0), 0.0)
        nxt = jnp.where(not_last, pltpu.roll(a, rows - 1, 0), 0.0)
        return prev * w_ref[0, j] + a * w_ref[1, j] + nxt * w_ref[2, j] + b_ref[j]

    x0o_ref[0] = conv(x0_ref, 0).astype(BF16)
    s_ref[0] = (conv(x1_ref, 1) * conv(v_ref, 2)).astype(BF16)


def _hy_pre(hy, w_short, b_short):
    b, l, _ = hy.shape
    nc = HY_WIDTH // LANES
    w4 = w_short.reshape(3, 3, nc, 1, LANES)
    b4 = b_short.reshape(3, nc, 1, LANES)

    def seg(j):
        return pl.BlockSpec((1, l, LANES), lambda bi, c, j=j: (bi, 0, j * nc + c))

    out = pl.BlockSpec((1, l, LANES), lambda bi, c: (bi, 0, c))
    return pl.pallas_call(
        _hy_pre_kernel,
        out_shape=(jax.ShapeDtypeStruct((b, l, HY_WIDTH), BF16), jax.ShapeDtypeStruct((b, l, HY_WIDTH), BF16)),
        grid=(b, nc),
        in_specs=[seg(0), seg(1), seg(2),
                  pl.BlockSpec((3, 3, None, 1, LANES), lambda bi, c: (0, 0, c, 0, 0)),
                  pl.BlockSpec((3, None, 1, LANES), lambda bi, c: (0, c, 0, 0))],
        out_specs=(out, out),
        compiler_params=_cparams(("parallel", "parallel"), 48),
        name="hy_pre",
    )(hy, hy, hy, w4, b4)


def _left_mm_kernel(f_ref, x_ref, o_ref, *, hi):
    if hi:
        o_ref[0] = _dot3(f_ref[...], x_ref[0]).astype(o_ref.dtype)
    else:
        o_ref[0] = _dot(f_ref[...], x_ref[0]).astype(o_ref.dtype)


def _left_mm(f, x, out_dtype, hi=False):
    b, k, n = x.shape
    m = f.shape[0]
    cb = min(HY_COLS, n)
    return pl.pallas_call(
        functools.partial(_left_mm_kernel, hi=hi),
        out_shape=jax.ShapeDtypeStruct((b, m, n), out_dtype),
        grid=(b, n // cb),
        in_specs=[_const_spec((m, k)), pl.BlockSpec((1, k, cb), lambda bi, c: (bi, 0, c))],
        out_specs=pl.BlockSpec((1, m, cb), lambda bi, c: (bi, 0, c)),
        compiler_params=_cparams(("parallel", "parallel"), 48),
        name="hy_outer_dft",
    )(f, x)


def _hy_mid_kernel(f_ref, kf_ref, g_ref, a_ref, o_ref):
    n2 = DFT_N2
    kr, ki = kf_ref[0, :n2], kf_ref[0, n2:]
    for bi in range(a_ref.shape[0]):
        x = _dot(f_ref[0], a_ref[bi, 0])
        xr, xi = x[:n2], x[n2:]
        y = jnp.concatenate([xr * kr - xi * ki, xr * ki + xi * kr], axis=0).astype(BF16)
        o_ref[bi, 0] = _dot(g_ref[0], y).astype(BF16)


def _hy_mid(a4, f2, kf, g2):
    b, n1, r, c = a4.shape
    mat = pl.BlockSpec((1, r, r), lambda i: (i, 0, 0))
    dat = pl.BlockSpec((b, 1, r, c), lambda i: (0, i, 0, 0))
    return pl.pallas_call(
        _hy_mid_kernel,
        out_shape=jax.ShapeDtypeStruct(a4.shape, BF16),
        grid=(n1,),
        in_specs=[mat, pl.BlockSpec((1, r, c), lambda i: (i, 0, 0)), mat, dat],
        out_specs=dat,
        compiler_params=_cparams(("parallel",), 48),
        name="hy_mid",
    )(f2, kf, g2, a4)


def _hy_post_kernel(g_ref, bi_ref, s_ref, x0_ref, bias_ref, o_ref):
    y = _dot(g_ref[...], bi_ref[0]) + s_ref[0].astype(F32) * bias_ref[...]
    o_ref[0] = (x0_ref[0].astype(F32) * y).astype(BF16)


def _hy_post(g1, bi2, s2, x02, bias_t):
    b, k, n = bi2.shape
    m = g1.shape[0]
    cb = HY_COLS
    dat = pl.BlockSpec((1, m, cb), lambda bi, c: (bi, 0, c))
    return pl.pallas_call(
        _hy_post_kernel,
        out_shape=jax.ShapeDtypeStruct((b, m, n), BF16),
        grid=(b, n // cb),
        in_specs=[_const_spec((m, k)), pl.BlockSpec((1, k, cb), lambda bi, c: (bi, 0, c)), dat, dat,
                  _const_spec((1, cb))],
        out_specs=dat,
        compiler_params=_cparams(("parallel", "parallel"), 48),
        name="hy_post",
    )(g1, bi2, s2, x02, bias_t)


def _hy_filter_kernel(z_ref, win_ref, w1_ref, b1_ref, w2_ref, b2_ref, w3_ref, fr_ref, o_ref, *, half_tiles):
    fr = fr_ref[...]
    h = jnp.sin(fr * (_dot3(z_ref[...], w1_ref[...]) + b1_ref[...]))
    h = jnp.sin(fr * (_dot3(h, w2_ref[...]) + b2_ref[...]))
    h3 = _dot3(h, w3_ref[...])
    backward = pl.program_id(0) >= half_tiles
    o_ref[...] = jnp.where(backward, h3[:, HY_WIDTH:], h3[:, :HY_WIDTH]) * win_ref[...]


def _hy_filter(z2, win2, w1p, b1, w2, b2, w3, fr):
    n = z2.shape[0]
    tl = 512
    return pl.pallas_call(
        functools.partial(_hy_filter_kernel, half_tiles=n // (2 * tl)),
        out_shape=jax.ShapeDtypeStruct((n, HY_WIDTH), F32),
        grid=(n // tl,),
        in_specs=[pl.BlockSpec((tl, LANES), lambda i: (i, 0)), pl.BlockSpec((tl, HY_WIDTH), lambda i: (i, 0)),
                  _const_spec(w1p.shape), _const_spec(b1.shape), _const_spec(w2.shape), _const_spec(b2.shape),
                  _const_spec(w3.shape), _const_spec(fr.shape)],
        out_specs=pl.BlockSpec((tl, HY_WIDTH), lambda i: (i, 0)),
        compiler_params=_cparams(("parallel",), 32),
        name="hy_filter",
    )(z2, win2, w1p, b1, w2, b2, w3, fr)


def _hy_spec_kernel(f_ref, a_ref, o_ref):
    o_ref[0] = _dot3(f_ref[0], a_ref[0])


def _hy_spec(f2, a3):
    n1, r, c = a3.shape
    return pl.pallas_call(
        _hy_spec_kernel,
        out_shape=jax.ShapeDtypeStruct(a3.shape, F32),
        grid=(n1,),
        in_specs=[pl.BlockSpec((1, r, r), lambda i: (i, 0, 0)), pl.BlockSpec((1, r, c), lambda i: (i, 0, 0))],
        out_specs=pl.BlockSpec((1, r, c), lambda i: (i, 0, 0)),
        compiler_params=_cparams(("parallel",), 32),
        name="hy_filter_spectrum",
    )(f2, a3)


def _merge_kernel(x_ref, yat_ref, yb_ref, yct_ref, gt_ref, wb_ref, wo_ref, o_ref):
    d = D_MODEL
    ba = _dot_tn(yat_ref[0], wb_ref[0])
    bb = _dot(yb_ref[0], wb_ref[1])
    bc = _dot_tn(yct_ref[0], wb_ref[2])
    merged = (gt_ref[0, :, 0:d].astype(F32) * ba + gt_ref[0, :, d:2 * d].astype(F32) * bb
              + gt_ref[0, :, 2 * d:3 * d].astype(F32) * bc)
    o_ref[0] = x_ref[0] + _dot(merged.astype(BF16), wo_ref[...])


def _merge(x, yat, yb, yct, gates, wb, wo):
    b, l, _ = x.shape
    tm = TOK_TILE

    def tok(w):
        return pl.BlockSpec((1, tm, w), lambda bi, i: (bi, i, 0))

    tr = pl.BlockSpec((1, BRANCH_W, tm), lambda bi, i: (bi, 0, i))
    return pl.pallas_call(
        _merge_kernel,
        out_shape=jax.ShapeDtypeStruct(x.shape, F32),
        grid=(b, l // tm),
        in_specs=[tok(D_MODEL), tr, tok(BRANCH_W), tr, tok(N_BRANCH * D_MODEL),
                  _const_spec(wb.shape), _const_spec(wo.shape)],
        out_specs=tok(D_MODEL),
        compiler_params=_cparams(("parallel", "parallel"), 48),
        name="merge",
    )(x, yat, yb, yct, gates, wb, wo)


def _rope_tables(l):
    rows = l // GRID_W
    row = jnp.repeat(jnp.arange(rows, dtype=F32), GRID_W)
    col = jnp.tile(jnp.arange(GRID_W, dtype=F32), rows)

    def tab(d_rot):
        n_freq = d_rot // 4
        inv = ROPE_THETA ** (-jnp.arange(n_freq, dtype=F32) / n_freq)
        ang = jnp.concatenate([row[:, None] * inv, col[:, None] * inv], axis=-1)
        c = jnp.repeat(jnp.cos(ang), 2, axis=-1)
        s = jnp.repeat(jnp.sin(ang), 2, axis=-1) * jnp.tile(jnp.array([-1.0, 1.0], F32), d_rot // 2)
        return c, s

    ca, sa = tab(GQA_HEAD_DIM)
    cm, sm = tab(MLA_ROPE)
    pad = ((0, 0), (0, LANES - MLA_ROPE))
    return (jnp.tile(ca, (1, 2)), jnp.tile(sa, (1, 2)), jnp.pad(cm, pad), jnp.pad(sm, pad))


def _hy_positions(l):
    r = jnp.arange(l, dtype=jnp.int32)
    pos = jnp.concatenate([r, (l - r) % l]).astype(F32)[:, None]
    live = jnp.concatenate([jnp.ones((l,), F32), (r > 0).astype(F32)])[:, None]
    t = pos * (1.0 / (l - 1))
    w = 2.0 * math.pi * pos / l
    f = jnp.linspace(1e-4, HY_BANDS - 1, HY_BANDS, dtype=F32)[None, :]
    z2 = jnp.concatenate([t, jnp.cos(f * w), -jnp.sin(f * w), jnp.zeros((2 * l, LANES - HY_EMB), F32)], axis=-1)
    max_decay = math.log(HY_TARGET) / HY_FAST
    min_decay = math.log(HY_TARGET) / HY_SLOW
    deltas = jnp.abs(jnp.linspace(min_decay, max_decay, HY_WIDTH, dtype=F32))
    return z2, jnp.exp(-t * deltas[None, :]) * live


def _hy_k1_count(l):
    n1 = 2 * l // DFT_N2
    return -(-(n1 // 2 + 1) // 8) * 8


def _dft_tables(l):
    n = 2 * l
    n2 = DFT_N2
    n1 = n // n2
    n1c = _hy_k1_count(l)
    two_pi = 2.0 * math.pi
    i1 = jnp.arange(n1, dtype=jnp.int32)
    ang1 = ((i1[:n1c, None] * i1[None, :]) % n1).astype(F32) * (two_pi / n1)
    live = (i1[:n1c] <= n1 // 2)[:, None]
    c1, s1 = jnp.where(live, jnp.cos(ang1), 0.0), jnp.where(live, jnp.sin(ang1), 0.0)
    f1_full = jnp.stack([c1, -s1], axis=1).reshape(2 * n1c, n1)
    fold = jnp.where((i1[:n1c] == 0) | (i1[:n1c] == n1 // 2), 1.0, 2.0)[:, None] * (1.0 / n)
    g1 = jnp.stack([(c1 * fold)[:, :n1 // 2].T, (-s1 * fold)[:, :n1 // 2].T], axis=2).reshape(n1 // 2, 2 * n1c)
    i2 = jnp.arange(n2, dtype=jnp.int32)
    kk = i1[:n1c, None, None] + n1 * i2[None, :, None]
    ang2 = ((kk * i2[None, None, :]) % n).astype(F32) * (two_pi / n)
    c2, s2 = jnp.cos(ang2), jnp.sin(ang2)
    f2 = jnp.concatenate([jnp.concatenate([c2, s2], axis=2), jnp.concatenate([-s2, c2], axis=2)], axis=1)
    c2t, s2t = jnp.swapaxes(c2, 1, 2), jnp.swapaxes(s2, 1, 2)
    g2 = jnp.concatenate([jnp.concatenate([c2t, -s2t], axis=2), jnp.concatenate([s2t, c2t], axis=2)], axis=1)
    return f1_full, g1, f2, g2


def _swap_pairs(n):
    return np.arange(n) ^ 1


def _proj_a_columns():
    zero = IN_WIDTH
    cols = np.full((PA_END,), zero, np.int64)
    q0, k0, v0, _, cq0, ckv0, kr0, _ = IN_OFFS[:8]
    for h in range(GQA_HEADS):
        dst = h * LANES + (h // (GQA_HEADS // GQA_KV_HEADS)) * GQA_HEAD_DIM
        src = q0 + h * GQA_HEAD_DIM + np.arange(GQA_HEAD_DIM)
        cols[PA_Q + dst:PA_Q + dst + GQA_HEAD_DIM] = src
        cols[PA_QS + dst:PA_QS + dst + GQA_HEAD_DIM] = q0 + h * GQA_HEAD_DIM + _swap_pairs(GQA_HEAD_DIM)
    cols[PA_K:PA_K + 128] = k0 + np.arange(128)
    cols[PA_KS:PA_KS + 128] = k0 + _swap_pairs(128)
    cols[PA_V:PA_V + 128] = v0 + np.arange(128)
    cols[PA_CQ:PA_CQ + MLA_Q_RANK] = cq0 + np.arange(MLA_Q_RANK)
    cols[PA_CKV:PA_CKV + MLA_KV_RANK] = ckv0 + np.arange(MLA_KV_RANK)
    cols[PA_KR:PA_KR + MLA_ROPE] = kr0 + np.arange(MLA_ROPE)
    cols[PA_KRS:PA_KRS + MLA_ROPE] = kr0 + _swap_pairs(MLA_ROPE)
    return cols


def _mla_q_columns():
    hd = MLA_NOPE + MLA_ROPE
    zero = MLA_HEADS * hd
    main = np.full((MLA_HEADS * MLA_KC,), zero, np.int64)
    swap = np.full((MLA_HEADS * LANES,), zero, np.int64)
    for h in range(MLA_HEADS):
        b0 = h * MLA_KC + (h % 2) * MLA_NOPE
        main[b0:b0 + MLA_NOPE] = h * hd + np.arange(MLA_NOPE)
        r0 = h * MLA_KC + LANES
        main[r0:r0 + MLA_ROPE] = h * hd + MLA_NOPE + np.arange(MLA_ROPE)
        swap[h * LANES:h * LANES + MLA_ROPE] = h * hd + MLA_NOPE + _swap_pairs(MLA_ROPE)
    return main, swap


def _mla_kv_columns():
    hd = MLA_NOPE + MLA_V
    knope = np.concatenate([h * hd + np.arange(MLA_NOPE) for h in range(MLA_HEADS)])
    val = np.concatenate([h * hd + MLA_NOPE + np.arange(MLA_V) for h in range(MLA_HEADS)])
    return np.concatenate([knope, val])


def _take_cols(w, cols):
    wz = jnp.concatenate([w, jnp.zeros((w.shape[0], 1), w.dtype)], axis=1)
    return jnp.take(wz, jnp.asarray(cols, jnp.int32), axis=1).astype(BF16)


def _tile2(g):
    return jnp.tile(g, 2)[None, :]


def _encoder(x, lw, g_final):
    b, l, d = x.shape
    n2 = DFT_N2
    n1 = 2 * l // n2
    c = HY_WIDTH
    tabs = _rope_tables(l)
    z2, win2 = _hy_positions(l)
    f1_full, g1, f2, g2 = _dft_tables(l)
    f1_b = f1_full[:, :n1 // 2].astype(BF16)
    g1_b = g1.astype(BF16)
    f2_b = f2.astype(BF16)
    g2_b = g2.astype(BF16)
    for li, w in enumerate(lw):
        x2 = _ffn(x.reshape(b * l, d), w["g_ffn1"], w["wg1"], w["wu1"], w["wd1"], g_final, False)
        x = x2.reshape(b, l, d)
        qg, kg, vgt, qm, km, vmt, bnd = _proj_a(x, w["g_mix"], w["w_a"], tabs, w["gq"], w["gqs"], w["gk"],
                                                w["gks"], w["gcq"], w["gckv"], w["wuq"], w["wuqs"], w["wukv"])
        hy, gates = _proj_b(x2, w["g_mix"], w["w_b"])
        top = bnd[:, :, :4, 0]
        limit = ATT_SAFE_SCORE ** 2
        safe_a = (top[:, :, 0] * jnp.max(top[:, :, 1], axis=1, keepdims=True) <= limit).astype(jnp.int32)
        safe_m = (top[:, :, 2] * jnp.max(top[:, :, 3], axis=1, keepdims=True) <= limit).astype(jnp.int32)
        per_kv = GQA_HEADS // GQA_KV_HEADS
        yat = _attention(safe_a, qg, kg, vgt, heads=ATT_HEADS, kc=GQA_KC, tk=ATT_TK_GQA,
                         key_of_head=(0,) * GQA_HEADS,
                         value_of_head=tuple(h // per_kv for h in range(GQA_HEADS)), name="attn_gqa")
        yct = _attention(safe_m, qm, km, vmt, heads=ATT_HEADS, kc=MLA_KC, tk=ATT_TK_MLA,
                         key_of_head=tuple(h // 2 for h in range(MLA_HEADS)),
                         value_of_head=tuple(range(MLA_HEADS)), name="attn_mla")
        kc_time = _hy_filter(z2, win2, w["w1p"], w["b1"], w["w2"], w["b2"], w["w3"], w["fr"])
        n1c = _hy_k1_count(l)
        ka =_left_mm(f1_full, kc_time.reshape(1, n1, n2 * c), F32, hi=True)
        kf = _hy_spec(f2, ka.reshape(n1c, 2 * n2, c))
        s, x0 = _hy_pre(hy.reshape(b, l, 3 * c), w["w_short"], w["b_short"])
        a = _left_mm(f1_b, s.reshape(b, n1 // 2, n2 * c), BF16)
        bi = _hy_mid(a.reshape(b, n1c, 2 * n2, c), f2_b, kf, g2_b)
        yb = _hy_post(g1_b, bi.reshape(b, 2 * n1c, n2 * c), s.reshape(b, n1 // 2, n2 * c),
                      x0.reshape(b, n1 // 2, n2 * c), w["bias_t"])
        x = _merge(x, yat, yb.reshape(b, l, c), yct, gates.reshape(b, l, N_BRANCH * d), w["wb"], w["wo"])
        x2 = _ffn(x.reshape(b * l, d), w["g_ffn2"], w["wg2"], w["wu2"], w["wd2"], g_final, li == len(lw) - 1)
        x = x2.reshape(b, l, d)
    return x


def kernel(x_prompt, x_sample, g_ffn1, w_ffn1_gate, w_ffn1_up, w_ffn1_down, g_mix, w_in, g_qnorm, g_knorm,
           w_hy_short, b_hy_short, w_hy_f1, b_hy_f1, w_hy_f2, b_hy_f2, w_hy_f3, hy_sin_freq, hy_bias,
           g_mla_q, w_mla_uq, g_mla_kv, w_mla_ukv, w_branch, w_out, g_ffn2, w_ffn2_gate, w_ffn2_up,
           w_ffn2_down, g_final):
    cols_a = _proj_a_columns()
    uq_main, uq_swap = _mla_q_columns()
    ukv_cols = _mla_kv_columns()
    sw64 = _swap_pairs(GQA_HEAD_DIM)
    hy0 = IN_OFFS[3]
    gt0 = IN_OFFS[7]
    lw = []
    for l in range(DEPTH):
        lw.append(dict(
            g_ffn1=g_ffn1[l][None], wg1=w_ffn1_gate[l].astype(BF16), wu1=w_ffn1_up[l].astype(BF16),
            wd1=w_ffn1_down[l].astype(BF16),
            g_ffn2=g_ffn2[l][None], wg2=w_ffn2_gate[l].astype(BF16), wu2=w_ffn2_up[l].astype(BF16),
            wd2=w_ffn2_down[l].astype(BF16),
            g_mix=g_mix[l][None],
            w_a=_take_cols(w_in[l], cols_a),
            w_b=jnp.concatenate([w_in[l][:, hy0:hy0 + 3 * HY_WIDTH], w_in[l][:, gt0:]], axis=1).astype(BF16),
            gq=_tile2(g_qnorm[l]), gqs=_tile2(g_qnorm[l][sw64]),
            gk=_tile2(g_knorm[l]), gks=_tile2(g_knorm[l][sw64]),
            gcq=g_mla_q[l][None], gckv=g_mla_kv[l][None],
            wuq=_take_cols(w_mla_uq[l], uq_main), wuqs=_take_cols(w_mla_uq[l], uq_swap),
            wukv=_take_cols(w_mla_ukv[l], ukv_cols),
            w_short=w_hy_short[l], b_short=b_hy_short[l],
            w1p=jnp.pad(w_hy_f1[l], ((0, LANES - HY_EMB), (0, 0))), b1=b_hy_f1[l][None],
            w2=w_hy_f2[l], b2=b_hy_f2[l][None], w3=w_hy_f3[l], fr=hy_sin_freq[l][None],
            bias_t=jnp.tile(hy_bias[l], HY_COLS // HY_WIDTH)[None],
            wb=w_branch[l].astype(BF16), wo=w_out[l].astype(BF16),
        ))
    gf = g_final[None]
    return _encoder(x_prompt, lw, gf), _encoder(x_sample, lw, gf)
```

```python
import functools
import math

import numpy as np
import jax
import jax.numpy as jnp
from jax import lax
from jax.experimental import pallas as pl
from jax.experimental.pallas import tpu as pltpu

F32 = jnp.float32
BF16 = jnp.bfloat16

D_MODEL = 1024
DEPTH = 2
GRID_W = 64
ROPE_THETA = 10000.0
EPS = 1e-6
D_FF = 2816
N_BRANCH = 3
BRANCH_W = 512
GQA_HEADS = 8
GQA_KV_HEADS = 2
GQA_HEAD_DIM = 64
HY_WIDTH = 512
HY_ORDER = 64
HY_EMB = 33
HY_BANDS = (HY_EMB - 1) // 2
HY_TARGET = 1e-2
HY_FAST = 0.3
HY_SLOW = 1.5
MLA_HEADS = 8
MLA_Q_RANK = 256
MLA_KV_RANK = 128
MLA_NOPE = 64
MLA_ROPE = 32
MLA_V = 64
IN_WIDTHS = (512, 128, 128, 3 * HY_WIDTH, MLA_Q_RANK, MLA_KV_RANK, MLA_ROPE, N_BRANCH * D_MODEL)
IN_OFFS = tuple(int(c) for c in np.cumsum((0,) + IN_WIDTHS))
IN_WIDTH = IN_OFFS[-1]

LANES = 128
V7X_VMEM_BYTES = 64 * 1024 * 1024

TOK_TILE = 512
ATT_TQ = 256
ATT_TQB = 512
ATT_HEADS = 4
ATT_UNROLL = 4
ATT_SAFE_SCORE = 64.0
ATT_TK_GQA = 2 * TOK_TILE
ATT_TK_MLA = 2 * TOK_TILE
FF_CHUNKS = ((0, 1024), (1024, 2048), (2048, D_FF))
DFT_N2 = 128
HY_COLS = 8192
ONES_ROWS = 16
NEG_BIG = -1e30
LOG2E = math.log2(math.e)


def _cparams(sem, vmem_mb):
    return pltpu.CompilerParams(dimension_semantics=sem, vmem_limit_bytes=vmem_mb * 1024 * 1024)


def _const_spec(shape):
    nd = len(shape)
    return pl.BlockSpec(shape, lambda *_: (0,) * nd, pipeline_mode=pl.Buffered(1))


def _rms(x, g):
    return x * lax.rsqrt(jnp.mean(x * x, axis=-1, keepdims=True) + EPS) * g


def _dot(a, b):
    return jnp.dot(a, b, preferred_element_type=F32)


def _dot_nt(a, b):
    return lax.dot_general(a, b, (((1,), (1,)), ((), ())), preferred_element_type=F32)


def _dot_tn(a, b):
    return lax.dot_general(a, b, (((0,), (0,)), ((), ())), preferred_element_type=F32)


def _split(a):
    hi = a.astype(BF16)
    return hi, (a - hi.astype(F32)).astype(BF16)


def _dot3(a, b):
    ah, al = _split(a)
    bh, bl = _split(b)
    return _dot(ah, bh) + (_dot(ah, bl) + _dot(al, bh))


def _ffn_kernel(x_ref, g_ref, wg_ref, wu_ref, wd_ref, gf_ref, o_ref, *, final):
    x = x_ref[...]
    xb = _rms(x, g_ref[...]).astype(BF16)
    acc = jnp.zeros_like(x)
    for c0, c1 in FF_CHUNKS:
        gate = _dot(xb, wg_ref[:, c0:c1])
        up = _dot(xb, wu_ref[:, c0:c1])
        h = (gate * jax.nn.sigmoid(gate) * up).astype(BF16)
        acc = acc + _dot(h, wd_ref[c0:c1, :])
    y = x + 0.5 * acc
    if final:
        y = _rms(y, gf_ref[...])
    o_ref[...] = y


def _ffn(x2d, g, wg, wu, wd, g_final, final):
    t = x2d.shape[0]
    row = pl.BlockSpec((TOK_TILE, D_MODEL), lambda i: (i, 0))
    return pl.pallas_call(
        functools.partial(_ffn_kernel, final=final),
        out_shape=jax.ShapeDtypeStruct((t, D_MODEL), F32),
        grid=(t // TOK_TILE,),
        in_specs=[row, _const_spec((1, D_MODEL)), _const_spec((D_MODEL, D_FF)),
                  _const_spec((D_MODEL, D_FF)), _const_spec((D_FF, D_MODEL)), _const_spec((1, D_MODEL))],
        out_specs=row,
        compiler_params=_cparams(("parallel",), 48),
        name="ffn",
    )(x2d, g, wg, wu, wd, g_final)


PA_Q, PA_QS, PA_K, PA_KS, PA_V, PA_CQ, PA_CKV, PA_KR, PA_KRS, PA_END = (
    0, 1024, 2048, 2176, 2304, 2432, 2688, 2816, 2944, 3072)
GQA_KC = 128
MLA_KC = 128
MLA_ROPE_LANE = MLA_NOPE


def _sq_norm(a_bf16):
    a = a_bf16.astype(F32)
    return jnp.sum(a * a, axis=-1, keepdims=True)


def _proj_a_kernel(x_ref, g_ref, w_ref, cosa_ref, sina_ref, cosm_ref, sinm_ref,
                   gq_ref, gqs_ref, gk_ref, gks_ref, gcq_ref, gckv_ref,
                   wuq_ref, wuqs_ref, wukv_ref,
                   qg_ref, kg_ref, vgt_ref, qm_ref, km_ref, vmt_ref, bnd_ref):
    ub = _rms(x_ref[0], g_ref[...]).astype(BF16)
    cosa, sina = cosa_ref[...], sina_ref[...]
    cosm, sinm = cosm_ref[...], sinm_ref[...]

    zq = _dot(ub, w_ref[:, PA_Q:PA_QS])
    zqs = _dot(ub, w_ref[:, PA_QS:PA_K])
    tq_c = gq_ref[...] * cosa
    tq_s = gqs_ref[...] * sina
    scale_a = GQA_HEAD_DIM ** -0.5 * LOG2E
    qa_n2 = None
    for h in range(GQA_HEADS):
        a = zq[:, h * LANES:(h + 1) * LANES]
        a_sw = zqs[:, h * LANES:(h + 1) * LANES]
        r = lax.rsqrt(jnp.sum(a * a, axis=-1, keepdims=True) * (1.0 / GQA_HEAD_DIM) + EPS)
        qb = ((a * tq_c + a_sw * tq_s) * (r * scale_a)).astype(BF16)
        qg_ref[0, :, h * LANES:(h + 1) * LANES] = qb
        qa_n2 = _sq_norm(qb) if qa_n2 is None else jnp.maximum(qa_n2, _sq_norm(qb))

    zk = _dot(ub, w_ref[:, PA_K:PA_KS])
    zks = _dot(ub, w_ref[:, PA_KS:PA_V])
    first = lax.broadcasted_iota(jnp.int32, zk.shape, 1) < GQA_HEAD_DIM
    sq = zk * zk
    r0 = lax.rsqrt(jnp.sum(jnp.where(first, sq, 0.0), axis=-1, keepdims=True) * (1.0 / GQA_HEAD_DIM) + EPS)
    r1 = lax.rsqrt(jnp.sum(jnp.where(first, 0.0, sq), axis=-1, keepdims=True) * (1.0 / GQA_HEAD_DIM) + EPS)
    kb = ((zk * (gk_ref[...] * cosa) + zks * (gks_ref[...] * sina)) * jnp.where(first, r0, r1)).astype(BF16)
    kg_ref[0] = kb
    kb32 = kb.astype(F32)
    kb_sq = kb32 * kb32
    ka_n2 = jnp.maximum(jnp.sum(jnp.where(first, kb_sq, 0.0), axis=-1, keepdims=True),
                        jnp.sum(jnp.where(first, 0.0, kb_sq), axis=-1, keepdims=True))

    vgt_ref[0, 0] = _dot(ub, w_ref[:, PA_V:PA_CQ]).T.astype(BF16)

    scale_m = (MLA_NOPE + MLA_ROPE) ** -0.5 * LOG2E
    cqn = _rms(_dot(ub, w_ref[:, PA_CQ:PA_CKV]), gcq_ref[...]).astype(BF16)
    zq2 = _dot(cqn, wuq_ref[...])
    zq2s = _dot(cqn, wuqs_ref[...])
    qm_n2 = None
    for h in range(MLA_HEADS):
        blk = slice(h * MLA_KC, (h + 1) * MLA_KC)
        qb = ((zq2[:, blk] * cosm + zq2s[:, blk] * sinm) * scale_m).astype(BF16)
        qm_ref[0, :, blk] = qb
        qm_n2 = _sq_norm(qb) if qm_n2 is None else jnp.maximum(qm_n2, _sq_norm(qb))

    ckvn = _rms(_dot(ub, w_ref[:, PA_CKV:PA_KR]), gckv_ref[...]).astype(BF16)
    zkv = _dot(ckvn, wukv_ref[...])
    krope = _dot(ub, w_ref[:, PA_KR:PA_KRS]) * cosm + _dot(ub, w_ref[:, PA_KRS:PA_END]) * sinm
    km_n2 = None
    for h in range(MLA_HEADS):
        blk = slice(h * MLA_KC, (h + 1) * MLA_KC)
        kb = (zkv[:, blk] + krope).astype(BF16)
        km_ref[0, :, blk] = kb
        km_n2 = _sq_norm(kb) if km_n2 is None else jnp.maximum(km_n2, _sq_norm(kb))
    vmt_ref[0, 0] = zkv[:, MLA_HEADS * MLA_KC:].T.astype(BF16)

    row = lax.broadcasted_iota(jnp.int32, (8, LANES), 0)
    tops = [jnp.max(v, axis=0, keepdims=True) for v in (qa_n2, ka_n2, qm_n2, km_n2)]
    bnd_ref[0, 0] = jnp.where(row == 0, tops[0], jnp.where(row == 1, tops[1], jnp.where(row == 2, tops[2], tops[3])))


def _proj_a(x, g_mix, w_a, tabs, gq, gqs, gk, gks, gcq, gckv, wuq, wuqs, wukv):
    b, l, _ = x.shape
    tm = TOK_TILE
    nt = l // tm
    cosa, sina, cosm, sinm = tabs
    tab = pl.BlockSpec((tm, LANES), lambda bi, i: (i, 0))

    def tok(w):
        return pl.BlockSpec((1, tm, w), lambda bi, i: (bi, i, 0))

    def tr(rows):
        return pl.BlockSpec((1, 1, rows, tm), lambda bi, i: (bi, i, 0, 0))

    out_shape = (
        jax.ShapeDtypeStruct((b, l, GQA_HEADS * GQA_KC), BF16),
        jax.ShapeDtypeStruct((b, l, GQA_KC), BF16),
        jax.ShapeDtypeStruct((b, nt, GQA_KV_HEADS * GQA_HEAD_DIM, tm), BF16),
        jax.ShapeDtypeStruct((b, l, MLA_HEADS * MLA_KC), BF16),
        jax.ShapeDtypeStruct((b, l, MLA_HEADS * MLA_KC), BF16),
        jax.ShapeDtypeStruct((b, nt, MLA_HEADS * MLA_V, tm), BF16),
        jax.ShapeDtypeStruct((b, nt, 8, LANES), F32),
    )
    return pl.pallas_call(
        _proj_a_kernel,
        out_shape=out_shape,
        grid=(b, nt),
        in_specs=[tok(D_MODEL), _const_spec((1, D_MODEL)), _const_spec(w_a.shape), tab, tab, tab, tab,
                  _const_spec((1, LANES)), _const_spec((1, LANES)), _const_spec((1, LANES)),
                  _const_spec((1, LANES)), _const_spec((1, MLA_Q_RANK)), _const_spec((1, MLA_KV_RANK)),
                  _const_spec(wuq.shape), _const_spec(wuqs.shape), _const_spec(wukv.shape)],
        out_specs=(tok(GQA_HEADS * GQA_KC), tok(GQA_KC), tr(GQA_KV_HEADS * GQA_HEAD_DIM),
                   tok(MLA_HEADS * MLA_KC), tok(MLA_HEADS * MLA_KC), tr(MLA_HEADS * MLA_V),
                   pl.BlockSpec((1, 1, 8, LANES), lambda bi, i: (bi, i, 0, 0))),
        compiler_params=_cparams(("parallel", "parallel"), 48),
        name="proj_a",
    )(x, g_mix, w_a, cosa, sina, cosm, sinm, gq, gqs, gk, gks, gcq, gckv, wuq, wuqs, wukv)


def _proj_b_kernel(x_ref, g_ref, w_ref, hy_ref, gt_ref):
    ub = _rms(x_ref[...], g_ref[...]).astype(BF16)
    hy_ref[...] = _dot(ub, w_ref[:, :3 * HY_WIDTH]).astype(BF16)
    gt_ref[...] = jax.nn.sigmoid(_dot(ub, w_ref[:, 3 * HY_WIDTH:])).astype(BF16)


def _proj_b(x2d, g_mix, w_b):
    t = x2d.shape[0]
    tm = TOK_TILE

    def row(w):
        return pl.BlockSpec((tm, w), lambda i: (i, 0))

    return pl.pallas_call(
        _proj_b_kernel,
        out_shape=(jax.ShapeDtypeStruct((t, 3 * HY_WIDTH), BF16),
                   jax.ShapeDtypeStruct((t, N_BRANCH * D_MODEL), BF16)),
        grid=(t // tm,),
        in_specs=[row(D_MODEL), _const_spec((1, D_MODEL)), _const_spec(w_b.shape)],
        out_specs=(row(3 * HY_WIDTH), row(N_BRANCH * D_MODEL)),
        compiler_params=_cparams(("parallel",), 48),
        name="proj_b",
    )(x2d, g_mix, w_b)


def _attn_kernel(safe_ref, q_ref, k_ref, vt_ref, o_ref, acc_scr, l_scr, p0_scr, p1_scr,
                 *, heads, groups, kc, key_of_head, value_of_head, nk, nk_tile):
    tv = vt_ref.shape[3]
    tk = nk_tile
    tq = ATT_TQ
    ones = jnp.ones((ONES_ROWS, tk), BF16)
    dv = MLA_V
    n_sub = q_ref.shape[1] // tq
    safe = safe_ref[pl.program_id(0), pl.program_id(1)] != 0

    def key_tile(j, h):
        kb = key_of_head[h]
        rows = j * tk if isinstance(j, int) else pl.multiple_of(j * tk, tk)
        return k_ref[0, pl.ds(rows, tk), kb * kc:(kb + 1) * kc]

    def values(j, h):
        vb = value_of_head[h]
        per = tk // tv
        return jnp.concatenate([vt_ref[0, j * per + u, vb * dv:(vb + 1) * dv, :] for u in range(per)], axis=1)

    def value_tile(j, h):
        return jnp.concatenate([values(j, h), ones], axis=0)

    def write_out(acc, l, h, sub):
        o_ref[0, h * dv:(h + 1) * dv, sub * tq:(sub + 1) * tq] = (acc / l).astype(BF16)

    @pl.when(safe)
    def _():
        bufs = (p0_scr, p1_scr)
        streams = [(sub, g) for sub in range(n_sub) for g in range(groups)]

        def probs(si, stream, j, hh, buf):
            sub, g = stream
            h = g * heads + hh
            st = _dot_nt(key_tile(j, h), q_ref[0, sub * tq:(sub + 1) * tq, h * kc:(h + 1) * kc])
            e = jnp.exp2(st)
            buf[hh] = e.astype(BF16)
            l_scr[si % 2, hh] += jnp.sum(e, axis=0, keepdims=True)

        def accumulate(si, stream, j, hh, buf):
            acc_scr[si % 2, hh] += _dot(values(j, stream[1] * heads + hh), buf[hh])

        acc_scr[...] = jnp.zeros(acc_scr.shape, F32)
        l_scr[...] = jnp.zeros(l_scr.shape, F32)
        for hh in range(heads):
            probs(0, streams[0], 0, hh, bufs[0])
        for si, stream in enumerate(streams):
            def step(j, cur, nxt, si=si, stream=stream):
                for hh in range(heads):
                    probs(si, stream, j + 1, hh, nxt)
                    accumulate(si, stream, j, hh, cur)

            unroll = min(ATT_UNROLL, nk)

            def body(jj, carry, step=step):
                for u in range(unroll):
                    step(unroll * jj + u, bufs[u % 2], bufs[1 - u % 2])
                return carry

            lax.fori_loop(0, nk // unroll - 1, body, 0)
            for u in range(unroll - 1):
                step(nk - unroll + u, bufs[u % 2], bufs[1 - u % 2])
            for hh in range(heads):
                if si + 1 < len(streams):
                    probs(si + 1, streams[si + 1], 0, hh, bufs[0])
                accumulate(si, stream, nk - 1, hh, bufs[1])
            sub, g = stream
            for hh in range(heads):
                write_out(acc_scr[si % 2, hh], l_scr[si % 2, hh], g * heads + hh, sub)
                if si + 2 < len(streams):
                    acc_scr[si % 2, hh] = jnp.zeros((dv, tq), F32)
                    l_scr[si % 2, hh] = jnp.zeros((1, tq), F32)

    @pl.when(jnp.logical_not(safe))
    def _():
        for h in range(groups * heads):
            for sub in range(n_sub):
                qh = q_ref[0, sub * tq:(sub + 1) * tq, h * kc:(h + 1) * kc]

                def body(j, carry, qh=qh, h=h):
                    m, acc = carry
                    st = _dot_nt(key_tile(j, h), qh)
                    m_new = jnp.maximum(m, jnp.max(st, axis=0, keepdims=True))
                    p = jnp.exp2(st - m_new).astype(BF16)
                    return m_new, jnp.exp2(m - m_new) * acc + _dot(value_tile(j, h), p)

                init = (jnp.full((1, tq), NEG_BIG, F32), jnp.zeros((dv + ONES_ROWS, tq), F32))
                _, acc = lax.fori_loop(0, nk, body, init)
                write_out(acc[:dv], acc[dv:dv + 1], h, sub)


def _attention(safe, q, k, vt, *, heads, kc, tk, key_of_head, value_of_head, name):
    b, l, qw = q.shape
    n_heads = qw // kc
    nv, tv = vt.shape[1], vt.shape[3]
    nk = l // tk
    tqb = ATT_TQB
    tq = ATT_TQ
    dv = MLA_V
    grid_spec = pltpu.PrefetchScalarGridSpec(
        num_scalar_prefetch=1,
        grid=(b, l // tqb),
        in_specs=[pl.BlockSpec((1, tqb, qw), lambda bi, i, s: (bi, i, 0)),
                  pl.BlockSpec((1, l, k.shape[2]), lambda bi, i, s: (bi, 0, 0), pipeline_mode=pl.Buffered(1)),
                  pl.BlockSpec((1, nv, vt.shape[2], tv), lambda bi, i, s: (bi, 0, 0, 0),
                               pipeline_mode=pl.Buffered(1))],
        out_specs=pl.BlockSpec((1, n_heads * dv, tqb), lambda bi, i, s: (bi, 0, i)),
        scratch_shapes=[pltpu.VMEM((2, heads, dv, tq), F32), pltpu.VMEM((2, heads, 1, tq), F32),
                        pltpu.VMEM((heads, tk, tq), BF16), pltpu.VMEM((heads, tk, tq), BF16)],
    )
    return pl.pallas_call(
        functools.partial(_attn_kernel, heads=heads, groups=n_heads // heads, kc=kc, key_of_head=key_of_head,
                          value_of_head=value_of_head, nk=nk, nk_tile=tk),
        out_shape=jax.ShapeDtypeStruct((b, n_heads * dv, l), BF16),
        grid_spec=grid_spec,
        compiler_params=_cparams(("parallel", "parallel"), 48),
        name=name,
    )(safe, q, k, vt)


def _hy_pre_kernel(x0_ref, x1_ref, v_ref, w_ref, b_ref, s_ref, x0o_ref):
    rows = x0_ref.shape[1]
    t = lax.broadcasted_iota(jnp.int32, (rows, LANES), 0)
    not_first = t > 0
    not_last = t < rows - 1

    def conv(ref, j):
        a = ref[0].astype(F32)
        prev = jnp.where(not_first, pltpu.roll(a, 1, 0), 0.0)
        nxt = jnp.where(not_last, pltpu.roll(a, rows - 1, 0), 0.0)
        return prev * w_ref[0, j] + a * w_ref[1, j] + nxt * w_ref[2, j] + b_ref[j]

    x0o_ref[0] = conv(x0_ref, 0).astype(BF16)
    s_ref[0] = (conv(x1_ref, 1) * conv(v_ref, 2)).astype(BF16)


def _hy_pre(hy, w_short, b_short):
    b, l, _ = hy.shape
    nc = HY_WIDTH // LANES
    w4 = w_short.reshape(3, 3, nc, 1, LANES)
    b4 = b_short.reshape(3, nc, 1, LANES)

    def seg(j):
        return pl.BlockSpec((1, l, LANES), lambda bi, c, j=j: (bi, 0, j * nc + c))

    out = pl.BlockSpec((1, l, LANES), lambda bi, c: (bi, 0, c))
    return pl.pallas_call(
        _hy_pre_kernel,
        out_shape=(jax.ShapeDtypeStruct((b, l, HY_WIDTH), BF16), jax.ShapeDtypeStruct((b, l, HY_WIDTH), BF16)),
        grid=(b, nc),
        in_specs=[seg(0), seg(1), seg(2),
                  pl.BlockSpec((3, 3, None, 1, LANES), lambda bi, c: (0, 0, c, 0, 0)),
                  pl.BlockSpec((3, None, 1, LANES), lambda bi, c: (0, c, 0, 0))],
        out_specs=(out, out),
        compiler_params=_cparams(("parallel", "parallel"), 48),
        name="hy_pre",
    )(hy, hy, hy, w4, b4)


def _left_mm_kernel(f_ref, x_ref, o_ref, *, hi):
    if hi:
        o_ref[0] = _dot3(f_ref[...], x_ref[0]).astype(o_ref.dtype)
    else:
        o_ref[0] = _dot(f_ref[...], x_ref[0]).astype(o_ref.dtype)


def _left_mm(f, x, out_dtype, hi=False):
    b, k, n = x.shape
    m = f.shape[0]
    cb = min(HY_COLS, n)
    return pl.pallas_call(
        functools.partial(_left_mm_kernel, hi=hi),
        out_shape=jax.ShapeDtypeStruct((b, m, n), out_dtype),
        grid=(b, n // cb),
        in_specs=[_const_spec((m, k)), pl.BlockSpec((1, k, cb), lambda bi, c: (bi, 0, c))],
        out_specs=pl.BlockSpec((1, m, cb), lambda bi, c: (bi, 0, c)),
        compiler_params=_cparams(("parallel", "parallel"), 48),
        name="hy_outer_dft",
    )(f, x)


def _hy_mid_kernel(f_ref, kf_ref, g_ref, a_ref, o_ref):
    n2 = DFT_N2
    kr, ki = kf_ref[0, :n2], kf_ref[0, n2:]
    for bi in range(a_ref.shape[0]):
        x = _dot(f_ref[0], a_ref[bi, 0])
        xr, xi = x[:n2], x[n2:]
        y = jnp.concatenate([xr * kr - xi * ki, xr * ki + xi * kr], axis=0).astype(BF16)
        o_ref[bi, 0] = _dot(g_ref[0], y).astype(BF16)


def _hy_mid(a4, f2, kf, g2):
    b, n1, r, c = a4.shape
    mat = pl.BlockSpec((1, r, r), lambda i: (i, 0, 0))
    dat = pl.BlockSpec((b, 1, r, c), lambda i: (0, i, 0, 0))
    return pl.pallas_call(
        _hy_mid_kernel,
        out_shape=jax.ShapeDtypeStruct(a4.shape, BF16),
        grid=(n1,),
        in_specs=[mat, pl.BlockSpec((1, r, c), lambda i: (i, 0, 0)), mat, dat],
        out_specs=dat,
        compiler_params=_cparams(("parallel",), 48),
        name="hy_mid",
    )(f2, kf, g2, a4)


def _hy_post_kernel(g_ref, bi_ref, s_ref, x0_ref, bias_ref, o_ref):
    y = _dot(g_ref[...], bi_ref[0]) + s_ref[0].astype(F32) * bias_ref[...]
    o_ref[0] = (x0_ref[0].astype(F32) * y).astype(BF16)


def _hy_post(g1, bi2, s2, x02, bias_t):
    b, k, n = bi2.shape
    m = g1.shape[0]
    cb = HY_COLS
    dat = pl.BlockSpec((1, m, cb), lambda bi, c: (bi, 0, c))
    return pl.pallas_call(
        _hy_post_kernel,
        out_shape=jax.ShapeDtypeStruct((b, m, n), BF16),
        grid=(b, n // cb),
        in_specs=[_const_spec((m, k)), pl.BlockSpec((1, k, cb), lambda bi, c: (bi, 0, c)), dat, dat,
                  _const_spec((1, cb))],
        out_specs=dat,
        compiler_params=_cparams(("parallel", "parallel"), 48),
        name="hy_post",
    )(g1, bi2, s2, x02, bias_t)


def _hy_filter_kernel(z_ref, win_ref, w1_ref, b1_ref, w2_ref, b2_ref, w3_ref, fr_ref, o_ref, *, half_tiles):
    fr = fr_ref[...]
    h = jnp.sin(fr * (_dot3(z_ref[...], w1_ref[...]) + b1_ref[...]))
    h = jnp.sin(fr * (_dot3(h, w2_ref[...]) + b2_ref[...]))
    h3 = _dot3(h, w3_ref[...])
    backward = pl.program_id(0) >= half_tiles
    o_ref[...] = jnp.where(backward, h3[:, HY_WIDTH:], h3[:, :HY_WIDTH]) * win_ref[...]


def _hy_filter(z2, win2, w1p, b1, w2, b2, w3, fr):
    n = z2.shape[0]
    tl = 512
    return pl.pallas_call(
        functools.partial(_hy_filter_kernel, half_tiles=n // (2 * tl)),
        out_shape=jax.ShapeDtypeStruct((n, HY_WIDTH), F32),
        grid=(n // tl,),
        in_specs=[pl.BlockSpec((tl, LANES), lambda i: (i, 0)), pl.BlockSpec((tl, HY_WIDTH), lambda i: (i, 0)),
                  _const_spec(w1p.shape), _const_spec(b1.shape), _const_spec(w2.shape), _const_spec(b2.shape),
                  _const_spec(w3.shape), _const_spec(fr.shape)],
        out_specs=pl.BlockSpec((tl, HY_WIDTH), lambda i: (i, 0)),
        compiler_params=_cparams(("parallel",), 32),
        name="hy_filter",
    )(z2, win2, w1p, b1, w2, b2, w3, fr)


def _hy_spec_kernel(f_ref, a_ref, o_ref):
    o_ref[0] = _dot3(f_ref[0], a_ref[0])


def _hy_spec(f2, a3):
    n1, r, c = a3.shape
    return pl.pallas_call(
        _hy_spec_kernel,
        out_shape=jax.ShapeDtypeStruct(a3.shape, F32),
        grid=(n1,),
        in_specs=[pl.BlockSpec((1, r, r), lambda i: (i, 0, 0)), pl.BlockSpec((1, r, c), lambda i: (i, 0, 0))],
        out_specs=pl.BlockSpec((1, r, c), lambda i: (i, 0, 0)),
        compiler_params=_cparams(("parallel",), 32),
        name="hy_filter_spectrum",
    )(f2, a3)


def _merge_kernel(x_ref, yat_ref, yb_ref, yct_ref, gt_ref, wb_ref, wo_ref, o_ref):
    d = D_MODEL
    ba = _dot_tn(yat_ref[0], wb_ref[0])
    bb = _dot(yb_ref[0], wb_ref[1])
    bc = _dot_tn(yct_ref[0], wb_ref[2])
    merged = (gt_ref[0, :, 0:d].astype(F32) * ba + gt_ref[0, :, d:2 * d].astype(F32) * bb
              + gt_ref[0, :, 2 * d:3 * d].astype(F32) * bc)
    o_ref[0] = x_ref[0] + _dot(merged.astype(BF16), wo_ref[...])


def _merge(x, yat, yb, yct, gates, wb, wo):
    b, l, _ = x.shape
    tm = TOK_TILE

    def tok(w):
        return pl.BlockSpec((1, tm, w), lambda bi, i: (bi, i, 0))

    tr = pl.BlockSpec((1, BRANCH_W, tm), lambda bi, i: (bi, 0, i))
    return pl.pallas_call(
        _merge_kernel,
        out_shape=jax.ShapeDtypeStruct(x.shape, F32),
        grid=(b, l // tm),
        in_specs=[tok(D_MODEL), tr, tok(BRANCH_W), tr, tok(N_BRANCH * D_MODEL),
                  _const_spec(wb.shape), _const_spec(wo.shape)],
        out_specs=tok(D_MODEL),
        compiler_params=_cparams(("parallel", "parallel"), 48),
        name="merge",
    )(x, yat, yb, yct, gates, wb, wo)


def _rope_tables(l):
    rows = l // GRID_W
    row = jnp.repeat(jnp.arange(rows, dtype=F32), GRID_W)
    col = jnp.tile(jnp.arange(GRID_W, dtype=F32), rows)

    def tab(d_rot):
        n_freq = d_rot // 4
        inv = ROPE_THETA ** (-jnp.arange(n_freq, dtype=F32) / n_freq)
        ang = jnp.concatenate([row[:, None] * inv, col[:, None] * inv], axis=-1)
        c = jnp.repeat(jnp.cos(ang), 2, axis=-1)
        s = jnp.repeat(jnp.sin(ang), 2, axis=-1) * jnp.tile(jnp.array([-1.0, 1.0], F32), d_rot // 2)
        return c, s

    ca, sa = tab(GQA_HEAD_DIM)
    cm, sm = tab(MLA_ROPE)
    tail = jnp.zeros((l, LANES - MLA_ROPE_LANE - MLA_ROPE), F32)
    cm = jnp.concatenate([jnp.ones((l, MLA_ROPE_LANE), F32), cm, tail], axis=1)
    sm = jnp.concatenate([jnp.zeros((l, MLA_ROPE_LANE), F32), sm, tail], axis=1)
    return (jnp.tile(ca, (1, 2)), jnp.tile(sa, (1, 2)), cm, sm)


def _hy_positions(l):
    r = jnp.arange(l, dtype=jnp.int32)
    pos = jnp.concatenate([r, (l - r) % l]).astype(F32)[:, None]
    live = jnp.concatenate([jnp.ones((l,), F32), (r > 0).astype(F32)])[:, None]
    t = pos * (1.0 / (l - 1))
    w = 2.0 * math.pi * pos / l
    f = jnp.linspace(1e-4, HY_BANDS - 1, HY_BANDS, dtype=F32)[None, :]
    z2 = jnp.concatenate([t, jnp.cos(f * w), -jnp.sin(f * w), jnp.zeros((2 * l, LANES - HY_EMB), F32)], axis=-1)
    max_decay = math.log(HY_TARGET) / HY_FAST
    min_decay = math.log(HY_TARGET) / HY_SLOW
    deltas = jnp.abs(jnp.linspace(min_decay, max_decay, HY_WIDTH, dtype=F32))
    return z2, jnp.exp(-t * deltas[None, :]) * live


def _hy_k1_count(l):
    n1 = 2 * l // DFT_N2
    return -(-(n1 // 2 + 1) // 8) * 8


def _dft_tables(l):
    n = 2 * l
    n2 = DFT_N2
    n1 = n // n2
    n1c = _hy_k1_count(l)
    two_pi = 2.0 * math.pi
    i1 = jnp.arange(n1, dtype=jnp.int32)
    ang1 = ((i1[:n1c, None] * i1[None, :]) % n1).astype(F32) * (two_pi / n1)
    live = (i1[:n1c] <= n1 // 2)[:, None]
    c1, s1 = jnp.where(live, jnp.cos(ang1), 0.0), jnp.where(live, jnp.sin(ang1), 0.0)
    f1_full = jnp.stack([c1, -s1], axis=1).reshape(2 * n1c, n1)
    fold = jnp.where((i1[:n1c] == 0) | (i1[:n1c] == n1 // 2), 1.0, 2.0)[:, None] * (1.0 / n)
    g1 = jnp.stack([(c1 * fold)[:, :n1 // 2].T, (-s1 * fold)[:, :n1 // 2].T], axis=2).reshape(n1 // 2, 2 * n1c)
    i2 = jnp.arange(n2, dtype=jnp.int32)
    kk = i1[:n1c, None, None] + n1 * i2[None, :, None]
    ang2 = ((kk * i2[None, None, :]) % n).astype(F32) * (two_pi / n)
    c2, s2 = jnp.cos(ang2), jnp.sin(ang2)
    f2 = jnp.concatenate([jnp.concatenate([c2, s2], axis=2), jnp.concatenate([-s2, c2], axis=2)], axis=1)
    c2t, s2t = jnp.swapaxes(c2, 1, 2), jnp.swapaxes(s2, 1, 2)
    g2 = jnp.concatenate([jnp.concatenate([c2t, -s2t], axis=2), jnp.concatenate([s2t, c2t], axis=2)], axis=1)
    return f1_full, g1, f2, g2


def _swap_pairs(n):
    return np.arange(n) ^ 1


def _proj_a_columns():
    zero = IN_WIDTH
    cols = np.full((PA_END,), zero, np.int64)
    q0, k0, v0, _, cq0, ckv0, kr0, _ = IN_OFFS[:8]
    for h in range(GQA_HEADS):
        dst = h * LANES + (h // (GQA_HEADS // GQA_KV_HEADS)) * GQA_HEAD_DIM
        src = q0 + h * GQA_HEAD_DIM + np.arange(GQA_HEAD_DIM)
        cols[PA_Q + dst:PA_Q + dst + GQA_HEAD_DIM] = src
        cols[PA_QS + dst:PA_QS + dst + GQA_HEAD_DIM] = q0 + h * GQA_HEAD_DIM + _swap_pairs(GQA_HEAD_DIM)
    cols[PA_K:PA_K + 128] = k0 + np.arange(128)
    cols[PA_KS:PA_KS + 128] = k0 + _swap_pairs(128)
    cols[PA_V:PA_V + 128] = v0 + np.arange(128)
    cols[PA_CQ:PA_CQ + MLA_Q_RANK] = cq0 + np.arange(MLA_Q_RANK)
    cols[PA_CKV:PA_CKV + MLA_KV_RANK] = ckv0 + np.arange(MLA_KV_RANK)
    r0 = MLA_ROPE_LANE
    cols[PA_KR + r0:PA_KR + r0 + MLA_ROPE] = kr0 + np.arange(MLA_ROPE)
    cols[PA_KRS + r0:PA_KRS + r0 + MLA_ROPE] = kr0 + _swap_pairs(MLA_ROPE)
    return cols


def _mla_q_columns():
    hd = MLA_NOPE + MLA_ROPE
    zero = MLA_HEADS * hd
    main = np.full((MLA_HEADS * MLA_KC,), zero, np.int64)
    swap = np.full((MLA_HEADS * MLA_KC,), zero, np.int64)
    for h in range(MLA_HEADS):
        b0 = h * MLA_KC
        main[b0:b0 + MLA_NOPE] = h * hd + np.arange(MLA_NOPE)
        r0 = b0 + MLA_ROPE_LANE
        main[r0:r0 + MLA_ROPE] = h * hd + MLA_NOPE + np.arange(MLA_ROPE)
        swap[r0:r0 + MLA_ROPE] = h * hd + MLA_NOPE + _swap_pairs(MLA_ROPE)
    return main, swap


def _mla_kv_columns():
    hd = MLA_NOPE + MLA_V
    zero = MLA_HEADS * hd
    knope = np.full((MLA_HEADS * MLA_KC,), zero, np.int64)
    for h in range(MLA_HEADS):
        knope[h * MLA_KC:h * MLA_KC + MLA_NOPE] = h * hd + np.arange(MLA_NOPE)
    val = np.concatenate([h * hd + MLA_NOPE + np.arange(MLA_V) for h in range(MLA_HEADS)])
    return np.concatenate([knope, val])


def _take_cols(w, cols):
    wz = jnp.concatenate([w, jnp.zeros((w.shape[0], 1), w.dtype)], axis=1)
    return jnp.take(wz, jnp.asarray(cols, jnp.int32), axis=1).astype(BF16)


def _tile2(g):
    return jnp.tile(g, 2)[None, :]


def _encoder(x, lw, g_final):
    b, l, d = x.shape
    n2 = DFT_N2
    n1 = 2 * l // n2
    c = HY_WIDTH
    tabs = _rope_tables(l)
    z2, win2 = _hy_positions(l)
    f1_full, g1, f2, g2 = _dft_tables(l)
    f1_b = f1_full[:, :n1 // 2].astype(BF16)
    g1_b = g1.astype(BF16)
    f2_b = f2.astype(BF16)
    g2_b = g2.astype(BF16)
    for li, w in enumerate(lw):
        x2 = _ffn(x.reshape(b * l, d), w["g_ffn1"], w["wg1"], w["wu1"], w["wd1"], g_final, False)
        x = x2.reshape(b, l, d)
        qg, kg, vgt, qm, km, vmt, bnd = _proj_a(x, w["g_mix"], w["w_a"], tabs, w["gq"], w["gqs"], w["gk"],
                                                w["gks"], w["gcq"], w["gckv"], w["wuq"], w["wuqs"], w["wukv"])
        hy, gates = _proj_b(x2, w["g_mix"], w["w_b"])
        top = bnd[:, :, :4, 0]
        limit = ATT_SAFE_SCORE ** 2
        safe_a = (top[:, :, 0] * jnp.max(top[:, :, 1], axis=1, keepdims=True) <= limit).astype(jnp.int32)
        safe_m = (top[:, :, 2] * jnp.max(top[:, :, 3], axis=1, keepdims=True) <= limit).astype(jnp.int32)
        per_kv = GQA_HEADS // GQA_KV_HEADS
        yat = _attention(safe_a, qg, kg, vgt, heads=ATT_HEADS, kc=GQA_KC, tk=ATT_TK_GQA,
                         key_of_head=(0,) * GQA_HEADS,
                         value_of_head=tuple(h // per_kv for h in range(GQA_HEADS)), name="attn_gqa")
        yct = _attention(safe_m, qm, km, vmt, heads=ATT_HEADS, kc=MLA_KC, tk=ATT_TK_MLA,
                         key_of_head=tuple(range(MLA_HEADS)),
                         value_of_head=tuple(range(MLA_HEADS)), name="attn_mla")
        kc_time = _hy_filter(z2, win2, w["w1p"], w["b1"], w["w2"], w["b2"], w["w3"], w["fr"])
        n1c = _hy_k1_count(l)
        ka =_left_mm(f1_full, kc_time.reshape(1, n1, n2 * c), F32, hi=True)
        kf = _hy_spec(f2, ka.reshape(n1c, 2 * n2, c))
        s, x0 = _hy_pre(hy.reshape(b, l, 3 * c), w["w_short"], w["b_short"])
        a = _left_mm(f1_b, s.reshape(b, n1 // 2, n2 * c), BF16)
        bi = _hy_mid(a.reshape(b, n1c, 2 * n2, c), f2_b, kf, g2_b)
        yb = _hy_post(g1_b, bi.reshape(b, 2 * n1c, n2 * c), s.reshape(b, n1 // 2, n2 * c),
                      x0.reshape(b, n1 // 2, n2 * c), w["bias_t"])
        x = _merge(x, yat, yb.reshape(b, l, c), yct, gates.reshape(b, l, N_BRANCH * d), w["wb"], w["wo"])
        x2 = _ffn(x.reshape(b * l, d), w["g_ffn2"], w["wg2"], w["wu2"], w["wd2"], g_final, li == len(lw) - 1)
        x = x2.reshape(b, l, d)
    return x


def kernel(x_prompt, x_sample, g_ffn1, w_ffn1_gate, w_ffn1_up, w_ffn1_down, g_mix, w_in, g_qnorm, g_knorm,
           w_hy_short, b_hy_short, w_hy_f1, b_hy_f1, w_hy_f2, b_hy_f2, w_hy_f3, hy_sin_freq, hy_bias,
           g_mla_q, w_mla_uq, g_mla_kv, w_mla_ukv, w_branch, w_out, g_ffn2, w_ffn2_gate, w_ffn2_up,
           w_ffn2_down, g_final):
    cols_a = _proj_a_columns()
    uq_main, uq_swap = _mla_q_columns()
    ukv_cols = _mla_kv_columns()
    sw64 = _swap_pairs(GQA_HEAD_DIM)
    hy0 = IN_OFFS[3]
    gt0 = IN_OFFS[7]
    lw = []
    for l in range(DEPTH):
        lw.append(dict(
            g_ffn1=g_ffn1[l][None], wg1=w_ffn1_gate[l].astype(BF16), wu1=w_ffn1_up[l].astype(BF16),
            wd1=w_ffn1_down[l].astype(BF16),
            g_ffn2=g_ffn2[l][None], wg2=w_ffn2_gate[l].astype(BF16), wu2=w_ffn2_up[l].astype(BF16),
            wd2=w_ffn2_down[l].astype(BF16),
            g_mix=g_mix[l][None],
            w_a=_take_cols(w_in[l], cols_a),
            w_b=jnp.concatenate([w_in[l][:, hy0:hy0 + 3 * HY_WIDTH], w_in[l][:, gt0:]], axis=1).astype(BF16),
            gq=_tile2(g_qnorm[l]), gqs=_tile2(g_qnorm[l][sw64]),
            gk=_tile2(g_knorm[l]), gks=_tile2(g_knorm[l][sw64]),
            gcq=g_mla_q[l][None], gckv=g_mla_kv[l][None],
            wuq=_take_cols(w_mla_uq[l], uq_main), wuqs=_take_cols(w_mla_uq[l], uq_swap),
            wukv=_take_cols(w_mla_ukv[l], ukv_cols),
            w_short=w_hy_short[l], b_short=b_hy_short[l],
            w1p=jnp.pad(w_hy_f1[l], ((0, LANES - HY_EMB), (0, 0))), b1=b_hy_f1[l][None],
            w2=w_hy_f2[l], b2=b_hy_f2[l][None], w3=w_hy_f3[l], fr=hy_sin_freq[l][None],
            bias_t=jnp.tile(hy_bias[l], HY_COLS // HY_WIDTH)[None],
            wb=w_branch[l].astype(BF16), wo=w_out[l].astype(BF16),
        ))
    gf = g_final[None]
    return _encoder(x_prompt, lw, gf), _encoder(x_sample, lw, gf)
```

```python
import functools
import math

import numpy as np
import jax
import jax.numpy as jnp
from jax import lax
from jax.experimental import pallas as pl
from jax.experimental.pallas import tpu as pltpu

F32 = jnp.float32
BF16 = jnp.bfloat16

D_MODEL = 1024
DEPTH = 2
GRID_W = 64
ROPE_THETA = 10000.0
EPS = 1e-6
D_FF = 2816
N_BRANCH = 3
BRANCH_W = 512
GQA_HEADS = 8
GQA_KV_HEADS = 2
GQA_HEAD_DIM = 64
HY_WIDTH = 512
HY_ORDER = 64
HY_EMB = 33
HY_BANDS = (HY_EMB - 1) // 2
HY_TARGET = 1e-2
HY_FAST = 0.3
HY_SLOW = 1.5
MLA_HEADS = 8
MLA_Q_RANK = 256
MLA_KV_RANK = 128
MLA_NOPE = 64
MLA_ROPE = 32
MLA_V = 64
IN_WIDTHS = (512, 128, 128, 3 * HY_WIDTH, MLA_Q_RANK, MLA_KV_RANK, MLA_ROPE, N_BRANCH * D_MODEL)
IN_OFFS = tuple(int(c) for c in np.cumsum((0,) + IN_WIDTHS))
IN_WIDTH = IN_OFFS[-1]

LANES = 128
V7X_VMEM_BYTES = 64 * 1024 * 1024

TOK_TILE = 512
ATT_TQ = 256
ATT_TQB = 512
ATT_HEADS = 4
ATT_UNROLL = 4
ATT_SAFE_SCORE = 64.0
ATT_TK_GQA = 2 * TOK_TILE
ATT_TK_MLA = 2 * TOK_TILE
FF_CHUNKS = ((0, 1024), (1024, 2048), (2048, D_FF))
DFT_N2 = 128
HY_COLS = 8192
ONES_ROWS = 16
NEG_BIG = -1e30
LOG2E = math.log2(math.e)


def _cparams(sem, vmem_mb):
    return pltpu.CompilerParams(dimension_semantics=sem, vmem_limit_bytes=vmem_mb * 1024 * 1024)


def _const_spec(shape):
    nd = len(shape)
    return pl.BlockSpec(shape, lambda *_: (0,) * nd, pipeline_mode=pl.Buffered(1))


def _rms(x, g):
    return x * lax.rsqrt(jnp.mean(x * x, axis=-1, keepdims=True) + EPS) * g


def _dot(a, b):
    return jnp.dot(a, b, preferred_element_type=F32)


def _dot_nt(a, b):
    return lax.dot_general(a, b, (((1,), (1,)), ((), ())), preferred_element_type=F32)


def _dot_tn(a, b):
    return lax.dot_general(a, b, (((0,), (0,)), ((), ())), preferred_element_type=F32)


def _split(a):
    hi = a.astype(BF16)
    return hi, (a - hi.astype(F32)).astype(BF16)


def _dot3(a, b):
    ah, al = _split(a)
    bh, bl = _split(b)
    return _dot(ah, bh) + (_dot(ah, bl) + _dot(al, bh))


def _ffn_kernel(x_ref, g_ref, wg_ref, wu_ref, wd_ref, gf_ref, o_ref, *, final):
    x = x_ref[...]
    xb = _rms(x, g_ref[...]).astype(BF16)
    acc = jnp.zeros_like(x)
    for c0, c1 in FF_CHUNKS:
        gate = _dot(xb, wg_ref[:, c0:c1])
        up = _dot(xb, wu_ref[:, c0:c1])
        h = (gate * jax.nn.sigmoid(gate) * up).astype(BF16)
        acc = acc + _dot(h, wd_ref[c0:c1, :])
    y = x + 0.5 * acc
    if final:
        y = _rms(y, gf_ref[...])
    o_ref[...] = y


def _ffn(x2d, g, wg, wu, wd, g_final, final):
    t = x2d.shape[0]
    row = pl.BlockSpec((TOK_TILE, D_MODEL), lambda i: (i, 0))
    return pl.pallas_call(
        functools.partial(_ffn_kernel, final=final),
        out_shape=jax.ShapeDtypeStruct((t, D_MODEL), F32),
        grid=(t // TOK_TILE,),
        in_specs=[row, _const_spec((1, D_MODEL)), _const_spec((D_MODEL, D_FF)),
                  _const_spec((D_MODEL, D_FF)), _const_spec((D_FF, D_MODEL)), _const_spec((1, D_MODEL))],
        out_specs=row,
        compiler_params=_cparams(("parallel",), 48),
        name="ffn",
    )(x2d, g, wg, wu, wd, g_final)


PA_Q, PA_K, PA_V, PA_CQ, PA_CKV, PA_KR, PA_END = (0, 1024, 1152, 1280, 1536, 1664, 1792)
GQA_KC = 128
MLA_KC = 128
MLA_ROPE_LANE = MLA_NOPE


def _sq_norm(a_bf16):
    a = a_bf16.astype(F32)
    return jnp.sum(a * a, axis=-1, keepdims=True)


def _swap_lane_pairs(a):
    even = (lax.broadcasted_iota(jnp.int32, a.shape, 1) & 1) == 0
    return jnp.where(even, pltpu.roll(a, LANES - 1, 1), pltpu.roll(a, 1, 1))


def _proj_a_kernel(x_ref, g_ref, w_ref, cosa_ref, sina_ref, cosm_ref, sinm_ref,
                   gq_ref, gqs_ref, gk_ref, gks_ref, gcq_ref, gckv_ref,
                   wuq_ref, wukv_ref,
                   qg_ref, kg_ref, vgt_ref, qm_ref, km_ref, vmt_ref, bnd_ref):
    ub = _rms(x_ref[0], g_ref[...]).astype(BF16)
    cosa, sina = cosa_ref[...], sina_ref[...]
    cosm, sinm = cosm_ref[...], sinm_ref[...]

    zq = _dot(ub, w_ref[:, PA_Q:PA_K])
    tq_c = gq_ref[...] * cosa
    tq_s = gqs_ref[...] * sina
    scale_a = GQA_HEAD_DIM ** -0.5 * LOG2E
    qa_n2 = None
    for h in range(GQA_HEADS):
        a = zq[:, h * LANES:(h + 1) * LANES]
        a_sw = _swap_lane_pairs(a)
        r = lax.rsqrt(jnp.sum(a * a, axis=-1, keepdims=True) * (1.0 / GQA_HEAD_DIM) + EPS)
        qb = ((a * tq_c + a_sw * tq_s) * (r * scale_a)).astype(BF16)
        qg_ref[0, :, h * LANES:(h + 1) * LANES] = qb
        qa_n2 = _sq_norm(qb) if qa_n2 is None else jnp.maximum(qa_n2, _sq_norm(qb))

    zk = _dot(ub, w_ref[:, PA_K:PA_V])
    zks = _swap_lane_pairs(zk)
    first = lax.broadcasted_iota(jnp.int32, zk.shape, 1) < GQA_HEAD_DIM
    sq = zk * zk
    r0 = lax.rsqrt(jnp.sum(jnp.where(first, sq, 0.0), axis=-1, keepdims=True) * (1.0 / GQA_HEAD_DIM) + EPS)
    r1 = lax.rsqrt(jnp.sum(jnp.where(first, 0.0, sq), axis=-1, keepdims=True) * (1.0 / GQA_HEAD_DIM) + EPS)
    kb = ((zk * (gk_ref[...] * cosa) + zks * (gks_ref[...] * sina)) * jnp.where(first, r0, r1)).astype(BF16)
    kg_ref[0] = kb
    kb32 = kb.astype(F32)
    kb_sq = kb32 * kb32
    ka_n2 = jnp.maximum(jnp.sum(jnp.where(first, kb_sq, 0.0), axis=-1, keepdims=True),
                        jnp.sum(jnp.where(first, 0.0, kb_sq), axis=-1, keepdims=True))

    vgt_ref[0, 0] = _dot(ub, w_ref[:, PA_V:PA_CQ]).T.astype(BF16)

    scale_m = (MLA_NOPE + MLA_ROPE) ** -0.5 * LOG2E
    cqn = _rms(_dot(ub, w_ref[:, PA_CQ:PA_CKV]), gcq_ref[...]).astype(BF16)
    zq2 = _dot(cqn, wuq_ref[...])
    qm_n2 = None
    for h in range(MLA_HEADS):
        blk = slice(h * MLA_KC, (h + 1) * MLA_KC)
        qb = ((zq2[:, blk] * cosm + _swap_lane_pairs(zq2[:, blk]) * sinm) * scale_m).astype(BF16)
        qm_ref[0, :, blk] = qb
        qm_n2 = _sq_norm(qb) if qm_n2 is None else jnp.maximum(qm_n2, _sq_norm(qb))

    ckvn = _rms(_dot(ub, w_ref[:, PA_CKV:PA_KR]), gckv_ref[...]).astype(BF16)
    zkv = _dot(ckvn, wukv_ref[...])
    zkr = _dot(ub, w_ref[:, PA_KR:PA_END])
    krope = zkr * cosm + _swap_lane_pairs(zkr) * sinm
    km_n2 = None
    for h in range(MLA_HEADS):
        blk = slice(h * MLA_KC, (h + 1) * MLA_KC)
        kb = (zkv[:, blk] + krope).astype(BF16)
        km_ref[0, :, blk] = kb
        km_n2 = _sq_norm(kb) if km_n2 is None else jnp.maximum(km_n2, _sq_norm(kb))
    vmt_ref[0, 0] = zkv[:, MLA_HEADS * MLA_KC:].T.astype(BF16)

    row = lax.broadcasted_iota(jnp.int32, (8, LANES), 0)
    tops = [jnp.max(v, axis=0, keepdims=True) for v in (qa_n2, ka_n2, qm_n2, km_n2)]
    bnd_ref[0, 0] = jnp.where(row == 0, tops[0], jnp.where(row == 1, tops[1], jnp.where(row == 2, tops[2], tops[3])))


def _proj_a(x, g_mix, w_a, tabs, gq, gqs, gk, gks, gcq, gckv, wuq, wukv):
    b, l, _ = x.shape
    tm = TOK_TILE
    nt = l // tm
    cosa, sina, cosm, sinm = tabs
    tab = pl.BlockSpec((tm, LANES), lambda bi, i: (i, 0))

    def tok(w):
        return pl.BlockSpec((1, tm, w), lambda bi, i: (bi, i, 0))

    def tr(rows):
        return pl.BlockSpec((1, 1, rows, tm), lambda bi, i: (bi, i, 0, 0))

    out_shape = (
        jax.ShapeDtypeStruct((b, l, GQA_HEADS * GQA_KC), BF16),
        jax.ShapeDtypeStruct((b, l, GQA_KC), BF16),
        jax.ShapeDtypeStruct((b, nt, GQA_KV_HEADS * GQA_HEAD_DIM, tm), BF16),
        jax.ShapeDtypeStruct((b, l, MLA_HEADS * MLA_KC), BF16),
        jax.ShapeDtypeStruct((b, l, MLA_HEADS * MLA_KC), BF16),
        jax.ShapeDtypeStruct((b, nt, MLA_HEADS * MLA_V, tm), BF16),
        jax.ShapeDtypeStruct((b, nt, 8, LANES), F32),
    )
    return pl.pallas_call(
        _proj_a_kernel,
        out_shape=out_shape,
        grid=(b, nt),
        in_specs=[tok(D_MODEL), _const_spec((1, D_MODEL)), _const_spec(w_a.shape), tab, tab, tab, tab,
                  _const_spec((1, LANES)), _const_spec((1, LANES)), _const_spec((1, LANES)),
                  _const_spec((1, LANES)), _const_spec((1, MLA_Q_RANK)), _const_spec((1, MLA_KV_RANK)),
                  _const_spec(wuq.shape), _const_spec(wukv.shape)],
        out_specs=(tok(GQA_HEADS * GQA_KC), tok(GQA_KC), tr(GQA_KV_HEADS * GQA_HEAD_DIM),
                   tok(MLA_HEADS * MLA_KC), tok(MLA_HEADS * MLA_KC), tr(MLA_HEADS * MLA_V),
                   pl.BlockSpec((1, 1, 8, LANES), lambda bi, i: (bi, i, 0, 0))),
        compiler_params=_cparams(("parallel", "parallel"), 48),
        name="proj_a",
    )(x, g_mix, w_a, cosa, sina, cosm, sinm, gq, gqs, gk, gks, gcq, gckv, wuq, wukv)


def _proj_b_kernel(x_ref, g_ref, w_ref, hy_ref, gt_ref):
    ub = _rms(x_ref[...], g_ref[...]).astype(BF16)
    hy_ref[...] = _dot(ub, w_ref[:, :3 * HY_WIDTH]).astype(BF16)
    gt_ref[...] = jax.nn.sigmoid(_dot(ub, w_ref[:, 3 * HY_WIDTH:])).astype(BF16)


def _proj_b(x2d, g_mix, w_b):
    t = x2d.shape[0]
    tm = TOK_TILE

    def row(w):
        return pl.BlockSpec((tm, w), lambda i: (i, 0))

    return pl.pallas_call(
        _proj_b_kernel,
        out_shape=(jax.ShapeDtypeStruct((t, 3 * HY_WIDTH), BF16),
                   jax.ShapeDtypeStruct((t, N_BRANCH * D_MODEL), BF16)),
        grid=(t // tm,),
        in_specs=[row(D_MODEL), _const_spec((1, D_MODEL)), _const_spec(w_b.shape)],
        out_specs=(row(3 * HY_WIDTH), row(N_BRANCH * D_MODEL)),
        compiler_params=_cparams(("parallel",), 48),
        name="proj_b",
    )(x2d, g_mix, w_b)


def _attn_kernel(safe_ref, q_ref, k_ref, vt_ref, o_ref, acc_scr, l_scr, p0_scr, p1_scr,
                 *, heads, groups, kc, key_of_head, value_of_head, nk, nk_tile):
    tv = vt_ref.shape[3]
    tk = nk_tile
    tq = ATT_TQ
    ones = jnp.ones((ONES_ROWS, tk), BF16)
    dv = MLA_V
    n_sub = q_ref.shape[1] // tq
    safe = safe_ref[pl.program_id(0), pl.program_id(1)] != 0

    def key_tile(j, h):
        kb = key_of_head[h]
        rows = j * tk if isinstance(j, int) else pl.multiple_of(j * tk, tk)
        return k_ref[0, pl.ds(rows, tk), kb * kc:(kb + 1) * kc]

    def values(j, h):
        vb = value_of_head[h]
        per = tk // tv
        return jnp.concatenate([vt_ref[0, j * per + u, vb * dv:(vb + 1) * dv, :] for u in range(per)], axis=1)

    def value_tile(j, h):
        return jnp.concatenate([values(j, h), ones], axis=0)

    def write_out(acc, l, h, sub):
        o_ref[0, h * dv:(h + 1) * dv, sub * tq:(sub + 1) * tq] = (acc / l).astype(BF16)

    @pl.when(safe)
    def _():
        bufs = (p0_scr, p1_scr)
        streams = [(sub, g) for sub in range(n_sub) for g in range(groups)]

        def probs(si, stream, j, hh, buf):
            sub, g = stream
            h = g * heads + hh
            st = _dot_nt(key_tile(j, h), q_ref[0, sub * tq:(sub + 1) * tq, h * kc:(h + 1) * kc])
            e = jnp.exp2(st)
            buf[hh] = e.astype(BF16)
            l_scr[si % 2, hh] += jnp.sum(e, axis=0, keepdims=True)

        def accumulate(si, stream, j, hh, buf):
            acc_scr[si % 2, hh] += _dot(values(j, stream[1] * heads + hh), buf[hh])

        acc_scr[...] = jnp.zeros(acc_scr.shape, F32)
        l_scr[...] = jnp.zeros(l_scr.shape, F32)
        for hh in range(heads):
            probs(0, streams[0], 0, hh, bufs[0])
        for si, stream in enumerate(streams):
            def step(j, cur, nxt, si=si, stream=stream):
                for hh in range(heads):
                    probs(si, stream, j + 1, hh, nxt)
                    accumulate(si, stream, j, hh, cur)

            unroll = min(ATT_UNROLL, nk)

            def body(jj, carry, step=step):
                for u in range(unroll):
                    step(unroll * jj + u, bufs[u % 2], bufs[1 - u % 2])
                return carry

            lax.fori_loop(0, nk // unroll - 1, body, 0)
            for u in range(unroll - 1):
                step(nk - unroll + u, bufs[u % 2], bufs[1 - u % 2])
            for hh in range(heads):
                if si + 1 < len(streams):
                    probs(si + 1, streams[si + 1], 0, hh, bufs[0])
                accumulate(si, stream, nk - 1, hh, bufs[1])
            sub, g = stream
            for hh in range(heads):
                write_out(acc_scr[si % 2, hh], l_scr[si % 2, hh], g * heads + hh, sub)
                if si + 2 < len(streams):
                    acc_scr[si % 2, hh] = jnp.zeros((dv, tq), F32)
                    l_scr[si % 2, hh] = jnp.zeros((1, tq), F32)

    @pl.when(jnp.logical_not(safe))
    def _():
        for h in range(groups * heads):
            for sub in range(n_sub):
                qh = q_ref[0, sub * tq:(sub + 1) * tq, h * kc:(h + 1) * kc]

                def body(j, carry, qh=qh, h=h):
                    m, acc = carry
                    st = _dot_nt(key_tile(j, h), qh)
                    m_new = jnp.maximum(m, jnp.max(st, axis=0, keepdims=True))
                    p = jnp.exp2(st - m_new).astype(BF16)
                    return m_new, jnp.exp2(m - m_new) * acc + _dot(value_tile(j, h), p)

                init = (jnp.full((1, tq), NEG_BIG, F32), jnp.zeros((dv + ONES_ROWS, tq), F32))
                _, acc = lax.fori_loop(0, nk, body, init)
                write_out(acc[:dv], acc[dv:dv + 1], h, sub)


def _attention(safe, q, k, vt, *, heads, kc, tk, key_of_head, value_of_head, name):
    b, l, qw = q.shape
    n_heads = qw // kc
    nv, tv = vt.shape[1], vt.shape[3]
    nk = l // tk
    tqb = ATT_TQB
    tq = ATT_TQ
    dv = MLA_V
    grid_spec = pltpu.PrefetchScalarGridSpec(
        num_scalar_prefetch=1,
        grid=(b, l // tqb),
        in_specs=[pl.BlockSpec((1, tqb, qw), lambda bi, i, s: (bi, i, 0)),
                  pl.BlockSpec((1, l, k.shape[2]), lambda bi, i, s: (bi, 0, 0), pipeline_mode=pl.Buffered(1)),
                  pl.BlockSpec((1, nv, vt.shape[2], tv), lambda bi, i, s: (bi, 0, 0, 0),
                               pipeline_mode=pl.Buffered(1))],
        out_specs=pl.BlockSpec((1, n_heads * dv, tqb), lambda bi, i, s: (bi, 0, i)),
        scratch_shapes=[pltpu.VMEM((2, heads, dv, tq), F32), pltpu.VMEM((2, heads, 1, tq), F32),
                        pltpu.VMEM((heads, tk, tq), BF16), pltpu.VMEM((heads, tk, tq), BF16)],
    )
    return pl.pallas_call(
        functools.partial(_attn_kernel, heads=heads, groups=n_heads // heads, kc=kc, key_of_head=key_of_head,
                          value_of_head=value_of_head, nk=nk, nk_tile=tk),
        out_shape=jax.ShapeDtypeStruct((b, n_heads * dv, l), BF16),
        grid_spec=grid_spec,
        compiler_params=_cparams(("parallel", "parallel"), 48),
        name=name,
    )(safe, q, k, vt)


def _hy_pre_kernel(x0_ref, x1_ref, v_ref, w_ref, b_ref, s_ref, x0o_ref):
    rows = x0_ref.shape[1]
    t = lax.broadcasted_iota(jnp.int32, (rows, LANES), 0)
    not_first = t > 0
    not_last = t < rows - 1

    def conv(ref, j):
        a = ref[0].astype(F32)
        prev = jnp.where(not_first, pltpu.roll(a, 1, 0), 0.0)
        nxt = jnp.where(not_last, pltpu.roll(a, rows - 1, 0), 0.0)
        return prev * w_ref[0, j] + a * w_ref[1, j] + nxt * w_ref[2, j] + b_ref[j]

    x0o_ref[0] = conv(x0_ref, 0).astype(BF16)
    s_ref[0] = (conv(x1_ref, 1) * conv(v_ref, 2)).astype(BF16)


def _hy_pre(hy, w_short, b_short):
    b, l, _ = hy.shape
    nc = HY_WIDTH // LANES
    w4 = w_short.reshape(3, 3, nc, 1, LANES)
    b4 = b_short.reshape(3, nc, 1, LANES)

    def seg(j):
        return pl.BlockSpec((1, l, LANES), lambda bi, c, j=j: (bi, 0, j * nc + c))

    out = pl.BlockSpec((1, l, LANES), lambda bi, c: (bi, 0, c))
    return pl.pallas_call(
        _hy_pre_kernel,
        out_shape=(jax.ShapeDtypeStruct((b, l, HY_WIDTH), BF16), jax.ShapeDtypeStruct((b, l, HY_WIDTH), BF16)),
        grid=(b, nc),
        in_specs=[seg(0), seg(1), seg(2),
                  pl.BlockSpec((3, 3, None, 1, LANES), lambda bi, c: (0, 0, c, 0, 0)),
                  pl.BlockSpec((3, None, 1, LANES), lambda bi, c: (0, c, 0, 0))],
        out_specs=(out, out),
        compiler_params=_cparams(("parallel", "parallel"), 48),
        name="hy_pre",
    )(hy, hy, hy, w4, b4)


def _left_mm_kernel(f_ref, x_ref, o_ref, *, hi):
    if hi:
        o_ref[0] = _dot3(f_ref[...], x_ref[0]).astype(o_ref.dtype)
    else:
        o_ref[0] = _dot(f_ref[...], x_ref[0]).astype(o_ref.dtype)


def _left_mm(f, x, out_dtype, hi=False):
    b, k, n = x.shape
    m = f.shape[0]
    cb = min(HY_COLS, n)
    return pl.pallas_call(
        functools.partial(_left_mm_kernel, hi=hi),
        out_shape=jax.ShapeDtypeStruct((b, m, n), out_dtype),
        grid=(b, n // cb),
        in_specs=[_const_spec((m, k)), pl.BlockSpec((1, k, cb), lambda bi, c: (bi, 0, c))],
        out_specs=pl.BlockSpec((1, m, cb), lambda bi, c: (bi, 0, c)),
        compiler_params=_cparams(("parallel", "parallel"), 48),
        name="hy_outer_dft",
    )(f, x)


def _hy_mid_kernel(f_ref, kf_ref, g_ref, a_ref, o_ref):
    n2 = DFT_N2
    kr, ki = kf_ref[0, :n2], kf_ref[0, n2:]
    for bi in range(a_ref.shape[0]):
        x = _dot(f_ref[0], a_ref[bi, 0])
        xr, xi = x[:n2], x[n2:]
        y = jnp.concatenate([xr * kr - xi * ki, xr * ki + xi * kr], axis=0).astype(BF16)
        o_ref[bi, 0] = _dot(g_ref[0], y).astype(BF16)


def _hy_mid(a4, f2, kf, g2):
    b, n1, r, c = a4.shape
    mat = pl.BlockSpec((1, r, r), lambda i: (i, 0, 0))
    dat = pl.BlockSpec((b, 1, r, c), lambda i: (0, i, 0, 0))
    return pl.pallas_call(
        _hy_mid_kernel,
        out_shape=jax.ShapeDtypeStruct(a4.shape, BF16),
        grid=(n1,),
        in_specs=[mat, pl.BlockSpec((1, r, c), lambda i: (i, 0, 0)), mat, dat],
        out_specs=dat,
        compiler_params=_cparams(("parallel",), 48),
        name="hy_mid",
    )(f2, kf, g2, a4)


def _hy_post_kernel(g_ref, bi_ref, s_ref, x0_ref, bias_ref, o_ref):
    y = _dot(g_ref[...], bi_ref[0]) + s_ref[0].astype(F32) * bias_ref[...]
    o_ref[0] = (x0_ref[0].astype(F32) * y).astype(BF16)


def _hy_post(g1, bi2, s2, x02, bias_t):
    b, k, n = bi2.shape
    m = g1.shape[0]
    cb = HY_COLS
    dat = pl.BlockSpec((1, m, cb), lambda bi, c: (bi, 0, c))
    return pl.pallas_call(
        _hy_post_kernel,
        out_shape=jax.ShapeDtypeStruct((b, m, n), BF16),
        grid=(b, n // cb),
        in_specs=[_const_spec((m, k)), pl.BlockSpec((1, k, cb), lambda bi, c: (bi, 0, c)), dat, dat,
                  _const_spec((1, cb))],
        out_specs=dat,
        compiler_params=_cparams(("parallel", "parallel"), 48),
        name="hy_post",
    )(g1, bi2, s2, x02, bias_t)


def _hy_filter_kernel(z_ref, win_ref, w1_ref, b1_ref, w2_ref, b2_ref, w3_ref, fr_ref, o_ref, *, half_tiles):
    fr = fr_ref[...]
    h = jnp.sin(fr * (_dot3(z_ref[...], w1_ref[...]) + b1_ref[...]))
    h = jnp.sin(fr * (_dot3(h, w2_ref[...]) + b2_ref[...]))
    h3 = _dot3(h, w3_ref[...])
    backward = pl.program_id(0) >= half_tiles
    o_ref[...] = jnp.where(backward, h3[:, HY_WIDTH:], h3[:, :HY_WIDTH]) * win_ref[...]


def _hy_filter(z2, win2, w1p, b1, w2, b2, w3, fr):
    n = z2.shape[0]
    tl = 512
    return pl.pallas_call(
        functools.partial(_hy_filter_kernel, half_tiles=n // (2 * tl)),
        out_shape=jax.ShapeDtypeStruct((n, HY_WIDTH), F32),
        grid=(n // tl,),
        in_specs=[pl.BlockSpec((tl, LANES), lambda i: (i, 0)), pl.BlockSpec((tl, HY_WIDTH), lambda i: (i, 0)),
                  _const_spec(w1p.shape), _const_spec(b1.shape), _const_spec(w2.shape), _const_spec(b2.shape),
                  _const_spec(w3.shape), _const_spec(fr.shape)],
        out_specs=pl.BlockSpec((tl, HY_WIDTH), lambda i: (i, 0)),
        compiler_params=_cparams(("parallel",), 32),
        name="hy_filter",
    )(z2, win2, w1p, b1, w2, b2, w3, fr)


def _hy_spec_kernel(f_ref, a_ref, o_ref):
    o_ref[0] = _dot3(f_ref[0], a_ref[0])


def _hy_spec(f2, a3):
    n1, r, c = a3.shape
    return pl.pallas_call(
        _hy_spec_kernel,
        out_shape=jax.ShapeDtypeStruct(a3.shape, F32),
        grid=(n1,),
        in_specs=[pl.BlockSpec((1, r, r), lambda i: (i, 0, 0)), pl.BlockSpec((1, r, c), lambda i: (i, 0, 0))],
        out_specs=pl.BlockSpec((1, r, c), lambda i: (i, 0, 0)),
        compiler_params=_cparams(("parallel",), 32),
        name="hy_filter_spectrum",
    )(f2, a3)


def _merge_kernel(x_ref, yat_ref, yb_ref, yct_ref, gt_ref, wb_ref, wo_ref, o_ref):
    d = D_MODEL
    ba = _dot_tn(yat_ref[0], wb_ref[0])
    bb = _dot(yb_ref[0], wb_ref[1])
    bc = _dot_tn(yct_ref[0], wb_ref[2])
    merged = (gt_ref[0, :, 0:d].astype(F32) * ba + gt_ref[0, :, d:2 * d].astype(F32) * bb
              + gt_ref[0, :, 2 * d:3 * d].astype(F32) * bc)
    o_ref[0] = x_ref[0] + _dot(merged.astype(BF16), wo_ref[...])


def _merge(x, yat, yb, yct, gates, wb, wo):
    b, l, _ = x.shape
    tm = TOK_TILE

    def tok(w):
        return pl.BlockSpec((1, tm, w), lambda bi, i: (bi, i, 0))

    tr = pl.BlockSpec((1, BRANCH_W, tm), lambda bi, i: (bi, 0, i))
    return pl.pallas_call(
        _merge_kernel,
        out_shape=jax.ShapeDtypeStruct(x.shape, F32),
        grid=(b, l // tm),
        in_specs=[tok(D_MODEL), tr, tok(BRANCH_W), tr, tok(N_BRANCH * D_MODEL),
                  _const_spec(wb.shape), _const_spec(wo.shape)],
        out_specs=tok(D_MODEL),
        compiler_params=_cparams(("parallel", "parallel"), 48),
        name="merge",
    )(x, yat, yb, yct, gates, wb, wo)


def _rope_tables(l):
    rows = l // GRID_W
    row = jnp.repeat(jnp.arange(rows, dtype=F32), GRID_W)
    col = jnp.tile(jnp.arange(GRID_W, dtype=F32), rows)

    def tab(d_rot):
        n_freq = d_rot // 4
        inv = ROPE_THETA ** (-jnp.arange(n_freq, dtype=F32) / n_freq)
        ang = jnp.concatenate([row[:, None] * inv, col[:, None] * inv], axis=-1)
        c = jnp.repeat(jnp.cos(ang), 2, axis=-1)
        s = jnp.repeat(jnp.sin(ang), 2, axis=-1) * jnp.tile(jnp.array([-1.0, 1.0], F32), d_rot // 2)
        return c, s

    ca, sa = tab(GQA_HEAD_DIM)
    cm, sm = tab(MLA_ROPE)
    tail = jnp.zeros((l, LANES - MLA_ROPE_LANE - MLA_ROPE), F32)
    cm = jnp.concatenate([jnp.ones((l, MLA_ROPE_LANE), F32), cm, tail], axis=1)
    sm = jnp.concatenate([jnp.zeros((l, MLA_ROPE_LANE), F32), sm, tail], axis=1)
    return (jnp.tile(ca, (1, 2)), jnp.tile(sa, (1, 2)), cm, sm)


def _hy_positions(l):
    r = jnp.arange(l, dtype=jnp.int32)
    pos = jnp.concatenate([r, (l - r) % l]).astype(F32)[:, None]
    live = jnp.concatenate([jnp.ones((l,), F32), (r > 0).astype(F32)])[:, None]
    t = pos * (1.0 / (l - 1))
    w = 2.0 * math.pi * pos / l
    f = jnp.linspace(1e-4, HY_BANDS - 1, HY_BANDS, dtype=F32)[None, :]
    z2 = jnp.concatenate([t, jnp.cos(f * w), -jnp.sin(f * w), jnp.zeros((2 * l, LANES - HY_EMB), F32)], axis=-1)
    max_decay = math.log(HY_TARGET) / HY_FAST
    min_decay = math.log(HY_TARGET) / HY_SLOW
    deltas = jnp.abs(jnp.linspace(min_decay, max_decay, HY_WIDTH, dtype=F32))
    return z2, jnp.exp(-t * deltas[None, :]) * live


def _hy_k1_count(l):
    n1 = 2 * l // DFT_N2
    return -(-(n1 // 2 + 1) // 8) * 8


def _dft_tables(l):
    n = 2 * l
    n2 = DFT_N2
    n1 = n // n2
    n1c = _hy_k1_count(l)
    two_pi = 2.0 * math.pi
    i1 = jnp.arange(n1, dtype=jnp.int32)
    ang1 = ((i1[:n1c, None] * i1[None, :]) % n1).astype(F32) * (two_pi / n1)
    live = (i1[:n1c] <= n1 // 2)[:, None]
    c1, s1 = jnp.where(live, jnp.cos(ang1), 0.0), jnp.where(live, jnp.sin(ang1), 0.0)
    f1_full = jnp.stack([c1, -s1], axis=1).reshape(2 * n1c, n1)
    fold = jnp.where((i1[:n1c] == 0) | (i1[:n1c] == n1 // 2), 1.0, 2.0)[:, None] * (1.0 / n)
    g1 = jnp.stack([(c1 * fold)[:, :n1 // 2].T, (-s1 * fold)[:, :n1 // 2].T], axis=2).reshape(n1 // 2, 2 * n1c)
    i2 = jnp.arange(n2, dtype=jnp.int32)
    kk = i1[:n1c, None, None] + n1 * i2[None, :, None]
    ang2 = ((kk * i2[None, None, :]) % n).astype(F32) * (two_pi / n)
    c2, s2 = jnp.cos(ang2), jnp.sin(ang2)
    f2 = jnp.concatenate([jnp.concatenate([c2, s2], axis=2), jnp.concatenate([-s2, c2], axis=2)], axis=1)
    c2t, s2t = jnp.swapaxes(c2, 1, 2), jnp.swapaxes(s2, 1, 2)
    g2 = jnp.concatenate([jnp.concatenate([c2t, -s2t], axis=2), jnp.concatenate([s2t, c2t], axis=2)], axis=1)
    return f1_full, g1, f2, g2


def _swap_pairs(n):
    return np.arange(n) ^ 1


def _proj_a_columns():
    zero = IN_WIDTH
    cols = np.full((PA_END,), zero, np.int64)
    q0, k0, v0, _, cq0, ckv0, kr0, _ = IN_OFFS[:8]
    for h in range(GQA_HEADS):
        dst = h * LANES + (h // (GQA_HEADS // GQA_KV_HEADS)) * GQA_HEAD_DIM
        src = q0 + h * GQA_HEAD_DIM + np.arange(GQA_HEAD_DIM)
        cols[PA_Q + dst:PA_Q + dst + GQA_HEAD_DIM] = src
    cols[PA_K:PA_K + 128] = k0 + np.arange(128)
    cols[PA_V:PA_V + 128] = v0 + np.arange(128)
    cols[PA_CQ:PA_CQ + MLA_Q_RANK] = cq0 + np.arange(MLA_Q_RANK)
    cols[PA_CKV:PA_CKV + MLA_KV_RANK] = ckv0 + np.arange(MLA_KV_RANK)
    r0 = MLA_ROPE_LANE
    cols[PA_KR + r0:PA_KR + r0 + MLA_ROPE] = kr0 + np.arange(MLA_ROPE)
    return cols


def _mla_q_columns():
    hd = MLA_NOPE + MLA_ROPE
    zero = MLA_HEADS * hd
    main = np.full((MLA_HEADS * MLA_KC,), zero, np.int64)
    for h in range(MLA_HEADS):
        b0 = h * MLA_KC
        main[b0:b0 + MLA_NOPE] = h * hd + np.arange(MLA_NOPE)
        r0 = b0 + MLA_ROPE_LANE
        main[r0:r0 + MLA_ROPE] = h * hd + MLA_NOPE + np.arange(MLA_ROPE)
    return main


def _mla_kv_columns():
    hd = MLA_NOPE + MLA_V
    zero = MLA_HEADS * hd
    knope = np.full((MLA_HEADS * MLA_KC,), zero, np.int64)
    for h in range(MLA_HEADS):
        knope[h * MLA_KC:h * MLA_KC + MLA_NOPE] = h * hd + np.arange(MLA_NOPE)
    val = np.concatenate([h * hd + MLA_NOPE + np.arange(MLA_V) for h in range(MLA_HEADS)])
    return np.concatenate([knope, val])


def _take_cols(w, cols):
    wz = jnp.concatenate([w, jnp.zeros((w.shape[0], 1), w.dtype)], axis=1)
    return jnp.take(wz, jnp.asarray(cols, jnp.int32), axis=1).astype(BF16)


def _tile2(g):
    return jnp.tile(g, 2)[None, :]


def _encoder(x, lw, g_final):
    b, l, d = x.shape
    n2 = DFT_N2
    n1 = 2 * l // n2
    c = HY_WIDTH
    tabs = _rope_tables(l)
    z2, win2 = _hy_positions(l)
    f1_full, g1, f2, g2 = _dft_tables(l)
    f1_b = f1_full[:, :n1 // 2].astype(BF16)
    g1_b = g1.astype(BF16)
    f2_b = f2.astype(BF16)
    g2_b = g2.astype(BF16)
    for li, w in enumerate(lw):
        x2 = _ffn(x.reshape(b * l, d), w["g_ffn1"], w["wg1"], w["wu1"], w["wd1"], g_final, False)
        x = x2.reshape(b, l, d)
        qg, kg, vgt, qm, km, vmt, bnd = _proj_a(x, w["g_mix"], w["w_a"], tabs, w["gq"], w["gqs"], w["gk"],
                                                w["gks"], w["gcq"], w["gckv"], w["wuq"], w["wukv"])
        hy, gates = _proj_b(x2, w["g_mix"], w["w_b"])
        top = bnd[:, :, :4, 0]
        limit = ATT_SAFE_SCORE ** 2
        safe_a = (top[:, :, 0] * jnp.max(top[:, :, 1], axis=1, keepdims=True) <= limit).astype(jnp.int32)
        safe_m = (top[:, :, 2] * jnp.max(top[:, :, 3], axis=1, keepdims=True) <= limit).astype(jnp.int32)
        per_kv = GQA_HEADS // GQA_KV_HEADS
        yat = _attention(safe_a, qg, kg, vgt, heads=ATT_HEADS, kc=GQA_KC, tk=ATT_TK_GQA,
                         key_of_head=(0,) * GQA_HEADS,
                         value_of_head=tuple(h // per_kv for h in range(GQA_HEADS)), name="attn_gqa")
        yct = _attention(safe_m, qm, km, vmt, heads=ATT_HEADS, kc=MLA_KC, tk=ATT_TK_MLA,
                         key_of_head=tuple(range(MLA_HEADS)),
                         value_of_head=tuple(range(MLA_HEADS)), name="attn_mla")
        kc_time = _hy_filter(z2, win2, w["w1p"], w["b1"], w["w2"], w["b2"], w["w3"], w["fr"])
        n1c = _hy_k1_count(l)
        ka =_left_mm(f1_full, kc_time.reshape(1, n1, n2 * c), F32, hi=True)
        kf = _hy_spec(f2, ka.reshape(n1c, 2 * n2, c))
        s, x0 = _hy_pre(hy.reshape(b, l, 3 * c), w["w_short"], w["b_short"])
        a = _left_mm(f1_b, s.reshape(b, n1 // 2, n2 * c), BF16)
        bi = _hy_mid(a.reshape(b, n1c, 2 * n2, c), f2_b, kf, g2_b)
        yb = _hy_post(g1_b, bi.reshape(b, 2 * n1c, n2 * c), s.reshape(b, n1 // 2, n2 * c),
                      x0.reshape(b, n1 // 2, n2 * c), w["bias_t"])
        x = _merge(x, yat, yb.reshape(b, l, c), yct, gates.reshape(b, l, N_BRANCH * d), w["wb"], w["wo"])
        x2 = _ffn(x.reshape(b * l, d), w["g_ffn2"], w["wg2"], w["wu2"], w["wd2"], g_final, li == len(lw) - 1)
        x = x2.reshape(b, l, d)
    return x


def kernel(x_prompt, x_sample, g_ffn1, w_ffn1_gate, w_ffn1_up, w_ffn1_down, g_mix, w_in, g_qnorm, g_knorm,
           w_hy_short, b_hy_short, w_hy_f1, b_hy_f1, w_hy_f2, b_hy_f2, w_hy_f3, hy_sin_freq, hy_bias,
           g_mla_q, w_mla_uq, g_mla_kv, w_mla_ukv, w_branch, w_out, g_ffn2, w_ffn2_gate, w_ffn2_up,
           w_ffn2_down, g_final):
    cols_a = _proj_a_columns()
    uq_main = _mla_q_columns()
    ukv_cols = _mla_kv_columns()
    sw64 = _swap_pairs(GQA_HEAD_DIM)
    hy0 = IN_OFFS[3]
    gt0 = IN_OFFS[7]
    lw = []
    for l in range(DEPTH):
        lw.append(dict(
            g_ffn1=g_ffn1[l][None], wg1=w_ffn1_gate[l].astype(BF16), wu1=w_ffn1_up[l].astype(BF16),
            wd1=w_ffn1_down[l].astype(BF16),
            g_ffn2=g_ffn2[l][None], wg2=w_ffn2_gate[l].astype(BF16), wu2=w_ffn2_up[l].astype(BF16),
            wd2=w_ffn2_down[l].astype(BF16),
            g_mix=g_mix[l][None],
            w_a=_take_cols(w_in[l], cols_a),
            w_b=jnp.concatenate([w_in[l][:, hy0:hy0 + 3 * HY_WIDTH], w_in[l][:, gt0:]], axis=1).astype(BF16),
            gq=_tile2(g_qnorm[l]), gqs=_tile2(g_qnorm[l][sw64]),
            gk=_tile2(g_knorm[l]), gks=_tile2(g_knorm[l][sw64]),
            gcq=g_mla_q[l][None], gckv=g_mla_kv[l][None],
            wuq=_take_cols(w_mla_uq[l], uq_main),
            wukv=_take_cols(w_mla_ukv[l], ukv_cols),
            w_short=w_hy_short[l], b_short=b_hy_short[l],
            w1p=jnp.pad(w_hy_f1[l], ((0, LANES - HY_EMB), (0, 0))), b1=b_hy_f1[l][None],
            w2=w_hy_f2[l], b2=b_hy_f2[l][None], w3=w_hy_f3[l], fr=hy_sin_freq[l][None],
            bias_t=jnp.tile(hy_bias[l], HY_COLS // HY_WIDTH)[None],
            wb=w_branch[l].astype(BF16), wo=w_out[l].astype(BF16),
        ))
    gf = g_final[None]
    return _encoder(x_prompt, lw, gf), _encoder(x_sample, lw, gf)
```

```python
import functools
import math

import numpy as np
import jax
import jax.numpy as jnp
from jax import lax
from jax.experimental import pallas as pl
from jax.experimental.pallas import tpu as pltpu

F32 = jnp.float32
BF16 = jnp.bfloat16

D_MODEL = 1024
DEPTH = 2
GRID_W = 64
ROPE_THETA = 10000.0
EPS = 1e-6
D_FF = 2816
N_BRANCH = 3
BRANCH_W = 512
GQA_HEADS = 8
GQA_KV_HEADS = 2
GQA_HEAD_DIM = 64
HY_WIDTH = 512
HY_ORDER = 64
HY_EMB = 33
HY_BANDS = (HY_EMB - 1) // 2
HY_TARGET = 1e-2
HY_FAST = 0.3
HY_SLOW = 1.5
MLA_HEADS = 8
MLA_Q_RANK = 256
MLA_KV_RANK = 128
MLA_NOPE = 64
MLA_ROPE = 32
MLA_V = 64
IN_WIDTHS = (512, 128, 128, 3 * HY_WIDTH, MLA_Q_RANK, MLA_KV_RANK, MLA_ROPE, N_BRANCH * D_MODEL)
IN_OFFS = tuple(int(c) for c in np.cumsum((0,) + IN_WIDTHS))
IN_WIDTH = IN_OFFS[-1]

LANES = 128
V7X_VMEM_BYTES = 64 * 1024 * 1024

TOK_TILE = 512
ATT_TQ = 256
ATT_TQB = 512
ATT_HEADS = 4
ATT_UNROLL = 4
ATT_SAFE_SCORE = 64.0
ATT_TK_GQA = 2 * TOK_TILE
ATT_TK_MLA = 2 * TOK_TILE
FF_CHUNKS = ((0, 1024), (1024, 2048), (2048, D_FF))
DFT_N2 = 128
HY_COLS = 8192
ONES_ROWS = 16
NEG_BIG = -1e30
LOG2E = math.log2(math.e)


def _cparams(sem, vmem_mb):
    return pltpu.CompilerParams(dimension_semantics=sem, vmem_limit_bytes=vmem_mb * 1024 * 1024)


def _const_spec(shape):
    nd = len(shape)
    return pl.BlockSpec(shape, lambda *_: (0,) * nd, pipeline_mode=pl.Buffered(1))


def _rms(x, g):
    return x * lax.rsqrt(jnp.mean(x * x, axis=-1, keepdims=True) + EPS) * g


def _dot(a, b):
    return jnp.dot(a, b, preferred_element_type=F32)


def _dot_nt(a, b):
    return lax.dot_general(a, b, (((1,), (1,)), ((), ())), preferred_element_type=F32)


def _dot_tn(a, b):
    return lax.dot_general(a, b, (((0,), (0,)), ((), ())), preferred_element_type=F32)


def _split(a):
    hi = a.astype(BF16)
    return hi, (a - hi.astype(F32)).astype(BF16)


def _dot3(a, b):
    ah, al = _split(a)
    bh, bl = _split(b)
    return _dot(ah, bh) + (_dot(ah, bl) + _dot(al, bh))


def _ffn_kernel(x_ref, g_ref, wg_ref, wu_ref, wd_ref, gf_ref, o_ref, *, final):
    x = x_ref[...]
    xb = _rms(x, g_ref[...]).astype(BF16)
    acc = jnp.zeros_like(x)
    for c0, c1 in FF_CHUNKS:
        gate = _dot(xb, wg_ref[:, c0:c1])
        up = _dot(xb, wu_ref[:, c0:c1])
        h = (gate * jax.nn.sigmoid(gate) * up).astype(BF16)
        acc = acc + _dot(h, wd_ref[c0:c1, :])
    y = x + 0.5 * acc
    if final:
        y = _rms(y, gf_ref[...])
    o_ref[...] = y


def _ffn(x2d, g, wg, wu, wd, g_final, final):
    t = x2d.shape[0]
    row = pl.BlockSpec((TOK_TILE, D_MODEL), lambda i: (i, 0))
    return pl.pallas_call(
        functools.partial(_ffn_kernel, final=final),
        out_shape=jax.ShapeDtypeStruct((t, D_MODEL), F32),
        grid=(t // TOK_TILE,),
        in_specs=[row, _const_spec((1, D_MODEL)), _const_spec((D_MODEL, D_FF)),
                  _const_spec((D_MODEL, D_FF)), _const_spec((D_FF, D_MODEL)), _const_spec((1, D_MODEL))],
        out_specs=row,
        compiler_params=_cparams(("parallel",), 48),
        name="ffn",
    )(x2d, g, wg, wu, wd, g_final)


PA_Q, PA_QS, PA_K, PA_KS, PA_V, PA_CQ, PA_CKV, PA_KR, PA_KRS, PA_END = (
    0, 1024, 2048, 2176, 2304, 2432, 2688, 2816, 2944, 3072)
GQA_KC = 128
MLA_KC = 128
MLA_ROPE_LANE = MLA_NOPE


def _sq_norm(a_bf16):
    a = a_bf16.astype(F32)
    return jnp.sum(a * a, axis=-1, keepdims=True)


def _proj_a_kernel(x_ref, g_ref, w_ref, cosa_ref, sina_ref, cosm_ref, sinm_ref,
                   gq_ref, gqs_ref, gk_ref, gks_ref, gcq_ref, gckv_ref,
                   wuq_ref, wuqs_ref, wukv_ref,
                   qg_ref, kg_ref, vgt_ref, qm_ref, km_ref, vmt_ref, bnd_ref):
    ub = _rms(x_ref[0], g_ref[...]).astype(BF16)
    cosa, sina = cosa_ref[...], sina_ref[...]
    cosm, sinm = cosm_ref[...], sinm_ref[...]

    zq = _dot(ub, w_ref[:, PA_Q:PA_QS])
    zqs = _dot(ub, w_ref[:, PA_QS:PA_K])
    tq_c = gq_ref[...] * cosa
    tq_s = gqs_ref[...] * sina
    scale_a = GQA_HEAD_DIM ** -0.5 * LOG2E
    qa_n2 = None
    for h in range(GQA_HEADS):
        a = zq[:, h * LANES:(h + 1) * LANES]
        a_sw = zqs[:, h * LANES:(h + 1) * LANES]
        r = lax.rsqrt(jnp.sum(a * a, axis=-1, keepdims=True) * (1.0 / GQA_HEAD_DIM) + EPS)
        qb = ((a * tq_c + a_sw * tq_s) * (r * scale_a)).astype(BF16)
        qg_ref[0, :, h * LANES:(h + 1) * LANES] = qb
        qa_n2 = _sq_norm(qb) if qa_n2 is None else jnp.maximum(qa_n2, _sq_norm(qb))

    zk = _dot(ub, w_ref[:, PA_K:PA_KS])
    zks = _dot(ub, w_ref[:, PA_KS:PA_V])
    first = lax.broadcasted_iota(jnp.int32, zk.shape, 1) < GQA_HEAD_DIM
    sq = zk * zk
    r0 = lax.rsqrt(jnp.sum(jnp.where(first, sq, 0.0), axis=-1, keepdims=True) * (1.0 / GQA_HEAD_DIM) + EPS)
    r1 = lax.rsqrt(jnp.sum(jnp.where(first, 0.0, sq), axis=-1, keepdims=True) * (1.0 / GQA_HEAD_DIM) + EPS)
    kb = ((zk * (gk_ref[...] * cosa) + zks * (gks_ref[...] * sina)) * jnp.where(first, r0, r1)).astype(BF16)
    kg_ref[0] = kb
    kb32 = kb.astype(F32)
    kb_sq = kb32 * kb32
    ka_n2 = jnp.maximum(jnp.sum(jnp.where(first, kb_sq, 0.0), axis=-1, keepdims=True),
                        jnp.sum(jnp.where(first, 0.0, kb_sq), axis=-1, keepdims=True))

    vgt_ref[0, 0] = _dot(ub, w_ref[:, PA_V:PA_CQ]).T.astype(BF16)

    scale_m = (MLA_NOPE + MLA_ROPE) ** -0.5 * LOG2E
    cqn = _rms(_dot(ub, w_ref[:, PA_CQ:PA_CKV]), gcq_ref[...]).astype(BF16)
    zq2 = _dot(cqn, wuq_ref[...])
    zq2s = _dot(cqn, wuqs_ref[...])
    qm_n2 = None
    for h in range(MLA_HEADS):
        blk = slice(h * MLA_KC, (h + 1) * MLA_KC)
        qb = ((zq2[:, blk] * cosm + zq2s[:, blk] * sinm) * scale_m).astype(BF16)
        qm_ref[0, :, blk] = qb
        qm_n2 = _sq_norm(qb) if qm_n2 is None else jnp.maximum(qm_n2, _sq_norm(qb))

    ckvn = _rms(_dot(ub, w_ref[:, PA_CKV:PA_KR]), gckv_ref[...]).astype(BF16)
    zkv = _dot(ckvn, wukv_ref[...])
    krope = _dot(ub, w_ref[:, PA_KR:PA_KRS]) * cosm + _dot(ub, w_ref[:, PA_KRS:PA_END]) * sinm
    km_n2 = None
    for h in range(MLA_HEADS):
        blk = slice(h * MLA_KC, (h + 1) * MLA_KC)
        kb = (zkv[:, blk] + krope).astype(BF16)
        km_ref[0, :, blk] = kb
        km_n2 = _sq_norm(kb) if km_n2 is None else jnp.maximum(km_n2, _sq_norm(kb))
    vmt_ref[0, 0] = zkv[:, MLA_HEADS * MLA_KC:].T.astype(BF16)

    row = lax.broadcasted_iota(jnp.int32, (8, LANES), 0)
    tops = [jnp.max(v, axis=0, keepdims=True) for v in (qa_n2, ka_n2, qm_n2, km_n2)]
    bnd_ref[0, 0] = jnp.where(row == 0, tops[0], jnp.where(row == 1, tops[1], jnp.where(row == 2, tops[2], tops[3])))


def _proj_a(x, g_mix, w_a, tabs, gq, gqs, gk, gks, gcq, gckv, wuq, wuqs, wukv):
    b, l, _ = x.shape
    tm = TOK_TILE
    nt = l // tm
    cosa, sina, cosm, sinm = tabs
    tab = pl.BlockSpec((tm, LANES), lambda bi, i: (i, 0))

    def tok(w):
        return pl.BlockSpec((1, tm, w), lambda bi, i: (bi, i, 0))

    def tr(rows):
        return pl.BlockSpec((1, 1, rows, tm), lambda bi, i: (bi, i, 0, 0))

    out_shape = (
        jax.ShapeDtypeStruct((b, l, GQA_HEADS * GQA_KC), BF16),
        jax.ShapeDtypeStruct((b, l, GQA_KC), BF16),
        jax.ShapeDtypeStruct((b, nt, GQA_KV_HEADS * GQA_HEAD_DIM, tm), BF16),
        jax.ShapeDtypeStruct((b, l, MLA_HEADS * MLA_KC), BF16),
        jax.ShapeDtypeStruct((b, l, MLA_HEADS * MLA_KC), BF16),
        jax.ShapeDtypeStruct((b, nt, MLA_HEADS * MLA_V, tm), BF16),
        jax.ShapeDtypeStruct((b, nt, 8, LANES), F32),
    )
    return pl.pallas_call(
        _proj_a_kernel,
        out_shape=out_shape,
        grid=(b, nt),
        in_specs=[tok(D_MODEL), _const_spec((1, D_MODEL)), _const_spec(w_a.shape), tab, tab, tab, tab,
                  _const_spec((1, LANES)), _const_spec((1, LANES)), _const_spec((1, LANES)),
                  _const_spec((1, LANES)), _const_spec((1, MLA_Q_RANK)), _const_spec((1, MLA_KV_RANK)),
                  _const_spec(wuq.shape), _const_spec(wuqs.shape), _const_spec(wukv.shape)],
        out_specs=(tok(GQA_HEADS * GQA_KC), tok(GQA_KC), tr(GQA_KV_HEADS * GQA_HEAD_DIM),
                   tok(MLA_HEADS * MLA_KC), tok(MLA_HEADS * MLA_KC), tr(MLA_HEADS * MLA_V),
                   pl.BlockSpec((1, 1, 8, LANES), lambda bi, i: (bi, i, 0, 0))),
        compiler_params=_cparams(("parallel", "parallel"), 48),
        name="proj_a",
    )(x, g_mix, w_a, cosa, sina, cosm, sinm, gq, gqs, gk, gks, gcq, gckv, wuq, wuqs, wukv)


def _proj_b_kernel(x_ref, g_ref, w_ref, hy_ref, gt_ref):
    ub = _rms(x_ref[...], g_ref[...]).astype(BF16)
    hy_ref[...] = _dot(ub, w_ref[:, :3 * HY_WIDTH]).astype(BF16)
    gt_ref[...] = jax.nn.sigmoid(_dot(ub, w_ref[:, 3 * HY_WIDTH:])).astype(BF16)


def _proj_b(x2d, g_mix, w_b):
    t = x2d.shape[0]
    tm = TOK_TILE

    def row(w):
        return pl.BlockSpec((tm, w), lambda i: (i, 0))

    return pl.pallas_call(
        _proj_b_kernel,
        out_shape=(jax.ShapeDtypeStruct((t, 3 * HY_WIDTH), BF16),
                   jax.ShapeDtypeStruct((t, N_BRANCH * D_MODEL), BF16)),
        grid=(t // tm,),
        in_specs=[row(D_MODEL), _const_spec((1, D_MODEL)), _const_spec(w_b.shape)],
        out_specs=(row(3 * HY_WIDTH), row(N_BRANCH * D_MODEL)),
        compiler_params=_cparams(("parallel",), 48),
        name="proj_b",
    )(x2d, g_mix, w_b)


def _attn_kernel(safe_ref, q_ref, k_ref, vt_ref, o_ref, acc_scr, p0_scr, p1_scr,
                 *, heads, groups, kc, key_of_head, value_of_head, nk, nk_tile):
    tv = vt_ref.shape[3]
    tk = nk_tile
    tq = ATT_TQ
    ones = jnp.ones((ONES_ROWS, tk), BF16)
    dv = MLA_V
    n_sub = q_ref.shape[1] // tq
    safe = safe_ref[pl.program_id(0), pl.program_id(1)] != 0

    def key_tile(j, h):
        kb = key_of_head[h]
        rows = j * tk if isinstance(j, int) else pl.multiple_of(j * tk, tk)
        return k_ref[0, pl.ds(rows, tk), kb * kc:(kb + 1) * kc]

    def values(j, h):
        vb = value_of_head[h]
        per = tk // tv
        return jnp.concatenate([vt_ref[0, j * per + u, vb * dv:(vb + 1) * dv, :] for u in range(per)], axis=1)

    def value_tile(j, h):
        return jnp.concatenate([values(j, h), ones], axis=0)

    def write_out(acc, l, h, sub):
        o_ref[0, h * dv:(h + 1) * dv, sub * tq:(sub + 1) * tq] = (acc / l).astype(BF16)

    @pl.when(safe)
    def _():
        bufs = (p0_scr, p1_scr)
        streams = [(sub, g) for sub in range(n_sub) for g in range(groups)]

        def probs(si, stream, j, hh, buf):
            sub, g = stream
            h = g * heads + hh
            st = _dot_nt(key_tile(j, h), q_ref[0, sub * tq:(sub + 1) * tq, h * kc:(h + 1) * kc])
            buf[hh] = jnp.exp2(st).astype(BF16)

        def accumulate(si, stream, j, hh, buf):
            acc_scr[si % 2, hh] += _dot(value_tile(j, stream[1] * heads + hh), buf[hh])

        acc_scr[...] = jnp.zeros(acc_scr.shape, F32)
        for hh in range(heads):
            probs(0, streams[0], 0, hh, bufs[0])
        for si, stream in enumerate(streams):
            def step(j, cur, nxt, si=si, stream=stream):
                for hh in range(heads):
                    probs(si, stream, j + 1, hh, nxt)
                    accumulate(si, stream, j, hh, cur)

            unroll = min(ATT_UNROLL, nk)

            def body(jj, carry, step=step):
                for u in range(unroll):
                    step(unroll * jj + u, bufs[u % 2], bufs[1 - u % 2])
                return carry

            lax.fori_loop(0, nk // unroll - 1, body, 0)
            for u in range(unroll - 1):
                step(nk - unroll + u, bufs[u % 2], bufs[1 - u % 2])
            for hh in range(heads):
                if si + 1 < len(streams):
                    probs(si + 1, streams[si + 1], 0, hh, bufs[0])
                accumulate(si, stream, nk - 1, hh, bufs[1])
            sub, g = stream
            for hh in range(heads):
                acc = acc_scr[si % 2, hh]
                write_out(acc[:dv], acc[dv:dv + 1], g * heads + hh, sub)
                if si + 2 < len(streams):
                    acc_scr[si % 2, hh] = jnp.zeros((dv + ONES_ROWS, tq), F32)

    @pl.when(jnp.logical_not(safe))
    def _():
        for h in range(groups * heads):
            for sub in range(n_sub):
                qh = q_ref[0, sub * tq:(sub + 1) * tq, h * kc:(h + 1) * kc]

                def body(j, carry, qh=qh, h=h):
                    m, acc = carry
                    st = _dot_nt(key_tile(j, h), qh)
                    m_new = jnp.maximum(m, jnp.max(st, axis=0, keepdims=True))
                    p = jnp.exp2(st - m_new).astype(BF16)
                    return m_new, jnp.exp2(m - m_new) * acc + _dot(value_tile(j, h), p)

                init = (jnp.full((1, tq), NEG_BIG, F32), jnp.zeros((dv + ONES_ROWS, tq), F32))
                _, acc = lax.fori_loop(0, nk, body, init)
                write_out(acc[:dv], acc[dv:dv + 1], h, sub)


def _attention(safe, q, k, vt, *, heads, kc, tk, key_of_head, value_of_head, name):
    b, l, qw = q.shape
    n_heads = qw // kc
    nv, tv = vt.shape[1], vt.shape[3]
    nk = l // tk
    tqb = ATT_TQB
    tq = ATT_TQ
    dv = MLA_V
    grid_spec = pltpu.PrefetchScalarGridSpec(
        num_scalar_prefetch=1,
        grid=(b, l // tqb),
        in_specs=[pl.BlockSpec((1, tqb, qw), lambda bi, i, s: (bi, i, 0)),
                  pl.BlockSpec((1, l, k.shape[2]), lambda bi, i, s: (bi, 0, 0), pipeline_mode=pl.Buffered(1)),
                  pl.BlockSpec((1, nv, vt.shape[2], tv), lambda bi, i, s: (bi, 0, 0, 0),
                               pipeline_mode=pl.Buffered(1))],
        out_specs=pl.BlockSpec((1, n_heads * dv, tqb), lambda bi, i, s: (bi, 0, i)),
        scratch_shapes=[pltpu.VMEM((2, heads, dv + ONES_ROWS, tq), F32),
                        pltpu.VMEM((heads, tk, tq), BF16), pltpu.VMEM((heads, tk, tq), BF16)],
    )
    return pl.pallas_call(
        functools.partial(_attn_kernel, heads=heads, groups=n_heads // heads, kc=kc, key_of_head=key_of_head,
                          value_of_head=value_of_head, nk=nk, nk_tile=tk),
        out_shape=jax.ShapeDtypeStruct((b, n_heads * dv, l), BF16),
        grid_spec=grid_spec,
        compiler_params=_cparams(("parallel", "parallel"), 48),
        name=name,
    )(safe, q, k, vt)


def _hy_pre_kernel(x0_ref, x1_ref, v_ref, w_ref, b_ref, s_ref, x0o_ref):
    rows = x0_ref.shape[1]
    t = lax.broadcasted_iota(jnp.int32, (rows, LANES), 0)
    not_first = t > 0
    not_last = t < rows - 1

    def conv(ref, j):
        a = ref[0].astype(F32)
        prev = jnp.where(not_first, pltpu.roll(a, 1, 0), 0.0)
        nxt = jnp.where(not_last, pltpu.roll(a, rows - 1, 0), 0.0)
        return prev * w_ref[0, j] + a * w_ref[1, j] + nxt * w_ref[2, j] + b_ref[j]

    x0o_ref[0] = conv(x0_ref, 0).astype(BF16)
    s_ref[0] = (conv(x1_ref, 1) * conv(v_ref, 2)).astype(BF16)


def _hy_pre(hy, w_short, b_short):
    b, l, _ = hy.shape
    nc = HY_WIDTH // LANES
    w4 = w_short.reshape(3, 3, nc, 1, LANES)
    b4 = b_short.reshape(3, nc, 1, LANES)

    def seg(j):
        return pl.BlockSpec((1, l, LANES), lambda bi, c, j=j: (bi, 0, j * nc + c))

    out = pl.BlockSpec((1, l, LANES), lambda bi, c: (bi, 0, c))
    return pl.pallas_call(
        _hy_pre_kernel,
        out_shape=(jax.ShapeDtypeStruct((b, l, HY_WIDTH), BF16), jax.ShapeDtypeStruct((b, l, HY_WIDTH), BF16)),
        grid=(b, nc),
        in_specs=[seg(0), seg(1), seg(2),
                  pl.BlockSpec((3, 3, None, 1, LANES), lambda bi, c: (0, 0, c, 0, 0)),
                  pl.BlockSpec((3, None, 1, LANES), lambda bi, c: (0, c, 0, 0))],
        out_specs=(out, out),
        compiler_params=_cparams(("parallel", "parallel"), 48),
        name="hy_pre",
    )(hy, hy, hy, w4, b4)


def _left_mm_kernel(f_ref, x_ref, o_ref, *, hi):
    if hi:
        o_ref[0] = _dot3(f_ref[...], x_ref[0]).astype(o_ref.dtype)
    else:
        o_ref[0] = _dot(f_ref[...], x_ref[0]).astype(o_ref.dtype)


def _left_mm(f, x, out_dtype, hi=False):
    b, k, n = x.shape
    m = f.shape[0]
    cb = min(HY_COLS, n)
    return pl.pallas_call(
        functools.partial(_left_mm_kernel, hi=hi),
        out_shape=jax.ShapeDtypeStruct((b, m, n), out_dtype),
        grid=(b, n // cb),
        in_specs=[_const_spec((m, k)), pl.BlockSpec((1, k, cb), lambda bi, c: (bi, 0, c))],
        out_specs=pl.BlockSpec((1, m, cb), lambda bi, c: (bi, 0, c)),
        compiler_params=_cparams(("parallel", "parallel"), 48),
        name="hy_outer_dft",
    )(f, x)


def _hy_mid_kernel(f_ref, kf_ref, g_ref, a_ref, o_ref):
    n2 = DFT_N2
    kr, ki = kf_ref[0, :n2], kf_ref[0, n2:]
    for bi in range(a_ref.shape[0]):
        x = _dot(f_ref[0], a_ref[bi, 0])
        xr, xi = x[:n2], x[n2:]
        y = jnp.concatenate([xr * kr - xi * ki, xr * ki + xi * kr], axis=0).astype(BF16)
        o_ref[bi, 0] = _dot(g_ref[0], y).astype(BF16)


def _hy_mid(a4, f2, kf, g2):
    b, n1, r, c = a4.shape
    mat = pl.BlockSpec((1, r, r), lambda i: (i, 0, 0))
    dat = pl.BlockSpec((b, 1, r, c), lambda i: (0, i, 0, 0))
    return pl.pallas_call(
        _hy_mid_kernel,
        out_shape=jax.ShapeDtypeStruct(a4.shape, BF16),
        grid=(n1,),
        in_specs=[mat, pl.BlockSpec((1, r, c), lambda i: (i, 0, 0)), mat, dat],
        out_specs=dat,
        compiler_params=_cparams(("parallel",), 48),
        name="hy_mid",
    )(f2, kf, g2, a4)


def _hy_post_kernel(g_ref, bi_ref, s_ref, x0_ref, bias_ref, o_ref):
    y = _dot(g_ref[...], bi_ref[0]) + s_ref[0].astype(F32) * bias_ref[...]
    o_ref[0] = (x0_ref[0].astype(F32) * y).astype(BF16)


def _hy_post(g1, bi2, s2, x02, bias_t):
    b, k, n = bi2.shape
    m = g1.shape[0]
    cb = HY_COLS
    dat = pl.BlockSpec((1, m, cb), lambda bi, c: (bi, 0, c))
    return pl.pallas_call(
        _hy_post_kernel,
        out_shape=jax.ShapeDtypeStruct((b, m, n), BF16),
        grid=(b, n // cb),
        in_specs=[_const_spec((m, k)), pl.BlockSpec((1, k, cb), lambda bi, c: (bi, 0, c)), dat, dat,
                  _const_spec((1, cb))],
        out_specs=dat,
        compiler_params=_cparams(("parallel", "parallel"), 48),
        name="hy_post",
    )(g1, bi2, s2, x02, bias_t)


def _hy_filter_kernel(z_ref, win_ref, w1_ref, b1_ref, w2_ref, b2_ref, w3_ref, fr_ref, o_ref, *, half_tiles):
    fr = fr_ref[...]
    h = jnp.sin(fr * (_dot3(z_ref[...], w1_ref[...]) + b1_ref[...]))
    h = jnp.sin(fr * (_dot3(h, w2_ref[...]) + b2_ref[...]))
    h3 = _dot3(h, w3_ref[...])
    backward = pl.program_id(0) >= half_tiles
    o_ref[...] = jnp.where(backward, h3[:, HY_WIDTH:], h3[:, :HY_WIDTH]) * win_ref[...]


def _hy_filter(z2, win2, w1p, b1, w2, b2, w3, fr):
    n = z2.shape[0]
    tl = 512
    return pl.pallas_call(
        functools.partial(_hy_filter_kernel, half_tiles=n // (2 * tl)),
        out_shape=jax.ShapeDtypeStruct((n, HY_WIDTH), F32),
        grid=(n // tl,),
        in_specs=[pl.BlockSpec((tl, LANES), lambda i: (i, 0)), pl.BlockSpec((tl, HY_WIDTH), lambda i: (i, 0)),
                  _const_spec(w1p.shape), _const_spec(b1.shape), _const_spec(w2.shape), _const_spec(b2.shape),
                  _const_spec(w3.shape), _const_spec(fr.shape)],
        out_specs=pl.BlockSpec((tl, HY_WIDTH), lambda i: (i, 0)),
        compiler_params=_cparams(("parallel",), 32),
        name="hy_filter",
    )(z2, win2, w1p, b1, w2, b2, w3, fr)


def _hy_spec_kernel(f_ref, a_ref, o_ref):
    o_ref[0] = _dot3(f_ref[0], a_ref[0])


def _hy_spec(f2, a3):
    n1, r, c = a3.shape
    return pl.pallas_call(
        _hy_spec_kernel,
        out_shape=jax.ShapeDtypeStruct(a3.shape, F32),
        grid=(n1,),
        in_specs=[pl.BlockSpec((1, r, r), lambda i: (i, 0, 0)), pl.BlockSpec((1, r, c), lambda i: (i, 0, 0))],
        out_specs=pl.BlockSpec((1, r, c), lambda i: (i, 0, 0)),
        compiler_params=_cparams(("parallel",), 32),
        name="hy_filter_spectrum",
    )(f2, a3)


def _merge_kernel(x_ref, yat_ref, yb_ref, yct_ref, gt_ref, wb_ref, wo_ref, o_ref):
    d = D_MODEL
    ba = _dot_tn(yat_ref[0], wb_ref[0])
    bb = _dot(yb_ref[0], wb_ref[1])
    bc = _dot_tn(yct_ref[0], wb_ref[2])
    merged = (gt_ref[0, :, 0:d].astype(F32) * ba + gt_ref[0, :, d:2 * d].astype(F32) * bb
              + gt_ref[0, :, 2 * d:3 * d].astype(F32) * bc)
    o_ref[0] = x_ref[0] + _dot(merged.astype(BF16), wo_ref[...])


def _merge(x, yat, yb, yct, gates, wb, wo):
    b, l, _ = x.shape
    tm = TOK_TILE

    def tok(w):
        return pl.BlockSpec((1, tm, w), lambda bi, i: (bi, i, 0))

    tr = pl.BlockSpec((1, BRANCH_W, tm), lambda bi, i: (bi, 0, i))
    return pl.pallas_call(
        _merge_kernel,
        out_shape=jax.ShapeDtypeStruct(x.shape, F32),
        grid=(b, l // tm),
        in_specs=[tok(D_MODEL), tr, tok(BRANCH_W), tr, tok(N_BRANCH * D_MODEL),
                  _const_spec(wb.shape), _const_spec(wo.shape)],
        out_specs=tok(D_MODEL),
        compiler_params=_cparams(("parallel", "parallel"), 48),
        name="merge",
    )(x, yat, yb, yct, gates, wb, wo)


def _rope_tables(l):
    rows = l // GRID_W
    row = jnp.repeat(jnp.arange(rows, dtype=F32), GRID_W)
    col = jnp.tile(jnp.arange(GRID_W, dtype=F32), rows)

    def tab(d_rot):
        n_freq = d_rot // 4
        inv = ROPE_THETA ** (-jnp.arange(n_freq, dtype=F32) / n_freq)
        ang = jnp.concatenate([row[:, None] * inv, col[:, None] * inv], axis=-1)
        c = jnp.repeat(jnp.cos(ang), 2, axis=-1)
        s = jnp.repeat(jnp.sin(ang), 2, axis=-1) * jnp.tile(jnp.array([-1.0, 1.0], F32), d_rot // 2)
        return c, s

    ca, sa = tab(GQA_HEAD_DIM)
    cm, sm = tab(MLA_ROPE)
    tail = jnp.zeros((l, LANES - MLA_ROPE_LANE - MLA_ROPE), F32)
    cm = jnp.concatenate([jnp.ones((l, MLA_ROPE_LANE), F32), cm, tail], axis=1)
    sm = jnp.concatenate([jnp.zeros((l, MLA_ROPE_LANE), F32), sm, tail], axis=1)
    return (jnp.tile(ca, (1, 2)), jnp.tile(sa, (1, 2)), cm, sm)


def _hy_positions(l):
    r = jnp.arange(l, dtype=jnp.int32)
    pos = jnp.concatenate([r, (l - r) % l]).astype(F32)[:, None]
    live = jnp.concatenate([jnp.ones((l,), F32), (r > 0).astype(F32)])[:, None]
    t = pos * (1.0 / (l - 1))
    w = 2.0 * math.pi * pos / l
    f = jnp.linspace(1e-4, HY_BANDS - 1, HY_BANDS, dtype=F32)[None, :]
    z2 = jnp.concatenate([t, jnp.cos(f * w), -jnp.sin(f * w), jnp.zeros((2 * l, LANES - HY_EMB), F32)], axis=-1)
    max_decay = math.log(HY_TARGET) / HY_FAST
    min_decay = math.log(HY_TARGET) / HY_SLOW
    deltas = jnp.abs(jnp.linspace(min_decay, max_decay, HY_WIDTH, dtype=F32))
    return z2, jnp.exp(-t * deltas[None, :]) * live


def _hy_k1_count(l):
    n1 = 2 * l // DFT_N2
    return -(-(n1 // 2 + 1) // 8) * 8


def _dft_tables(l):
    n = 2 * l
    n2 = DFT_N2
    n1 = n // n2
    n1c = _hy_k1_count(l)
    two_pi = 2.0 * math.pi
    i1 = jnp.arange(n1, dtype=jnp.int32)
    ang1 = ((i1[:n1c, None] * i1[None, :]) % n1).astype(F32) * (two_pi / n1)
    live = (i1[:n1c] <= n1 // 2)[:, None]
    c1, s1 = jnp.where(live, jnp.cos(ang1), 0.0), jnp.where(live, jnp.sin(ang1), 0.0)
    f1_full = jnp.stack([c1, -s1], axis=1).reshape(2 * n1c, n1)
    fold = jnp.where((i1[:n1c] == 0) | (i1[:n1c] == n1 // 2), 1.0, 2.0)[:, None] * (1.0 / n)
    g1 = jnp.stack([(c1 * fold)[:, :n1 // 2].T, (-s1 * fold)[:, :n1 // 2].T], axis=2).reshape(n1 // 2, 2 * n1c)
    i2 = jnp.arange(n2, dtype=jnp.int32)
    kk = i1[:n1c, None, None] + n1 * i2[None, :, None]
    ang2 = ((kk * i2[None, None, :]) % n).astype(F32) * (two_pi / n)
    c2, s2 = jnp.cos(ang2), jnp.sin(ang2)
    f2 = jnp.concatenate([jnp.concatenate([c2, s2], axis=2), jnp.concatenate([-s2, c2], axis=2)], axis=1)
    c2t, s2t = jnp.swapaxes(c2, 1, 2), jnp.swapaxes(s2, 1, 2)
    g2 = jnp.concatenate([jnp.concatenate([c2t, -s2t], axis=2), jnp.concatenate([s2t, c2t], axis=2)], axis=1)
    return f1_full, g1, f2, g2


def _swap_pairs(n):
    return np.arange(n) ^ 1


def _proj_a_columns():
    zero = IN_WIDTH
    cols = np.full((PA_END,), zero, np.int64)
    q0, k0, v0, _, cq0, ckv0, kr0, _ = IN_OFFS[:8]
    for h in range(GQA_HEADS):
        dst = h * LANES + (h // (GQA_HEADS // GQA_KV_HEADS)) * GQA_HEAD_DIM
        src = q0 + h * GQA_HEAD_DIM + np.arange(GQA_HEAD_DIM)
        cols[PA_Q + dst:PA_Q + dst + GQA_HEAD_DIM] = src
        cols[PA_QS + dst:PA_QS + dst + GQA_HEAD_DIM] = q0 + h * GQA_HEAD_DIM + _swap_pairs(GQA_HEAD_DIM)
    cols[PA_K:PA_K + 128] = k0 + np.arange(128)
    cols[PA_KS:PA_KS + 128] = k0 + _swap_pairs(128)
    cols[PA_V:PA_V + 128] = v0 + np.arange(128)
    cols[PA_CQ:PA_CQ + MLA_Q_RANK] = cq0 + np.arange(MLA_Q_RANK)
    cols[PA_CKV:PA_CKV + MLA_KV_RANK] = ckv0 + np.arange(MLA_KV_RANK)
    r0 = MLA_ROPE_LANE
    cols[PA_KR + r0:PA_KR + r0 + MLA_ROPE] = kr0 + np.arange(MLA_ROPE)
    cols[PA_KRS + r0:PA_KRS + r0 + MLA_ROPE] = kr0 + _swap_pairs(MLA_ROPE)
    return cols


def _mla_q_columns():
    hd = MLA_NOPE + MLA_ROPE
    zero = MLA_HEADS * hd
    main = np.full((MLA_HEADS * MLA_KC,), zero, np.int64)
    swap = np.full((MLA_HEADS * MLA_KC,), zero, np.int64)
    for h in range(MLA_HEADS):
        b0 = h * MLA_KC
        main[b0:b0 + MLA_NOPE] = h * hd + np.arange(MLA_NOPE)
        r0 = b0 + MLA_ROPE_LANE
        main[r0:r0 + MLA_ROPE] = h * hd + MLA_NOPE + np.arange(MLA_ROPE)
        swap[r0:r0 + MLA_ROPE] = h * hd + MLA_NOPE + _swap_pairs(MLA_ROPE)
    return main, swap


def _mla_kv_columns():
    hd = MLA_NOPE + MLA_V
    zero = MLA_HEADS * hd
    knope = np.full((MLA_HEADS * MLA_KC,), zero, np.int64)
    for h in range(MLA_HEADS):
        knope[h * MLA_KC:h * MLA_KC + MLA_NOPE] = h * hd + np.arange(MLA_NOPE)
    val = np.concatenate([h * hd + MLA_NOPE + np.arange(MLA_V) for h in range(MLA_HEADS)])
    return np.concatenate([knope, val])


def _take_cols(w, cols):
    wz = jnp.concatenate([w, jnp.zeros((w.shape[0], 1), w.dtype)], axis=1)
    return jnp.take(wz, jnp.asarray(cols, jnp.int32), axis=1).astype(BF16)


def _tile2(g):
    return jnp.tile(g, 2)[None, :]


def _encoder(x, lw, g_final):
    b, l, d = x.shape
    n2 = DFT_N2
    n1 = 2 * l // n2
    c = HY_WIDTH
    tabs = _rope_tables(l)
    z2, win2 = _hy_positions(l)
    f1_full, g1, f2, g2 = _dft_tables(l)
    f1_b = f1_full[:, :n1 // 2].astype(BF16)
    g1_b = g1.astype(BF16)
    f2_b = f2.astype(BF16)
    g2_b = g2.astype(BF16)
    for li, w in enumerate(lw):
        x2 = _ffn(x.reshape(b * l, d), w["g_ffn1"], w["wg1"], w["wu1"], w["wd1"], g_final, False)
        x = x2.reshape(b, l, d)
        qg, kg, vgt, qm, km, vmt, bnd = _proj_a(x, w["g_mix"], w["w_a"], tabs, w["gq"], w["gqs"], w["gk"],
                                                w["gks"], w["gcq"], w["gckv"], w["wuq"], w["wuqs"], w["wukv"])
        hy, gates = _proj_b(x2, w["g_mix"], w["w_b"])
        top = bnd[:, :, :4, 0]
        limit = ATT_SAFE_SCORE ** 2
        safe_a = (top[:, :, 0] * jnp.max(top[:, :, 1], axis=1, keepdims=True) <= limit).astype(jnp.int32)
        safe_m = (top[:, :, 2] * jnp.max(top[:, :, 3], axis=1, keepdims=True) <= limit).astype(jnp.int32)
        per_kv = GQA_HEADS // GQA_KV_HEADS
        yat = _attention(safe_a, qg, kg, vgt, heads=ATT_HEADS, kc=GQA_KC, tk=ATT_TK_GQA,
                         key_of_head=(0,) * GQA_HEADS,
                         value_of_head=tuple(h // per_kv for h in range(GQA_HEADS)), name="attn_gqa")
        yct = _attention(safe_m, qm, km, vmt, heads=ATT_HEADS, kc=MLA_KC, tk=ATT_TK_MLA,
                         key_of_head=tuple(range(MLA_HEADS)),
                         value_of_head=tuple(range(MLA_HEADS)), name="attn_mla")
        kc_time = _hy_filter(z2, win2, w["w1p"], w["b1"], w["w2"], w["b2"], w["w3"], w["fr"])
        n1c = _hy_k1_count(l)
        ka =_left_mm(f1_full, kc_time.reshape(1, n1, n2 * c), F32, hi=True)
        kf = _hy_spec(f2, ka.reshape(n1c, 2 * n2, c))
        s, x0 = _hy_pre(hy.reshape(b, l, 3 * c), w["w_short"], w["b_short"])
        a = _left_mm(f1_b, s.reshape(b, n1 // 2, n2 * c), BF16)
        bi = _hy_mid(a.reshape(b, n1c, 2 * n2, c), f2_b, kf, g2_b)
        yb = _hy_post(g1_b, bi.reshape(b, 2 * n1c, n2 * c), s.reshape(b, n1 // 2, n2 * c),
                      x0.reshape(b, n1 // 2, n2 * c), w["bias_t"])
        x = _merge(x, yat, yb.reshape(b, l, c), yct, gates.reshape(b, l, N_BRANCH * d), w["wb"], w["wo"])
        x2 = _ffn(x.reshape(b * l, d), w["g_ffn2"], w["wg2"], w["wu2"], w["wd2"], g_final, li == len(lw) - 1)
        x = x2.reshape(b, l, d)
    return x


def kernel(x_prompt, x_sample, g_ffn1, w_ffn1_gate, w_ffn1_up, w_ffn1_down, g_mix, w_in, g_qnorm, g_knorm,
           w_hy_short, b_hy_short, w_hy_f1, b_hy_f1, w_hy_f2, b_hy_f2, w_hy_f3, hy_sin_freq, hy_bias,
           g_mla_q, w_mla_uq, g_mla_kv, w_mla_ukv, w_branch, w_out, g_ffn2, w_ffn2_gate, w_ffn2_up,
           w_ffn2_down, g_final):
    cols_a = _proj_a_columns()
    uq_main, uq_swap = _mla_q_columns()
    ukv_cols = _mla_kv_columns()
    sw64 = _swap_pairs(GQA_HEAD_DIM)
    hy0 = IN_OFFS[3]
    gt0 = IN_OFFS[7]
    lw = []
    for l in range(DEPTH):
        lw.append(dict(
            g_ffn1=g_ffn1[l][None], wg1=w_ffn1_gate[l].astype(BF16), wu1=w_ffn1_up[l].astype(BF16),
            wd1=w_ffn1_down[l].astype(BF16),
            g_ffn2=g_ffn2[l][None], wg2=w_ffn2_gate[l].astype(BF16), wu2=w_ffn2_up[l].astype(BF16),
            wd2=w_ffn2_down[l].astype(BF16),
            g_mix=g_mix[l][None],
            w_a=_take_cols(w_in[l], cols_a),
            w_b=jnp.concatenate([w_in[l][:, hy0:hy0 + 3 * HY_WIDTH], w_in[l][:, gt0:]], axis=1).astype(BF16),
            gq=_tile2(g_qnorm[l]), gqs=_tile2(g_qnorm[l][sw64]),
            gk=_tile2(g_knorm[l]), gks=_tile2(g_knorm[l][sw64]),
            gcq=g_mla_q[l][None], gckv=g_mla_kv[l][None],
            wuq=_take_cols(w_mla_uq[l], uq_main), wuqs=_take_cols(w_mla_uq[l], uq_swap),
            wukv=_take_cols(w_mla_ukv[l], ukv_cols),
            w_short=w_hy_short[l], b_short=b_hy_short[l],
            w1p=jnp.pad(w_hy_f1[l], ((0, LANES - HY_EMB), (0, 0))), b1=b_hy_f1[l][None],
            w2=w_hy_f2[l], b2=b_hy_f2[l][None], w3=w_hy_f3[l], fr=hy_sin_freq[l][None],
            bias_t=jnp.tile(hy_bias[l], HY_COLS // HY_WIDTH)[None],
            wb=w_branch[l].astype(BF16), wo=w_out[l].astype(BF16),
        ))
    gf = g_final[None]
    return _encoder(x_prompt, lw, gf), _encoder(x_sample, lw, gf)
```

```python
import functools
import math

import numpy as np
import jax
import jax.numpy as jnp
from jax import lax
from jax.experimental import pallas as pl
from jax.experimental.pallas import tpu as pltpu

F32 = jnp.float32
BF16 = jnp.bfloat16

D_MODEL = 1024
DEPTH = 2
GRID_W = 64
ROPE_THETA = 10000.0
EPS = 1e-6
D_FF = 2816
N_BRANCH = 3
BRANCH_W = 512
GQA_HEADS = 8
GQA_KV_HEADS = 2
GQA_HEAD_DIM = 64
HY_WIDTH = 512
HY_ORDER = 64
HY_EMB = 33
HY_BANDS = (HY_EMB - 1) // 2
HY_TARGET = 1e-2
HY_FAST = 0.3
HY_SLOW = 1.5
MLA_HEADS = 8
MLA_Q_RANK = 256
MLA_KV_RANK = 128
MLA_NOPE = 64
MLA_ROPE = 32
MLA_V = 64
IN_WIDTHS = (512, 128, 128, 3 * HY_WIDTH, MLA_Q_RANK, MLA_KV_RANK, MLA_ROPE, N_BRANCH * D_MODEL)
IN_OFFS = tuple(int(c) for c in np.cumsum((0,) + IN_WIDTHS))
IN_WIDTH = IN_OFFS[-1]

LANES = 128
V7X_VMEM_BYTES = 64 * 1024 * 1024

TOK_TILE = 512
ATT_TQ = 512
ATT_TQB = 512
ATT_HEADS = 4
ATT_UNROLL = 4
ATT_SAFE_SCORE = 64.0
ATT_TK_GQA = 2 * TOK_TILE
ATT_TK_MLA = 2 * TOK_TILE
FF_CHUNKS = ((0, 1024), (1024, 2048), (2048, D_FF))
DFT_N2 = 128
HY_COLS = 8192
ONES_ROWS = 16
NEG_BIG = -1e30
LOG2E = math.log2(math.e)


def _cparams(sem, vmem_mb):
    return pltpu.CompilerParams(dimension_semantics=sem, vmem_limit_bytes=vmem_mb * 1024 * 1024)


def _const_spec(shape):
    nd = len(shape)
    return pl.BlockSpec(shape, lambda *_: (0,) * nd, pipeline_mode=pl.Buffered(1))


def _rms(x, g):
    return x * lax.rsqrt(jnp.mean(x * x, axis=-1, keepdims=True) + EPS) * g


def _dot(a, b):
    return jnp.dot(a, b, preferred_element_type=F32)


def _dot_nt(a, b):
    return lax.dot_general(a, b, (((1,), (1,)), ((), ())), preferred_element_type=F32)


def _dot_tn(a, b):
    return lax.dot_general(a, b, (((0,), (0,)), ((), ())), preferred_element_type=F32)


def _split(a):
    hi = a.astype(BF16)
    return hi, (a - hi.astype(F32)).astype(BF16)


def _dot3(a, b):
    ah, al = _split(a)
    bh, bl = _split(b)
    return _dot(ah, bh) + (_dot(ah, bl) + _dot(al, bh))


def _ffn_kernel(x_ref, g_ref, wg_ref, wu_ref, wd_ref, gf_ref, o_ref, *, final):
    x = x_ref[...]
    xb = _rms(x, g_ref[...]).astype(BF16)
    acc = jnp.zeros_like(x)
    for c0, c1 in FF_CHUNKS:
        gate = _dot(xb, wg_ref[:, c0:c1])
        up = _dot(xb, wu_ref[:, c0:c1])
        h = (gate * jax.nn.sigmoid(gate) * up).astype(BF16)
        acc = acc + _dot(h, wd_ref[c0:c1, :])
    y = x + 0.5 * acc
    if final:
        y = _rms(y, gf_ref[...])
    o_ref[...] = y


def _ffn(x2d, g, wg, wu, wd, g_final, final):
    t = x2d.shape[0]
    row = pl.BlockSpec((TOK_TILE, D_MODEL), lambda i: (i, 0))
    return pl.pallas_call(
        functools.partial(_ffn_kernel, final=final),
        out_shape=jax.ShapeDtypeStruct((t, D_MODEL), F32),
        grid=(t // TOK_TILE,),
        in_specs=[row, _const_spec((1, D_MODEL)), _const_spec((D_MODEL, D_FF)),
                  _const_spec((D_MODEL, D_FF)), _const_spec((D_FF, D_MODEL)), _const_spec((1, D_MODEL))],
        out_specs=row,
        compiler_params=_cparams(("parallel",), 48),
        name="ffn",
    )(x2d, g, wg, wu, wd, g_final)


PA_Q, PA_QS, PA_K, PA_KS, PA_V, PA_CQ, PA_CKV, PA_KR, PA_KRS, PA_END = (
    0, 1024, 2048, 2176, 2304, 2432, 2688, 2816, 2944, 3072)
GQA_KC = 128
MLA_KC = 128
MLA_ROPE_LANE = MLA_NOPE


def _sq_norm(a_bf16):
    a = a_bf16.astype(F32)
    return jnp.sum(a * a, axis=-1, keepdims=True)


def _proj_a_kernel(x_ref, g_ref, w_ref, cosa_ref, sina_ref, cosm_ref, sinm_ref,
                   gq_ref, gqs_ref, gk_ref, gks_ref, gcq_ref, gckv_ref,
                   wuq_ref, wuqs_ref, wukv_ref,
                   qg_ref, kg_ref, vgt_ref, qm_ref, km_ref, vmt_ref, bnd_ref):
    ub = _rms(x_ref[0], g_ref[...]).astype(BF16)
    cosa, sina = cosa_ref[...], sina_ref[...]
    cosm, sinm = cosm_ref[...], sinm_ref[...]

    zq = _dot(ub, w_ref[:, PA_Q:PA_QS])
    zqs = _dot(ub, w_ref[:, PA_QS:PA_K])
    tq_c = gq_ref[...] * cosa
    tq_s = gqs_ref[...] * sina
    scale_a = GQA_HEAD_DIM ** -0.5 * LOG2E
    qa_n2 = None
    for h in range(GQA_HEADS):
        a = zq[:, h * LANES:(h + 1) * LANES]
        a_sw = zqs[:, h * LANES:(h + 1) * LANES]
        r = lax.rsqrt(jnp.sum(a * a, axis=-1, keepdims=True) * (1.0 / GQA_HEAD_DIM) + EPS)
        qb = ((a * tq_c + a_sw * tq_s) * (r * scale_a)).astype(BF16)
        qg_ref[0, :, h * LANES:(h + 1) * LANES] = qb
        qa_n2 = _sq_norm(qb) if qa_n2 is None else jnp.maximum(qa_n2, _sq_norm(qb))

    zk = _dot(ub, w_ref[:, PA_K:PA_KS])
    zks = _dot(ub, w_ref[:, PA_KS:PA_V])
    first = lax.broadcasted_iota(jnp.int32, zk.shape, 1) < GQA_HEAD_DIM
    sq = zk * zk
    r0 = lax.rsqrt(jnp.sum(jnp.where(first, sq, 0.0), axis=-1, keepdims=True) * (1.0 / GQA_HEAD_DIM) + EPS)
    r1 = lax.rsqrt(jnp.sum(jnp.where(first, 0.0, sq), axis=-1, keepdims=True) * (1.0 / GQA_HEAD_DIM) + EPS)
    kb = ((zk * (gk_ref[...] * cosa) + zks * (gks_ref[...] * sina)) * jnp.where(first, r0, r1)).astype(BF16)
    kg_ref[0] = kb
    kb32 = kb.astype(F32)
    kb_sq = kb32 * kb32
    ka_n2 = jnp.maximum(jnp.sum(jnp.where(first, kb_sq, 0.0), axis=-1, keepdims=True),
                        jnp.sum(jnp.where(first, 0.0, kb_sq), axis=-1, keepdims=True))

    vgt_ref[0, 0] = _dot(ub, w_ref[:, PA_V:PA_CQ]).T.astype(BF16)

    scale_m = (MLA_NOPE + MLA_ROPE) ** -0.5 * LOG2E
    cqn = _rms(_dot(ub, w_ref[:, PA_CQ:PA_CKV]), gcq_ref[...]).astype(BF16)
    zq2 = _dot(cqn, wuq_ref[...])
    zq2s = _dot(cqn, wuqs_ref[...])
    qm_n2 = None
    for h in range(MLA_HEADS):
        blk = slice(h * MLA_KC, (h + 1) * MLA_KC)
        qb = ((zq2[:, blk] * cosm + zq2s[:, blk] * sinm) * scale_m).astype(BF16)
        qm_ref[0, :, blk] = qb
        qm_n2 = _sq_norm(qb) if qm_n2 is None else jnp.maximum(qm_n2, _sq_norm(qb))

    ckvn = _rms(_dot(ub, w_ref[:, PA_CKV:PA_KR]), gckv_ref[...]).astype(BF16)
    zkv = _dot(ckvn, wukv_ref[...])
    krope = _dot(ub, w_ref[:, PA_KR:PA_KRS]) * cosm + _dot(ub, w_ref[:, PA_KRS:PA_END]) * sinm
    km_n2 = None
    for h in range(MLA_HEADS):
        blk = slice(h * MLA_KC, (h + 1) * MLA_KC)
        kb = (zkv[:, blk] + krope).astype(BF16)
        km_ref[0, :, blk] = kb
        km_n2 = _sq_norm(kb) if km_n2 is None else jnp.maximum(km_n2, _sq_norm(kb))
    vmt_ref[0, 0] = zkv[:, MLA_HEADS * MLA_KC:].T.astype(BF16)

    row = lax.broadcasted_iota(jnp.int32, (8, LANES), 0)
    tops = [jnp.max(v, axis=0, keepdims=True) for v in (qa_n2, ka_n2, qm_n2, km_n2)]
    bnd_ref[0, 0] = jnp.where(row == 0, tops[0], jnp.where(row == 1, tops[1], jnp.where(row == 2, tops[2], tops[3])))


def _proj_a(x, g_mix, w_a, tabs, gq, gqs, gk, gks, gcq, gckv, wuq, wuqs, wukv):
    b, l, _ = x.shape
    tm = TOK_TILE
    nt = l // tm
    cosa, sina, cosm, sinm = tabs
    tab = pl.BlockSpec((tm, LANES), lambda bi, i: (i, 0))

    def tok(w):
        return pl.BlockSpec((1, tm, w), lambda bi, i: (bi, i, 0))

    def tr(rows):
        return pl.BlockSpec((1, 1, rows, tm), lambda bi, i: (bi, i, 0, 0))

    out_shape = (
        jax.ShapeDtypeStruct((b, l, GQA_HEADS * GQA_KC), BF16),
        jax.ShapeDtypeStruct((b, l, GQA_KC), BF16),
        jax.ShapeDtypeStruct((b, nt, GQA_KV_HEADS * GQA_HEAD_DIM, tm), BF16),
        jax.ShapeDtypeStruct((b, l, MLA_HEADS * MLA_KC), BF16),
        jax.ShapeDtypeStruct((b, l, MLA_HEADS * MLA_KC), BF16),
        jax.ShapeDtypeStruct((b, nt, MLA_HEADS * MLA_V, tm), BF16),
        jax.ShapeDtypeStruct((b, nt, 8, LANES), F32),
    )
    return pl.pallas_call(
        _proj_a_kernel,
        out_shape=out_shape,
        grid=(b, nt),
        in_specs=[tok(D_MODEL), _const_spec((1, D_MODEL)), _const_spec(w_a.shape), tab, tab, tab, tab,
                  _const_spec((1, LANES)), _const_spec((1, LANES)), _const_spec((1, LANES)),
                  _const_spec((1, LANES)), _const_spec((1, MLA_Q_RANK)), _const_spec((1, MLA_KV_RANK)),
                  _const_spec(wuq.shape), _const_spec(wuqs.shape), _const_spec(wukv.shape)],
        out_specs=(tok(GQA_HEADS * GQA_KC), tok(GQA_KC), tr(GQA_KV_HEADS * GQA_HEAD_DIM),
                   tok(MLA_HEADS * MLA_KC), tok(MLA_HEADS * MLA_KC), tr(MLA_HEADS * MLA_V),
                   pl.BlockSpec((1, 1, 8, LANES), lambda bi, i: (bi, i, 0, 0))),
        compiler_params=_cparams(("parallel", "parallel"), 48),
        name="proj_a",
    )(x, g_mix, w_a, cosa, sina, cosm, sinm, gq, gqs, gk, gks, gcq, gckv, wuq, wuqs, wukv)


def _proj_b_kernel(x_ref, g_ref, w_ref, hy_ref, gt_ref):
    ub = _rms(x_ref[...], g_ref[...]).astype(BF16)
    hy_ref[...] = _dot(ub, w_ref[:, :3 * HY_WIDTH]).astype(BF16)
    gt_ref[...] = jax.nn.sigmoid(_dot(ub, w_ref[:, 3 * HY_WIDTH:])).astype(BF16)


def _proj_b(x2d, g_mix, w_b):
    t = x2d.shape[0]
    tm = TOK_TILE

    def row(w):
        return pl.BlockSpec((tm, w), lambda i: (i, 0))

    return pl.pallas_call(
        _proj_b_kernel,
        out_shape=(jax.ShapeDtypeStruct((t, 3 * HY_WIDTH), BF16),
                   jax.ShapeDtypeStruct((t, N_BRANCH * D_MODEL), BF16)),
        grid=(t // tm,),
        in_specs=[row(D_MODEL), _const_spec((1, D_MODEL)), _const_spec(w_b.shape)],
        out_specs=(row(3 * HY_WIDTH), row(N_BRANCH * D_MODEL)),
        compiler_params=_cparams(("parallel",), 48),
        name="proj_b",
    )(x2d, g_mix, w_b)


def _attn_kernel(safe_ref, q_ref, k_ref, vt_ref, o_ref, acc_scr, p0_scr, p1_scr,
                 *, heads, groups, kc, key_of_head, value_of_head, nk, nk_tile):
    tv = vt_ref.shape[3]
    tk = nk_tile
    tq = ATT_TQ
    ones = jnp.ones((ONES_ROWS, tk), BF16)
    dv = MLA_V
    n_sub = q_ref.shape[1] // tq
    safe = safe_ref[pl.program_id(0), pl.program_id(1)] != 0

    def key_tile(j, h):
        kb = key_of_head[h]
        rows = j * tk if isinstance(j, int) else pl.multiple_of(j * tk, tk)
        return k_ref[0, pl.ds(rows, tk), kb * kc:(kb + 1) * kc]

    def values(j, h):
        vb = value_of_head[h]
        per = tk // tv
        return jnp.concatenate([vt_ref[0, j * per + u, vb * dv:(vb + 1) * dv, :] for u in range(per)], axis=1)

    def value_tile(j, h):
        return jnp.concatenate([values(j, h), ones], axis=0)

    def write_out(acc, l, h, sub):
        o_ref[0, h * dv:(h + 1) * dv, sub * tq:(sub + 1) * tq] = (acc / l).astype(BF16)

    @pl.when(safe)
    def _():
        bufs = (p0_scr, p1_scr)
        streams = [(sub, g) for sub in range(n_sub) for g in range(groups)]

        def probs(si, stream, j, hh, buf):
            sub, g = stream
            h = g * heads + hh
            st = _dot_nt(key_tile(j, h), q_ref[0, sub * tq:(sub + 1) * tq, h * kc:(h + 1) * kc])
            buf[hh] = jnp.exp2(st).astype(BF16)

        def accumulate(si, stream, j, hh, buf):
            acc_scr[si % 2, hh] += _dot(value_tile(j, stream[1] * heads + hh), buf[hh])

        acc_scr[...] = jnp.zeros(acc_scr.shape, F32)
        for hh in range(heads):
            probs(0, streams[0], 0, hh, bufs[0])
        for si, stream in enumerate(streams):
            def step(j, cur, nxt, si=si, stream=stream):
                for hh in range(heads):
                    probs(si, stream, j + 1, hh, nxt)
                    accumulate(si, stream, j, hh, cur)

            unroll = min(ATT_UNROLL, nk)

            def body(jj, carry, step=step):
                for u in range(unroll):
                    step(unroll * jj + u, bufs[u % 2], bufs[1 - u % 2])
                return carry

            lax.fori_loop(0, nk // unroll - 1, body, 0)
            for u in range(unroll - 1):
                step(nk - unroll + u, bufs[u % 2], bufs[1 - u % 2])
            for hh in range(heads):
                if si + 1 < len(streams):
                    probs(si + 1, streams[si + 1], 0, hh, bufs[0])
                accumulate(si, stream, nk - 1, hh, bufs[1])
            sub, g = stream
            for hh in range(heads):
                acc = acc_scr[si % 2, hh]
                write_out(acc[:dv], acc[dv:dv + 1], g * heads + hh, sub)
                if si + 2 < len(streams):
                    acc_scr[si % 2, hh] = jnp.zeros((dv + ONES_ROWS, tq), F32)

    @pl.when(jnp.logical_not(safe))
    def _():
        for h in range(groups * heads):
            for sub in range(n_sub):
                qh = q_ref[0, sub * tq:(sub + 1) * tq, h * kc:(h + 1) * kc]

                def body(j, carry, qh=qh, h=h):
                    m, acc = carry
                    st = _dot_nt(key_tile(j, h), qh)
                    m_new = jnp.maximum(m, jnp.max(st, axis=0, keepdims=True))
                    p = jnp.exp2(st - m_new).astype(BF16)
                    return m_new, jnp.exp2(m - m_new) * acc + _dot(value_tile(j, h), p)

                init = (jnp.full((1, tq), NEG_BIG, F32), jnp.zeros((dv + ONES_ROWS, tq), F32))
                _, acc = lax.fori_loop(0, nk, body, init)
                write_out(acc[:dv], acc[dv:dv + 1], h, sub)


def _attention(safe, q, k, vt, *, heads, kc, tk, key_of_head, value_of_head, name):
    b, l, qw = q.shape
    n_heads = qw // kc
    nv, tv = vt.shape[1], vt.shape[3]
    nk = l // tk
    tqb = ATT_TQB
    tq = ATT_TQ
    dv = MLA_V
    grid_spec = pltpu.PrefetchScalarGridSpec(
        num_scalar_prefetch=1,
        grid=(b, l // tqb),
        in_specs=[pl.BlockSpec((1, tqb, qw), lambda bi, i, s: (bi, i, 0)),
                  pl.BlockSpec((1, l, k.shape[2]), lambda bi, i, s: (bi, 0, 0), pipeline_mode=pl.Buffered(1)),
                  pl.BlockSpec((1, nv, vt.shape[2], tv), lambda bi, i, s: (bi, 0, 0, 0),
                               pipeline_mode=pl.Buffered(1))],
        out_specs=pl.BlockSpec((1, n_heads * dv, tqb), lambda bi, i, s: (bi, 0, i)),
        scratch_shapes=[pltpu.VMEM((2, heads, dv + ONES_ROWS, tq), F32),
                        pltpu.VMEM((heads, tk, tq), BF16), pltpu.VMEM((heads, tk, tq), BF16)],
    )
    return pl.pallas_call(
        functools.partial(_attn_kernel, heads=heads, groups=n_heads // heads, kc=kc, key_of_head=key_of_head,
                          value_of_head=value_of_head, nk=nk, nk_tile=tk),
        out_shape=jax.ShapeDtypeStruct((b, n_heads * dv, l), BF16),
        grid_spec=grid_spec,
        compiler_params=_cparams(("parallel", "parallel"), 48),
        name=name,
    )(safe, q, k, vt)


def _hy_pre_kernel(x0_ref, x1_ref, v_ref, w_ref, b_ref, s_ref, x0o_ref):
    rows = x0_ref.shape[1]
    t = lax.broadcasted_iota(jnp.int32, (rows, LANES), 0)
    not_first = t > 0
    not_last = t < rows - 1

    def conv(ref, j):
        a = ref[0].astype(F32)
        prev = jnp.where(not_first, pltpu.roll(a, 1, 0), 0.0)
        nxt = jnp.where(not_last, pltpu.roll(a, rows - 1, 0), 0.0)
        return prev * w_ref[0, j] + a * w_ref[1, j] + nxt * w_ref[2, j] + b_ref[j]

    x0o_ref[0] = conv(x0_ref, 0).astype(BF16)
    s_ref[0] = (conv(x1_ref, 1) * conv(v_ref, 2)).astype(BF16)


def _hy_pre(hy, w_short, b_short):
    b, l, _ = hy.shape
    nc = HY_WIDTH // LANES
    w4 = w_short.reshape(3, 3, nc, 1, LANES)
    b4 = b_short.reshape(3, nc, 1, LANES)

    def seg(j):
        return pl.BlockSpec((1, l, LANES), lambda bi, c, j=j: (bi, 0, j * nc + c))

    out = pl.BlockSpec((1, l, LANES), lambda bi, c: (bi, 0, c))
    return pl.pallas_call(
        _hy_pre_kernel,
        out_shape=(jax.ShapeDtypeStruct((b, l, HY_WIDTH), BF16), jax.ShapeDtypeStruct((b, l, HY_WIDTH), BF16)),
        grid=(b, nc),
        in_specs=[seg(0), seg(1), seg(2),
                  pl.BlockSpec((3, 3, None, 1, LANES), lambda bi, c: (0, 0, c, 0, 0)),
                  pl.BlockSpec((3, None, 1, LANES), lambda bi, c: (0, c, 0, 0))],
        out_specs=(out, out),
        compiler_params=_cparams(("parallel", "parallel"), 48),
        name="hy_pre",
    )(hy, hy, hy, w4, b4)


def _left_mm_kernel(f_ref, x_ref, o_ref, *, hi):
    if hi:
        o_ref[0] = _dot3(f_ref[...], x_ref[0]).astype(o_ref.dtype)
    else:
        o_ref[0] = _dot(f_ref[...], x_ref[0]).astype(o_ref.dtype)


def _left_mm(f, x, out_dtype, hi=False):
    b, k, n = x.shape
    m = f.shape[0]
    cb = min(HY_COLS, n)
    return pl.pallas_call(
        functools.partial(_left_mm_kernel, hi=hi),
        out_shape=jax.ShapeDtypeStruct((b, m, n), out_dtype),
        grid=(b, n // cb),
        in_specs=[_const_spec((m, k)), pl.BlockSpec((1, k, cb), lambda bi, c: (bi, 0, c))],
        out_specs=pl.BlockSpec((1, m, cb), lambda bi, c: (bi, 0, c)),
        compiler_params=_cparams(("parallel", "parallel"), 48),
        name="hy_outer_dft",
    )(f, x)


def _hy_mid_kernel(f_ref, kf_ref, g_ref, a_ref, o_ref):
    n2 = DFT_N2
    kr, ki = kf_ref[0, :n2], kf_ref[0, n2:]
    for bi in range(a_ref.shape[0]):
        x = _dot(f_ref[0], a_ref[bi, 0])
        xr, xi = x[:n2], x[n2:]
        y = jnp.concatenate([xr * kr - xi * ki, xr * ki + xi * kr], axis=0).astype(BF16)
        o_ref[bi, 0] = _dot(g_ref[0], y).astype(BF16)


def _hy_mid(a4, f2, kf, g2):
    b, n1, r, c = a4.shape
    mat = pl.BlockSpec((1, r, r), lambda i: (i, 0, 0))
    dat = pl.BlockSpec((b, 1, r, c), lambda i: (0, i, 0, 0))
    return pl.pallas_call(
        _hy_mid_kernel,
        out_shape=jax.ShapeDtypeStruct(a4.shape, BF16),
        grid=(n1,),
        in_specs=[mat, pl.BlockSpec((1, r, c), lambda i: (i, 0, 0)), mat, dat],
        out_specs=dat,
        compiler_params=_cparams(("parallel",), 48),
        name="hy_mid",
    )(f2, kf, g2, a4)


def _hy_post_kernel(g_ref, bi_ref, s_ref, x0_ref, bias_ref, o_ref):
    y = _dot(g_ref[...], bi_ref[0]) + s_ref[0].astype(F32) * bias_ref[...]
    o_ref[0] = (x0_ref[0].astype(F32) * y).astype(BF16)


def _hy_post(g1, bi2, s2, x02, bias_t):
    b, k, n = bi2.shape
    m = g1.shape[0]
    cb = HY_COLS
    dat = pl.BlockSpec((1, m, cb), lambda bi, c: (bi, 0, c))
    return pl.pallas_call(
        _hy_post_kernel,
        out_shape=jax.ShapeDtypeStruct((b, m, n), BF16),
        grid=(b, n // cb),
        in_specs=[_const_spec((m, k)), pl.BlockSpec((1, k, cb), lambda bi, c: (bi, 0, c)), dat, dat,
                  _const_spec((1, cb))],
        out_specs=dat,
        compiler_params=_cparams(("parallel", "parallel"), 48),
        name="hy_post",
    )(g1, bi2, s2, x02, bias_t)


def _hy_filter_kernel(z_ref, win_ref, w1_ref, b1_ref, w2_ref, b2_ref, w3_ref, fr_ref, o_ref, *, half_tiles):
    fr = fr_ref[...]
    h = jnp.sin(fr * (_dot3(z_ref[...], w1_ref[...]) + b1_ref[...]))
    h = jnp.sin(fr * (_dot3(h, w2_ref[...]) + b2_ref[...]))
    h3 = _dot3(h, w3_ref[...])
    backward = pl.program_id(0) >= half_tiles
    o_ref[...] = jnp.where(backward, h3[:, HY_WIDTH:], h3[:, :HY_WIDTH]) * win_ref[...]


def _hy_filter(z2, win2, w1p, b1, w2, b2, w3, fr):
    n = z2.shape[0]
    tl = 512
    return pl.pallas_call(
        functools.partial(_hy_filter_kernel, half_tiles=n // (2 * tl)),
        out_shape=jax.ShapeDtypeStruct((n, HY_WIDTH), F32),
        grid=(n // tl,),
        in_specs=[pl.BlockSpec((tl, LANES), lambda i: (i, 0)), pl.BlockSpec((tl, HY_WIDTH), lambda i: (i, 0)),
                  _const_spec(w1p.shape), _const_spec(b1.shape), _const_spec(w2.shape), _const_spec(b2.shape),
                  _const_spec(w3.shape), _const_spec(fr.shape)],
        out_specs=pl.BlockSpec((tl, HY_WIDTH), lambda i: (i, 0)),
        compiler_params=_cparams(("parallel",), 32),
        name="hy_filter",
    )(z2, win2, w1p, b1, w2, b2, w3, fr)


def _hy_spec_kernel(f_ref, a_ref, o_ref):
    o_ref[0] = _dot3(f_ref[0], a_ref[0])


def _hy_spec(f2, a3):
    n1, r, c = a3.shape
    return pl.pallas_call(
        _hy_spec_kernel,
        out_shape=jax.ShapeDtypeStruct(a3.shape, F32),
        grid=(n1,),
        in_specs=[pl.BlockSpec((1, r, r), lambda i: (i, 0, 0)), pl.BlockSpec((1, r, c), lambda i: (i, 0, 0))],
        out_specs=pl.BlockSpec((1, r, c), lambda i: (i, 0, 0)),
        compiler_params=_cparams(("parallel",), 32),
        name="hy_filter_spectrum",
    )(f2, a3)


def _merge_kernel(x_ref, yat_ref, yb_ref, yct_ref, gt_ref, wb_ref, wo_ref, o_ref):
    d = D_MODEL
    ba = _dot_tn(yat_ref[0], wb_ref[0])
    bb = _dot(yb_ref[0], wb_ref[1])
    bc = _dot_tn(yct_ref[0], wb_ref[2])
    merged = (gt_ref[0, :, 0:d].astype(F32) * ba + gt_ref[0, :, d:2 * d].astype(F32) * bb
              + gt_ref[0, :, 2 * d:3 * d].astype(F32) * bc)
    o_ref[0] = x_ref[0] + _dot(merged.astype(BF16), wo_ref[...])


def _merge(x, yat, yb, yct, gates, wb, wo):
    b, l, _ = x.shape
    tm = TOK_TILE

    def tok(w):
        return pl.BlockSpec((1, tm, w), lambda bi, i: (bi, i, 0))

    tr = pl.BlockSpec((1, BRANCH_W, tm), lambda bi, i: (bi, 0, i))
    return pl.pallas_call(
        _merge_kernel,
        out_shape=jax.ShapeDtypeStruct(x.shape, F32),
        grid=(b, l // tm),
        in_specs=[tok(D_MODEL), tr, tok(BRANCH_W), tr, tok(N_BRANCH * D_MODEL),
                  _const_spec(wb.shape), _const_spec(wo.shape)],
        out_specs=tok(D_MODEL),
        compiler_params=_cparams(("parallel", "parallel"), 48),
        name="merge",
    )(x, yat, yb, yct, gates, wb, wo)


def _rope_tables(l):
    rows = l // GRID_W
    row = jnp.repeat(jnp.arange(rows, dtype=F32), GRID_W)
    col = jnp.tile(jnp.arange(GRID_W, dtype=F32), rows)

    def tab(d_rot):
        n_freq = d_rot // 4
        inv = ROPE_THETA ** (-jnp.arange(n_freq, dtype=F32) / n_freq)
        ang = jnp.concatenate([row[:, None] * inv, col[:, None] * inv], axis=-1)
        c = jnp.repeat(jnp.cos(ang), 2, axis=-1)
        s = jnp.repeat(jnp.sin(ang), 2, axis=-1) * jnp.tile(jnp.array([-1.0, 1.0], F32), d_rot // 2)
        return c, s

    ca, sa = tab(GQA_HEAD_DIM)
    cm, sm = tab(MLA_ROPE)
    tail = jnp.zeros((l, LANES - MLA_ROPE_LANE - MLA_ROPE), F32)
    cm = jnp.concatenate([jnp.ones((l, MLA_ROPE_LANE), F32), cm, tail], axis=1)
    sm = jnp.concatenate([jnp.zeros((l, MLA_ROPE_LANE), F32), sm, tail], axis=1)
    return (jnp.tile(ca, (1, 2)), jnp.tile(sa, (1, 2)), cm, sm)


def _hy_positions(l):
    r = jnp.arange(l, dtype=jnp.int32)
    pos = jnp.concatenate([r, (l - r) % l]).astype(F32)[:, None]
    live = jnp.concatenate([jnp.ones((l,), F32), (r > 0).astype(F32)])[:, None]
    t = pos * (1.0 / (l - 1))
    w = 2.0 * math.pi * pos / l
    f = jnp.linspace(1e-4, HY_BANDS - 1, HY_BANDS, dtype=F32)[None, :]
    z2 = jnp.concatenate([t, jnp.cos(f * w), -jnp.sin(f * w), jnp.zeros((2 * l, LANES - HY_EMB), F32)], axis=-1)
    max_decay = math.log(HY_TARGET) / HY_FAST
    min_decay = math.log(HY_TARGET) / HY_SLOW
    deltas = jnp.abs(jnp.linspace(min_decay, max_decay, HY_WIDTH, dtype=F32))
    return z2, jnp.exp(-t * deltas[None, :]) * live


def _hy_k1_count(l):
    n1 = 2 * l // DFT_N2
    return -(-(n1 // 2 + 1) // 8) * 8


def _dft_tables(l):
    n = 2 * l
    n2 = DFT_N2
    n1 = n // n2
    n1c = _hy_k1_count(l)
    two_pi = 2.0 * math.pi
    i1 = jnp.arange(n1, dtype=jnp.int32)
    ang1 = ((i1[:n1c, None] * i1[None, :]) % n1).astype(F32) * (two_pi / n1)
    live = (i1[:n1c] <= n1 // 2)[:, None]
    c1, s1 = jnp.where(live, jnp.cos(ang1), 0.0), jnp.where(live, jnp.sin(ang1), 0.0)
    f1_full = jnp.stack([c1, -s1], axis=1).reshape(2 * n1c, n1)
    fold = jnp.where((i1[:n1c] == 0) | (i1[:n1c] == n1 // 2), 1.0, 2.0)[:, None] * (1.0 / n)
    g1 = jnp.stack([(c1 * fold)[:, :n1 // 2].T, (-s1 * fold)[:, :n1 // 2].T], axis=2).reshape(n1 // 2, 2 * n1c)
    i2 = jnp.arange(n2, dtype=jnp.int32)
    kk = i1[:n1c, None, None] + n1 * i2[None, :, None]
    ang2 = ((kk * i2[None, None, :]) % n).astype(F32) * (two_pi / n)
    c2, s2 = jnp.cos(ang2), jnp.sin(ang2)
    f2 = jnp.concatenate([jnp.concatenate([c2, s2], axis=2), jnp.concatenate([-s2, c2], axis=2)], axis=1)
    c2t, s2t = jnp.swapaxes(c2, 1, 2), jnp.swapaxes(s2, 1, 2)
    g2 = jnp.concatenate([jnp.concatenate([c2t, -s2t], axis=2), jnp.concatenate([s2t, c2t], axis=2)], axis=1)
    return f1_full, g1, f2, g2


def _swap_pairs(n):
    return np.arange(n) ^ 1


def _proj_a_columns():
    zero = IN_WIDTH
    cols = np.full((PA_END,), zero, np.int64)
    q0, k0, v0, _, cq0, ckv0, kr0, _ = IN_OFFS[:8]
    for h in range(GQA_HEADS):
        dst = h * LANES + (h // (GQA_HEADS // GQA_KV_HEADS)) * GQA_HEAD_DIM
        src = q0 + h * GQA_HEAD_DIM + np.arange(GQA_HEAD_DIM)
        cols[PA_Q + dst:PA_Q + dst + GQA_HEAD_DIM] = src
        cols[PA_QS + dst:PA_QS + dst + GQA_HEAD_DIM] = q0 + h * GQA_HEAD_DIM + _swap_pairs(GQA_HEAD_DIM)
    cols[PA_K:PA_K + 128] = k0 + np.arange(128)
    cols[PA_KS:PA_KS + 128] = k0 + _swap_pairs(128)
    cols[PA_V:PA_V + 128] = v0 + np.arange(128)
    cols[PA_CQ:PA_CQ + MLA_Q_RANK] = cq0 + np.arange(MLA_Q_RANK)
    cols[PA_CKV:PA_CKV + MLA_KV_RANK] = ckv0 + np.arange(MLA_KV_RANK)
    r0 = MLA_ROPE_LANE
    cols[PA_KR + r0:PA_KR + r0 + MLA_ROPE] = kr0 + np.arange(MLA_ROPE)
    cols[PA_KRS + r0:PA_KRS + r0 + MLA_ROPE] = kr0 + _swap_pairs(MLA_ROPE)
    return cols


def _mla_q_columns():
    hd = MLA_NOPE + MLA_ROPE
    zero = MLA_HEADS * hd
    main = np.full((MLA_HEADS * MLA_KC,), zero, np.int64)
    swap = np.full((MLA_HEADS * MLA_KC,), zero, np.int64)
    for h in range(MLA_HEADS):
        b0 = h * MLA_KC
        main[b0:b0 + MLA_NOPE] = h * hd + np.arange(MLA_NOPE)
        r0 = b0 + MLA_ROPE_LANE
        main[r0:r0 + MLA_ROPE] = h * hd + MLA_NOPE + np.arange(MLA_ROPE)
        swap[r0:r0 + MLA_ROPE] = h * hd + MLA_NOPE + _swap_pairs(MLA_ROPE)
    return main, swap


def _mla_kv_columns():
    hd = MLA_NOPE + MLA_V
    zero = MLA_HEADS * hd
    knope = np.full((MLA_HEADS * MLA_KC,), zero, np.int64)
    for h in range(MLA_HEADS):
        knope[h * MLA_KC:h * MLA_KC + MLA_NOPE] = h * hd + np.arange(MLA_NOPE)
    val = np.concatenate([h * hd + MLA_NOPE + np.arange(MLA_V) for h in range(MLA_HEADS)])
    return np.concatenate([knope, val])


def _take_cols(w, cols):
    wz = jnp.concatenate([w, jnp.zeros((w.shape[0], 1), w.dtype)], axis=1)
    return jnp.take(wz, jnp.asarray(cols, jnp.int32), axis=1).astype(BF16)


def _tile2(g):
    return jnp.tile(g, 2)[None, :]


def _encoder(x, lw, g_final):
    b, l, d = x.shape
    n2 = DFT_N2
    n1 = 2 * l // n2
    c = HY_WIDTH
    tabs = _rope_tables(l)
    z2, win2 = _hy_positions(l)
    f1_full, g1, f2, g2 = _dft_tables(l)
    f1_b = f1_full[:, :n1 // 2].astype(BF16)
    g1_b = g1.astype(BF16)
    f2_b = f2.astype(BF16)
    g2_b = g2.astype(BF16)
    for li, w in enumerate(lw):
        x2 = _ffn(x.reshape(b * l, d), w["g_ffn1"], w["wg1"], w["wu1"], w["wd1"], g_final, False)
        x = x2.reshape(b, l, d)
        qg, kg, vgt, qm, km, vmt, bnd = _proj_a(x, w["g_mix"], w["w_a"], tabs, w["gq"], w["gqs"], w["gk"],
                                                w["gks"], w["gcq"], w["gckv"], w["wuq"], w["wuqs"], w["wukv"])
        hy, gates = _proj_b(x2, w["g_mix"], w["w_b"])
        top = bnd[:, :, :4, 0]
        limit = ATT_SAFE_SCORE ** 2
        safe_a = (top[:, :, 0] * jnp.max(top[:, :, 1], axis=1, keepdims=True) <= limit).astype(jnp.int32)
        safe_m = (top[:, :, 2] * jnp.max(top[:, :, 3], axis=1, keepdims=True) <= limit).astype(jnp.int32)
        per_kv = GQA_HEADS // GQA_KV_HEADS
        yat = _attention(safe_a, qg, kg, vgt, heads=ATT_HEADS, kc=GQA_KC, tk=ATT_TK_GQA,
                         key_of_head=(0,) * GQA_HEADS,
                         value_of_head=tuple(h // per_kv for h in range(GQA_HEADS)), name="attn_gqa")
        yct = _attention(safe_m, qm, km, vmt, heads=ATT_HEADS, kc=MLA_KC, tk=ATT_TK_MLA,
                         key_of_head=tuple(range(MLA_HEADS)),
                         value_of_head=tuple(range(MLA_HEADS)), name="attn_mla")
        kc_time = _hy_filter(z2, win2, w["w1p"], w["b1"], w["w2"], w["b2"], w["w3"], w["fr"])
        n1c = _hy_k1_count(l)
        ka =_left_mm(f1_full, kc_time.reshape(1, n1, n2 * c), F32, hi=True)
        kf = _hy_spec(f2, ka.reshape(n1c, 2 * n2, c))
        s, x0 = _hy_pre(hy.reshape(b, l, 3 * c), w["w_short"], w["b_short"])
        a = _left_mm(f1_b, s.reshape(b, n1 // 2, n2 * c), BF16)
        bi = _hy_mid(a.reshape(b, n1c, 2 * n2, c), f2_b, kf, g2_b)
        yb = _hy_post(g1_b, bi.reshape(b, 2 * n1c, n2 * c), s.reshape(b, n1 // 2, n2 * c),
                      x0.reshape(b, n1 // 2, n2 * c), w["bias_t"])
        x = _merge(x, yat, yb.reshape(b, l, c), yct, gates.reshape(b, l, N_BRANCH * d), w["wb"], w["wo"])
        x2 = _ffn(x.reshape(b * l, d), w["g_ffn2"], w["wg2"], w["wu2"], w["wd2"], g_final, li == len(lw) - 1)
        x = x2.reshape(b, l, d)
    return x


def kernel(x_prompt, x_sample, g_ffn1, w_ffn1_gate, w_ffn1_up, w_ffn1_down, g_mix, w_in, g_qnorm, g_knorm,
           w_hy_short, b_hy_short, w_hy_f1, b_hy_f1, w_hy_f2, b_hy_f2, w_hy_f3, hy_sin_freq, hy_bias,
           g_mla_q, w_mla_uq, g_mla_kv, w_mla_ukv, w_branch, w_out, g_ffn2, w_ffn2_gate, w_ffn2_up,
           w_ffn2_down, g_final):
    cols_a = _proj_a_columns()
    uq_main, uq_swap = _mla_q_columns()
    ukv_cols = _mla_kv_columns()
    sw64 = _swap_pairs(GQA_HEAD_DIM)
    hy0 = IN_OFFS[3]
    gt0 = IN_OFFS[7]
    lw = []
    for l in range(DEPTH):
        lw.append(dict(
            g_ffn1=g_ffn1[l][None], wg1=w_ffn1_gate[l].astype(BF16), wu1=w_ffn1_up[l].astype(BF16),
            wd1=w_ffn1_down[l].astype(BF16),
            g_ffn2=g_ffn2[l][None], wg2=w_ffn2_gate[l].astype(BF16), wu2=w_ffn2_up[l].astype(BF16),
            wd2=w_ffn2_down[l].astype(BF16),
            g_mix=g_mix[l][None],
            w_a=_take_cols(w_in[l], cols_a),
            w_b=jnp.concatenate([w_in[l][:, hy0:hy0 + 3 * HY_WIDTH], w_in[l][:, gt0:]], axis=1).astype(BF16),
            gq=_tile2(g_qnorm[l]), gqs=_tile2(g_qnorm[l][sw64]),
            gk=_tile2(g_knorm[l]), gks=_tile2(g_knorm[l][sw64]),
            gcq=g_mla_q[l][None], gckv=g_mla_kv[l][None],
            wuq=_take_cols(w_mla_uq[l], uq_main), wuqs=_take_cols(w_mla_uq[l], uq_swap),
            wukv=_take_cols(w_mla_ukv[l], ukv_cols),
            w_short=w_hy_short[l], b_short=b_hy_short[l],
            w1p=jnp.pad(w_hy_f1[l], ((0, LANES - HY_EMB), (0, 0))), b1=b_hy_f1[l][None],
            w2=w_hy_f2[l], b2=b_hy_f2[l][None], w3=w_hy_f3[l], fr=hy_sin_freq[l][None],
            bias_t=jnp.tile(hy_bias[l], HY_COLS // HY_WIDTH)[None],
            wb=w_branch[l].astype(BF16), wo=w_out[l].astype(BF16),
        ))
    gf = g_final[None]
    return _encoder(x_prompt, lw, gf), _encoder(x_sample, lw, gf)
```

```python
import functools
import math

import numpy as np
import jax
import jax.numpy as jnp
from jax import lax
from jax.experimental import pallas as pl
from jax.experimental.pallas import tpu as pltpu

F32 = jnp.float32
BF16 = jnp.bfloat16

D_MODEL = 1024
DEPTH = 2
GRID_W = 64
ROPE_THETA = 10000.0
EPS = 1e-6
D_FF = 2816
N_BRANCH = 3
BRANCH_W = 512
GQA_HEADS = 8
GQA_KV_HEADS = 2
GQA_HEAD_DIM = 64
HY_WIDTH = 512
HY_ORDER = 64
HY_EMB = 33
HY_BANDS = (HY_EMB - 1) // 2
HY_TARGET = 1e-2
HY_FAST = 0.3
HY_SLOW = 1.5
MLA_HEADS = 8
MLA_Q_RANK = 256
MLA_KV_RANK = 128
MLA_NOPE = 64
MLA_ROPE = 32
MLA_V = 64
IN_WIDTHS = (512, 128, 128, 3 * HY_WIDTH, MLA_Q_RANK, MLA_KV_RANK, MLA_ROPE, N_BRANCH * D_MODEL)
IN_OFFS = tuple(int(c) for c in np.cumsum((0,) + IN_WIDTHS))
IN_WIDTH = IN_OFFS[-1]

LANES = 128
V7X_VMEM_BYTES = 64 * 1024 * 1024

TOK_TILE = 512
ATT_TQ = 256
ATT_TQB = 512
ATT_HEADS = 4
ATT_UNROLL = 4
ATT_SAFE_SCORE = 64.0
ATT_TK_GQA = 2 * TOK_TILE
ATT_TK_MLA = 2 * TOK_TILE
FF_CHUNKS = ((0, 1024), (1024, 2048), (2048, D_FF))
DFT_N2 = 128
HY_COLS = 8192
ONES_ROWS = 16
NEG_BIG = -1e30
LOG2E = math.log2(math.e)


def _cparams(sem, vmem_mb):
    return pltpu.CompilerParams(dimension_semantics=sem, vmem_limit_bytes=vmem_mb * 1024 * 1024)


def _const_spec(shape):
    nd = len(shape)
    return pl.BlockSpec(shape, lambda *_: (0,) * nd, pipeline_mode=pl.Buffered(1))


def _rms(x, g):
    return x * lax.rsqrt(jnp.mean(x * x, axis=-1, keepdims=True) + EPS) * g


def _dot(a, b):
    return jnp.dot(a, b, preferred_element_type=F32)


def _dot_nt(a, b):
    return lax.dot_general(a, b, (((1,), (1,)), ((), ())), preferred_element_type=F32)


def _dot_tn(a, b):
    return lax.dot_general(a, b, (((0,), (0,)), ((), ())), preferred_element_type=F32)


def _split(a):
    hi = a.astype(BF16)
    return hi, (a - hi.astype(F32)).astype(BF16)


def _dot3(a, b):
    ah, al = _split(a)
    bh, bl = _split(b)
    return _dot(ah, bh) + (_dot(ah, bl) + _dot(al, bh))


def _ffn_body(x, g, wg_ref, wu_ref, wd_ref):
    xb = _rms(x, g).astype(BF16)
    acc = jnp.zeros_like(x)
    for c0, c1 in FF_CHUNKS:
        gate = _dot(xb, wg_ref[:, c0:c1])
        up = _dot(xb, wu_ref[:, c0:c1])
        h = (gate * jax.nn.sigmoid(gate) * up).astype(BF16)
        acc = acc + _dot(h, wd_ref[c0:c1, :])
    return x + 0.5 * acc


def _ffn_kernel(x_ref, g_ref, wg_ref, wu_ref, wd_ref, gf_ref, o_ref, *, final):
    y = _ffn_body(x_ref[...], g_ref[...], wg_ref, wu_ref, wd_ref)
    if final:
        y = _rms(y, gf_ref[...])
    o_ref[...] = y


def _ffn(x2d, g, wg, wu, wd, g_final, final):
    t = x2d.shape[0]
    row = pl.BlockSpec((TOK_TILE, D_MODEL), lambda i: (i, 0))
    return pl.pallas_call(
        functools.partial(_ffn_kernel, final=final),
        out_shape=jax.ShapeDtypeStruct((t, D_MODEL), F32),
        grid=(t // TOK_TILE,),
        in_specs=[row, _const_spec((1, D_MODEL)), _const_spec((D_MODEL, D_FF)),
                  _const_spec((D_MODEL, D_FF)), _const_spec((D_FF, D_MODEL)), _const_spec((1, D_MODEL))],
        out_specs=row,
        compiler_params=_cparams(("parallel",), 48),
        name="ffn",
    )(x2d, g, wg, wu, wd, g_final)


PA_Q, PA_QS, PA_K, PA_KS, PA_V, PA_CQ, PA_CKV, PA_KR, PA_KRS, PA_END = (
    0, 1024, 2048, 2176, 2304, 2432, 2688, 2816, 2944, 3072)
GQA_KC = 128
MLA_KC = 128
MLA_ROPE_LANE = MLA_NOPE


def _sq_norm(a_bf16):
    a = a_bf16.astype(F32)
    return jnp.sum(a * a, axis=-1, keepdims=True)


def _proj_a_kernel(x_ref, g_ref, w_ref, cosa_ref, sina_ref, cosm_ref, sinm_ref,
                   gq_ref, gqs_ref, gk_ref, gks_ref, gcq_ref, gckv_ref,
                   wuq_ref, wuqs_ref, wukv_ref,
                   qg_ref, kg_ref, vgt_ref, qm_ref, km_ref, vmt_ref, bnd_ref):
    ub = _rms(x_ref[0], g_ref[...]).astype(BF16)
    cosa, sina = cosa_ref[...], sina_ref[...]
    cosm, sinm = cosm_ref[...], sinm_ref[...]

    zq = _dot(ub, w_ref[:, PA_Q:PA_QS])
    zqs = _dot(ub, w_ref[:, PA_QS:PA_K])
    tq_c = gq_ref[...] * cosa
    tq_s = gqs_ref[...] * sina
    scale_a = GQA_HEAD_DIM ** -0.5 * LOG2E
    qa_n2 = None
    for h in range(GQA_HEADS):
        a = zq[:, h * LANES:(h + 1) * LANES]
        a_sw = zqs[:, h * LANES:(h + 1) * LANES]
        r = lax.rsqrt(jnp.sum(a * a, axis=-1, keepdims=True) * (1.0 / GQA_HEAD_DIM) + EPS)
        qb = ((a * tq_c + a_sw * tq_s) * (r * scale_a)).astype(BF16)
        qg_ref[0, :, h * LANES:(h + 1) * LANES] = qb
        qa_n2 = _sq_norm(qb) if qa_n2 is None else jnp.maximum(qa_n2, _sq_norm(qb))

    zk = _dot(ub, w_ref[:, PA_K:PA_KS])
    zks = _dot(ub, w_ref[:, PA_KS:PA_V])
    first = lax.broadcasted_iota(jnp.int32, zk.shape, 1) < GQA_HEAD_DIM
    sq = zk * zk
    r0 = lax.rsqrt(jnp.sum(jnp.where(first, sq, 0.0), axis=-1, keepdims=True) * (1.0 / GQA_HEAD_DIM) + EPS)
    r1 = lax.rsqrt(jnp.sum(jnp.where(first, 0.0, sq), axis=-1, keepdims=True) * (1.0 / GQA_HEAD_DIM) + EPS)
    kb = ((zk * (gk_ref[...] * cosa) + zks * (gks_ref[...] * sina)) * jnp.where(first, r0, r1)).astype(BF16)
    kg_ref[0] = kb
    kb32 = kb.astype(F32)
    kb_sq = kb32 * kb32
    ka_n2 = jnp.maximum(jnp.sum(jnp.where(first, kb_sq, 0.0), axis=-1, keepdims=True),
                        jnp.sum(jnp.where(first, 0.0, kb_sq), axis=-1, keepdims=True))

    vgt_ref[0, 0] = _dot(ub, w_ref[:, PA_V:PA_CQ]).T.astype(BF16)

    scale_m = (MLA_NOPE + MLA_ROPE) ** -0.5 * LOG2E
    cqn = _rms(_dot(ub, w_ref[:, PA_CQ:PA_CKV]), gcq_ref[...]).astype(BF16)
    zq2 = _dot(cqn, wuq_ref[...])
    zq2s = _dot(cqn, wuqs_ref[...])
    qm_n2 = None
    for h in range(MLA_HEADS):
        blk = slice(h * MLA_KC, (h + 1) * MLA_KC)
        qb = ((zq2[:, blk] * cosm + zq2s[:, blk] * sinm) * scale_m).astype(BF16)
        qm_ref[0, :, blk] = qb
        qm_n2 = _sq_norm(qb) if qm_n2 is None else jnp.maximum(qm_n2, _sq_norm(qb))

    ckvn = _rms(_dot(ub, w_ref[:, PA_CKV:PA_KR]), gckv_ref[...]).astype(BF16)
    zkv = _dot(ckvn, wukv_ref[...])
    krope = _dot(ub, w_ref[:, PA_KR:PA_KRS]) * cosm + _dot(ub, w_ref[:, PA_KRS:PA_END]) * sinm
    km_n2 = None
    for h in range(MLA_HEADS):
        blk = slice(h * MLA_KC, (h + 1) * MLA_KC)
        kb = (zkv[:, blk] + krope).astype(BF16)
        km_ref[0, :, blk] = kb
        km_n2 = _sq_norm(kb) if km_n2 is None else jnp.maximum(km_n2, _sq_norm(kb))
    vmt_ref[0, 0] = zkv[:, MLA_HEADS * MLA_KC:].T.astype(BF16)

    row = lax.broadcasted_iota(jnp.int32, (8, LANES), 0)
    tops = [jnp.max(v, axis=0, keepdims=True) for v in (qa_n2, ka_n2, qm_n2, km_n2)]
    bnd_ref[0, 0] = jnp.where(row == 0, tops[0], jnp.where(row == 1, tops[1], jnp.where(row == 2, tops[2], tops[3])))


def _proj_a(x, g_mix, w_a, tabs, gq, gqs, gk, gks, gcq, gckv, wuq, wuqs, wukv):
    b, l, _ = x.shape
    tm = TOK_TILE
    nt = l // tm
    cosa, sina, cosm, sinm = tabs
    tab = pl.BlockSpec((tm, LANES), lambda bi, i: (i, 0))

    def tok(w):
        return pl.BlockSpec((1, tm, w), lambda bi, i: (bi, i, 0))

    def tr(rows):
        return pl.BlockSpec((1, 1, rows, tm), lambda bi, i: (bi, i, 0, 0))

    out_shape = (
        jax.ShapeDtypeStruct((b, l, GQA_HEADS * GQA_KC), BF16),
        jax.ShapeDtypeStruct((b, l, GQA_KC), BF16),
        jax.ShapeDtypeStruct((b, nt, GQA_KV_HEADS * GQA_HEAD_DIM, tm), BF16),
        jax.ShapeDtypeStruct((b, l, MLA_HEADS * MLA_KC), BF16),
        jax.ShapeDtypeStruct((b, l, MLA_HEADS * MLA_KC), BF16),
        jax.ShapeDtypeStruct((b, nt, MLA_HEADS * MLA_V, tm), BF16),
        jax.ShapeDtypeStruct((b, nt, 8, LANES), F32),
    )
    return pl.pallas_call(
        _proj_a_kernel,
        out_shape=out_shape,
        grid=(b, nt),
        in_specs=[tok(D_MODEL), _const_spec((1, D_MODEL)), _const_spec(w_a.shape), tab, tab, tab, tab,
                  _const_spec((1, LANES)), _const_spec((1, LANES)), _const_spec((1, LANES)),
                  _const_spec((1, LANES)), _const_spec((1, MLA_Q_RANK)), _const_spec((1, MLA_KV_RANK)),
                  _const_spec(wuq.shape), _const_spec(wuqs.shape), _const_spec(wukv.shape)],
        out_specs=(tok(GQA_HEADS * GQA_KC), tok(GQA_KC), tr(GQA_KV_HEADS * GQA_HEAD_DIM),
                   tok(MLA_HEADS * MLA_KC), tok(MLA_HEADS * MLA_KC), tr(MLA_HEADS * MLA_V),
                   pl.BlockSpec((1, 1, 8, LANES), lambda bi, i: (bi, i, 0, 0))),
        compiler_params=_cparams(("parallel", "parallel"), 48),
        name="proj_a",
    )(x, g_mix, w_a, cosa, sina, cosm, sinm, gq, gqs, gk, gks, gcq, gckv, wuq, wuqs, wukv)


def _proj_b_kernel(x_ref, g_ref, w_ref, hy_ref, gt_ref):
    ub = _rms(x_ref[...], g_ref[...]).astype(BF16)
    hy_ref[...] = _dot(ub, w_ref[:, :3 * HY_WIDTH]).astype(BF16)
    gt_ref[...] = jax.nn.sigmoid(_dot(ub, w_ref[:, 3 * HY_WIDTH:])).astype(BF16)


def _proj_b(x2d, g_mix, w_b):
    t = x2d.shape[0]
    tm = TOK_TILE

    def row(w):
        return pl.BlockSpec((tm, w), lambda i: (i, 0))

    return pl.pallas_call(
        _proj_b_kernel,
        out_shape=(jax.ShapeDtypeStruct((t, 3 * HY_WIDTH), BF16),
                   jax.ShapeDtypeStruct((t, N_BRANCH * D_MODEL), BF16)),
        grid=(t // tm,),
        in_specs=[row(D_MODEL), _const_spec((1, D_MODEL)), _const_spec(w_b.shape)],
        out_specs=(row(3 * HY_WIDTH), row(N_BRANCH * D_MODEL)),
        compiler_params=_cparams(("parallel",), 48),
        name="proj_b",
    )(x2d, g_mix, w_b)


def _attn_kernel(safe_ref, q_ref, k_ref, vt_ref, o_ref, acc_scr, p0_scr, p1_scr,
                 *, heads, groups, kc, key_of_head, value_of_head, nk, nk_tile):
    tv = vt_ref.shape[3]
    tk = nk_tile
    tq = ATT_TQ
    ones = jnp.ones((ONES_ROWS, tk), BF16)
    dv = MLA_V
    n_sub = q_ref.shape[1] // tq
    safe = safe_ref[pl.program_id(0), pl.program_id(1)] != 0

    def key_tile(j, h):
        kb = key_of_head[h]
        rows = j * tk if isinstance(j, int) else pl.multiple_of(j * tk, tk)
        return k_ref[0, pl.ds(rows, tk), kb * kc:(kb + 1) * kc]

    def values(j, h):
        vb = value_of_head[h]
        per = tk // tv
        return jnp.concatenate([vt_ref[0, j * per + u, vb * dv:(vb + 1) * dv, :] for u in range(per)], axis=1)

    def value_tile(j, h):
        return jnp.concatenate([values(j, h), ones], axis=0)

    def write_out(acc, l, h, sub):
        o_ref[0, h * dv:(h + 1) * dv, sub * tq:(sub + 1) * tq] = (acc / l).astype(BF16)

    @pl.when(safe)
    def _():
        bufs = (p0_scr, p1_scr)
        streams = [(sub, g) for sub in range(n_sub) for g in range(groups)]

        def probs(si, stream, j, hh, buf):
            sub, g = stream
            h = g * heads + hh
            st = _dot_nt(key_tile(j, h), q_ref[0, sub * tq:(sub + 1) * tq, h * kc:(h + 1) * kc])
            buf[hh] = jnp.exp2(st).astype(BF16)

        def accumulate(si, stream, j, hh, buf):
            acc_scr[si % 2, hh] += _dot(value_tile(j, stream[1] * heads + hh), buf[hh])

        acc_scr[...] = jnp.zeros(acc_scr.shape, F32)
        for hh in range(heads):
            probs(0, streams[0], 0, hh, bufs[0])
        for si, stream in enumerate(streams):
            def step(j, cur, nxt, si=si, stream=stream):
                for hh in range(heads):
                    probs(si, stream, j + 1, hh, nxt)
                    accumulate(si, stream, j, hh, cur)

            unroll = min(ATT_UNROLL, nk)

            def body(jj, carry, step=step):
                for u in range(unroll):
                    step(unroll * jj + u, bufs[u % 2], bufs[1 - u % 2])
                return carry

            lax.fori_loop(0, nk // unroll - 1, body, 0)
            for u in range(unroll - 1):
                step(nk - unroll + u, bufs[u % 2], bufs[1 - u % 2])
            for hh in range(heads):
                if si + 1 < len(streams):
                    probs(si + 1, streams[si + 1], 0, hh, bufs[0])
                accumulate(si, stream, nk - 1, hh, bufs[1])
            sub, g = stream
            for hh in range(heads):
                acc = acc_scr[si % 2, hh]
                write_out(acc[:dv], acc[dv:dv + 1], g * heads + hh, sub)
                if si + 2 < len(streams):
                    acc_scr[si % 2, hh] = jnp.zeros((dv + ONES_ROWS, tq), F32)

    @pl.when(jnp.logical_not(safe))
    def _():
        for h in range(groups * heads):
            for sub in range(n_sub):
                qh = q_ref[0, sub * tq:(sub + 1) * tq, h * kc:(h + 1) * kc]

                def body(j, carry, qh=qh, h=h):
                    m, acc = carry
                    st = _dot_nt(key_tile(j, h), qh)
                    m_new = jnp.maximum(m, jnp.max(st, axis=0, keepdims=True))
                    p = jnp.exp2(st - m_new).astype(BF16)
                    return m_new, jnp.exp2(m - m_new) * acc + _dot(value_tile(j, h), p)

                init = (jnp.full((1, tq), NEG_BIG, F32), jnp.zeros((dv + ONES_ROWS, tq), F32))
                _, acc = lax.fori_loop(0, nk, body, init)
                write_out(acc[:dv], acc[dv:dv + 1], h, sub)


def _attention(safe, q, k, vt, *, heads, kc, tk, key_of_head, value_of_head, name):
    b, l, qw = q.shape
    n_heads = qw // kc
    nv, tv = vt.shape[1], vt.shape[3]
    nk = l // tk
    tqb = ATT_TQB
    tq = ATT_TQ
    dv = MLA_V
    grid_spec = pltpu.PrefetchScalarGridSpec(
        num_scalar_prefetch=1,
        grid=(b, l // tqb),
        in_specs=[pl.BlockSpec((1, tqb, qw), lambda bi, i, s: (bi, i, 0)),
                  pl.BlockSpec((1, l, k.shape[2]), lambda bi, i, s: (bi, 0, 0), pipeline_mode=pl.Buffered(1)),
                  pl.BlockSpec((1, nv, vt.shape[2], tv), lambda bi, i, s: (bi, 0, 0, 0),
                               pipeline_mode=pl.Buffered(1))],
        out_specs=pl.BlockSpec((1, n_heads * dv, tqb), lambda bi, i, s: (bi, 0, i)),
        scratch_shapes=[pltpu.VMEM((2, heads, dv + ONES_ROWS, tq), F32),
                        pltpu.VMEM((heads, tk, tq), BF16), pltpu.VMEM((heads, tk, tq), BF16)],
    )
    return pl.pallas_call(
        functools.partial(_attn_kernel, heads=heads, groups=n_heads // heads, kc=kc, key_of_head=key_of_head,
                          value_of_head=value_of_head, nk=nk, nk_tile=tk),
        out_shape=jax.ShapeDtypeStruct((b, n_heads * dv, l), BF16),
        grid_spec=grid_spec,
        compiler_params=_cparams(("parallel", "parallel"), 48),
        name=name,
    )(safe, q, k, vt)


def _hy_pre_kernel(x0_ref, x1_ref, v_ref, w_ref, b_ref, s_ref, x0o_ref):
    rows = x0_ref.shape[1]
    t = lax.broadcasted_iota(jnp.int32, (rows, LANES), 0)
    not_first = t > 0
    not_last = t < rows - 1

    def conv(ref, j):
        a = ref[0].astype(F32)
        prev = jnp.where(not_first, pltpu.roll(a, 1, 0), 0.0)
        nxt = jnp.where(not_last, pltpu.roll(a, rows - 1, 0), 0.0)
        return prev * w_ref[0, j] + a * w_ref[1, j] + nxt * w_ref[2, j] + b_ref[j]

    x0o_ref[0] = conv(x0_ref, 0).astype(BF16)
    s_ref[0] = (conv(x1_ref, 1) * conv(v_ref, 2)).astype(BF16)


def _hy_pre(hy, w_short, b_short):
    b, l, _ = hy.shape
    nc = HY_WIDTH // LANES
    w4 = w_short.reshape(3, 3, nc, 1, LANES)
    b4 = b_short.reshape(3, nc, 1, LANES)

    def seg(j):
        return pl.BlockSpec((1, l, LANES), lambda bi, c, j=j: (bi, 0, j * nc + c))

    out = pl.BlockSpec((1, l, LANES), lambda bi, c: (bi, 0, c))
    return pl.pallas_call(
        _hy_pre_kernel,
        out_shape=(jax.ShapeDtypeStruct((b, l, HY_WIDTH), BF16), jax.ShapeDtypeStruct((b, l, HY_WIDTH), BF16)),
        grid=(b, nc),
        in_specs=[seg(0), seg(1), seg(2),
                  pl.BlockSpec((3, 3, None, 1, LANES), lambda bi, c: (0, 0, c, 0, 0)),
                  pl.BlockSpec((3, None, 1, LANES), lambda bi, c: (0, c, 0, 0))],
        out_specs=(out, out),
        compiler_params=_cparams(("parallel", "parallel"), 48),
        name="hy_pre",
    )(hy, hy, hy, w4, b4)


def _left_mm_kernel(f_ref, x_ref, o_ref, *, hi):
    if hi:
        o_ref[0] = _dot3(f_ref[...], x_ref[0]).astype(o_ref.dtype)
    else:
        o_ref[0] = _dot(f_ref[...], x_ref[0]).astype(o_ref.dtype)


def _left_mm(f, x, out_dtype, hi=False):
    b, k, n = x.shape
    m = f.shape[0]
    cb = min(HY_COLS, n)
    return pl.pallas_call(
        functools.partial(_left_mm_kernel, hi=hi),
        out_shape=jax.ShapeDtypeStruct((b, m, n), out_dtype),
        grid=(b, n // cb),
        in_specs=[_const_spec((m, k)), pl.BlockSpec((1, k, cb), lambda bi, c: (bi, 0, c))],
        out_specs=pl.BlockSpec((1, m, cb), lambda bi, c: (bi, 0, c)),
        compiler_params=_cparams(("parallel", "parallel"), 48),
        name="hy_outer_dft",
    )(f, x)


def _hy_mid_kernel(f_ref, kf_ref, g_ref, a_ref, o_ref):
    n2 = DFT_N2
    kr, ki = kf_ref[0, :n2], kf_ref[0, n2:]
    for bi in range(a_ref.shape[0]):
        x = _dot(f_ref[0], a_ref[bi, 0])
        xr, xi = x[:n2], x[n2:]
        y = jnp.concatenate([xr * kr - xi * ki, xr * ki + xi * kr], axis=0).astype(BF16)
        o_ref[bi, 0] = _dot(g_ref[0], y).astype(BF16)


def _hy_mid(a4, f2, kf, g2):
    b, n1, r, c = a4.shape
    mat = pl.BlockSpec((1, r, r), lambda i: (i, 0, 0))
    dat = pl.BlockSpec((b, 1, r, c), lambda i: (0, i, 0, 0))
    return pl.pallas_call(
        _hy_mid_kernel,
        out_shape=jax.ShapeDtypeStruct(a4.shape, BF16),
        grid=(n1,),
        in_specs=[mat, pl.BlockSpec((1, r, c), lambda i: (i, 0, 0)), mat, dat],
        out_specs=dat,
        compiler_params=_cparams(("parallel",), 48),
        name="hy_mid",
    )(f2, kf, g2, a4)


def _hy_post_kernel(g_ref, bi_ref, s_ref, x0_ref, bias_ref, o_ref):
    y = _dot(g_ref[...], bi_ref[0]) + s_ref[0].astype(F32) * bias_ref[...]
    o_ref[0] = (x0_ref[0].astype(F32) * y).astype(BF16)


def _hy_post(g1, bi2, s2, x02, bias_t):
    b, k, n = bi2.shape
    m = g1.shape[0]
    cb = HY_COLS
    dat = pl.BlockSpec((1, m, cb), lambda bi, c: (bi, 0, c))
    return pl.pallas_call(
        _hy_post_kernel,
        out_shape=jax.ShapeDtypeStruct((b, m, n), BF16),
        grid=(b, n // cb),
        in_specs=[_const_spec((m, k)), pl.BlockSpec((1, k, cb), lambda bi, c: (bi, 0, c)), dat, dat,
                  _const_spec((1, cb))],
        out_specs=dat,
        compiler_params=_cparams(("parallel", "parallel"), 48),
        name="hy_post",
    )(g1, bi2, s2, x02, bias_t)


def _hy_filter_kernel(z_ref, win_ref, w1_ref, b1_ref, w2_ref, b2_ref, w3_ref, fr_ref, o_ref, *, half_tiles):
    fr = fr_ref[...]
    h = jnp.sin(fr * (_dot3(z_ref[...], w1_ref[...]) + b1_ref[...]))
    h = jnp.sin(fr * (_dot3(h, w2_ref[...]) + b2_ref[...]))
    h3 = _dot3(h, w3_ref[...])
    backward = pl.program_id(0) >= half_tiles
    o_ref[...] = jnp.where(backward, h3[:, HY_WIDTH:], h3[:, :HY_WIDTH]) * win_ref[...]


def _hy_filter(z2, win2, w1p, b1, w2, b2, w3, fr):
    n = z2.shape[0]
    tl = 512
    return pl.pallas_call(
        functools.partial(_hy_filter_kernel, half_tiles=n // (2 * tl)),
        out_shape=jax.ShapeDtypeStruct((n, HY_WIDTH), F32),
        grid=(n // tl,),
        in_specs=[pl.BlockSpec((tl, LANES), lambda i: (i, 0)), pl.BlockSpec((tl, HY_WIDTH), lambda i: (i, 0)),
                  _const_spec(w1p.shape), _const_spec(b1.shape), _const_spec(w2.shape), _const_spec(b2.shape),
                  _const_spec(w3.shape), _const_spec(fr.shape)],
        out_specs=pl.BlockSpec((tl, HY_WIDTH), lambda i: (i, 0)),
        compiler_params=_cparams(("parallel",), 32),
        name="hy_filter",
    )(z2, win2, w1p, b1, w2, b2, w3, fr)


def _hy_spec_kernel(f_ref, a_ref, o_ref):
    o_ref[0] = _dot3(f_ref[0], a_ref[0])


def _hy_spec(f2, a3):
    n1, r, c = a3.shape
    return pl.pallas_call(
        _hy_spec_kernel,
        out_shape=jax.ShapeDtypeStruct(a3.shape, F32),
        grid=(n1,),
        in_specs=[pl.BlockSpec((1, r, r), lambda i: (i, 0, 0)), pl.BlockSpec((1, r, c), lambda i: (i, 0, 0))],
        out_specs=pl.BlockSpec((1, r, c), lambda i: (i, 0, 0)),
        compiler_params=_cparams(("parallel",), 32),
        name="hy_filter_spectrum",
    )(f2, a3)


def _merge_kernel(x_ref, yat_ref, yb_ref, yct_ref, gt_ref, wb_ref, wo_ref, g_ref, wg_ref, wu_ref, wd_ref, gf_ref,
                  o_ref, *, final):
    d = D_MODEL
    ba = _dot_tn(yat_ref[0], wb_ref[0])
    bb = _dot(yb_ref[0], wb_ref[1])
    bc = _dot_tn(yct_ref[0], wb_ref[2])
    merged = (gt_ref[0, :, 0:d].astype(F32) * ba + gt_ref[0, :, d:2 * d].astype(F32) * bb
              + gt_ref[0, :, 2 * d:3 * d].astype(F32) * bc)
    x = x_ref[0] + _dot(merged.astype(BF16), wo_ref[...])
    y = _ffn_body(x, g_ref[...], wg_ref, wu_ref, wd_ref)
    if final:
        y = _rms(y, gf_ref[...])
    o_ref[0] = y


def _merge_ffn(x, yat, yb, yct, gates, wb, wo, g, wg, wu, wd, g_final, final):
    b, l, _ = x.shape
    tm = TOK_TILE

    def tok(w):
        return pl.BlockSpec((1, tm, w), lambda bi, i: (bi, i, 0))

    tr = pl.BlockSpec((1, BRANCH_W, tm), lambda bi, i: (bi, 0, i))
    return pl.pallas_call(
        functools.partial(_merge_kernel, final=final),
        out_shape=jax.ShapeDtypeStruct(x.shape, F32),
        grid=(b, l // tm),
        in_specs=[tok(D_MODEL), tr, tok(BRANCH_W), tr, tok(N_BRANCH * D_MODEL),
                  _const_spec(wb.shape), _const_spec(wo.shape), _const_spec((1, D_MODEL)),
                  _const_spec((D_MODEL, D_FF)), _const_spec((D_MODEL, D_FF)), _const_spec((D_FF, D_MODEL)),
                  _const_spec((1, D_MODEL))],
        out_specs=tok(D_MODEL),
        compiler_params=_cparams(("parallel", "parallel"), 58),
        name="merge_ffn",
    )(x, yat, yb, yct, gates, wb, wo, g, wg, wu, wd, g_final)


def _rope_tables(l):
    rows = l // GRID_W
    row = jnp.repeat(jnp.arange(rows, dtype=F32), GRID_W)
    col = jnp.tile(jnp.arange(GRID_W, dtype=F32), rows)

    def tab(d_rot):
        n_freq = d_rot // 4
        inv = ROPE_THETA ** (-jnp.arange(n_freq, dtype=F32) / n_freq)
        ang = jnp.concatenate([row[:, None] * inv, col[:, None] * inv], axis=-1)
        c = jnp.repeat(jnp.cos(ang), 2, axis=-1)
        s = jnp.repeat(jnp.sin(ang), 2, axis=-1) * jnp.tile(jnp.array([-1.0, 1.0], F32), d_rot // 2)
        return c, s

    ca, sa = tab(GQA_HEAD_DIM)
    cm, sm = tab(MLA_ROPE)
    tail = jnp.zeros((l, LANES - MLA_ROPE_LANE - MLA_ROPE), F32)
    cm = jnp.concatenate([jnp.ones((l, MLA_ROPE_LANE), F32), cm, tail], axis=1)
    sm = jnp.concatenate([jnp.zeros((l, MLA_ROPE_LANE), F32), sm, tail], axis=1)
    return (jnp.tile(ca, (1, 2)), jnp.tile(sa, (1, 2)), cm, sm)


def _hy_positions(l):
    r = jnp.arange(l, dtype=jnp.int32)
    pos = jnp.concatenate([r, (l - r) % l]).astype(F32)[:, None]
    live = jnp.concatenate([jnp.ones((l,), F32), (r > 0).astype(F32)])[:, None]
    t = pos * (1.0 / (l - 1))
    w = 2.0 * math.pi * pos / l
    f = jnp.linspace(1e-4, HY_BANDS - 1, HY_BANDS, dtype=F32)[None, :]
    z2 = jnp.concatenate([t, jnp.cos(f * w), -jnp.sin(f * w), jnp.zeros((2 * l, LANES - HY_EMB), F32)], axis=-1)
    max_decay = math.log(HY_TARGET) / HY_FAST
    min_decay = math.log(HY_TARGET) / HY_SLOW
    deltas = jnp.abs(jnp.linspace(min_decay, max_decay, HY_WIDTH, dtype=F32))
    return z2, jnp.exp(-t * deltas[None, :]) * live


def _hy_k1_count(l):
    n1 = 2 * l // DFT_N2
    return -(-(n1 // 2 + 1) // 8) * 8


def _dft_tables(l):
    n = 2 * l
    n2 = DFT_N2
    n1 = n // n2
    n1c = _hy_k1_count(l)
    two_pi = 2.0 * math.pi
    i1 = jnp.arange(n1, dtype=jnp.int32)
    ang1 = ((i1[:n1c, None] * i1[None, :]) % n1).astype(F32) * (two_pi / n1)
    live = (i1[:n1c] <= n1 // 2)[:, None]
    c1, s1 = jnp.where(live, jnp.cos(ang1), 0.0), jnp.where(live, jnp.sin(ang1), 0.0)
    f1_full = jnp.stack([c1, -s1], axis=1).reshape(2 * n1c, n1)
    fold = jnp.where((i1[:n1c] == 0) | (i1[:n1c] == n1 // 2), 1.0, 2.0)[:, None] * (1.0 / n)
    g1 = jnp.stack([(c1 * fold)[:, :n1 // 2].T, (-s1 * fold)[:, :n1 // 2].T], axis=2).reshape(n1 // 2, 2 * n1c)
    i2 = jnp.arange(n2, dtype=jnp.int32)
    kk = i1[:n1c, None, None] + n1 * i2[None, :, None]
    ang2 = ((kk * i2[None, None, :]) % n).astype(F32) * (two_pi / n)
    c2, s2 = jnp.cos(ang2), jnp.sin(ang2)
    f2 = jnp.concatenate([jnp.concatenate([c2, s2], axis=2), jnp.concatenate([-s2, c2], axis=2)], axis=1)
    c2t, s2t = jnp.swapaxes(c2, 1, 2), jnp.swapaxes(s2, 1, 2)
    g2 = jnp.concatenate([jnp.concatenate([c2t, -s2t], axis=2), jnp.concatenate([s2t, c2t], axis=2)], axis=1)
    return f1_full, g1, f2, g2


def _swap_pairs(n):
    return np.arange(n) ^ 1


def _proj_a_columns():
    zero = IN_WIDTH
    cols = np.full((PA_END,), zero, np.int64)
    q0, k0, v0, _, cq0, ckv0, kr0, _ = IN_OFFS[:8]
    for h in range(GQA_HEADS):
        dst = h * LANES + (h // (GQA_HEADS // GQA_KV_HEADS)) * GQA_HEAD_DIM
        src = q0 + h * GQA_HEAD_DIM + np.arange(GQA_HEAD_DIM)
        cols[PA_Q + dst:PA_Q + dst + GQA_HEAD_DIM] = src
        cols[PA_QS + dst:PA_QS + dst + GQA_HEAD_DIM] = q0 + h * GQA_HEAD_DIM + _swap_pairs(GQA_HEAD_DIM)
    cols[PA_K:PA_K + 128] = k0 + np.arange(128)
    cols[PA_KS:PA_KS + 128] = k0 + _swap_pairs(128)
    cols[PA_V:PA_V + 128] = v0 + np.arange(128)
    cols[PA_CQ:PA_CQ + MLA_Q_RANK] = cq0 + np.arange(MLA_Q_RANK)
    cols[PA_CKV:PA_CKV + MLA_KV_RANK] = ckv0 + np.arange(MLA_KV_RANK)
    r0 = MLA_ROPE_LANE
    cols[PA_KR + r0:PA_KR + r0 + MLA_ROPE] = kr0 + np.arange(MLA_ROPE)
    cols[PA_KRS + r0:PA_KRS + r0 + MLA_ROPE] = kr0 + _swap_pairs(MLA_ROPE)
    return cols


def _mla_q_columns():
    hd = MLA_NOPE + MLA_ROPE
    zero = MLA_HEADS * hd
    main = np.full((MLA_HEADS * MLA_KC,), zero, np.int64)
    swap = np.full((MLA_HEADS * MLA_KC,), zero, np.int64)
    for h in range(MLA_HEADS):
        b0 = h * MLA_KC
        main[b0:b0 + MLA_NOPE] = h * hd + np.arange(MLA_NOPE)
        r0 = b0 + MLA_ROPE_LANE
        main[r0:r0 + MLA_ROPE] = h * hd + MLA_NOPE + np.arange(MLA_ROPE)
        swap[r0:r0 + MLA_ROPE] = h * hd + MLA_NOPE + _swap_pairs(MLA_ROPE)
    return main, swap


def _mla_kv_columns():
    hd = MLA_NOPE + MLA_V
    zero = MLA_HEADS * hd
    knope = np.full((MLA_HEADS * MLA_KC,), zero, np.int64)
    for h in range(MLA_HEADS):
        knope[h * MLA_KC:h * MLA_KC + MLA_NOPE] = h * hd + np.arange(MLA_NOPE)
    val = np.concatenate([h * hd + MLA_NOPE + np.arange(MLA_V) for h in range(MLA_HEADS)])
    return np.concatenate([knope, val])


def _take_cols(w, cols):
    wz = jnp.concatenate([w, jnp.zeros((w.shape[0], 1), w.dtype)], axis=1)
    return jnp.take(wz, jnp.asarray(cols, jnp.int32), axis=1).astype(BF16)


def _tile2(g):
    return jnp.tile(g, 2)[None, :]


def _encoder(x, lw, g_final):
    b, l, d = x.shape
    n2 = DFT_N2
    n1 = 2 * l // n2
    c = HY_WIDTH
    tabs = _rope_tables(l)
    z2, win2 = _hy_positions(l)
    f1_full, g1, f2, g2 = _dft_tables(l)
    f1_b = f1_full[:, :n1 // 2].astype(BF16)
    g1_b = g1.astype(BF16)
    f2_b = f2.astype(BF16)
    g2_b = g2.astype(BF16)
    for li, w in enumerate(lw):
        x2 = _ffn(x.reshape(b * l, d), w["g_ffn1"], w["wg1"], w["wu1"], w["wd1"], g_final, False)
        x = x2.reshape(b, l, d)
        qg, kg, vgt, qm, km, vmt, bnd = _proj_a(x, w["g_mix"], w["w_a"], tabs, w["gq"], w["gqs"], w["gk"],
                                                w["gks"], w["gcq"], w["gckv"], w["wuq"], w["wuqs"], w["wukv"])
        hy, gates = _proj_b(x2, w["g_mix"], w["w_b"])
        top = bnd[:, :, :4, 0]
        limit = ATT_SAFE_SCORE ** 2
        safe_a = (top[:, :, 0] * jnp.max(top[:, :, 1], axis=1, keepdims=True) <= limit).astype(jnp.int32)
        safe_m = (top[:, :, 2] * jnp.max(top[:, :, 3], axis=1, keepdims=True) <= limit).astype(jnp.int32)
        per_kv = GQA_HEADS // GQA_KV_HEADS
        yat = _attention(safe_a, qg, kg, vgt, heads=ATT_HEADS, kc=GQA_KC, tk=ATT_TK_GQA,
                         key_of_head=(0,) * GQA_HEADS,
                         value_of_head=tuple(h // per_kv for h in range(GQA_HEADS)), name="attn_gqa")
        yct = _attention(safe_m, qm, km, vmt, heads=ATT_HEADS, kc=MLA_KC, tk=ATT_TK_MLA,
                         key_of_head=tuple(range(MLA_HEADS)),
                         value_of_head=tuple(range(MLA_HEADS)), name="attn_mla")
        kc_time = _hy_filter(z2, win2, w["w1p"], w["b1"], w["w2"], w["b2"], w["w3"], w["fr"])
        n1c = _hy_k1_count(l)
        ka =_left_mm(f1_full, kc_time.reshape(1, n1, n2 * c), F32, hi=True)
        kf = _hy_spec(f2, ka.reshape(n1c, 2 * n2, c))
        s, x0 = _hy_pre(hy.reshape(b, l, 3 * c), w["w_short"], w["b_short"])
        a = _left_mm(f1_b, s.reshape(b, n1 // 2, n2 * c), BF16)
        bi = _hy_mid(a.reshape(b, n1c, 2 * n2, c), f2_b, kf, g2_b)
        yb = _hy_post(g1_b, bi.reshape(b, 2 * n1c, n2 * c), s.reshape(b, n1 // 2, n2 * c),
                      x0.reshape(b, n1 // 2, n2 * c), w["bias_t"])
        x = _merge_ffn(x, yat, yb.reshape(b, l, c), yct, gates.reshape(b, l, N_BRANCH * d), w["wb"], w["wo"],
                       w["g_ffn2"], w["wg2"], w["wu2"], w["wd2"], g_final, li == len(lw) - 1)
    return x


def kernel(x_prompt, x_sample, g_ffn1, w_ffn1_gate, w_ffn1_up, w_ffn1_down, g_mix, w_in, g_qnorm, g_knorm,
           w_hy_short, b_hy_short, w_hy_f1, b_hy_f1, w_hy_f2, b_hy_f2, w_hy_f3, hy_sin_freq, hy_bias,
           g_mla_q, w_mla_uq, g_mla_kv, w_mla_ukv, w_branch, w_out, g_ffn2, w_ffn2_gate, w_ffn2_up,
           w_ffn2_down, g_final):
    cols_a = _proj_a_columns()
    uq_main, uq_swap = _mla_q_columns()
    ukv_cols = _mla_kv_columns()
    sw64 = _swap_pairs(GQA_HEAD_DIM)
    hy0 = IN_OFFS[3]
    gt0 = IN_OFFS[7]
    lw = []
    for l in range(DEPTH):
        lw.append(dict(
            g_ffn1=g_ffn1[l][None], wg1=w_ffn1_gate[l].astype(BF16), wu1=w_ffn1_up[l].astype(BF16),
            wd1=w_ffn1_down[l].astype(BF16),
            g_ffn2=g_ffn2[l][None], wg2=w_ffn2_gate[l].astype(BF16), wu2=w_ffn2_up[l].astype(BF16),
            wd2=w_ffn2_down[l].astype(BF16),
            g_mix=g_mix[l][None],
            w_a=_take_cols(w_in[l], cols_a),
            w_b=jnp.concatenate([w_in[l][:, hy0:hy0 + 3 * HY_WIDTH], w_in[l][:, gt0:]], axis=1).astype(BF16),
            gq=_tile2(g_qnorm[l]), gqs=_tile2(g_qnorm[l][sw64]),
            gk=_tile2(g_knorm[l]), gks=_tile2(g_knorm[l][sw64]),
            gcq=g_mla_q[l][None], gckv=g_mla_kv[l][None],
            wuq=_take_cols(w_mla_uq[l], uq_main), wuqs=_take_cols(w_mla_uq[l], uq_swap),
            wukv=_take_cols(w_mla_ukv[l], ukv_cols),
            w_short=w_hy_short[l], b_short=b_hy_short[l],
            w1p=jnp.pad(w_hy_f1[l], ((0, LANES - HY_EMB), (0, 0))), b1=b_hy_f1[l][None],
            w2=w_hy_f2[l], b2=b_hy_f2[l][None], w3=w_hy_f3[l], fr=hy_sin_freq[l][None],
            bias_t=jnp.tile(hy_bias[l], HY_COLS // HY_WIDTH)[None],
            wb=w_branch[l].astype(BF16), wo=w_out[l].astype(BF16),
        ))
    gf = g_final[None]
    return _encoder(x_prompt, lw, gf), _encoder(x_sample, lw, gf)
```

```python
import functools
import math

import numpy as np
import jax
import jax.numpy as jnp
from jax import lax
from jax.experimental import pallas as pl
from jax.experimental.pallas import tpu as pltpu

F32 = jnp.float32
BF16 = jnp.bfloat16

D_MODEL = 1024
DEPTH = 2
GRID_W = 64
ROPE_THETA = 10000.0
EPS = 1e-6
D_FF = 2816
N_BRANCH = 3
BRANCH_W = 512
GQA_HEADS = 8
GQA_KV_HEADS = 2
GQA_HEAD_DIM = 64
HY_WIDTH = 512
HY_ORDER = 64
HY_EMB = 33
HY_BANDS = (HY_EMB - 1) // 2
HY_TARGET = 1e-2
HY_FAST = 0.3
HY_SLOW = 1.5
MLA_HEADS = 8
MLA_Q_RANK = 256
MLA_KV_RANK = 128
MLA_NOPE = 64
MLA_ROPE = 32
MLA_V = 64
IN_WIDTHS = (512, 128, 128, 3 * HY_WIDTH, MLA_Q_RANK, MLA_KV_RANK, MLA_ROPE, N_BRANCH * D_MODEL)
IN_OFFS = tuple(int(c) for c in np.cumsum((0,) + IN_WIDTHS))
IN_WIDTH = IN_OFFS[-1]

LANES = 128
V7X_VMEM_BYTES = 64 * 1024 * 1024

TOK_TILE = 512
ATT_TQ = 256
ATT_TQB = 512
ATT_HEADS = 4
ATT_UNROLL = 4
ATT_SAFE_SCORE = 64.0
ATT_TK_GQA = 2 * TOK_TILE
ATT_TK_MLA = 2 * TOK_TILE
FF_CHUNKS = ((0, 1024), (1024, 2048), (2048, D_FF))
DFT_N2 = 128
HY_COLS = 8192
ONES_ROWS = 16
NEG_BIG = -1e30
LOG2E = math.log2(math.e)


def _cparams(sem, vmem_mb):
    return pltpu.CompilerParams(dimension_semantics=sem, vmem_limit_bytes=vmem_mb * 1024 * 1024)


def _const_spec(shape):
    nd = len(shape)
    return pl.BlockSpec(shape, lambda *_: (0,) * nd, pipeline_mode=pl.Buffered(1))


def _rms(x, g):
    return x * lax.rsqrt(jnp.mean(x * x, axis=-1, keepdims=True) + EPS) * g


def _dot(a, b):
    return jnp.dot(a, b, preferred_element_type=F32)


def _dot_nt(a, b):
    return lax.dot_general(a, b, (((1,), (1,)), ((), ())), preferred_element_type=F32)


def _dot_tn(a, b):
    return lax.dot_general(a, b, (((0,), (0,)), ((), ())), preferred_element_type=F32)


def _split(a):
    hi = a.astype(BF16)
    return hi, (a - hi.astype(F32)).astype(BF16)


def _dot3(a, b):
    ah, al = _split(a)
    bh, bl = _split(b)
    return _dot(ah, bh) + (_dot(ah, bl) + _dot(al, bh))


def _ffn_body(x, g, wg_ref, wu_ref, wd_ref):
    xb = _rms(x, g).astype(BF16)
    acc = jnp.zeros_like(x)
    for c0, c1 in FF_CHUNKS:
        gate = _dot(xb, wg_ref[:, c0:c1])
        up = _dot(xb, wu_ref[:, c0:c1])
        h = (gate * jax.nn.sigmoid(gate) * up).astype(BF16)
        acc = acc + _dot(h, wd_ref[c0:c1, :])
    return x + 0.5 * acc


PA_Q, PA_QS, PA_K, PA_KS, PA_V, PA_CQ, PA_CKV, PA_KR, PA_KRS, PA_END = (
    0, 1024, 2048, 2176, 2304, 2432, 2688, 2816, 2944, 3072)
GQA_KC = 128
MLA_KC = 128
MLA_ROPE_LANE = MLA_NOPE


def _sq_norm(a_bf16):
    a = a_bf16.astype(F32)
    return jnp.sum(a * a, axis=-1, keepdims=True)


def _proj_a_kernel(x_ref, g_ref, w_ref, cosa_ref, sina_ref, cosm_ref, sinm_ref,
                   gq_ref, gqs_ref, gk_ref, gks_ref, gcq_ref, gckv_ref,
                   wuq_ref, wuqs_ref, wukv_ref,
                   qg_ref, kg_ref, vgt_ref, qm_ref, km_ref, vmt_ref, bnd_ref):
    ub = _rms(x_ref[0], g_ref[...]).astype(BF16)
    cosa, sina = cosa_ref[...], sina_ref[...]
    cosm, sinm = cosm_ref[...], sinm_ref[...]

    zq = _dot(ub, w_ref[:, PA_Q:PA_QS])
    zqs = _dot(ub, w_ref[:, PA_QS:PA_K])
    tq_c = gq_ref[...] * cosa
    tq_s = gqs_ref[...] * sina
    scale_a = GQA_HEAD_DIM ** -0.5 * LOG2E
    qa_n2 = None
    for h in range(GQA_HEADS):
        a = zq[:, h * LANES:(h + 1) * LANES]
        a_sw = zqs[:, h * LANES:(h + 1) * LANES]
        r = lax.rsqrt(jnp.sum(a * a, axis=-1, keepdims=True) * (1.0 / GQA_HEAD_DIM) + EPS)
        qb = ((a * tq_c + a_sw * tq_s) * (r * scale_a)).astype(BF16)
        qg_ref[0, :, h * LANES:(h + 1) * LANES] = qb
        qa_n2 = _sq_norm(qb) if qa_n2 is None else jnp.maximum(qa_n2, _sq_norm(qb))

    zk = _dot(ub, w_ref[:, PA_K:PA_KS])
    zks = _dot(ub, w_ref[:, PA_KS:PA_V])
    first = lax.broadcasted_iota(jnp.int32, zk.shape, 1) < GQA_HEAD_DIM
    sq = zk * zk
    r0 = lax.rsqrt(jnp.sum(jnp.where(first, sq, 0.0), axis=-1, keepdims=True) * (1.0 / GQA_HEAD_DIM) + EPS)
    r1 = lax.rsqrt(jnp.sum(jnp.where(first, 0.0, sq), axis=-1, keepdims=True) * (1.0 / GQA_HEAD_DIM) + EPS)
    kb = ((zk * (gk_ref[...] * cosa) + zks * (gks_ref[...] * sina)) * jnp.where(first, r0, r1)).astype(BF16)
    kg_ref[0] = kb
    kb32 = kb.astype(F32)
    kb_sq = kb32 * kb32
    ka_n2 = jnp.maximum(jnp.sum(jnp.where(first, kb_sq, 0.0), axis=-1, keepdims=True),
                        jnp.sum(jnp.where(first, 0.0, kb_sq), axis=-1, keepdims=True))

    vgt_ref[0, 0] = _dot(ub, w_ref[:, PA_V:PA_CQ]).T.astype(BF16)

    scale_m = (MLA_NOPE + MLA_ROPE) ** -0.5 * LOG2E
    cqn = _rms(_dot(ub, w_ref[:, PA_CQ:PA_CKV]), gcq_ref[...]).astype(BF16)
    zq2 = _dot(cqn, wuq_ref[...])
    zq2s = _dot(cqn, wuqs_ref[...])
    qm_n2 = None
    for h in range(MLA_HEADS):
        blk = slice(h * MLA_KC, (h + 1) * MLA_KC)
        qb = ((zq2[:, blk] * cosm + zq2s[:, blk] * sinm) * scale_m).astype(BF16)
        qm_ref[0, :, blk] = qb
        qm_n2 = _sq_norm(qb) if qm_n2 is None else jnp.maximum(qm_n2, _sq_norm(qb))

    ckvn = _rms(_dot(ub, w_ref[:, PA_CKV:PA_KR]), gckv_ref[...]).astype(BF16)
    zkv = _dot(ckvn, wukv_ref[...])
    krope = _dot(ub, w_ref[:, PA_KR:PA_KRS]) * cosm + _dot(ub, w_ref[:, PA_KRS:PA_END]) * sinm
    km_n2 = None
    for h in range(MLA_HEADS):
        blk = slice(h * MLA_KC, (h + 1) * MLA_KC)
        kb = (zkv[:, blk] + krope).astype(BF16)
        km_ref[0, :, blk] = kb
        km_n2 = _sq_norm(kb) if km_n2 is None else jnp.maximum(km_n2, _sq_norm(kb))
    vmt_ref[0, 0] = zkv[:, MLA_HEADS * MLA_KC:].T.astype(BF16)

    row = lax.broadcasted_iota(jnp.int32, (8, LANES), 0)
    tops = [jnp.max(v, axis=0, keepdims=True) for v in (qa_n2, ka_n2, qm_n2, km_n2)]
    bnd_ref[0, 0] = jnp.where(row == 0, tops[0], jnp.where(row == 1, tops[1], jnp.where(row == 2, tops[2], tops[3])))


def _proj_a(x, g_mix, w_a, tabs, gq, gqs, gk, gks, gcq, gckv, wuq, wuqs, wukv):
    b, l, _ = x.shape
    tm = TOK_TILE
    nt = l // tm
    cosa, sina, cosm, sinm = tabs
    tab = pl.BlockSpec((tm, LANES), lambda bi, i: (i, 0))

    def tok(w):
        return pl.BlockSpec((1, tm, w), lambda bi, i: (bi, i, 0))

    def tr(rows):
        return pl.BlockSpec((1, 1, rows, tm), lambda bi, i: (bi, i, 0, 0))

    out_shape = (
        jax.ShapeDtypeStruct((b, l, GQA_HEADS * GQA_KC), BF16),
        jax.ShapeDtypeStruct((b, l, GQA_KC), BF16),
        jax.ShapeDtypeStruct((b, nt, GQA_KV_HEADS * GQA_HEAD_DIM, tm), BF16),
        jax.ShapeDtypeStruct((b, l, MLA_HEADS * MLA_KC), BF16),
        jax.ShapeDtypeStruct((b, l, MLA_HEADS * MLA_KC), BF16),
        jax.ShapeDtypeStruct((b, nt, MLA_HEADS * MLA_V, tm), BF16),
        jax.ShapeDtypeStruct((b, nt, 8, LANES), F32),
    )
    return pl.pallas_call(
        _proj_a_kernel,
        out_shape=out_shape,
        grid=(b, nt),
        in_specs=[tok(D_MODEL), _const_spec((1, D_MODEL)), _const_spec(w_a.shape), tab, tab, tab, tab,
                  _const_spec((1, LANES)), _const_spec((1, LANES)), _const_spec((1, LANES)),
                  _const_spec((1, LANES)), _const_spec((1, MLA_Q_RANK)), _const_spec((1, MLA_KV_RANK)),
                  _const_spec(wuq.shape), _const_spec(wuqs.shape), _const_spec(wukv.shape)],
        out_specs=(tok(GQA_HEADS * GQA_KC), tok(GQA_KC), tr(GQA_KV_HEADS * GQA_HEAD_DIM),
                   tok(MLA_HEADS * MLA_KC), tok(MLA_HEADS * MLA_KC), tr(MLA_HEADS * MLA_V),
                   pl.BlockSpec((1, 1, 8, LANES), lambda bi, i: (bi, i, 0, 0))),
        compiler_params=_cparams(("parallel", "parallel"), 48),
        name="proj_a",
    )(x, g_mix, w_a, cosa, sina, cosm, sinm, gq, gqs, gk, gks, gcq, gckv, wuq, wuqs, wukv)


def _ffn_proj_b_kernel(x_ref, g_ref, wg_ref, wu_ref, wd_ref, gm_ref, w_ref, xo_ref, hy_ref, gt_ref):
    x = _ffn_body(x_ref[...], g_ref[...], wg_ref, wu_ref, wd_ref)
    xo_ref[...] = x
    ub = _rms(x, gm_ref[...]).astype(BF16)
    hy_ref[...] = _dot(ub, w_ref[:, :3 * HY_WIDTH]).astype(BF16)
    gt_ref[...] = jax.nn.sigmoid(_dot(ub, w_ref[:, 3 * HY_WIDTH:])).astype(BF16)


def _ffn_proj_b(x2d, g, wg, wu, wd, g_mix, w_b):
    t = x2d.shape[0]
    tm = TOK_TILE

    def row(w):
        return pl.BlockSpec((tm, w), lambda i: (i, 0))

    return pl.pallas_call(
        _ffn_proj_b_kernel,
        out_shape=(jax.ShapeDtypeStruct((t, D_MODEL), F32),
                   jax.ShapeDtypeStruct((t, 3 * HY_WIDTH), BF16),
                   jax.ShapeDtypeStruct((t, N_BRANCH * D_MODEL), BF16)),
        grid=(t // tm,),
        in_specs=[row(D_MODEL), _const_spec((1, D_MODEL)), _const_spec((D_MODEL, D_FF)),
                  _const_spec((D_MODEL, D_FF)), _const_spec((D_FF, D_MODEL)), _const_spec((1, D_MODEL)),
                  _const_spec(w_b.shape)],
        out_specs=(row(D_MODEL), row(3 * HY_WIDTH), row(N_BRANCH * D_MODEL)),
        compiler_params=_cparams(("parallel",), 58),
        name="ffn_proj_b",
    )(x2d, g, wg, wu, wd, g_mix, w_b)


def _attn_kernel(safe_ref, q_ref, k_ref, vt_ref, o_ref, acc_scr, p0_scr, p1_scr,
                 *, heads, groups, kc, key_of_head, value_of_head, nk, nk_tile):
    tv = vt_ref.shape[3]
    tk = nk_tile
    tq = ATT_TQ
    ones = jnp.ones((ONES_ROWS, tk), BF16)
    dv = MLA_V
    n_sub = q_ref.shape[1] // tq
    safe = safe_ref[pl.program_id(0), pl.program_id(1)] != 0

    def key_tile(j, h):
        kb = key_of_head[h]
        rows = j * tk if isinstance(j, int) else pl.multiple_of(j * tk, tk)
        return k_ref[0, pl.ds(rows, tk), kb * kc:(kb + 1) * kc]

    def values(j, h):
        vb = value_of_head[h]
        per = tk // tv
        return jnp.concatenate([vt_ref[0, j * per + u, vb * dv:(vb + 1) * dv, :] for u in range(per)], axis=1)

    def value_tile(j, h):
        return jnp.concatenate([values(j, h), ones], axis=0)

    def write_out(acc, l, h, sub):
        o_ref[0, h * dv:(h + 1) * dv, sub * tq:(sub + 1) * tq] = (acc / l).astype(BF16)

    @pl.when(safe)
    def _():
        bufs = (p0_scr, p1_scr)
        streams = [(sub, g) for sub in range(n_sub) for g in range(groups)]

        def probs(si, stream, j, hh, buf):
            sub, g = stream
            h = g * heads + hh
            st = _dot_nt(key_tile(j, h), q_ref[0, sub * tq:(sub + 1) * tq, h * kc:(h + 1) * kc])
            buf[hh] = jnp.exp2(st).astype(BF16)

        def accumulate(si, stream, j, hh, buf):
            acc_scr[si % 2, hh] += _dot(value_tile(j, stream[1] * heads + hh), buf[hh])

        acc_scr[...] = jnp.zeros(acc_scr.shape, F32)
        for hh in range(heads):
            probs(0, streams[0], 0, hh, bufs[0])
        for si, stream in enumerate(streams):
            def step(j, cur, nxt, si=si, stream=stream):
                for hh in range(heads):
                    probs(si, stream, j + 1, hh, nxt)
                    accumulate(si, stream, j, hh, cur)

            unroll = min(ATT_UNROLL, nk)

            def body(jj, carry, step=step):
                for u in range(unroll):
                    step(unroll * jj + u, bufs[u % 2], bufs[1 - u % 2])
                return carry

            lax.fori_loop(0, nk // unroll - 1, body, 0)
            for u in range(unroll - 1):
                step(nk - unroll + u, bufs[u % 2], bufs[1 - u % 2])
            for hh in range(heads):
                if si + 1 < len(streams):
                    probs(si + 1, streams[si + 1], 0, hh, bufs[0])
                accumulate(si, stream, nk - 1, hh, bufs[1])
            sub, g = stream
            for hh in range(heads):
                acc = acc_scr[si % 2, hh]
                write_out(acc[:dv], acc[dv:dv + 1], g * heads + hh, sub)
                if si + 2 < len(streams):
                    acc_scr[si % 2, hh] = jnp.zeros((dv + ONES_ROWS, tq), F32)

    @pl.when(jnp.logical_not(safe))
    def _():
        for h in range(groups * heads):
            for sub in range(n_sub):
                qh = q_ref[0, sub * tq:(sub + 1) * tq, h * kc:(h + 1) * kc]

                def body(j, carry, qh=qh, h=h):
                    m, acc = carry
                    st = _dot_nt(key_tile(j, h), qh)
                    m_new = jnp.maximum(m, jnp.max(st, axis=0, keepdims=True))
                    p = jnp.exp2(st - m_new).astype(BF16)
                    return m_new, jnp.exp2(m - m_new) * acc + _dot(value_tile(j, h), p)

                init = (jnp.full((1, tq), NEG_BIG, F32), jnp.zeros((dv + ONES_ROWS, tq), F32))
                _, acc = lax.fori_loop(0, nk, body, init)
                write_out(acc[:dv], acc[dv:dv + 1], h, sub)


def _attention(safe, q, k, vt, *, heads, kc, tk, key_of_head, value_of_head, name):
    b, l, qw = q.shape
    n_heads = qw // kc
    nv, tv = vt.shape[1], vt.shape[3]
    nk = l // tk
    tqb = ATT_TQB
    tq = ATT_TQ
    dv = MLA_V
    grid_spec = pltpu.PrefetchScalarGridSpec(
        num_scalar_prefetch=1,
        grid=(b, l // tqb),
        in_specs=[pl.BlockSpec((1, tqb, qw), lambda bi, i, s: (bi, i, 0)),
                  pl.BlockSpec((1, l, k.shape[2]), lambda bi, i, s: (bi, 0, 0), pipeline_mode=pl.Buffered(1)),
                  pl.BlockSpec((1, nv, vt.shape[2], tv), lambda bi, i, s: (bi, 0, 0, 0),
                               pipeline_mode=pl.Buffered(1))],
        out_specs=pl.BlockSpec((1, n_heads * dv, tqb), lambda bi, i, s: (bi, 0, i)),
        scratch_shapes=[pltpu.VMEM((2, heads, dv + ONES_ROWS, tq), F32),
                        pltpu.VMEM((heads, tk, tq), BF16), pltpu.VMEM((heads, tk, tq), BF16)],
    )
    return pl.pallas_call(
        functools.partial(_attn_kernel, heads=heads, groups=n_heads // heads, kc=kc, key_of_head=key_of_head,
                          value_of_head=value_of_head, nk=nk, nk_tile=tk),
        out_shape=jax.ShapeDtypeStruct((b, n_heads * dv, l), BF16),
        grid_spec=grid_spec,
        compiler_params=_cparams(("parallel", "parallel"), 48),
        name=name,
    )(safe, q, k, vt)


def _hy_pre_kernel(x0_ref, x1_ref, v_ref, w_ref, b_ref, s_ref, x0o_ref):
    rows = x0_ref.shape[1]
    t = lax.broadcasted_iota(jnp.int32, (rows, LANES), 0)
    not_first = t > 0
    not_last = t < rows - 1

    def conv(ref, j):
        a = ref[0].astype(F32)
        prev = jnp.where(not_first, pltpu.roll(a, 1, 0), 0.0)
        nxt = jnp.where(not_last, pltpu.roll(a, rows - 1, 0), 0.0)
        return prev * w_ref[0, j] + a * w_ref[1, j] + nxt * w_ref[2, j] + b_ref[j]

    x0o_ref[0] = conv(x0_ref, 0).astype(BF16)
    s_ref[0] = (conv(x1_ref, 1) * conv(v_ref, 2)).astype(BF16)


def _hy_pre(hy, w_short, b_short):
    b, l, _ = hy.shape
    nc = HY_WIDTH // LANES
    w4 = w_short.reshape(3, 3, nc, 1, LANES)
    b4 = b_short.reshape(3, nc, 1, LANES)

    def seg(j):
        return pl.BlockSpec((1, l, LANES), lambda bi, c, j=j: (bi, 0, j * nc + c))

    out = pl.BlockSpec((1, l, LANES), lambda bi, c: (bi, 0, c))
    return pl.pallas_call(
        _hy_pre_kernel,
        out_shape=(jax.ShapeDtypeStruct((b, l, HY_WIDTH), BF16), jax.ShapeDtypeStruct((b, l, HY_WIDTH), BF16)),
        grid=(b, nc),
        in_specs=[seg(0), seg(1), seg(2),
                  pl.BlockSpec((3, 3, None, 1, LANES), lambda bi, c: (0, 0, c, 0, 0)),
                  pl.BlockSpec((3, None, 1, LANES), lambda bi, c: (0, c, 0, 0))],
        out_specs=(out, out),
        compiler_params=_cparams(("parallel", "parallel"), 48),
        name="hy_pre",
    )(hy, hy, hy, w4, b4)


def _left_mm_kernel(f_ref, x_ref, o_ref, *, hi):
    if hi:
        o_ref[0] = _dot3(f_ref[...], x_ref[0]).astype(o_ref.dtype)
    else:
        o_ref[0] = _dot(f_ref[...], x_ref[0]).astype(o_ref.dtype)


def _left_mm(f, x, out_dtype, hi=False):
    b, k, n = x.shape
    m = f.shape[0]
    cb = min(HY_COLS, n)
    return pl.pallas_call(
        functools.partial(_left_mm_kernel, hi=hi),
        out_shape=jax.ShapeDtypeStruct((b, m, n), out_dtype),
        grid=(b, n // cb),
        in_specs=[_const_spec((m, k)), pl.BlockSpec((1, k, cb), lambda bi, c: (bi, 0, c))],
        out_specs=pl.BlockSpec((1, m, cb), lambda bi, c: (bi, 0, c)),
        compiler_params=_cparams(("parallel", "parallel"), 48),
        name="hy_outer_dft",
    )(f, x)


def _hy_mid_kernel(f_ref, kf_ref, g_ref, a_ref, o_ref):
    n2 = DFT_N2
    kr, ki = kf_ref[0, :n2], kf_ref[0, n2:]
    for bi in range(a_ref.shape[0]):
        x = _dot(f_ref[0], a_ref[bi, 0])
        xr, xi = x[:n2], x[n2:]
        y = jnp.concatenate([xr * kr - xi * ki, xr * ki + xi * kr], axis=0).astype(BF16)
        o_ref[bi, 0] = _dot(g_ref[0], y).astype(BF16)


def _hy_mid(a4, f2, kf, g2):
    b, n1, r, c = a4.shape
    mat = pl.BlockSpec((1, r, r), lambda i: (i, 0, 0))
    dat = pl.BlockSpec((b, 1, r, c), lambda i: (0, i, 0, 0))
    return pl.pallas_call(
        _hy_mid_kernel,
        out_shape=jax.ShapeDtypeStruct(a4.shape, BF16),
        grid=(n1,),
        in_specs=[mat, pl.BlockSpec((1, r, c), lambda i: (i, 0, 0)), mat, dat],
        out_specs=dat,
        compiler_params=_cparams(("parallel",), 48),
        name="hy_mid",
    )(f2, kf, g2, a4)


def _hy_post_kernel(g_ref, bi_ref, s_ref, x0_ref, bias_ref, o_ref):
    y = _dot(g_ref[...], bi_ref[0]) + s_ref[0].astype(F32) * bias_ref[...]
    o_ref[0] = (x0_ref[0].astype(F32) * y).astype(BF16)


def _hy_post(g1, bi2, s2, x02, bias_t):
    b, k, n = bi2.shape
    m = g1.shape[0]
    cb = HY_COLS
    dat = pl.BlockSpec((1, m, cb), lambda bi, c: (bi, 0, c))
    return pl.pallas_call(
        _hy_post_kernel,
        out_shape=jax.ShapeDtypeStruct((b, m, n), BF16),
        grid=(b, n // cb),
        in_specs=[_const_spec((m, k)), pl.BlockSpec((1, k, cb), lambda bi, c: (bi, 0, c)), dat, dat,
                  _const_spec((1, cb))],
        out_specs=dat,
        compiler_params=_cparams(("parallel", "parallel"), 48),
        name="hy_post",
    )(g1, bi2, s2, x02, bias_t)


def _hy_filter_kernel(z_ref, win_ref, w1_ref, b1_ref, w2_ref, b2_ref, w3_ref, fr_ref, o_ref, *, half_tiles):
    fr = fr_ref[...]
    h = jnp.sin(fr * (_dot3(z_ref[...], w1_ref[...]) + b1_ref[...]))
    h = jnp.sin(fr * (_dot3(h, w2_ref[...]) + b2_ref[...]))
    h3 = _dot3(h, w3_ref[...])
    backward = pl.program_id(0) >= half_tiles
    o_ref[...] = jnp.where(backward, h3[:, HY_WIDTH:], h3[:, :HY_WIDTH]) * win_ref[...]


def _hy_filter(z2, win2, w1p, b1, w2, b2, w3, fr):
    n = z2.shape[0]
    tl = 512
    return pl.pallas_call(
        functools.partial(_hy_filter_kernel, half_tiles=n // (2 * tl)),
        out_shape=jax.ShapeDtypeStruct((n, HY_WIDTH), F32),
        grid=(n // tl,),
        in_specs=[pl.BlockSpec((tl, LANES), lambda i: (i, 0)), pl.BlockSpec((tl, HY_WIDTH), lambda i: (i, 0)),
                  _const_spec(w1p.shape), _const_spec(b1.shape), _const_spec(w2.shape), _const_spec(b2.shape),
                  _const_spec(w3.shape), _const_spec(fr.shape)],
        out_specs=pl.BlockSpec((tl, HY_WIDTH), lambda i: (i, 0)),
        compiler_params=_cparams(("parallel",), 32),
        name="hy_filter",
    )(z2, win2, w1p, b1, w2, b2, w3, fr)


def _hy_spec_kernel(f_ref, a_ref, o_ref):
    o_ref[0] = _dot3(f_ref[0], a_ref[0])


def _hy_spec(f2, a3):
    n1, r, c = a3.shape
    return pl.pallas_call(
        _hy_spec_kernel,
        out_shape=jax.ShapeDtypeStruct(a3.shape, F32),
        grid=(n1,),
        in_specs=[pl.BlockSpec((1, r, r), lambda i: (i, 0, 0)), pl.BlockSpec((1, r, c), lambda i: (i, 0, 0))],
        out_specs=pl.BlockSpec((1, r, c), lambda i: (i, 0, 0)),
        compiler_params=_cparams(("parallel",), 32),
        name="hy_filter_spectrum",
    )(f2, a3)


def _merge_kernel(x_ref, yat_ref, yb_ref, yct_ref, gt_ref, wb_ref, wo_ref, g_ref, wg_ref, wu_ref, wd_ref, gf_ref,
                  o_ref, *, final):
    d = D_MODEL
    ba = _dot_tn(yat_ref[0], wb_ref[0])
    bb = _dot(yb_ref[0], wb_ref[1])
    bc = _dot_tn(yct_ref[0], wb_ref[2])
    merged = (gt_ref[0, :, 0:d].astype(F32) * ba + gt_ref[0, :, d:2 * d].astype(F32) * bb
              + gt_ref[0, :, 2 * d:3 * d].astype(F32) * bc)
    x = x_ref[0] + _dot(merged.astype(BF16), wo_ref[...])
    y = _ffn_body(x, g_ref[...], wg_ref, wu_ref, wd_ref)
    if final:
        y = _rms(y, gf_ref[...])
    o_ref[0] = y


def _merge_ffn(x, yat, yb, yct, gates, wb, wo, g, wg, wu, wd, g_final, final):
    b, l, _ = x.shape
    tm = TOK_TILE

    def tok(w):
        return pl.BlockSpec((1, tm, w), lambda bi, i: (bi, i, 0))

    tr = pl.BlockSpec((1, BRANCH_W, tm), lambda bi, i: (bi, 0, i))
    return pl.pallas_call(
        functools.partial(_merge_kernel, final=final),
        out_shape=jax.ShapeDtypeStruct(x.shape, F32),
        grid=(b, l // tm),
        in_specs=[tok(D_MODEL), tr, tok(BRANCH_W), tr, tok(N_BRANCH * D_MODEL),
                  _const_spec(wb.shape), _const_spec(wo.shape), _const_spec((1, D_MODEL)),
                  _const_spec((D_MODEL, D_FF)), _const_spec((D_MODEL, D_FF)), _const_spec((D_FF, D_MODEL)),
                  _const_spec((1, D_MODEL))],
        out_specs=tok(D_MODEL),
        compiler_params=_cparams(("parallel", "parallel"), 58),
        name="merge_ffn",
    )(x, yat, yb, yct, gates, wb, wo, g, wg, wu, wd, g_final)


def _rope_tables(l):
    rows = l // GRID_W
    row = jnp.repeat(jnp.arange(rows, dtype=F32), GRID_W)
    col = jnp.tile(jnp.arange(GRID_W, dtype=F32), rows)

    def tab(d_rot):
        n_freq = d_rot // 4
        inv = ROPE_THETA ** (-jnp.arange(n_freq, dtype=F32) / n_freq)
        ang = jnp.concatenate([row[:, None] * inv, col[:, None] * inv], axis=-1)
        c = jnp.repeat(jnp.cos(ang), 2, axis=-1)
        s = jnp.repeat(jnp.sin(ang), 2, axis=-1) * jnp.tile(jnp.array([-1.0, 1.0], F32), d_rot // 2)
        return c, s

    ca, sa = tab(GQA_HEAD_DIM)
    cm, sm = tab(MLA_ROPE)
    tail = jnp.zeros((l, LANES - MLA_ROPE_LANE - MLA_ROPE), F32)
    cm = jnp.concatenate([jnp.ones((l, MLA_ROPE_LANE), F32), cm, tail], axis=1)
    sm = jnp.concatenate([jnp.zeros((l, MLA_ROPE_LANE), F32), sm, tail], axis=1)
    return (jnp.tile(ca, (1, 2)), jnp.tile(sa, (1, 2)), cm, sm)


def _hy_positions(l):
    r = jnp.arange(l, dtype=jnp.int32)
    pos = jnp.concatenate([r, (l - r) % l]).astype(F32)[:, None]
    live = jnp.concatenate([jnp.ones((l,), F32), (r > 0).astype(F32)])[:, None]
    t = pos * (1.0 / (l - 1))
    w = 2.0 * math.pi * pos / l
    f = jnp.linspace(1e-4, HY_BANDS - 1, HY_BANDS, dtype=F32)[None, :]
    z2 = jnp.concatenate([t, jnp.cos(f * w), -jnp.sin(f * w), jnp.zeros((2 * l, LANES - HY_EMB), F32)], axis=-1)
    max_decay = math.log(HY_TARGET) / HY_FAST
    min_decay = math.log(HY_TARGET) / HY_SLOW
    deltas = jnp.abs(jnp.linspace(min_decay, max_decay, HY_WIDTH, dtype=F32))
    return z2, jnp.exp(-t * deltas[None, :]) * live


def _hy_k1_count(l):
    n1 = 2 * l // DFT_N2
    return -(-(n1 // 2 + 1) // 8) * 8


def _dft_tables(l):
    n = 2 * l
    n2 = DFT_N2
    n1 = n // n2
    n1c = _hy_k1_count(l)
    two_pi = 2.0 * math.pi
    i1 = jnp.arange(n1, dtype=jnp.int32)
    ang1 = ((i1[:n1c, None] * i1[None, :]) % n1).astype(F32) * (two_pi / n1)
    live = (i1[:n1c] <= n1 // 2)[:, None]
    c1, s1 = jnp.where(live, jnp.cos(ang1), 0.0), jnp.where(live, jnp.sin(ang1), 0.0)
    f1_full = jnp.stack([c1, -s1], axis=1).reshape(2 * n1c, n1)
    fold = jnp.where((i1[:n1c] == 0) | (i1[:n1c] == n1 // 2), 1.0, 2.0)[:, None] * (1.0 / n)
    g1 = jnp.stack([(c1 * fold)[:, :n1 // 2].T, (-s1 * fold)[:, :n1 // 2].T], axis=2).reshape(n1 // 2, 2 * n1c)
    i2 = jnp.arange(n2, dtype=jnp.int32)
    kk = i1[:n1c, None, None] + n1 * i2[None, :, None]
    ang2 = ((kk * i2[None, None, :]) % n).astype(F32) * (two_pi / n)
    c2, s2 = jnp.cos(ang2), jnp.sin(ang2)
    f2 = jnp.concatenate([jnp.concatenate([c2, s2], axis=2), jnp.concatenate([-s2, c2], axis=2)], axis=1)
    c2t, s2t = jnp.swapaxes(c2, 1, 2), jnp.swapaxes(s2, 1, 2)
    g2 = jnp.concatenate([jnp.concatenate([c2t, -s2t], axis=2), jnp.concatenate([s2t, c2t], axis=2)], axis=1)
    return f1_full, g1, f2, g2


def _swap_pairs(n):
    return np.arange(n) ^ 1


def _proj_a_columns():
    zero = IN_WIDTH
    cols = np.full((PA_END,), zero, np.int64)
    q0, k0, v0, _, cq0, ckv0, kr0, _ = IN_OFFS[:8]
    for h in range(GQA_HEADS):
        dst = h * LANES + (h // (GQA_HEADS // GQA_KV_HEADS)) * GQA_HEAD_DIM
        src = q0 + h * GQA_HEAD_DIM + np.arange(GQA_HEAD_DIM)
        cols[PA_Q + dst:PA_Q + dst + GQA_HEAD_DIM] = src
        cols[PA_QS + dst:PA_QS + dst + GQA_HEAD_DIM] = q0 + h * GQA_HEAD_DIM + _swap_pairs(GQA_HEAD_DIM)
    cols[PA_K:PA_K + 128] = k0 + np.arange(128)
    cols[PA_KS:PA_KS + 128] = k0 + _swap_pairs(128)
    cols[PA_V:PA_V + 128] = v0 + np.arange(128)
    cols[PA_CQ:PA_CQ + MLA_Q_RANK] = cq0 + np.arange(MLA_Q_RANK)
    cols[PA_CKV:PA_CKV + MLA_KV_RANK] = ckv0 + np.arange(MLA_KV_RANK)
    r0 = MLA_ROPE_LANE
    cols[PA_KR + r0:PA_KR + r0 + MLA_ROPE] = kr0 + np.arange(MLA_ROPE)
    cols[PA_KRS + r0:PA_KRS + r0 + MLA_ROPE] = kr0 + _swap_pairs(MLA_ROPE)
    return cols


def _mla_q_columns():
    hd = MLA_NOPE + MLA_ROPE
    zero = MLA_HEADS * hd
    main = np.full((MLA_HEADS * MLA_KC,), zero, np.int64)
    swap = np.full((MLA_HEADS * MLA_KC,), zero, np.int64)
    for h in range(MLA_HEADS):
        b0 = h * MLA_KC
        main[b0:b0 + MLA_NOPE] = h * hd + np.arange(MLA_NOPE)
        r0 = b0 + MLA_ROPE_LANE
        main[r0:r0 + MLA_ROPE] = h * hd + MLA_NOPE + np.arange(MLA_ROPE)
        swap[r0:r0 + MLA_ROPE] = h * hd + MLA_NOPE + _swap_pairs(MLA_ROPE)
    return main, swap


def _mla_kv_columns():
    hd = MLA_NOPE + MLA_V
    zero = MLA_HEADS * hd
    knope = np.full((MLA_HEADS * MLA_KC,), zero, np.int64)
    for h in range(MLA_HEADS):
        knope[h * MLA_KC:h * MLA_KC + MLA_NOPE] = h * hd + np.arange(MLA_NOPE)
    val = np.concatenate([h * hd + MLA_NOPE + np.arange(MLA_V) for h in range(MLA_HEADS)])
    return np.concatenate([knope, val])


def _take_cols(w, cols):
    wz = jnp.concatenate([w, jnp.zeros((w.shape[0], 1), w.dtype)], axis=1)
    return jnp.take(wz, jnp.asarray(cols, jnp.int32), axis=1).astype(BF16)


def _tile2(g):
    return jnp.tile(g, 2)[None, :]


def _encoder(x, lw, g_final):
    b, l, d = x.shape
    n2 = DFT_N2
    n1 = 2 * l // n2
    c = HY_WIDTH
    tabs = _rope_tables(l)
    z2, win2 = _hy_positions(l)
    f1_full, g1, f2, g2 = _dft_tables(l)
    f1_b = f1_full[:, :n1 // 2].astype(BF16)
    g1_b = g1.astype(BF16)
    f2_b = f2.astype(BF16)
    g2_b = g2.astype(BF16)
    for li, w in enumerate(lw):
        x2, hy, gates = _ffn_proj_b(x.reshape(b * l, d), w["g_ffn1"], w["wg1"], w["wu1"], w["wd1"], w["g_mix"],
                                    w["w_b"])
        x = x2.reshape(b, l, d)
        qg, kg, vgt, qm, km, vmt, bnd = _proj_a(x, w["g_mix"], w["w_a"], tabs, w["gq"], w["gqs"], w["gk"],
                                                w["gks"], w["gcq"], w["gckv"], w["wuq"], w["wuqs"], w["wukv"])
        top = bnd[:, :, :4, 0]
        limit = ATT_SAFE_SCORE ** 2
        safe_a = (top[:, :, 0] * jnp.max(top[:, :, 1], axis=1, keepdims=True) <= limit).astype(jnp.int32)
        safe_m = (top[:, :, 2] * jnp.max(top[:, :, 3], axis=1, keepdims=True) <= limit).astype(jnp.int32)
        per_kv = GQA_HEADS // GQA_KV_HEADS
        yat = _attention(safe_a, qg, kg, vgt, heads=ATT_HEADS, kc=GQA_KC, tk=ATT_TK_GQA,
                         key_of_head=(0,) * GQA_HEADS,
                         value_of_head=tuple(h // per_kv for h in range(GQA_HEADS)), name="attn_gqa")
        yct = _attention(safe_m, qm, km, vmt, heads=ATT_HEADS, kc=MLA_KC, tk=ATT_TK_MLA,
                         key_of_head=tuple(range(MLA_HEADS)),
                         value_of_head=tuple(range(MLA_HEADS)), name="attn_mla")
        kc_time = _hy_filter(z2, win2, w["w1p"], w["b1"], w["w2"], w["b2"], w["w3"], w["fr"])
        n1c = _hy_k1_count(l)
        ka =_left_mm(f1_full, kc_time.reshape(1, n1, n2 * c), F32, hi=True)
        kf = _hy_spec(f2, ka.reshape(n1c, 2 * n2, c))
        s, x0 = _hy_pre(hy.reshape(b, l, 3 * c), w["w_short"], w["b_short"])
        a = _left_mm(f1_b, s.reshape(b, n1 // 2, n2 * c), BF16)
        bi = _hy_mid(a.reshape(b, n1c, 2 * n2, c), f2_b, kf, g2_b)
        yb = _hy_post(g1_b, bi.reshape(b, 2 * n1c, n2 * c), s.reshape(b, n1 // 2, n2 * c),
                      x0.reshape(b, n1 // 2, n2 * c), w["bias_t"])
        x = _merge_ffn(x, yat, yb.reshape(b, l, c), yct, gates.reshape(b, l, N_BRANCH * d), w["wb"], w["wo"],
                       w["g_ffn2"], w["wg2"], w["wu2"], w["wd2"], g_final, li == len(lw) - 1)
    return x


def kernel(x_prompt, x_sample, g_ffn1, w_ffn1_gate, w_ffn1_up, w_ffn1_down, g_mix, w_in, g_qnorm, g_knorm,
           w_hy_short, b_hy_short, w_hy_f1, b_hy_f1, w_hy_f2, b_hy_f2, w_hy_f3, hy_sin_freq, hy_bias,
           g_mla_q, w_mla_uq, g_mla_kv, w_mla_ukv, w_branch, w_out, g_ffn2, w_ffn2_gate, w_ffn2_up,
           w_ffn2_down, g_final):
    cols_a = _proj_a_columns()
    uq_main, uq_swap = _mla_q_columns()
    ukv_cols = _mla_kv_columns()
    sw64 = _swap_pairs(GQA_HEAD_DIM)
    hy0 = IN_OFFS[3]
    gt0 = IN_OFFS[7]
    lw = []
    for l in range(DEPTH):
        lw.append(dict(
            g_ffn1=g_ffn1[l][None], wg1=w_ffn1_gate[l].astype(BF16), wu1=w_ffn1_up[l].astype(BF16),
            wd1=w_ffn1_down[l].astype(BF16),
            g_ffn2=g_ffn2[l][None], wg2=w_ffn2_gate[l].astype(BF16), wu2=w_ffn2_up[l].astype(BF16),
            wd2=w_ffn2_down[l].astype(BF16),
            g_mix=g_mix[l][None],
            w_a=_take_cols(w_in[l], cols_a),
            w_b=jnp.concatenate([w_in[l][:, hy0:hy0 + 3 * HY_WIDTH], w_in[l][:, gt0:]], axis=1).astype(BF16),
            gq=_tile2(g_qnorm[l]), gqs=_tile2(g_qnorm[l][sw64]),
            gk=_tile2(g_knorm[l]), gks=_tile2(g_knorm[l][sw64]),
            gcq=g_mla_q[l][None], gckv=g_mla_kv[l][None],
            wuq=_take_cols(w_mla_uq[l], uq_main), wuqs=_take_cols(w_mla_uq[l], uq_swap),
            wukv=_take_cols(w_mla_ukv[l], ukv_cols),
            w_short=w_hy_short[l], b_short=b_hy_short[l],
            w1p=jnp.pad(w_hy_f1[l], ((0, LANES - HY_EMB), (0, 0))), b1=b_hy_f1[l][None],
            w2=w_hy_f2[l], b2=b_hy_f2[l][None], w3=w_hy_f3[l], fr=hy_sin_freq[l][None],
            bias_t=jnp.tile(hy_bias[l], HY_COLS // HY_WIDTH)[None],
            wb=w_branch[l].astype(BF16), wo=w_out[l].astype(BF16),
        ))
    gf = g_final[None]
    return _encoder(x_prompt, lw, gf), _encoder(x_sample, lw, gf)
```

```python
import functools
import math

import numpy as np
import jax
import jax.numpy as jnp
from jax import lax
from jax.experimental import pallas as pl
from jax.experimental.pallas import tpu as pltpu

F32 = jnp.float32
BF16 = jnp.bfloat16

D_MODEL = 1024
DEPTH = 2
GRID_W = 64
ROPE_THETA = 10000.0
EPS = 1e-6
D_FF = 2816
N_BRANCH = 3
BRANCH_W = 512
GQA_HEADS = 8
GQA_KV_HEADS = 2
GQA_HEAD_DIM = 64
HY_WIDTH = 512
HY_ORDER = 64
HY_EMB = 33
HY_BANDS = (HY_EMB - 1) // 2
HY_TARGET = 1e-2
HY_FAST = 0.3
HY_SLOW = 1.5
MLA_HEADS = 8
MLA_Q_RANK = 256
MLA_KV_RANK = 128
MLA_NOPE = 64
MLA_ROPE = 32
MLA_V = 64
IN_WIDTHS = (512, 128, 128, 3 * HY_WIDTH, MLA_Q_RANK, MLA_KV_RANK, MLA_ROPE, N_BRANCH * D_MODEL)
IN_OFFS = tuple(int(c) for c in np.cumsum((0,) + IN_WIDTHS))
IN_WIDTH = IN_OFFS[-1]

LANES = 128
V7X_VMEM_BYTES = 64 * 1024 * 1024

TOK_TILE = 512
ATT_TQ = 256
ATT_TQB = 512
ATT_HEADS = 4
ATT_UNROLL = 4
ATT_SAFE_SCORE = 64.0
ATT_TK_GQA = 2 * TOK_TILE
ATT_TK_MLA = 2 * TOK_TILE
FF_CHUNKS = ((0, 1024), (1024, 2048), (2048, D_FF))
DFT_N2 = 128
HY_COLS = 8192
ONES_ROWS = 16
NEG_BIG = -1e30
LOG2E = math.log2(math.e)


def _cparams(sem, vmem_mb):
    return pltpu.CompilerParams(dimension_semantics=sem, vmem_limit_bytes=vmem_mb * 1024 * 1024)


def _const_spec(shape):
    nd = len(shape)
    return pl.BlockSpec(shape, lambda *_: (0,) * nd, pipeline_mode=pl.Buffered(1))


def _rms(x, g):
    return x * lax.rsqrt(jnp.mean(x * x, axis=-1, keepdims=True) + EPS) * g


def _dot(a, b):
    return jnp.dot(a, b, preferred_element_type=F32)


def _dot_nt(a, b):
    return lax.dot_general(a, b, (((1,), (1,)), ((), ())), preferred_element_type=F32)


def _dot_tn(a, b):
    return lax.dot_general(a, b, (((0,), (0,)), ((), ())), preferred_element_type=F32)


def _split(a):
    hi = a.astype(BF16)
    return hi, (a - hi.astype(F32)).astype(BF16)


def _dot3(a, b):
    ah, al = _split(a)
    bh, bl = _split(b)
    return _dot(ah, bh) + (_dot(ah, bl) + _dot(al, bh))


def _ffn_body(x, g, wg_ref, wu_ref, wd_ref):
    xb = _rms(x, g).astype(BF16)
    acc = jnp.zeros_like(x)
    for c0, c1 in FF_CHUNKS:
        gate = _dot(xb, wg_ref[:, c0:c1])
        up = _dot(xb, wu_ref[:, c0:c1])
        h = (gate * jax.nn.sigmoid(gate) * up).astype(BF16)
        acc = acc + _dot(h, wd_ref[c0:c1, :])
    return x + 0.5 * acc


PA_Q, PA_QS, PA_K, PA_KS, PA_V, PA_CQ, PA_CKV, PA_KR, PA_KRS, PA_END = (
    0, 1024, 2048, 2176, 2304, 2432, 2688, 2816, 2944, 3072)
GQA_KC = 128
MLA_KC = 128
MLA_ROPE_LANE = MLA_NOPE


def _sq_norm(a_bf16):
    a = a_bf16.astype(F32)
    return jnp.sum(a * a, axis=-1, keepdims=True)


def _proj_a_kernel(x_ref, g_ref, w_ref, cosa_ref, sina_ref, cosm_ref, sinm_ref,
                   gq_ref, gqs_ref, gk_ref, gks_ref, gcq_ref, gckv_ref,
                   wuq_ref, wuqs_ref, wukv_ref,
                   qg_ref, kg_ref, vgt_ref, qm_ref, km_ref, vmt_ref, bnd_ref):
    ub = _rms(x_ref[0], g_ref[...]).astype(BF16)
    cosa, sina = cosa_ref[...], sina_ref[...]
    cosm, sinm = cosm_ref[...], sinm_ref[...]

    zq = _dot(ub, w_ref[:, PA_Q:PA_QS])
    zqs = _dot(ub, w_ref[:, PA_QS:PA_K])
    tq_c = gq_ref[...] * cosa
    tq_s = gqs_ref[...] * sina
    scale_a = GQA_HEAD_DIM ** -0.5 * LOG2E
    qa_n2 = None
    for h in range(GQA_HEADS):
        a = zq[:, h * LANES:(h + 1) * LANES]
        a_sw = zqs[:, h * LANES:(h + 1) * LANES]
        r = lax.rsqrt(jnp.sum(a * a, axis=-1, keepdims=True) * (1.0 / GQA_HEAD_DIM) + EPS)
        qb = ((a * tq_c + a_sw * tq_s) * (r * scale_a)).astype(BF16)
        qg_ref[0, :, h * LANES:(h + 1) * LANES] = qb
        qa_n2 = _sq_norm(qb) if qa_n2 is None else jnp.maximum(qa_n2, _sq_norm(qb))

    zk = _dot(ub, w_ref[:, PA_K:PA_KS])
    zks = _dot(ub, w_ref[:, PA_KS:PA_V])
    first = lax.broadcasted_iota(jnp.int32, zk.shape, 1) < GQA_HEAD_DIM
    sq = zk * zk
    r0 = lax.rsqrt(jnp.sum(jnp.where(first, sq, 0.0), axis=-1, keepdims=True) * (1.0 / GQA_HEAD_DIM) + EPS)
    r1 = lax.rsqrt(jnp.sum(jnp.where(first, 0.0, sq), axis=-1, keepdims=True) * (1.0 / GQA_HEAD_DIM) + EPS)
    kb = ((zk * (gk_ref[...] * cosa) + zks * (gks_ref[...] * sina)) * jnp.where(first, r0, r1)).astype(BF16)
    kg_ref[0] = kb
    kb32 = kb.astype(F32)
    kb_sq = kb32 * kb32
    ka_n2 = jnp.maximum(jnp.sum(jnp.where(first, kb_sq, 0.0), axis=-1, keepdims=True),
                        jnp.sum(jnp.where(first, 0.0, kb_sq), axis=-1, keepdims=True))

    vgt_ref[0, 0] = _dot(ub, w_ref[:, PA_V:PA_CQ]).T.astype(BF16)

    scale_m = (MLA_NOPE + MLA_ROPE) ** -0.5 * LOG2E
    cqn = _rms(_dot(ub, w_ref[:, PA_CQ:PA_CKV]), gcq_ref[...]).astype(BF16)
    zq2 = _dot(cqn, wuq_ref[...])
    zq2s = _dot(cqn, wuqs_ref[...])
    qm_n2 = None
    for h in range(MLA_HEADS):
        blk = slice(h * MLA_KC, (h + 1) * MLA_KC)
        qb = ((zq2[:, blk] * cosm + zq2s[:, blk] * sinm) * scale_m).astype(BF16)
        qm_ref[0, :, blk] = qb
        qm_n2 = _sq_norm(qb) if qm_n2 is None else jnp.maximum(qm_n2, _sq_norm(qb))

    ckvn = _rms(_dot(ub, w_ref[:, PA_CKV:PA_KR]), gckv_ref[...]).astype(BF16)
    zkv = _dot(ckvn, wukv_ref[...])
    krope = _dot(ub, w_ref[:, PA_KR:PA_KRS]) * cosm + _dot(ub, w_ref[:, PA_KRS:PA_END]) * sinm
    km_n2 = None
    for h in range(MLA_HEADS):
        blk = slice(h * MLA_KC, (h + 1) * MLA_KC)
        kb = (zkv[:, blk] + krope).astype(BF16)
        km_ref[0, :, blk] = kb
        km_n2 = _sq_norm(kb) if km_n2 is None else jnp.maximum(km_n2, _sq_norm(kb))
    vmt_ref[0, 0] = zkv[:, MLA_HEADS * MLA_KC:].T.astype(BF16)

    row = lax.broadcasted_iota(jnp.int32, (8, LANES), 0)
    tops = [jnp.max(v, axis=0, keepdims=True) for v in (qa_n2, ka_n2, qm_n2, km_n2)]
    bnd_ref[0, 0] = jnp.where(row == 0, tops[0], jnp.where(row == 1, tops[1], jnp.where(row == 2, tops[2], tops[3])))


def _proj_a(x, g_mix, w_a, tabs, gq, gqs, gk, gks, gcq, gckv, wuq, wuqs, wukv):
    b, l, _ = x.shape
    tm = TOK_TILE
    nt = l // tm
    cosa, sina, cosm, sinm = tabs
    tab = pl.BlockSpec((tm, LANES), lambda bi, i: (i, 0))

    def tok(w):
        return pl.BlockSpec((1, tm, w), lambda bi, i: (bi, i, 0))

    def tr(rows):
        return pl.BlockSpec((1, 1, rows, tm), lambda bi, i: (bi, i, 0, 0))

    out_shape = (
        jax.ShapeDtypeStruct((b, l, GQA_HEADS * GQA_KC), BF16),
        jax.ShapeDtypeStruct((b, l, GQA_KC), BF16),
        jax.ShapeDtypeStruct((b, nt, GQA_KV_HEADS * GQA_HEAD_DIM, tm), BF16),
        jax.ShapeDtypeStruct((b, l, MLA_HEADS * MLA_KC), BF16),
        jax.ShapeDtypeStruct((b, l, MLA_HEADS * MLA_KC), BF16),
        jax.ShapeDtypeStruct((b, nt, MLA_HEADS * MLA_V, tm), BF16),
        jax.ShapeDtypeStruct((b, nt, 8, LANES), F32),
    )
    return pl.pallas_call(
        _proj_a_kernel,
        out_shape=out_shape,
        grid=(b, nt),
        in_specs=[tok(D_MODEL), _const_spec((1, D_MODEL)), _const_spec(w_a.shape), tab, tab, tab, tab,
                  _const_spec((1, LANES)), _const_spec((1, LANES)), _const_spec((1, LANES)),
                  _const_spec((1, LANES)), _const_spec((1, MLA_Q_RANK)), _const_spec((1, MLA_KV_RANK)),
                  _const_spec(wuq.shape), _const_spec(wuqs.shape), _const_spec(wukv.shape)],
        out_specs=(tok(GQA_HEADS * GQA_KC), tok(GQA_KC), tr(GQA_KV_HEADS * GQA_HEAD_DIM),
                   tok(MLA_HEADS * MLA_KC), tok(MLA_HEADS * MLA_KC), tr(MLA_HEADS * MLA_V),
                   pl.BlockSpec((1, 1, 8, LANES), lambda bi, i: (bi, i, 0, 0))),
        compiler_params=_cparams(("parallel", "parallel"), 48),
        name="proj_a",
    )(x, g_mix, w_a, cosa, sina, cosm, sinm, gq, gqs, gk, gks, gcq, gckv, wuq, wuqs, wukv)


def _ffn_proj_b_kernel(x_ref, g_ref, wg_ref, wu_ref, wd_ref, gm_ref, w_ref, xo_ref, hy_ref, gt_ref):
    x = _ffn_body(x_ref[...], g_ref[...], wg_ref, wu_ref, wd_ref)
    xo_ref[...] = x
    ub = _rms(x, gm_ref[...]).astype(BF16)
    hy_ref[...] = _dot(ub, w_ref[:, :3 * HY_WIDTH]).astype(BF16)
    gt_ref[...] = jax.nn.sigmoid(_dot(ub, w_ref[:, 3 * HY_WIDTH:])).astype(BF16)


def _ffn_proj_b(x2d, g, wg, wu, wd, g_mix, w_b):
    t = x2d.shape[0]
    tm = TOK_TILE

    def row(w):
        return pl.BlockSpec((tm, w), lambda i: (i, 0))

    return pl.pallas_call(
        _ffn_proj_b_kernel,
        out_shape=(jax.ShapeDtypeStruct((t, D_MODEL), F32),
                   jax.ShapeDtypeStruct((t, 3 * HY_WIDTH), BF16),
                   jax.ShapeDtypeStruct((t, N_BRANCH * D_MODEL), BF16)),
        grid=(t // tm,),
        in_specs=[row(D_MODEL), _const_spec((1, D_MODEL)), _const_spec((D_MODEL, D_FF)),
                  _const_spec((D_MODEL, D_FF)), _const_spec((D_FF, D_MODEL)), _const_spec((1, D_MODEL)),
                  _const_spec(w_b.shape)],
        out_specs=(row(D_MODEL), row(3 * HY_WIDTH), row(N_BRANCH * D_MODEL)),
        compiler_params=_cparams(("parallel",), 58),
        name="ffn_proj_b",
    )(x2d, g, wg, wu, wd, g_mix, w_b)


def _attn_kernel(safe_ref, q_ref, k_ref, vt_ref, o_ref, acc_scr, p0_scr, p1_scr,
                 *, heads, groups, kc, key_of_head, value_of_head, nk, nk_tile):
    tv = vt_ref.shape[3]
    tk = nk_tile
    tq = ATT_TQ
    ones = jnp.ones((ONES_ROWS, tk), BF16)
    dv = MLA_V
    n_sub = q_ref.shape[1] // tq
    safe = safe_ref[pl.program_id(0), pl.program_id(1)] != 0

    def key_tile(j, h):
        kb = key_of_head[h]
        rows = j * tk if isinstance(j, int) else pl.multiple_of(j * tk, tk)
        return k_ref[0, pl.ds(rows, tk), kb * kc:(kb + 1) * kc]

    def values(j, h):
        vb = value_of_head[h]
        per = tk // tv
        return jnp.concatenate([vt_ref[0, j * per + u, vb * dv:(vb + 1) * dv, :] for u in range(per)], axis=1)

    def value_tile(j, h):
        return jnp.concatenate([values(j, h), ones], axis=0)

    def write_out(acc, l, h, sub):
        o_ref[0, h * dv:(h + 1) * dv, sub * tq:(sub + 1) * tq] = (acc / l).astype(BF16)

    @pl.when(safe)
    def _():
        bufs = (p0_scr, p1_scr)
        streams = [(sub, g) for sub in range(n_sub) for g in range(groups)]

        def probs(si, stream, j, hh, buf):
            sub, g = stream
            h = g * heads + hh
            st = _dot_nt(key_tile(j, h), q_ref[0, sub * tq:(sub + 1) * tq, h * kc:(h + 1) * kc])
            buf[hh] = jnp.exp2(st).astype(BF16)

        def accumulate(si, stream, j, hh, buf):
            acc_scr[si % 2, hh] += _dot(value_tile(j, stream[1] * heads + hh), buf[hh])

        acc_scr[...] = jnp.zeros(acc_scr.shape, F32)
        for hh in range(heads):
            probs(0, streams[0], 0, hh, bufs[0])
        for si, stream in enumerate(streams):
            def step(j, cur, nxt, si=si, stream=stream):
                for hh in range(heads):
                    probs(si, stream, j + 1, hh, nxt)
                    accumulate(si, stream, j, hh, cur)

            unroll = min(ATT_UNROLL, nk)

            def body(jj, carry, step=step):
                for u in range(unroll):
                    step(unroll * jj + u, bufs[u % 2], bufs[1 - u % 2])
                return carry

            lax.fori_loop(0, nk // unroll - 1, body, 0)
            for u in range(unroll - 1):
                step(nk - unroll + u, bufs[u % 2], bufs[1 - u % 2])
            for hh in range(heads):
                if si + 1 < len(streams):
                    probs(si + 1, streams[si + 1], 0, hh, bufs[0])
                accumulate(si, stream, nk - 1, hh, bufs[1])
            sub, g = stream
            for hh in range(heads):
                acc = acc_scr[si % 2, hh]
                write_out(acc[:dv], acc[dv:dv + 1], g * heads + hh, sub)
                if si + 2 < len(streams):
                    acc_scr[si % 2, hh] = jnp.zeros((dv + ONES_ROWS, tq), F32)

    @pl.when(jnp.logical_not(safe))
    def _():
        for h in range(groups * heads):
            for sub in range(n_sub):
                qh = q_ref[0, sub * tq:(sub + 1) * tq, h * kc:(h + 1) * kc]

                def body(j, carry, qh=qh, h=h):
                    m, acc = carry
                    st = _dot_nt(key_tile(j, h), qh)
                    m_new = jnp.maximum(m, jnp.max(st, axis=0, keepdims=True))
                    p = jnp.exp2(st - m_new).astype(BF16)
                    return m_new, jnp.exp2(m - m_new) * acc + _dot(value_tile(j, h), p)

                init = (jnp.full((1, tq), NEG_BIG, F32), jnp.zeros((dv + ONES_ROWS, tq), F32))
                _, acc = lax.fori_loop(0, nk, body, init)
                write_out(acc[:dv], acc[dv:dv + 1], h, sub)


def _attention(safe, q, k, vt, *, heads, kc, tk, key_of_head, value_of_head, name):
    b, l, qw = q.shape
    n_heads = qw // kc
    nv, tv = vt.shape[1], vt.shape[3]
    nk = l // tk
    tqb = ATT_TQB
    tq = ATT_TQ
    dv = MLA_V
    grid_spec = pltpu.PrefetchScalarGridSpec(
        num_scalar_prefetch=1,
        grid=(b, l // tqb),
        in_specs=[pl.BlockSpec((1, tqb, qw), lambda bi, i, s: (bi, i, 0)),
                  pl.BlockSpec((1, l, k.shape[2]), lambda bi, i, s: (bi, 0, 0), pipeline_mode=pl.Buffered(1)),
                  pl.BlockSpec((1, nv, vt.shape[2], tv), lambda bi, i, s: (bi, 0, 0, 0),
                               pipeline_mode=pl.Buffered(1))],
        out_specs=pl.BlockSpec((1, n_heads * dv, tqb), lambda bi, i, s: (bi, 0, i)),
        scratch_shapes=[pltpu.VMEM((2, heads, dv + ONES_ROWS, tq), F32),
                        pltpu.VMEM((heads, tk, tq), BF16), pltpu.VMEM((heads, tk, tq), BF16)],
    )
    return pl.pallas_call(
        functools.partial(_attn_kernel, heads=heads, groups=n_heads // heads, kc=kc, key_of_head=key_of_head,
                          value_of_head=value_of_head, nk=nk, nk_tile=tk),
        out_shape=jax.ShapeDtypeStruct((b, n_heads * dv, l), BF16),
        grid_spec=grid_spec,
        compiler_params=_cparams(("parallel", "parallel"), 48),
        name=name,
    )(safe, q, k, vt)


def _hy_pre_kernel(x0_ref, x1_ref, v_ref, w_ref, b_ref, s_ref, x0o_ref):
    rows = x0_ref.shape[1]
    t = lax.broadcasted_iota(jnp.int32, (rows, LANES), 0)
    not_first = t > 0
    not_last = t < rows - 1

    def conv(ref, j):
        a = ref[0].astype(F32)
        prev = jnp.where(not_first, pltpu.roll(a, 1, 0), 0.0)
        nxt = jnp.where(not_last, pltpu.roll(a, rows - 1, 0), 0.0)
        return prev * w_ref[0, j] + a * w_ref[1, j] + nxt * w_ref[2, j] + b_ref[j]

    x0o_ref[0] = conv(x0_ref, 0).astype(BF16)
    s_ref[0] = (conv(x1_ref, 1) * conv(v_ref, 2)).astype(BF16)


def _hy_pre(hy, w_short, b_short):
    b, l, _ = hy.shape
    nc = HY_WIDTH // LANES
    w4 = w_short.reshape(3, 3, nc, 1, LANES)
    b4 = b_short.reshape(3, nc, 1, LANES)

    def seg(j):
        return pl.BlockSpec((1, l, LANES), lambda bi, c, j=j: (bi, 0, j * nc + c))

    out = pl.BlockSpec((1, l, LANES), lambda bi, c: (bi, 0, c))
    return pl.pallas_call(
        _hy_pre_kernel,
        out_shape=(jax.ShapeDtypeStruct((b, l, HY_WIDTH), BF16), jax.ShapeDtypeStruct((b, l, HY_WIDTH), BF16)),
        grid=(b, nc),
        in_specs=[seg(0), seg(1), seg(2),
                  pl.BlockSpec((3, 3, None, 1, LANES), lambda bi, c: (0, 0, c, 0, 0)),
                  pl.BlockSpec((3, None, 1, LANES), lambda bi, c: (0, c, 0, 0))],
        out_specs=(out, out),
        compiler_params=_cparams(("parallel", "parallel"), 48),
        name="hy_pre",
    )(hy, hy, hy, w4, b4)


def _left_mm_kernel(f_ref, x_ref, o_ref, *, hi):
    if hi:
        o_ref[0] = _dot3(f_ref[...], x_ref[0]).astype(o_ref.dtype)
    else:
        o_ref[0] = _dot(f_ref[...], x_ref[0]).astype(o_ref.dtype)


def _left_mm(f, x, out_dtype, hi=False):
    b, k, n = x.shape
    m = f.shape[0]
    cb = min(HY_COLS, n)
    return pl.pallas_call(
        functools.partial(_left_mm_kernel, hi=hi),
        out_shape=jax.ShapeDtypeStruct((b, m, n), out_dtype),
        grid=(b, n // cb),
        in_specs=[_const_spec((m, k)), pl.BlockSpec((1, k, cb), lambda bi, c: (bi, 0, c))],
        out_specs=pl.BlockSpec((1, m, cb), lambda bi, c: (bi, 0, c)),
        compiler_params=_cparams(("parallel", "parallel"), 48),
        name="hy_outer_dft",
    )(f, x)


def _hy_mid_kernel(f_ref, kf_ref, g_ref, a_ref, o_ref):
    n2 = DFT_N2
    kr, ki = kf_ref[0, :n2], kf_ref[0, n2:]
    for bi in range(a_ref.shape[0]):
        x = _dot(f_ref[0], a_ref[bi, 0])
        xr, xi = x[:n2], x[n2:]
        y = jnp.concatenate([xr * kr - xi * ki, xr * ki + xi * kr], axis=0).astype(BF16)
        o_ref[bi, 0] = _dot(g_ref[0], y).astype(BF16)


def _hy_mid(a4, f2, kf, g2):
    b, n1, r, c = a4.shape
    mat = pl.BlockSpec((1, r, r), lambda i: (i, 0, 0))
    dat = pl.BlockSpec((b, 1, r, c), lambda i: (0, i, 0, 0))
    return pl.pallas_call(
        _hy_mid_kernel,
        out_shape=jax.ShapeDtypeStruct(a4.shape, BF16),
        grid=(n1,),
        in_specs=[mat, pl.BlockSpec((1, r, c), lambda i: (i, 0, 0)), mat, dat],
        out_specs=dat,
        compiler_params=_cparams(("parallel",), 48),
        name="hy_mid",
    )(f2, kf, g2, a4)


def _hy_filter_kernel(z_ref, win_ref, w1_ref, b1_ref, w2_ref, b2_ref, w3_ref, fr_ref, o_ref, *, half_tiles):
    fr = fr_ref[...]
    h = jnp.sin(fr * (_dot3(z_ref[...], w1_ref[...]) + b1_ref[...]))
    h = jnp.sin(fr * (_dot3(h, w2_ref[...]) + b2_ref[...]))
    h3 = _dot3(h, w3_ref[...])
    backward = pl.program_id(0) >= half_tiles
    o_ref[...] = jnp.where(backward, h3[:, HY_WIDTH:], h3[:, :HY_WIDTH]) * win_ref[...]


def _hy_filter(z2, win2, w1p, b1, w2, b2, w3, fr):
    n = z2.shape[0]
    tl = 512
    return pl.pallas_call(
        functools.partial(_hy_filter_kernel, half_tiles=n // (2 * tl)),
        out_shape=jax.ShapeDtypeStruct((n, HY_WIDTH), F32),
        grid=(n // tl,),
        in_specs=[pl.BlockSpec((tl, LANES), lambda i: (i, 0)), pl.BlockSpec((tl, HY_WIDTH), lambda i: (i, 0)),
                  _const_spec(w1p.shape), _const_spec(b1.shape), _const_spec(w2.shape), _const_spec(b2.shape),
                  _const_spec(w3.shape), _const_spec(fr.shape)],
        out_specs=pl.BlockSpec((tl, HY_WIDTH), lambda i: (i, 0)),
        compiler_params=_cparams(("parallel",), 32),
        name="hy_filter",
    )(z2, win2, w1p, b1, w2, b2, w3, fr)


def _hy_spec_kernel(f_ref, a_ref, o_ref):
    o_ref[0] = _dot3(f_ref[0], a_ref[0])


def _hy_spec(f2, a3):
    n1, r, c = a3.shape
    return pl.pallas_call(
        _hy_spec_kernel,
        out_shape=jax.ShapeDtypeStruct(a3.shape, F32),
        grid=(n1,),
        in_specs=[pl.BlockSpec((1, r, r), lambda i: (i, 0, 0)), pl.BlockSpec((1, r, c), lambda i: (i, 0, 0))],
        out_specs=pl.BlockSpec((1, r, c), lambda i: (i, 0, 0)),
        compiler_params=_cparams(("parallel",), 32),
        name="hy_filter_spectrum",
    )(f2, a3)


def _merge_kernel(x_ref, yat_ref, yc_ref, s_ref, x0_ref, hb_ref, yct_ref, gt_ref, wb_ref, wo_ref, g_ref, wg_ref,
                  wu_ref, wd_ref, gf_ref, o_ref, *, final):
    d = D_MODEL
    ba = _dot_tn(yat_ref[0], wb_ref[0])
    yb = x0_ref[0].astype(F32) * (yc_ref[0].astype(F32) + s_ref[0].astype(F32) * hb_ref[...])
    bb = _dot(yb.astype(BF16), wb_ref[1])
    bc = _dot_tn(yct_ref[0], wb_ref[2])
    merged = (gt_ref[0, :, 0:d].astype(F32) * ba + gt_ref[0, :, d:2 * d].astype(F32) * bb
              + gt_ref[0, :, 2 * d:3 * d].astype(F32) * bc)
    x = x_ref[0] + _dot(merged.astype(BF16), wo_ref[...])
    y = _ffn_body(x, g_ref[...], wg_ref, wu_ref, wd_ref)
    if final:
        y = _rms(y, gf_ref[...])
    o_ref[0] = y


def _merge_ffn(x, yat, yc, s, x0, hy_bias, yct, gates, wb, wo, g, wg, wu, wd, g_final, final):
    b, l, _ = x.shape
    tm = TOK_TILE

    def tok(w):
        return pl.BlockSpec((1, tm, w), lambda bi, i: (bi, i, 0))

    tr = pl.BlockSpec((1, BRANCH_W, tm), lambda bi, i: (bi, 0, i))
    return pl.pallas_call(
        functools.partial(_merge_kernel, final=final),
        out_shape=jax.ShapeDtypeStruct(x.shape, F32),
        grid=(b, l // tm),
        in_specs=[tok(D_MODEL), tr, tok(BRANCH_W), tok(BRANCH_W), tok(BRANCH_W), _const_spec((1, HY_WIDTH)), tr,
                  tok(N_BRANCH * D_MODEL),
                  _const_spec(wb.shape), _const_spec(wo.shape), _const_spec((1, D_MODEL)),
                  _const_spec((D_MODEL, D_FF)), _const_spec((D_MODEL, D_FF)), _const_spec((D_FF, D_MODEL)),
                  _const_spec((1, D_MODEL))],
        out_specs=tok(D_MODEL),
        compiler_params=_cparams(("parallel", "parallel"), 58),
        name="merge_ffn",
    )(x, yat, yc, s, x0, hy_bias, yct, gates, wb, wo, g, wg, wu, wd, g_final)


def _rope_tables(l):
    rows = l // GRID_W
    row = jnp.repeat(jnp.arange(rows, dtype=F32), GRID_W)
    col = jnp.tile(jnp.arange(GRID_W, dtype=F32), rows)

    def tab(d_rot):
        n_freq = d_rot // 4
        inv = ROPE_THETA ** (-jnp.arange(n_freq, dtype=F32) / n_freq)
        ang = jnp.concatenate([row[:, None] * inv, col[:, None] * inv], axis=-1)
        c = jnp.repeat(jnp.cos(ang), 2, axis=-1)
        s = jnp.repeat(jnp.sin(ang), 2, axis=-1) * jnp.tile(jnp.array([-1.0, 1.0], F32), d_rot // 2)
        return c, s

    ca, sa = tab(GQA_HEAD_DIM)
    cm, sm = tab(MLA_ROPE)
    tail = jnp.zeros((l, LANES - MLA_ROPE_LANE - MLA_ROPE), F32)
    cm = jnp.concatenate([jnp.ones((l, MLA_ROPE_LANE), F32), cm, tail], axis=1)
    sm = jnp.concatenate([jnp.zeros((l, MLA_ROPE_LANE), F32), sm, tail], axis=1)
    return (jnp.tile(ca, (1, 2)), jnp.tile(sa, (1, 2)), cm, sm)


def _hy_positions(l):
    r = jnp.arange(l, dtype=jnp.int32)
    pos = jnp.concatenate([r, (l - r) % l]).astype(F32)[:, None]
    live = jnp.concatenate([jnp.ones((l,), F32), (r > 0).astype(F32)])[:, None]
    t = pos * (1.0 / (l - 1))
    w = 2.0 * math.pi * pos / l
    f = jnp.linspace(1e-4, HY_BANDS - 1, HY_BANDS, dtype=F32)[None, :]
    z2 = jnp.concatenate([t, jnp.cos(f * w), -jnp.sin(f * w), jnp.zeros((2 * l, LANES - HY_EMB), F32)], axis=-1)
    max_decay = math.log(HY_TARGET) / HY_FAST
    min_decay = math.log(HY_TARGET) / HY_SLOW
    deltas = jnp.abs(jnp.linspace(min_decay, max_decay, HY_WIDTH, dtype=F32))
    return z2, jnp.exp(-t * deltas[None, :]) * live


def _hy_k1_count(l):
    n1 = 2 * l // DFT_N2
    return -(-(n1 // 2 + 1) // 8) * 8


def _dft_tables(l):
    n = 2 * l
    n2 = DFT_N2
    n1 = n // n2
    n1c = _hy_k1_count(l)
    two_pi = 2.0 * math.pi
    i1 = jnp.arange(n1, dtype=jnp.int32)
    ang1 = ((i1[:n1c, None] * i1[None, :]) % n1).astype(F32) * (two_pi / n1)
    live = (i1[:n1c] <= n1 // 2)[:, None]
    c1, s1 = jnp.where(live, jnp.cos(ang1), 0.0), jnp.where(live, jnp.sin(ang1), 0.0)
    f1_full = jnp.stack([c1, -s1], axis=1).reshape(2 * n1c, n1)
    fold = jnp.where((i1[:n1c] == 0) | (i1[:n1c] == n1 // 2), 1.0, 2.0)[:, None] * (1.0 / n)
    g1 = jnp.stack([(c1 * fold)[:, :n1 // 2].T, (-s1 * fold)[:, :n1 // 2].T], axis=2).reshape(n1 // 2, 2 * n1c)
    i2 = jnp.arange(n2, dtype=jnp.int32)
    kk = i1[:n1c, None, None] + n1 * i2[None, :, None]
    ang2 = ((kk * i2[None, None, :]) % n).astype(F32) * (two_pi / n)
    c2, s2 = jnp.cos(ang2), jnp.sin(ang2)
    f2 = jnp.concatenate([jnp.concatenate([c2, s2], axis=2), jnp.concatenate([-s2, c2], axis=2)], axis=1)
    c2t, s2t = jnp.swapaxes(c2, 1, 2), jnp.swapaxes(s2, 1, 2)
    g2 = jnp.concatenate([jnp.concatenate([c2t, -s2t], axis=2), jnp.concatenate([s2t, c2t], axis=2)], axis=1)
    return f1_full, g1, f2, g2


def _swap_pairs(n):
    return np.arange(n) ^ 1


def _proj_a_columns():
    zero = IN_WIDTH
    cols = np.full((PA_END,), zero, np.int64)
    q0, k0, v0, _, cq0, ckv0, kr0, _ = IN_OFFS[:8]
    for h in range(GQA_HEADS):
        dst = h * LANES + (h // (GQA_HEADS // GQA_KV_HEADS)) * GQA_HEAD_DIM
        src = q0 + h * GQA_HEAD_DIM + np.arange(GQA_HEAD_DIM)
        cols[PA_Q + dst:PA_Q + dst + GQA_HEAD_DIM] = src
        cols[PA_QS + dst:PA_QS + dst + GQA_HEAD_DIM] = q0 + h * GQA_HEAD_DIM + _swap_pairs(GQA_HEAD_DIM)
    cols[PA_K:PA_K + 128] = k0 + np.arange(128)
    cols[PA_KS:PA_KS + 128] = k0 + _swap_pairs(128)
    cols[PA_V:PA_V + 128] = v0 + np.arange(128)
    cols[PA_CQ:PA_CQ + MLA_Q_RANK] = cq0 + np.arange(MLA_Q_RANK)
    cols[PA_CKV:PA_CKV + MLA_KV_RANK] = ckv0 + np.arange(MLA_KV_RANK)
    r0 = MLA_ROPE_LANE
    cols[PA_KR + r0:PA_KR + r0 + MLA_ROPE] = kr0 + np.arange(MLA_ROPE)
    cols[PA_KRS + r0:PA_KRS + r0 + MLA_ROPE] = kr0 + _swap_pairs(MLA_ROPE)
    return cols


def _mla_q_columns():
    hd = MLA_NOPE + MLA_ROPE
    zero = MLA_HEADS * hd
    main = np.full((MLA_HEADS * MLA_KC,), zero, np.int64)
    swap = np.full((MLA_HEADS * MLA_KC,), zero, np.int64)
    for h in range(MLA_HEADS):
        b0 = h * MLA_KC
        main[b0:b0 + MLA_NOPE] = h * hd + np.arange(MLA_NOPE)
        r0 = b0 + MLA_ROPE_LANE
        main[r0:r0 + MLA_ROPE] = h * hd + MLA_NOPE + np.arange(MLA_ROPE)
        swap[r0:r0 + MLA_ROPE] = h * hd + MLA_NOPE + _swap_pairs(MLA_ROPE)
    return main, swap


def _mla_kv_columns():
    hd = MLA_NOPE + MLA_V
    zero = MLA_HEADS * hd
    knope = np.full((MLA_HEADS * MLA_KC,), zero, np.int64)
    for h in range(MLA_HEADS):
        knope[h * MLA_KC:h * MLA_KC + MLA_NOPE] = h * hd + np.arange(MLA_NOPE)
    val = np.concatenate([h * hd + MLA_NOPE + np.arange(MLA_V) for h in range(MLA_HEADS)])
    return np.concatenate([knope, val])


def _take_cols(w, cols):
    wz = jnp.concatenate([w, jnp.zeros((w.shape[0], 1), w.dtype)], axis=1)
    return jnp.take(wz, jnp.asarray(cols, jnp.int32), axis=1).astype(BF16)


def _tile2(g):
    return jnp.tile(g, 2)[None, :]


def _encoder(x, lw, g_final):
    b, l, d = x.shape
    n2 = DFT_N2
    n1 = 2 * l // n2
    c = HY_WIDTH
    tabs = _rope_tables(l)
    z2, win2 = _hy_positions(l)
    f1_full, g1, f2, g2 = _dft_tables(l)
    f1_b = f1_full[:, :n1 // 2].astype(BF16)
    g1_b = g1.astype(BF16)
    f2_b = f2.astype(BF16)
    g2_b = g2.astype(BF16)
    for li, w in enumerate(lw):
        x2, hy, gates = _ffn_proj_b(x.reshape(b * l, d), w["g_ffn1"], w["wg1"], w["wu1"], w["wd1"], w["g_mix"],
                                    w["w_b"])
        x = x2.reshape(b, l, d)
        qg, kg, vgt, qm, km, vmt, bnd = _proj_a(x, w["g_mix"], w["w_a"], tabs, w["gq"], w["gqs"], w["gk"],
                                                w["gks"], w["gcq"], w["gckv"], w["wuq"], w["wuqs"], w["wukv"])
        top = bnd[:, :, :4, 0]
        limit = ATT_SAFE_SCORE ** 2
        safe_a = (top[:, :, 0] * jnp.max(top[:, :, 1], axis=1, keepdims=True) <= limit).astype(jnp.int32)
        safe_m = (top[:, :, 2] * jnp.max(top[:, :, 3], axis=1, keepdims=True) <= limit).astype(jnp.int32)
        per_kv = GQA_HEADS // GQA_KV_HEADS
        yat = _attention(safe_a, qg, kg, vgt, heads=ATT_HEADS, kc=GQA_KC, tk=ATT_TK_GQA,
                         key_of_head=(0,) * GQA_HEADS,
                         value_of_head=tuple(h // per_kv for h in range(GQA_HEADS)), name="attn_gqa")
        yct = _attention(safe_m, qm, km, vmt, heads=ATT_HEADS, kc=MLA_KC, tk=ATT_TK_MLA,
                         key_of_head=tuple(range(MLA_HEADS)),
                         value_of_head=tuple(range(MLA_HEADS)), name="attn_mla")
        kc_time = _hy_filter(z2, win2, w["w1p"], w["b1"], w["w2"], w["b2"], w["w3"], w["fr"])
        n1c = _hy_k1_count(l)
        ka =_left_mm(f1_full, kc_time.reshape(1, n1, n2 * c), F32, hi=True)
        kf = _hy_spec(f2, ka.reshape(n1c, 2 * n2, c))
        s, x0 = _hy_pre(hy.reshape(b, l, 3 * c), w["w_short"], w["b_short"])
        a = _left_mm(f1_b, s.reshape(b, n1 // 2, n2 * c), BF16)
        bi = _hy_mid(a.reshape(b, n1c, 2 * n2, c), f2_b, kf, g2_b)
        yc = _left_mm(g1_b, bi.reshape(b, 2 * n1c, n2 * c), BF16)
        x = _merge_ffn(x, yat, yc.reshape(b, l, c), s, x0, w["hy_bias"], yct, gates.reshape(b, l, N_BRANCH * d),
                       w["wb"], w["wo"],
                       w["g_ffn2"], w["wg2"], w["wu2"], w["wd2"], g_final, li == len(lw) - 1)
    return x


def kernel(x_prompt, x_sample, g_ffn1, w_ffn1_gate, w_ffn1_up, w_ffn1_down, g_mix, w_in, g_qnorm, g_knorm,
           w_hy_short, b_hy_short, w_hy_f1, b_hy_f1, w_hy_f2, b_hy_f2, w_hy_f3, hy_sin_freq, hy_bias,
           g_mla_q, w_mla_uq, g_mla_kv, w_mla_ukv, w_branch, w_out, g_ffn2, w_ffn2_gate, w_ffn2_up,
           w_ffn2_down, g_final):
    cols_a = _proj_a_columns()
    uq_main, uq_swap = _mla_q_columns()
    ukv_cols = _mla_kv_columns()
    sw64 = _swap_pairs(GQA_HEAD_DIM)
    hy0 = IN_OFFS[3]
    gt0 = IN_OFFS[7]
    lw = []
    for l in range(DEPTH):
        lw.append(dict(
            g_ffn1=g_ffn1[l][None], wg1=w_ffn1_gate[l].astype(BF16), wu1=w_ffn1_up[l].astype(BF16),
            wd1=w_ffn1_down[l].astype(BF16),
            g_ffn2=g_ffn2[l][None], wg2=w_ffn2_gate[l].astype(BF16), wu2=w_ffn2_up[l].astype(BF16),
            wd2=w_ffn2_down[l].astype(BF16),
            g_mix=g_mix[l][None],
            w_a=_take_cols(w_in[l], cols_a),
            w_b=jnp.concatenate([w_in[l][:, hy0:hy0 + 3 * HY_WIDTH], w_in[l][:, gt0:]], axis=1).astype(BF16),
            gq=_tile2(g_qnorm[l]), gqs=_tile2(g_qnorm[l][sw64]),
            gk=_tile2(g_knorm[l]), gks=_tile2(g_knorm[l][sw64]),
            gcq=g_mla_q[l][None], gckv=g_mla_kv[l][None],
            wuq=_take_cols(w_mla_uq[l], uq_main), wuqs=_take_cols(w_mla_uq[l], uq_swap),
            wukv=_take_cols(w_mla_ukv[l], ukv_cols),
            w_short=w_hy_short[l], b_short=b_hy_short[l],
            w1p=jnp.pad(w_hy_f1[l], ((0, LANES - HY_EMB), (0, 0))), b1=b_hy_f1[l][None],
            w2=w_hy_f2[l], b2=b_hy_f2[l][None], w3=w_hy_f3[l], fr=hy_sin_freq[l][None],
            hy_bias=hy_bias[l][None],
            wb=w_branch[l].astype(BF16), wo=w_out[l].astype(BF16),
        ))
    gf = g_final[None]
    return _encoder(x_prompt, lw, gf), _encoder(x_sample, lw, gf)
```

```python
import functools
import math

import numpy as np
import jax
import jax.numpy as jnp
from jax import lax
from jax.experimental import pallas as pl
from jax.experimental.pallas import tpu as pltpu

F32 = jnp.float32
BF16 = jnp.bfloat16

D_MODEL = 1024
DEPTH = 2
GRID_W = 64
ROPE_THETA = 10000.0
EPS = 1e-6
D_FF = 2816
N_BRANCH = 3
BRANCH_W = 512
GQA_HEADS = 8
GQA_KV_HEADS = 2
GQA_HEAD_DIM = 64
HY_WIDTH = 512
HY_ORDER = 64
HY_EMB = 33
HY_BANDS = (HY_EMB - 1) // 2
HY_TARGET = 1e-2
HY_FAST = 0.3
HY_SLOW = 1.5
MLA_HEADS = 8
MLA_Q_RANK = 256
MLA_KV_RANK = 128
MLA_NOPE = 64
MLA_ROPE = 32
MLA_V = 64
IN_WIDTHS = (512, 128, 128, 3 * HY_WIDTH, MLA_Q_RANK, MLA_KV_RANK, MLA_ROPE, N_BRANCH * D_MODEL)
IN_OFFS = tuple(int(c) for c in np.cumsum((0,) + IN_WIDTHS))
IN_WIDTH = IN_OFFS[-1]

LANES = 128
V7X_VMEM_BYTES = 64 * 1024 * 1024

TOK_TILE = 512
ATT_TQ = 256
ATT_TQB = 512
ATT_HEADS = 4
ATT_UNROLL = 4
ATT_SAFE_SCORE = 64.0
ATT_TK_GQA = 2 * TOK_TILE
ATT_TK_MLA = 2 * TOK_TILE
FF_CHUNKS = ((0, 1024), (1024, 2048), (2048, D_FF))
DFT_N2 = 128
HY_COLS = 8192
HY_MID_K1 = 2
ONES_ROWS = 16
NEG_BIG = -1e30
LOG2E = math.log2(math.e)


def _cparams(sem, vmem_mb):
    return pltpu.CompilerParams(dimension_semantics=sem, vmem_limit_bytes=vmem_mb * 1024 * 1024)


def _const_spec(shape):
    nd = len(shape)
    return pl.BlockSpec(shape, lambda *_: (0,) * nd, pipeline_mode=pl.Buffered(1))


def _rms(x, g):
    return x * lax.rsqrt(jnp.mean(x * x, axis=-1, keepdims=True) + EPS) * g


def _dot(a, b):
    return jnp.dot(a, b, preferred_element_type=F32)


def _dot_nt(a, b):
    return lax.dot_general(a, b, (((1,), (1,)), ((), ())), preferred_element_type=F32)


def _dot_tn(a, b):
    return lax.dot_general(a, b, (((0,), (0,)), ((), ())), preferred_element_type=F32)


def _split(a):
    hi = a.astype(BF16)
    return hi, (a - hi.astype(F32)).astype(BF16)


def _dot3(a, b):
    ah, al = _split(a)
    bh, bl = _split(b)
    return _dot(ah, bh) + (_dot(ah, bl) + _dot(al, bh))


def _ffn_body(x, g, wg_ref, wu_ref, wd_ref):
    xb = _rms(x, g).astype(BF16)
    acc = jnp.zeros_like(x)
    for c0, c1 in FF_CHUNKS:
        gate = _dot(xb, wg_ref[:, c0:c1])
        up = _dot(xb, wu_ref[:, c0:c1])
        h = (gate * jax.nn.sigmoid(gate) * up).astype(BF16)
        acc = acc + _dot(h, wd_ref[c0:c1, :])
    return x + 0.5 * acc


PA_Q, PA_QS, PA_K, PA_KS, PA_V, PA_CQ, PA_CKV, PA_KR, PA_KRS, PA_END = (
    0, 1024, 2048, 2176, 2304, 2432, 2688, 2816, 2944, 3072)
GQA_KC = 128
MLA_KC = 128
MLA_ROPE_LANE = MLA_NOPE


def _sq_norm(a_bf16):
    a = a_bf16.astype(F32)
    return jnp.sum(a * a, axis=-1, keepdims=True)


def _proj_a_kernel(x_ref, g_ref, w_ref, cosa_ref, sina_ref, cosm_ref, sinm_ref,
                   gq_ref, gqs_ref, gk_ref, gks_ref, gcq_ref, gckv_ref,
                   wuq_ref, wuqs_ref, wukv_ref,
                   qg_ref, kg_ref, vgt_ref, qm_ref, km_ref, vmt_ref, bnd_ref):
    ub = _rms(x_ref[0], g_ref[...]).astype(BF16)
    cosa, sina = cosa_ref[...], sina_ref[...]
    cosm, sinm = cosm_ref[...], sinm_ref[...]

    zq = _dot(ub, w_ref[:, PA_Q:PA_QS])
    zqs = _dot(ub, w_ref[:, PA_QS:PA_K])
    tq_c = gq_ref[...] * cosa
    tq_s = gqs_ref[...] * sina
    scale_a = GQA_HEAD_DIM ** -0.5 * LOG2E
    qa_n2 = None
    for h in range(GQA_HEADS):
        a = zq[:, h * LANES:(h + 1) * LANES]
        a_sw = zqs[:, h * LANES:(h + 1) * LANES]
        r = lax.rsqrt(jnp.sum(a * a, axis=-1, keepdims=True) * (1.0 / GQA_HEAD_DIM) + EPS)
        qb = ((a * tq_c + a_sw * tq_s) * (r * scale_a)).astype(BF16)
        qg_ref[0, :, h * LANES:(h + 1) * LANES] = qb
        qa_n2 = _sq_norm(qb) if qa_n2 is None else jnp.maximum(qa_n2, _sq_norm(qb))

    zk = _dot(ub, w_ref[:, PA_K:PA_KS])
    zks = _dot(ub, w_ref[:, PA_KS:PA_V])
    first = lax.broadcasted_iota(jnp.int32, zk.shape, 1) < GQA_HEAD_DIM
    sq = zk * zk
    r0 = lax.rsqrt(jnp.sum(jnp.where(first, sq, 0.0), axis=-1, keepdims=True) * (1.0 / GQA_HEAD_DIM) + EPS)
    r1 = lax.rsqrt(jnp.sum(jnp.where(first, 0.0, sq), axis=-1, keepdims=True) * (1.0 / GQA_HEAD_DIM) + EPS)
    kb = ((zk * (gk_ref[...] * cosa) + zks * (gks_ref[...] * sina)) * jnp.where(first, r0, r1)).astype(BF16)
    kg_ref[0] = kb
    kb32 = kb.astype(F32)
    kb_sq = kb32 * kb32
    ka_n2 = jnp.maximum(jnp.sum(jnp.where(first, kb_sq, 0.0), axis=-1, keepdims=True),
                        jnp.sum(jnp.where(first, 0.0, kb_sq), axis=-1, keepdims=True))

    vgt_ref[0, 0] = _dot(ub, w_ref[:, PA_V:PA_CQ]).T.astype(BF16)

    scale_m = (MLA_NOPE + MLA_ROPE) ** -0.5 * LOG2E
    cqn = _rms(_dot(ub, w_ref[:, PA_CQ:PA_CKV]), gcq_ref[...]).astype(BF16)
    zq2 = _dot(cqn, wuq_ref[...])
    zq2s = _dot(cqn, wuqs_ref[...])
    qm_n2 = None
    for h in range(MLA_HEADS):
        blk = slice(h * MLA_KC, (h + 1) * MLA_KC)
        qb = ((zq2[:, blk] * cosm + zq2s[:, blk] * sinm) * scale_m).astype(BF16)
        qm_ref[0, :, blk] = qb
        qm_n2 = _sq_norm(qb) if qm_n2 is None else jnp.maximum(qm_n2, _sq_norm(qb))

    ckvn = _rms(_dot(ub, w_ref[:, PA_CKV:PA_KR]), gckv_ref[...]).astype(BF16)
    zkv = _dot(ckvn, wukv_ref[...])
    krope = _dot(ub, w_ref[:, PA_KR:PA_KRS]) * cosm + _dot(ub, w_ref[:, PA_KRS:PA_END]) * sinm
    km_n2 = None
    for h in range(MLA_HEADS):
        blk = slice(h * MLA_KC, (h + 1) * MLA_KC)
        kb = (zkv[:, blk] + krope).astype(BF16)
        km_ref[0, :, blk] = kb
        km_n2 = _sq_norm(kb) if km_n2 is None else jnp.maximum(km_n2, _sq_norm(kb))
    vmt_ref[0, 0] = zkv[:, MLA_HEADS * MLA_KC:].T.astype(BF16)

    row = lax.broadcasted_iota(jnp.int32, (8, LANES), 0)
    tops = [jnp.max(v, axis=0, keepdims=True) for v in (qa_n2, ka_n2, qm_n2, km_n2)]
    bnd_ref[0, 0] = jnp.where(row == 0, tops[0], jnp.where(row == 1, tops[1], jnp.where(row == 2, tops[2], tops[3])))


def _proj_a(x, g_mix, w_a, tabs, gq, gqs, gk, gks, gcq, gckv, wuq, wuqs, wukv):
    b, l, _ = x.shape
    tm = TOK_TILE
    nt = l // tm
    cosa, sina, cosm, sinm = tabs
    tab = pl.BlockSpec((tm, LANES), lambda bi, i: (i, 0))

    def tok(w):
        return pl.BlockSpec((1, tm, w), lambda bi, i: (bi, i, 0))

    def tr(rows):
        return pl.BlockSpec((1, 1, rows, tm), lambda bi, i: (bi, i, 0, 0))

    out_shape = (
        jax.ShapeDtypeStruct((b, l, GQA_HEADS * GQA_KC), BF16),
        jax.ShapeDtypeStruct((b, l, GQA_KC), BF16),
        jax.ShapeDtypeStruct((b, nt, GQA_KV_HEADS * GQA_HEAD_DIM, tm), BF16),
        jax.ShapeDtypeStruct((b, l, MLA_HEADS * MLA_KC), BF16),
        jax.ShapeDtypeStruct((b, l, MLA_HEADS * MLA_KC), BF16),
        jax.ShapeDtypeStruct((b, nt, MLA_HEADS * MLA_V, tm), BF16),
        jax.ShapeDtypeStruct((b, nt, 8, LANES), F32),
    )
    return pl.pallas_call(
        _proj_a_kernel,
        out_shape=out_shape,
        grid=(b, nt),
        in_specs=[tok(D_MODEL), _const_spec((1, D_MODEL)), _const_spec(w_a.shape), tab, tab, tab, tab,
                  _const_spec((1, LANES)), _const_spec((1, LANES)), _const_spec((1, LANES)),
                  _const_spec((1, LANES)), _const_spec((1, MLA_Q_RANK)), _const_spec((1, MLA_KV_RANK)),
                  _const_spec(wuq.shape), _const_spec(wuqs.shape), _const_spec(wukv.shape)],
        out_specs=(tok(GQA_HEADS * GQA_KC), tok(GQA_KC), tr(GQA_KV_HEADS * GQA_HEAD_DIM),
                   tok(MLA_HEADS * MLA_KC), tok(MLA_HEADS * MLA_KC), tr(MLA_HEADS * MLA_V),
                   pl.BlockSpec((1, 1, 8, LANES), lambda bi, i: (bi, i, 0, 0))),
        compiler_params=_cparams(("parallel", "parallel"), 48),
        name="proj_a",
    )(x, g_mix, w_a, cosa, sina, cosm, sinm, gq, gqs, gk, gks, gcq, gckv, wuq, wuqs, wukv)


def _ffn_proj_b_kernel(x_ref, g_ref, wg_ref, wu_ref, wd_ref, gm_ref, w_ref, xo_ref, hy_ref, gt_ref):
    x = _ffn_body(x_ref[...], g_ref[...], wg_ref, wu_ref, wd_ref)
    xo_ref[...] = x
    ub = _rms(x, gm_ref[...]).astype(BF16)
    hy_ref[...] = _dot(ub, w_ref[:, :3 * HY_WIDTH]).astype(BF16)
    gt_ref[...] = jax.nn.sigmoid(_dot(ub, w_ref[:, 3 * HY_WIDTH:])).astype(BF16)


def _ffn_proj_b(x2d, g, wg, wu, wd, g_mix, w_b):
    t = x2d.shape[0]
    tm = TOK_TILE

    def row(w):
        return pl.BlockSpec((tm, w), lambda i: (i, 0))

    return pl.pallas_call(
        _ffn_proj_b_kernel,
        out_shape=(jax.ShapeDtypeStruct((t, D_MODEL), F32),
                   jax.ShapeDtypeStruct((t, 3 * HY_WIDTH), BF16),
                   jax.ShapeDtypeStruct((t, N_BRANCH * D_MODEL), BF16)),
        grid=(t // tm,),
        in_specs=[row(D_MODEL), _const_spec((1, D_MODEL)), _const_spec((D_MODEL, D_FF)),
                  _const_spec((D_MODEL, D_FF)), _const_spec((D_FF, D_MODEL)), _const_spec((1, D_MODEL)),
                  _const_spec(w_b.shape)],
        out_specs=(row(D_MODEL), row(3 * HY_WIDTH), row(N_BRANCH * D_MODEL)),
        compiler_params=_cparams(("parallel",), 58),
        name="ffn_proj_b",
    )(x2d, g, wg, wu, wd, g_mix, w_b)


def _attn_kernel(safe_ref, q_ref, k_ref, vt_ref, o_ref, acc_scr, p0_scr, p1_scr,
                 *, heads, groups, kc, key_of_head, value_of_head, nk, nk_tile):
    tv = vt_ref.shape[3]
    tk = nk_tile
    tq = ATT_TQ
    ones = jnp.ones((ONES_ROWS, tk), BF16)
    dv = MLA_V
    n_sub = q_ref.shape[1] // tq
    safe = safe_ref[pl.program_id(0), pl.program_id(1)] != 0

    def key_tile(j, h):
        kb = key_of_head[h]
        rows = j * tk if isinstance(j, int) else pl.multiple_of(j * tk, tk)
        return k_ref[0, pl.ds(rows, tk), kb * kc:(kb + 1) * kc]

    def values(j, h):
        vb = value_of_head[h]
        per = tk // tv
        return jnp.concatenate([vt_ref[0, j * per + u, vb * dv:(vb + 1) * dv, :] for u in range(per)], axis=1)

    def value_tile(j, h):
        return jnp.concatenate([values(j, h), ones], axis=0)

    def write_out(acc, l, h, sub):
        o_ref[0, h * dv:(h + 1) * dv, sub * tq:(sub + 1) * tq] = (acc / l).astype(BF16)

    @pl.when(safe)
    def _():
        bufs = (p0_scr, p1_scr)
        streams = [(sub, g) for sub in range(n_sub) for g in range(groups)]

        def probs(si, stream, j, hh, buf):
            sub, g = stream
            h = g * heads + hh
            st = _dot_nt(key_tile(j, h), q_ref[0, sub * tq:(sub + 1) * tq, h * kc:(h + 1) * kc])
            buf[hh] = jnp.exp2(st).astype(BF16)

        def accumulate(si, stream, j, hh, buf):
            acc_scr[si % 2, hh] += _dot(value_tile(j, stream[1] * heads + hh), buf[hh])

        acc_scr[...] = jnp.zeros(acc_scr.shape, F32)
        for hh in range(heads):
            probs(0, streams[0], 0, hh, bufs[0])
        for si, stream in enumerate(streams):
            def step(j, cur, nxt, si=si, stream=stream):
                for hh in range(heads):
                    probs(si, stream, j + 1, hh, nxt)
                    accumulate(si, stream, j, hh, cur)

            unroll = min(ATT_UNROLL, nk)

            def body(jj, carry, step=step):
                for u in range(unroll):
                    step(unroll * jj + u, bufs[u % 2], bufs[1 - u % 2])
                return carry

            lax.fori_loop(0, nk // unroll - 1, body, 0)
            for u in range(unroll - 1):
                step(nk - unroll + u, bufs[u % 2], bufs[1 - u % 2])
            for hh in range(heads):
                if si + 1 < len(streams):
                    probs(si + 1, streams[si + 1], 0, hh, bufs[0])
                accumulate(si, stream, nk - 1, hh, bufs[1])
            sub, g = stream
            for hh in range(heads):
                acc = acc_scr[si % 2, hh]
                write_out(acc[:dv], acc[dv:dv + 1], g * heads + hh, sub)
                if si + 2 < len(streams):
                    acc_scr[si % 2, hh] = jnp.zeros((dv + ONES_ROWS, tq), F32)

    @pl.when(jnp.logical_not(safe))
    def _():
        for h in range(groups * heads):
            for sub in range(n_sub):
                qh = q_ref[0, sub * tq:(sub + 1) * tq, h * kc:(h + 1) * kc]

                def body(j, carry, qh=qh, h=h):
                    m, acc = carry
                    st = _dot_nt(key_tile(j, h), qh)
                    m_new = jnp.maximum(m, jnp.max(st, axis=0, keepdims=True))
                    p = jnp.exp2(st - m_new).astype(BF16)
                    return m_new, jnp.exp2(m - m_new) * acc + _dot(value_tile(j, h), p)

                init = (jnp.full((1, tq), NEG_BIG, F32), jnp.zeros((dv + ONES_ROWS, tq), F32))
                _, acc = lax.fori_loop(0, nk, body, init)
                write_out(acc[:dv], acc[dv:dv + 1], h, sub)


def _attention(safe, q, k, vt, *, heads, kc, tk, key_of_head, value_of_head, name):
    b, l, qw = q.shape
    n_heads = qw // kc
    nv, tv = vt.shape[1], vt.shape[3]
    nk = l // tk
    tqb = ATT_TQB
    tq = ATT_TQ
    dv = MLA_V
    grid_spec = pltpu.PrefetchScalarGridSpec(
        num_scalar_prefetch=1,
        grid=(b, l // tqb),
        in_specs=[pl.BlockSpec((1, tqb, qw), lambda bi, i, s: (bi, i, 0)),
                  pl.BlockSpec((1, l, k.shape[2]), lambda bi, i, s: (bi, 0, 0), pipeline_mode=pl.Buffered(1)),
                  pl.BlockSpec((1, nv, vt.shape[2], tv), lambda bi, i, s: (bi, 0, 0, 0),
                               pipeline_mode=pl.Buffered(1))],
        out_specs=pl.BlockSpec((1, n_heads * dv, tqb), lambda bi, i, s: (bi, 0, i)),
        scratch_shapes=[pltpu.VMEM((2, heads, dv + ONES_ROWS, tq), F32),
                        pltpu.VMEM((heads, tk, tq), BF16), pltpu.VMEM((heads, tk, tq), BF16)],
    )
    return pl.pallas_call(
        functools.partial(_attn_kernel, heads=heads, groups=n_heads // heads, kc=kc, key_of_head=key_of_head,
                          value_of_head=value_of_head, nk=nk, nk_tile=tk),
        out_shape=jax.ShapeDtypeStruct((b, n_heads * dv, l), BF16),
        grid_spec=grid_spec,
        compiler_params=_cparams(("parallel", "parallel"), 48),
        name=name,
    )(safe, q, k, vt)


def _hy_pre_kernel(x0_ref, x1_ref, v_ref, w_ref, b_ref, s_ref, x0o_ref):
    rows = x0_ref.shape[1]
    t = lax.broadcasted_iota(jnp.int32, (rows, LANES), 0)
    not_first = t > 0
    not_last = t < rows - 1

    def conv(ref, j):
        a = ref[0].astype(F32)
        prev = jnp.where(not_first, pltpu.roll(a, 1, 0), 0.0)
        nxt = jnp.where(not_last, pltpu.roll(a, rows - 1, 0), 0.0)
        return prev * w_ref[0, j] + a * w_ref[1, j] + nxt * w_ref[2, j] + b_ref[j]

    x0o_ref[0] = conv(x0_ref, 0).astype(BF16)
    s_ref[0] = (conv(x1_ref, 1) * conv(v_ref, 2)).astype(BF16)


def _hy_pre(hy, w_short, b_short):
    b, l, _ = hy.shape
    nc = HY_WIDTH // LANES
    w4 = w_short.reshape(3, 3, nc, 1, LANES)
    b4 = b_short.reshape(3, nc, 1, LANES)

    def seg(j):
        return pl.BlockSpec((1, l, LANES), lambda bi, c, j=j: (bi, 0, j * nc + c))

    out = pl.BlockSpec((1, l, LANES), lambda bi, c: (bi, 0, c))
    return pl.pallas_call(
        _hy_pre_kernel,
        out_shape=(jax.ShapeDtypeStruct((b, l, HY_WIDTH), BF16), jax.ShapeDtypeStruct((b, l, HY_WIDTH), BF16)),
        grid=(b, nc),
        in_specs=[seg(0), seg(1), seg(2),
                  pl.BlockSpec((3, 3, None, 1, LANES), lambda bi, c: (0, 0, c, 0, 0)),
                  pl.BlockSpec((3, None, 1, LANES), lambda bi, c: (0, c, 0, 0))],
        out_specs=(out, out),
        compiler_params=_cparams(("parallel", "parallel"), 48),
        name="hy_pre",
    )(hy, hy, hy, w4, b4)


def _left_mm_kernel(f_ref, x_ref, o_ref, *, hi):
    if hi:
        o_ref[0] = _dot3(f_ref[...], x_ref[0]).astype(o_ref.dtype)
    else:
        o_ref[0] = _dot(f_ref[...], x_ref[0]).astype(o_ref.dtype)


def _left_mm(f, x, out_dtype, hi=False):
    b, k, n = x.shape
    m = f.shape[0]
    cb = min(HY_COLS, n)
    return pl.pallas_call(
        functools.partial(_left_mm_kernel, hi=hi),
        out_shape=jax.ShapeDtypeStruct((b, m, n), out_dtype),
        grid=(b, n // cb),
        in_specs=[_const_spec((m, k)), pl.BlockSpec((1, k, cb), lambda bi, c: (bi, 0, c))],
        out_specs=pl.BlockSpec((1, m, cb), lambda bi, c: (bi, 0, c)),
        compiler_params=_cparams(("parallel", "parallel"), 48),
        name="hy_outer_dft",
    )(f, x)


def _hy_mid_kernel(f_ref, kf_ref, g_ref, a_ref, o_ref):
    n2 = DFT_N2
    for k in range(a_ref.shape[1]):
        kr, ki = kf_ref[k, :n2], kf_ref[k, n2:]
        for bi in range(a_ref.shape[0]):
            x = _dot(f_ref[k], a_ref[bi, k])
            xr, xi = x[:n2], x[n2:]
            y = jnp.concatenate([xr * kr - xi * ki, xr * ki + xi * kr], axis=0).astype(BF16)
            o_ref[bi, k] = _dot(g_ref[k], y).astype(BF16)


def _hy_mid(a4, f2, kf, g2):
    b, n1, r, c = a4.shape
    kb = HY_MID_K1
    mat = pl.BlockSpec((kb, r, r), lambda i: (i, 0, 0))
    dat = pl.BlockSpec((b, kb, r, c), lambda i: (0, i, 0, 0))
    return pl.pallas_call(
        _hy_mid_kernel,
        out_shape=jax.ShapeDtypeStruct(a4.shape, BF16),
        grid=(n1 // kb,),
        in_specs=[mat, pl.BlockSpec((kb, r, c), lambda i: (i, 0, 0)), mat, dat],
        out_specs=dat,
        compiler_params=_cparams(("parallel",), 48),
        name="hy_mid",
    )(f2, kf, g2, a4)


def _hy_filter_kernel(z_ref, win_ref, w1_ref, b1_ref, w2_ref, b2_ref, w3_ref, fr_ref, o_ref, *, half_tiles):
    fr = fr_ref[...]
    h = jnp.sin(fr * (_dot3(z_ref[...], w1_ref[...]) + b1_ref[...]))
    h = jnp.sin(fr * (_dot3(h, w2_ref[...]) + b2_ref[...]))
    h3 = _dot3(h, w3_ref[...])
    backward = pl.program_id(0) >= half_tiles
    o_ref[...] = jnp.where(backward, h3[:, HY_WIDTH:], h3[:, :HY_WIDTH]) * win_ref[...]


def _hy_filter(z2, win2, w1p, b1, w2, b2, w3, fr):
    n = z2.shape[0]
    tl = 512
    return pl.pallas_call(
        functools.partial(_hy_filter_kernel, half_tiles=n // (2 * tl)),
        out_shape=jax.ShapeDtypeStruct((n, HY_WIDTH), F32),
        grid=(n // tl,),
        in_specs=[pl.BlockSpec((tl, LANES), lambda i: (i, 0)), pl.BlockSpec((tl, HY_WIDTH), lambda i: (i, 0)),
                  _const_spec(w1p.shape), _const_spec(b1.shape), _const_spec(w2.shape), _const_spec(b2.shape),
                  _const_spec(w3.shape), _const_spec(fr.shape)],
        out_specs=pl.BlockSpec((tl, HY_WIDTH), lambda i: (i, 0)),
        compiler_params=_cparams(("parallel",), 32),
        name="hy_filter",
    )(z2, win2, w1p, b1, w2, b2, w3, fr)


def _hy_spec_kernel(f_ref, a_ref, o_ref):
    o_ref[0] = _dot3(f_ref[0], a_ref[0])


def _hy_spec(f2, a3):
    n1, r, c = a3.shape
    return pl.pallas_call(
        _hy_spec_kernel,
        out_shape=jax.ShapeDtypeStruct(a3.shape, F32),
        grid=(n1,),
        in_specs=[pl.BlockSpec((1, r, r), lambda i: (i, 0, 0)), pl.BlockSpec((1, r, c), lambda i: (i, 0, 0))],
        out_specs=pl.BlockSpec((1, r, c), lambda i: (i, 0, 0)),
        compiler_params=_cparams(("parallel",), 32),
        name="hy_filter_spectrum",
    )(f2, a3)


def _merge_kernel(x_ref, yat_ref, yc_ref, s_ref, x0_ref, hb_ref, yct_ref, gt_ref, wb_ref, wo_ref, g_ref, wg_ref,
                  wu_ref, wd_ref, gf_ref, o_ref, *, final):
    d = D_MODEL
    ba = _dot_tn(yat_ref[0], wb_ref[0])
    yb = x0_ref[0].astype(F32) * (yc_ref[0].astype(F32) + s_ref[0].astype(F32) * hb_ref[...])
    bb = _dot(yb.astype(BF16), wb_ref[1])
    bc = _dot_tn(yct_ref[0], wb_ref[2])
    merged = (gt_ref[0, :, 0:d].astype(F32) * ba + gt_ref[0, :, d:2 * d].astype(F32) * bb
              + gt_ref[0, :, 2 * d:3 * d].astype(F32) * bc)
    x = x_ref[0] + _dot(merged.astype(BF16), wo_ref[...])
    y = _ffn_body(x, g_ref[...], wg_ref, wu_ref, wd_ref)
    if final:
        y = _rms(y, gf_ref[...])
    o_ref[0] = y


def _merge_ffn(x, yat, yc, s, x0, hy_bias, yct, gates, wb, wo, g, wg, wu, wd, g_final, final):
    b, l, _ = x.shape
    tm = TOK_TILE

    def tok(w):
        return pl.BlockSpec((1, tm, w), lambda bi, i: (bi, i, 0))

    tr = pl.BlockSpec((1, BRANCH_W, tm), lambda bi, i: (bi, 0, i))
    return pl.pallas_call(
        functools.partial(_merge_kernel, final=final),
        out_shape=jax.ShapeDtypeStruct(x.shape, F32),
        grid=(b, l // tm),
        in_specs=[tok(D_MODEL), tr, tok(BRANCH_W), tok(BRANCH_W), tok(BRANCH_W), _const_spec((1, HY_WIDTH)), tr,
                  tok(N_BRANCH * D_MODEL),
                  _const_spec(wb.shape), _const_spec(wo.shape), _const_spec((1, D_MODEL)),
                  _const_spec((D_MODEL, D_FF)), _const_spec((D_MODEL, D_FF)), _const_spec((D_FF, D_MODEL)),
                  _const_spec((1, D_MODEL))],
        out_specs=tok(D_MODEL),
        compiler_params=_cparams(("parallel", "parallel"), 58),
        name="merge_ffn",
    )(x, yat, yc, s, x0, hy_bias, yct, gates, wb, wo, g, wg, wu, wd, g_final)


def _rope_tables(l):
    rows = l // GRID_W
    row = jnp.repeat(jnp.arange(rows, dtype=F32), GRID_W)
    col = jnp.tile(jnp.arange(GRID_W, dtype=F32), rows)

    def tab(d_rot):
        n_freq = d_rot // 4
        inv = ROPE_THETA ** (-jnp.arange(n_freq, dtype=F32) / n_freq)
        ang = jnp.concatenate([row[:, None] * inv, col[:, None] * inv], axis=-1)
        c = jnp.repeat(jnp.cos(ang), 2, axis=-1)
        s = jnp.repeat(jnp.sin(ang), 2, axis=-1) * jnp.tile(jnp.array([-1.0, 1.0], F32), d_rot // 2)
        return c, s

    ca, sa = tab(GQA_HEAD_DIM)
    cm, sm = tab(MLA_ROPE)
    tail = jnp.zeros((l, LANES - MLA_ROPE_LANE - MLA_ROPE), F32)
    cm = jnp.concatenate([jnp.ones((l, MLA_ROPE_LANE), F32), cm, tail], axis=1)
    sm = jnp.concatenate([jnp.zeros((l, MLA_ROPE_LANE), F32), sm, tail], axis=1)
    return (jnp.tile(ca, (1, 2)), jnp.tile(sa, (1, 2)), cm, sm)


def _hy_positions(l):
    r = jnp.arange(l, dtype=jnp.int32)
    pos = jnp.concatenate([r, (l - r) % l]).astype(F32)[:, None]
    live = jnp.concatenate([jnp.ones((l,), F32), (r > 0).astype(F32)])[:, None]
    t = pos * (1.0 / (l - 1))
    w = 2.0 * math.pi * pos / l
    f = jnp.linspace(1e-4, HY_BANDS - 1, HY_BANDS, dtype=F32)[None, :]
    z2 = jnp.concatenate([t, jnp.cos(f * w), -jnp.sin(f * w), jnp.zeros((2 * l, LANES - HY_EMB), F32)], axis=-1)
    max_decay = math.log(HY_TARGET) / HY_FAST
    min_decay = math.log(HY_TARGET) / HY_SLOW
    deltas = jnp.abs(jnp.linspace(min_decay, max_decay, HY_WIDTH, dtype=F32))
    return z2, jnp.exp(-t * deltas[None, :]) * live


def _hy_k1_count(l):
    n1 = 2 * l // DFT_N2
    return -(-(n1 // 2 + 1) // 8) * 8


def _dft_tables(l):
    n = 2 * l
    n2 = DFT_N2
    n1 = n // n2
    n1c = _hy_k1_count(l)
    two_pi = 2.0 * math.pi
    i1 = jnp.arange(n1, dtype=jnp.int32)
    ang1 = ((i1[:n1c, None] * i1[None, :]) % n1).astype(F32) * (two_pi / n1)
    live = (i1[:n1c] <= n1 // 2)[:, None]
    c1, s1 = jnp.where(live, jnp.cos(ang1), 0.0), jnp.where(live, jnp.sin(ang1), 0.0)
    f1_full = jnp.stack([c1, -s1], axis=1).reshape(2 * n1c, n1)
    fold = jnp.where((i1[:n1c] == 0) | (i1[:n1c] == n1 // 2), 1.0, 2.0)[:, None] * (1.0 / n)
    g1 = jnp.stack([(c1 * fold)[:, :n1 // 2].T, (-s1 * fold)[:, :n1 // 2].T], axis=2).reshape(n1 // 2, 2 * n1c)
    i2 = jnp.arange(n2, dtype=jnp.int32)
    kk = i1[:n1c, None, None] + n1 * i2[None, :, None]
    ang2 = ((kk * i2[None, None, :]) % n).astype(F32) * (two_pi / n)
    c2, s2 = jnp.cos(ang2), jnp.sin(ang2)
    f2 = jnp.concatenate([jnp.concatenate([c2, s2], axis=2), jnp.concatenate([-s2, c2], axis=2)], axis=1)
    c2t, s2t = jnp.swapaxes(c2, 1, 2), jnp.swapaxes(s2, 1, 2)
    g2 = jnp.concatenate([jnp.concatenate([c2t, -s2t], axis=2), jnp.concatenate([s2t, c2t], axis=2)], axis=1)
    return f1_full, g1, f2, g2


def _swap_pairs(n):
    return np.arange(n) ^ 1


def _proj_a_columns():
    zero = IN_WIDTH
    cols = np.full((PA_END,), zero, np.int64)
    q0, k0, v0, _, cq0, ckv0, kr0, _ = IN_OFFS[:8]
    for h in range(GQA_HEADS):
        dst = h * LANES + (h // (GQA_HEADS // GQA_KV_HEADS)) * GQA_HEAD_DIM
        src = q0 + h * GQA_HEAD_DIM + np.arange(GQA_HEAD_DIM)
        cols[PA_Q + dst:PA_Q + dst + GQA_HEAD_DIM] = src
        cols[PA_QS + dst:PA_QS + dst + GQA_HEAD_DIM] = q0 + h * GQA_HEAD_DIM + _swap_pairs(GQA_HEAD_DIM)
    cols[PA_K:PA_K + 128] = k0 + np.arange(128)
    cols[PA_KS:PA_KS + 128] = k0 + _swap_pairs(128)
    cols[PA_V:PA_V + 128] = v0 + np.arange(128)
    cols[PA_CQ:PA_CQ + MLA_Q_RANK] = cq0 + np.arange(MLA_Q_RANK)
    cols[PA_CKV:PA_CKV + MLA_KV_RANK] = ckv0 + np.arange(MLA_KV_RANK)
    r0 = MLA_ROPE_LANE
    cols[PA_KR + r0:PA_KR + r0 + MLA_ROPE] = kr0 + np.arange(MLA_ROPE)
    cols[PA_KRS + r0:PA_KRS + r0 + MLA_ROPE] = kr0 + _swap_pairs(MLA_ROPE)
    return cols


def _mla_q_columns():
    hd = MLA_NOPE + MLA_ROPE
    zero = MLA_HEADS * hd
    main = np.full((MLA_HEADS * MLA_KC,), zero, np.int64)
    swap = np.full((MLA_HEADS * MLA_KC,), zero, np.int64)
    for h in range(MLA_HEADS):
        b0 = h * MLA_KC
        main[b0:b0 + MLA_NOPE] = h * hd + np.arange(MLA_NOPE)
        r0 = b0 + MLA_ROPE_LANE
        main[r0:r0 + MLA_ROPE] = h * hd + MLA_NOPE + np.arange(MLA_ROPE)
        swap[r0:r0 + MLA_ROPE] = h * hd + MLA_NOPE + _swap_pairs(MLA_ROPE)
    return main, swap


def _mla_kv_columns():
    hd = MLA_NOPE + MLA_V
    zero = MLA_HEADS * hd
    knope = np.full((MLA_HEADS * MLA_KC,), zero, np.int64)
    for h in range(MLA_HEADS):
        knope[h * MLA_KC:h * MLA_KC + MLA_NOPE] = h * hd + np.arange(MLA_NOPE)
    val = np.concatenate([h * hd + MLA_NOPE + np.arange(MLA_V) for h in range(MLA_HEADS)])
    return np.concatenate([knope, val])


def _take_cols(w, cols):
    wz = jnp.concatenate([w, jnp.zeros((w.shape[0], 1), w.dtype)], axis=1)
    return jnp.take(wz, jnp.asarray(cols, jnp.int32), axis=1).astype(BF16)


def _tile2(g):
    return jnp.tile(g, 2)[None, :]


def _encoder(x, lw, g_final):
    b, l, d = x.shape
    n2 = DFT_N2
    n1 = 2 * l // n2
    c = HY_WIDTH
    tabs = _rope_tables(l)
    z2, win2 = _hy_positions(l)
    f1_full, g1, f2, g2 = _dft_tables(l)
    f1_b = f1_full[:, :n1 // 2].astype(BF16)
    g1_b = g1.astype(BF16)
    f2_b = f2.astype(BF16)
    g2_b = g2.astype(BF16)
    for li, w in enumerate(lw):
        x2, hy, gates = _ffn_proj_b(x.reshape(b * l, d), w["g_ffn1"], w["wg1"], w["wu1"], w["wd1"], w["g_mix"],
                                    w["w_b"])
        x = x2.reshape(b, l, d)
        qg, kg, vgt, qm, km, vmt, bnd = _proj_a(x, w["g_mix"], w["w_a"], tabs, w["gq"], w["gqs"], w["gk"],
                                                w["gks"], w["gcq"], w["gckv"], w["wuq"], w["wuqs"], w["wukv"])
        top = bnd[:, :, :4, 0]
        limit = ATT_SAFE_SCORE ** 2
        safe_a = (top[:, :, 0] * jnp.max(top[:, :, 1], axis=1, keepdims=True) <= limit).astype(jnp.int32)
        safe_m = (top[:, :, 2] * jnp.max(top[:, :, 3], axis=1, keepdims=True) <= limit).astype(jnp.int32)
        per_kv = GQA_HEADS // GQA_KV_HEADS
        yat = _attention(safe_a, qg, kg, vgt, heads=ATT_HEADS, kc=GQA_KC, tk=ATT_TK_GQA,
                         key_of_head=(0,) * GQA_HEADS,
                         value_of_head=tuple(h // per_kv for h in range(GQA_HEADS)), name="attn_gqa")
        yct = _attention(safe_m, qm, km, vmt, heads=ATT_HEADS, kc=MLA_KC, tk=ATT_TK_MLA,
                         key_of_head=tuple(range(MLA_HEADS)),
                         value_of_head=tuple(range(MLA_HEADS)), name="attn_mla")
        kc_time = _hy_filter(z2, win2, w["w1p"], w["b1"], w["w2"], w["b2"], w["w3"], w["fr"])
        n1c = _hy_k1_count(l)
        ka =_left_mm(f1_full, kc_time.reshape(1, n1, n2 * c), F32, hi=True)
        kf = _hy_spec(f2, ka.reshape(n1c, 2 * n2, c))
        s, x0 = _hy_pre(hy.reshape(b, l, 3 * c), w["w_short"], w["b_short"])
        a = _left_mm(f1_b, s.reshape(b, n1 // 2, n2 * c), BF16)
        bi = _hy_mid(a.reshape(b, n1c, 2 * n2, c), f2_b, kf, g2_b)
        yc = _left_mm(g1_b, bi.reshape(b, 2 * n1c, n2 * c), BF16)
        x = _merge_ffn(x, yat, yc.reshape(b, l, c), s, x0, w["hy_bias"], yct, gates.reshape(b, l, N_BRANCH * d),
                       w["wb"], w["wo"],
                       w["g_ffn2"], w["wg2"], w["wu2"], w["wd2"], g_final, li == len(lw) - 1)
    return x


def kernel(x_prompt, x_sample, g_ffn1, w_ffn1_gate, w_ffn1_up, w_ffn1_down, g_mix, w_in, g_qnorm, g_knorm,
           w_hy_short, b_hy_short, w_hy_f1, b_hy_f1, w_hy_f2, b_hy_f2, w_hy_f3, hy_sin_freq, hy_bias,
           g_mla_q, w_mla_uq, g_mla_kv, w_mla_ukv, w_branch, w_out, g_ffn2, w_ffn2_gate, w_ffn2_up,
           w_ffn2_down, g_final):
    cols_a = _proj_a_columns()
    uq_main, uq_swap = _mla_q_columns()
    ukv_cols = _mla_kv_columns()
    sw64 = _swap_pairs(GQA_HEAD_DIM)
    hy0 = IN_OFFS[3]
    gt0 = IN_OFFS[7]
    lw = []
    for l in range(DEPTH):
        lw.append(dict(
            g_ffn1=g_ffn1[l][None], wg1=w_ffn1_gate[l].astype(BF16), wu1=w_ffn1_up[l].astype(BF16),
            wd1=w_ffn1_down[l].astype(BF16),
            g_ffn2=g_ffn2[l][None], wg2=w_ffn2_gate[l].astype(BF16), wu2=w_ffn2_up[l].astype(BF16),
            wd2=w_ffn2_down[l].astype(BF16),
            g_mix=g_mix[l][None],
            w_a=_take_cols(w_in[l], cols_a),
            w_b=jnp.concatenate([w_in[l][:, hy0:hy0 + 3 * HY_WIDTH], w_in[l][:, gt0:]], axis=1).astype(BF16),
            gq=_tile2(g_qnorm[l]), gqs=_tile2(g_qnorm[l][sw64]),
            gk=_tile2(g_knorm[l]), gks=_tile2(g_knorm[l][sw64]),
            gcq=g_mla_q[l][None], gckv=g_mla_kv[l][None],
            wuq=_take_cols(w_mla_uq[l], uq_main), wuqs=_take_cols(w_mla_uq[l], uq_swap),
            wukv=_take_cols(w_mla_ukv[l], ukv_cols),
            w_short=w_hy_short[l], b_short=b_hy_short[l],
            w1p=jnp.pad(w_hy_f1[l], ((0, LANES - HY_EMB), (0, 0))), b1=b_hy_f1[l][None],
            w2=w_hy_f2[l], b2=b_hy_f2[l][None], w3=w_hy_f3[l], fr=hy_sin_freq[l][None],
            hy_bias=hy_bias[l][None],
            wb=w_branch[l].astype(BF16), wo=w_out[l].astype(BF16),
        ))
    gf = g_final[None]
    return _encoder(x_prompt, lw, gf), _encoder(x_sample, lw, gf)
```
